```python
import math
import jax
import jax.numpy as jnp
from jax import lax
import numpy as np

D_MODEL = 1024
BATCH = 1
SEQ = 16384
DEPTH = 1
DEC_BATCH = 128
DEC_SEQ = 1
PAST_LEN = 16384
PAGE_SIZE = 128

HEAD_DIM = 64
ATTN_HEADS = 8
ATTN_KV_HEADS = 2
WINDOW = 128
N_BUCKETS = 32
MAX_DISTANCE = 128
GLA_HEADS = 4
GLA_DK = 64
GLA_DV = 128
GLA_RANK = 16
GLA_TAU = 16.0
GLA_CHUNK = 64
ATTN_WIDTH = ATTN_HEADS * HEAD_DIM
GLA_WIDTH = GLA_HEADS * GLA_DV
MIX_WIDTH = ATTN_WIDTH + GLA_WIDTH
D_FF = -(-8 * D_MODEL // (3 * 256)) * 256
SPLIT_SIZES = (ATTN_WIDTH, ATTN_KV_HEADS * HEAD_DIM, ATTN_KV_HEADS * HEAD_DIM,
               GLA_HEADS * GLA_DK, GLA_HEADS * GLA_DK, GLA_WIDTH, GLA_WIDTH, GLA_RANK)
IN_WIDTH = sum(SPLIT_SIZES)
EPS = 1e-6

kernel_name = "hymba_swa_sink_gla_decode_step"


def rmsnorm(x, g):
    xf = x.astype(jnp.float32)
    r = lax.rsqrt(jnp.mean(xf * xf, axis=-1, keepdims=True) + EPS)
    return (xf * r * g.astype(jnp.float32)).astype(x.dtype)


def t5_bucket(dist):
    n = jnp.maximum(dist, 0)
    max_exact = N_BUCKETS // 2
    nf = jnp.maximum(n, 1).astype(jnp.float32)
    large = max_exact + (jnp.log(nf / max_exact) / math.log(MAX_DISTANCE / max_exact)
                         * (N_BUCKETS - max_exact)).astype(jnp.int32)
    large = jnp.minimum(large, N_BUCKETS - 1)
    return jnp.where(n < max_exact, n, large)


def project(h, w_in, q_norm_g, k_norm_g, w_gla_gate2, b_gla_gate):
    N, T, _ = h.shape
    idx = [int(s) for s in np.cumsum(SPLIT_SIZES)[:-1]]
    u_qa, u_ka, u_va, u_qg, u_kg, u_vg, u_rg, u_lr = jnp.split(h @ w_in, idx, axis=-1)
    q_a = rmsnorm(u_qa.reshape(N, T, ATTN_HEADS, HEAD_DIM), q_norm_g)
    k_a = rmsnorm(u_ka.reshape(N, T, ATTN_KV_HEADS, HEAD_DIM), k_norm_g)
    v_a = u_va.reshape(N, T, ATTN_KV_HEADS, HEAD_DIM)
    q_g = u_qg.reshape(N, T, GLA_HEADS, GLA_DK) * (GLA_DK ** -0.5)
    k_g = u_kg.reshape(N, T, GLA_HEADS, GLA_DK)
    v_g = u_vg.reshape(N, T, GLA_HEADS, GLA_DV)
    log_a = jax.nn.log_sigmoid((u_lr @ w_gla_gate2 + b_gla_gate).astype(jnp.float32)) / GLA_TAU
    log_a = log_a.reshape(N, T, GLA_HEADS, GLA_DK)
    return q_a, k_a, v_a, q_g, k_g, v_g, u_rg, log_a


def sink_attention(q, k, v, dist, valid, rel_bias, sinks):
    N, Tq, H, hd = q.shape
    Tk, kvh = k.shape[1], k.shape[2]
    G = H // kvh
    qg = q.reshape(N, Tq, kvh, G, hd).astype(jnp.float32)
    s = jnp.einsum('nqkgd,nskd->nkgqs', qg, k.astype(jnp.float32)) * (hd ** -0.5)
    bias = rel_bias[t5_bucket(dist)].astype(jnp.float32)
    bias = jnp.transpose(bias, (2, 0, 1)).reshape(kvh, G, Tq, Tk)
    s = jnp.where(valid[:, None, None], s + bias, -jnp.inf)
    sk = sinks.astype(jnp.float32).reshape(kvh, G, 1, 1)
    m = jnp.maximum(jnp.max(s, axis=-1, keepdims=True), sk)
    p = jnp.exp(s - m)
    denom = jnp.sum(p, axis=-1, keepdims=True) + jnp.exp(sk - m)
    o = jnp.einsum('nkgqs,nskd->nqkgd', p / denom, v.astype(jnp.float32))
    return o.reshape(N, Tq, H * hd)


def prompt_window_attention(q, k, v, rel_bias, sinks):
    B, T, H, hd = q.shape
    kvh = k.shape[2]
    W = WINDOW
    nb = T // W
    qb = q.reshape(B * nb, W, H, hd)

    def pair(t):
        tb = t.reshape(B, nb, W, kvh, hd)
        prev = jnp.concatenate([jnp.zeros_like(tb[:, :1]), tb[:, :-1]], axis=1)
        return jnp.concatenate([prev, tb], axis=2).reshape(B * nb, 2 * W, kvh, hd)

    i = jnp.arange(W)[:, None]
    j = jnp.arange(2 * W)[None, :]
    dist = W + i - j
    band = (dist >= 0) & (dist < WINDOW)
    has_prev = (jnp.arange(nb) > 0)[:, None, None] | (j >= W)[None]
    valid = jnp.tile(band[None] & has_prev, (B, 1, 1))
    o = sink_attention(qb, pair(k), pair(v), dist, valid, rel_bias, sinks)
    return o.reshape(B, T, H * hd)


def sample_window_attention(q, k_new, v_new, k_buf, v_buf, rel_bias, sinks):
    Ts = q.shape[1]
    wb = k_buf.shape[1]
    k_all = jnp.concatenate([k_buf, k_new], axis=1)
    v_all = jnp.concatenate([v_buf, v_new], axis=1)
    dist = wb + jnp.arange(Ts)[:, None] - jnp.arange(wb + Ts)[None, :]
    valid = ((dist >= 0) & (dist < WINDOW))[None]
    o = sink_attention(q, k_all, v_all, dist, valid, rel_bias, sinks)
    return o, k_all[:, -wb:], v_all[:, -wb:]


def gla_chunked(q, k, v, log_a, s0):
    N, T, H, _ = q.shape
    C = min(GLA_CHUNK, T)
    pad = (-T) % C
    nc = (T + pad) // C

    def blocks(t):
        t = jnp.pad(t.astype(jnp.float32), ((0, 0), (0, pad), (0, 0), (0, 0)))
        return t.reshape(N, nc, C, H, t.shape[-1]).transpose(1, 0, 3, 2, 4)

    causal = jnp.arange(C)[:, None] >= jnp.arange(C)[None, :]

    def step(S, inp):
        qc, kc, vc, ac = inp
        b = jnp.cumsum(ac, axis=2)
        o_inter = jnp.einsum('nhtd,nhdv->nhtv', qc * jnp.exp(b), S)
        diff = b[:, :, :, None, :] - b[:, :, None, :, :]
        decay = jnp.exp(jnp.where(causal[:, :, None], diff, -jnp.inf))
        scores = jnp.einsum('nhtd,nhsd,nhtsd->nhts', qc, kc, decay)
        o = o_inter + jnp.einsum('nhts,nhsv->nhtv', scores, vc)
        b_last = b[:, :, -1:, :]
        S = (jnp.exp(b_last[:, :, 0, :, None]) * S
             + jnp.einsum('nhsd,nhsv->nhdv', kc * jnp.exp(b_last - b), vc))
        return S, o

    S, o = lax.scan(step, s0, (blocks(q), blocks(k), blocks(v), blocks(log_a)))
    o = o.transpose(1, 0, 3, 2, 4).reshape(N, nc * C, H, GLA_DV)[:, :T]
    return o, S


def finish(x, o_attn, o_gla, r_gla, gla_norm_g, w_o, ffn_norm_g, w_gate, w_up, w_down):
    N, T, _ = x.shape
    g = rmsnorm(o_gla.astype(x.dtype), gla_norm_g).reshape(N, T, GLA_WIDTH) * jax.nn.silu(r_gla)
    h = x + jnp.concatenate([o_attn.astype(x.dtype), g], axis=-1) @ w_o
    z = rmsnorm(h, ffn_norm_g)
    return h + (jax.nn.silu(z @ w_gate) * (z @ w_up)) @ w_down


def setup_inputs(seed: int = 0) -> dict:
    key = jax.random.key(seed)
    ks = jax.random.split(key, 20)
    wb = min(WINDOW, PAST_LEN)

    def nrm(k, shape, scale):
        return scale * jax.random.normal(k, shape, jnp.float32)

    return {
        "x_prompt": nrm(ks[0], (BATCH, SEQ, D_MODEL), 1.0),
        "x_sample": nrm(ks[1], (DEC_BATCH, DEC_SEQ, D_MODEL), 1.0),
        "cache_k": nrm(ks[2], (DEPTH, DEC_BATCH, wb, ATTN_KV_HEADS, HEAD_DIM), 1.0),
        "cache_v": nrm(ks[3], (DEPTH, DEC_BATCH, wb, ATTN_KV_HEADS, HEAD_DIM), 1.0),
        "state_gla": nrm(ks[4], (DEPTH, DEC_BATCH, GLA_HEADS, GLA_DK, GLA_DV), 0.3),
        "attn_norm_g": 1.0 + nrm(ks[5], (DEPTH, D_MODEL), 0.1),
        "w_in": nrm(ks[6], (DEPTH, D_MODEL, IN_WIDTH), D_MODEL ** -0.5),
        "q_norm_g": 1.0 + nrm(ks[7], (DEPTH, HEAD_DIM), 0.1),
        "k_norm_g": 1.0 + nrm(ks[8], (DEPTH, HEAD_DIM), 0.1),
        "attn_sinks": nrm(ks[9], (DEPTH, ATTN_HEADS), 0.5),
        "rel_bias": nrm(ks[10], (N_BUCKETS, ATTN_HEADS), 0.5),
        "w_gla_gate2": nrm(ks[11], (DEPTH, GLA_RANK, GLA_HEADS * GLA_DK), GLA_RANK ** -0.5),
        "b_gla_gate": nrm(ks[12], (DEPTH, GLA_HEADS * GLA_DK), 0.1),
        "gla_norm_g": 1.0 + nrm(ks[13], (DEPTH, GLA_DV), 0.1),
        "w_o": nrm(ks[14], (DEPTH, MIX_WIDTH, D_MODEL), MIX_WIDTH ** -0.5),
        "ffn_norm_g": 1.0 + nrm(ks[15], (DEPTH, D_MODEL), 0.1),
        "w_gate": nrm(ks[16], (DEPTH, D_MODEL, D_FF), D_MODEL ** -0.5),
        "w_up": nrm(ks[17], (DEPTH, D_MODEL, D_FF), D_MODEL ** -0.5),
        "w_down": nrm(ks[18], (DEPTH, D_FF, D_MODEL), D_FF ** -0.5),
    }


def reference(x_prompt, x_sample, cache_k, cache_v, state_gla, attn_norm_g, w_in, q_norm_g,
              k_norm_g, attn_sinks, rel_bias, w_gla_gate2, b_gla_gate, gla_norm_g, w_o,
              ffn_norm_g, w_gate, w_up, w_down):
    xp, xs = x_prompt, x_sample
    wb = cache_k.shape[2]
    kp_l, vp_l, sp_l, ks_l, vs_l, ss_l = [], [], [], [], [], []
    for l in range(DEPTH):
        proj_w = (w_in[l], q_norm_g[l], k_norm_g[l], w_gla_gate2[l], b_gla_gate[l])
        ffn_w = (gla_norm_g[l], w_o[l], ffn_norm_g[l], w_gate[l], w_up[l], w_down[l])
        q_a, k_a, v_a, q_g, k_g, v_g, r_g, log_a = project(rmsnorm(xp, attn_norm_g[l]), *proj_w)
        o_a = prompt_window_attention(q_a, k_a, v_a, rel_bias, attn_sinks[l])
        s0 = jnp.zeros((xp.shape[0], GLA_HEADS, GLA_DK, GLA_DV), jnp.float32)
        o_g, s_p = gla_chunked(q_g, k_g, v_g, log_a, s0)
        xp = finish(xp, o_a, o_g, r_g, *ffn_w)
        kp_l.append(k_a[:, -wb:])
        vp_l.append(v_a[:, -wb:])
        sp_l.append(s_p.astype(state_gla.dtype))
        q_a, k_a, v_a, q_g, k_g, v_g, r_g, log_a = project(rmsnorm(xs, attn_norm_g[l]), *proj_w)
        o_a, k_buf, v_buf = sample_window_attention(q_a, k_a, v_a, cache_k[l], cache_v[l],
                                                    rel_bias, attn_sinks[l])
        o_g, s_s = gla_chunked(q_g, k_g, v_g, log_a, state_gla[l].astype(jnp.float32))
        xs = finish(xs, o_a, o_g, r_g, *ffn_w)
        ks_l.append(k_buf)
        vs_l.append(v_buf)
        ss_l.append(s_s.astype(state_gla.dtype))
    k_win_prompt = jnp.stack(kp_l)
    v_win_prompt = jnp.stack(vp_l)
    gla_state_prompt = jnp.stack(sp_l)
    k_win_sample = jnp.stack(ks_l)
    v_win_sample = jnp.stack(vs_l)
    gla_state_sample = jnp.stack(ss_l)
    return (xp, xs, k_win_prompt, v_win_prompt, gla_state_prompt,
            k_win_sample, v_win_sample, gla_state_sample)
```

```python
import functools
import math

import numpy as np
import jax
import jax.numpy as jnp
from jax import lax
from jax.experimental import pallas as pl
from jax.experimental.pallas import tpu as pltpu

F32 = jnp.float32
BF16 = jnp.bfloat16

D_MODEL = 1024
HEAD_DIM = 64
ATTN_HEADS = 8
ATTN_KV_HEADS = 2
WINDOW = 128
N_BUCKETS = 32
MAX_DISTANCE = 128
GLA_HEADS = 4
GLA_DK = 64
GLA_DV = 128
GLA_RANK = 16
GLA_TAU = 16.0
EPS = 1e-6
ATTN_WIDTH = ATTN_HEADS * HEAD_DIM
KV_WIDTH = ATTN_KV_HEADS * HEAD_DIM
GLA_QK_WIDTH = GLA_HEADS * GLA_DK
GLA_WIDTH = GLA_HEADS * GLA_DV
MAIN_WIDTH = ATTN_WIDTH + 2 * KV_WIDTH + 2 * GLA_QK_WIDTH + 2 * GLA_WIDTH
LANES = 128
RANK_PAD = LANES
IN_PAD_WIDTH = MAIN_WIDTH + RANK_PAD

C_QA = 0
C_KA = C_QA + ATTN_WIDTH
C_VA = C_KA + KV_WIDTH
C_QG = C_VA + KV_WIDTH
C_KG = C_QG + GLA_QK_WIDTH
C_VG = C_KG + GLA_QK_WIDTH
C_RG = C_VG + GLA_WIDTH
C_LA = C_RG + GLA_WIDTH
P_WIDTH = C_LA + GLA_QK_WIDTH

BLK = 128
N_LEVELS = 7
NEG = -1e30
ATTN_SCALE = HEAD_DIM ** -0.5
SAMPLE_BLK = 8
VMEM_LIMIT = 56 * 1024 * 1024


def _t5_bucket_np(dist):
    n = np.maximum(dist, 0)
    max_exact = N_BUCKETS // 2
    nf = np.maximum(n, 1).astype(np.float64)
    large = max_exact + (np.log(nf / max_exact) / math.log(MAX_DISTANCE / max_exact)
                         * (N_BUCKETS - max_exact)).astype(np.int32)
    large = np.minimum(large, N_BUCKETS - 1)
    return np.where(n < max_exact, n, large).astype(np.int32)


def _prompt_bucket_tables():
    i = np.arange(BLK)[:, None]
    j = np.arange(2 * BLK)[None, :]
    dist = BLK + i - j
    band = (dist >= 0) & (dist < WINDOW)
    bucket = _t5_bucket_np(dist)
    t0 = np.where(band, bucket, -1)
    t1 = np.where(band & (j >= BLK), bucket, -1)
    return np.stack([t0, t1]).astype(np.int32)


def _level_tables():
    t = np.arange(BLK)[:, None]
    s = np.arange(BLK)[None, :]
    x = t ^ s
    lev = np.where(x > 0, np.floor(np.log2(np.maximum(x, 1))).astype(np.int32) + 1, 0)
    lev = np.where(s > t, -1, lev).astype(np.int32)
    sel = np.zeros((N_LEVELS * BLK, BLK), np.float32)
    for L in range(1, N_LEVELS + 1):
        m = 1 << (L - 1)
        for r in range(BLK):
            sel[(L - 1) * BLK + r, ((r >> L) << L) + m - 1] = 1.0
    tri = (s <= t).astype(np.float32)
    return lev, sel, tri


_BUCKET_PROMPT = _prompt_bucket_tables()
_LEV, _SEL, _TRI = _level_tables()
_BUCKET_SAMPLE = _t5_bucket_np((WINDOW - 1) - np.arange(WINDOW))[None, :].astype(np.int32)


def _nt_dot(a, b):
    return lax.dot_general(a, b, (((1,), (1,)), ((), ())), preferred_element_type=F32)


def _tn_dot(a, b):
    return lax.dot_general(a, b, (((0,), (0,)), ((), ())), preferred_element_type=F32)


def _group_allsum(x, group):
    lane = lax.broadcasted_iota(jnp.int32, (x.shape[0], LANES), 1)
    outs = []
    for c in range(x.shape[1] // LANES):
        y = x[:, c * LANES:(c + 1) * LANES]
        k = group // 2
        while k >= 1:
            partner = jnp.where((lane & k) != 0, pltpu.roll(y, k, 1), pltpu.roll(y, LANES - k, 1))
            y = y + partner
            k //= 2
        outs.append(y)
    return outs[0] if len(outs) == 1 else jnp.concatenate(outs, axis=1)


def _sigmoid(x):
    return 1.0 / (1.0 + jnp.exp(-x))


def _split3(x):
    hi = x.astype(BF16)
    r1 = x - hi.astype(F32)
    mid = r1.astype(BF16)
    lo = (r1 - mid.astype(F32)).astype(BF16)
    return jnp.concatenate([hi, mid, lo], axis=1)


def _sum3(y):
    w = y.shape[1] // 3
    return y[:, :w] + y[:, w:2 * w] + y[:, 2 * w:]


def _proj_kernel(x_ref, g_ref, w_ref, qn_ref, kn_ref, w2_ref, b2_ref, out_ref):
    x = x_ref[...]
    r = lax.rsqrt(jnp.mean(x * x, axis=-1, keepdims=True) + EPS)
    n = (x * r * g_ref[...]).astype(BF16)

    def seg(c0, c1):
        return jnp.dot(n, w_ref[:, c0:c1], preferred_element_type=F32)

    q = seg(C_QA, C_KA)
    qr = lax.rsqrt(_group_allsum(q * q, HEAD_DIM) * (1.0 / HEAD_DIM) + EPS)
    out_ref[:, C_QA:C_KA] = q * qr * qn_ref[...]
    k = seg(C_KA, C_VA)
    kr = lax.rsqrt(_group_allsum(k * k, HEAD_DIM) * (1.0 / HEAD_DIM) + EPS)
    out_ref[:, C_KA:C_VA] = k * kr * kn_ref[...]
    out_ref[:, C_VA:C_QG] = seg(C_VA, C_QG)
    out_ref[:, C_QG:C_KG] = seg(C_QG, C_KG) * (GLA_DK ** -0.5)
    out_ref[:, C_KG:C_VG] = seg(C_KG, C_VG)
    out_ref[:, C_VG:C_RG] = seg(C_VG, C_RG)
    out_ref[:, C_RG:C_LA] = seg(C_RG, C_LA)
    lr = seg(MAIN_WIDTH, IN_PAD_WIDTH).astype(BF16)
    z = jnp.dot(lr, w2_ref[...], preferred_element_type=F32) + b2_ref[...]
    log_sig = jnp.minimum(z, 0.0) - jnp.log1p(jnp.exp(-jnp.abs(z)))
    out_ref[:, C_LA:P_WIDTH] = log_sig / GLA_TAU


def _const_spec(shape):
    nd = len(shape)
    return pl.BlockSpec(shape, lambda i: (0,) * nd, pipeline_mode=pl.Buffered(1))


def _project(x, g_attn, w_in_p, qn, kn, w2p, b2, rows):
    t = x.shape[0]
    return pl.pallas_call(
        _proj_kernel,
        grid=(t // rows,),
        in_specs=[
            pl.BlockSpec((rows, D_MODEL), lambda i: (i, 0)),
            _const_spec((1, D_MODEL)),
            _const_spec((D_MODEL, IN_PAD_WIDTH)),
            _const_spec((1, ATTN_WIDTH)),
            _const_spec((1, KV_WIDTH)),
            _const_spec((RANK_PAD, GLA_QK_WIDTH)),
            _const_spec((1, GLA_QK_WIDTH)),
        ],
        out_specs=pl.BlockSpec((rows, P_WIDTH), lambda i: (i, 0)),
        out_shape=jax.ShapeDtypeStruct((t, P_WIDTH), F32),
        compiler_params=pltpu.CompilerParams(
            dimension_semantics=("arbitrary",), vmem_limit_bytes=VMEM_LIMIT),
        name="proj",
    )(x, g_attn, w_in_p, qn, kn, w2p, b2)


def _prompt_mixer_kernel(relb_ref, sink_ref, p_ref, bucket_ref, lev_ref, tri_ref, sel_ref,
                         gn_ref, omix_ref, st_ref, kprev, vprev, st_scr, mb_scr):
    i = pl.program_id(0)

    @pl.when(i == 0)
    def _init():
        kprev[...] = jnp.zeros_like(kprev)
        vprev[...] = jnp.zeros_like(vprev)
        st_scr[...] = jnp.zeros_like(st_scr)
        for tb in range(2):
            bk = bucket_ref[tb]
            acc = [jnp.zeros(bk.shape, F32) for _ in range(ATTN_HEADS)]
            for b in range(N_BUCKETS):
                hit = bk == b
                for h in range(ATTN_HEADS):
                    acc[h] = jnp.where(hit, relb_ref[b * ATTN_HEADS + h], acc[h])
            masked = bk < 0
            for h in range(ATTN_HEADS):
                mb_scr[tb, h // 2, :, (h % 2) * 2 * BLK:(h % 2 + 1) * 2 * BLK] = (
                    jnp.where(masked, NEG, acc[h]))

    k_cur = p_ref[:, C_KA:C_VA]
    v_cur = p_ref[:, C_VA:C_QG]
    kcat = jnp.concatenate([kprev[...], k_cur], axis=0)
    vcat = jnp.concatenate([vprev[...], v_cur], axis=0)
    kroll = pltpu.roll(kcat, HEAD_DIM, 1)
    vroll = pltpu.roll(vcat, HEAD_DIM, 1)
    lo2 = lax.broadcasted_iota(jnp.int32, kcat.shape, 1) < HEAD_DIM

    def block_diag(same, swapped):
        g0 = jnp.concatenate([jnp.where(lo2, same, 0.0), jnp.where(lo2, 0.0, swapped)], axis=0)
        g1 = jnp.concatenate([jnp.where(lo2, swapped, 0.0), jnp.where(lo2, 0.0, same)], axis=0)
        return [g0.astype(BF16), g1.astype(BF16)]

    kbd = block_diag(kcat, kroll)
    vbd = block_diag(vcat, vroll)
    table = jnp.where(i == 0, 1, 0)
    lo1 = lax.broadcasted_iota(jnp.int32, (BLK, LANES), 1) < HEAD_DIM
    heads_per_kv = ATTN_HEADS // ATTN_KV_HEADS
    for j in range(ATTN_HEADS // 2):
        g = (2 * j) // heads_per_kv
        qp = p_ref[:, C_QA + j * LANES:C_QA + (j + 1) * LANES].astype(BF16)
        s = _nt_dot(qp, kbd[g]) * ATTN_SCALE + mb_scr[table, j]
        probs, inv = [], []
        for e in range(2):
            se = s[:, e * 2 * BLK:(e + 1) * 2 * BLK]
            sk = sink_ref[2 * j + e]
            m = jnp.maximum(jnp.max(se, axis=-1, keepdims=True), sk)
            pe = jnp.exp(se - m)
            den = jnp.sum(pe, axis=-1, keepdims=True) + jnp.exp(sk - m)
            probs.append(pe.astype(BF16))
            inv.append(1.0 / den)
        o = jnp.dot(jnp.concatenate(probs, axis=1), vbd[g], preferred_element_type=F32)
        o = o * jnp.where(lo1, inv[0], inv[1])
        omix_ref[:, j * LANES:(j + 1) * LANES] = o.astype(omix_ref.dtype)
    kprev[...] = k_cur
    vprev[...] = v_cur

    qg = p_ref[:, C_QG:C_KG]
    kg = p_ref[:, C_KG:C_VG]
    la = p_ref[:, C_LA:P_WIDTH]
    b = _sum3(jnp.dot(tri_ref[...], _split3(la), preferred_element_type=F32))
    rall = jnp.dot(sel_ref[...], _split3(b), preferred_element_type=F32)
    row = lax.broadcasted_iota(jnp.int32, (BLK, GLA_QK_WIDTH), 0)
    hl = lax.broadcasted_iota(jnp.int32, (BLK, GLA_QK_WIDTH), 1) // GLA_DK
    lev = lev_ref[...]

    def per_head(x):
        return [jnp.where(hl == h, x, 0.0).astype(BF16) for h in range(GLA_HEADS)]

    sc = [jnp.zeros((BLK, BLK), F32) for _ in range(GLA_HEADS)]
    for L in range(N_LEVELS + 1):
        if L == 0:
            qt, kt = qg, kg
        else:
            r_l = _sum3(rall[(L - 1) * BLK:L * BLK, :])
            upper = ((row >> (L - 1)) & 1) == 1
            qt = qg * jnp.exp(jnp.where(upper, b - r_l, NEG))
            kt = kg * jnp.exp(jnp.where(upper, NEG, r_l - b))
        ktb = kt.astype(BF16)
        at_level = lev == L
        for h, qh in enumerate(per_head(qt)):
            sc[h] = jnp.where(at_level, _nt_dot(qh, ktb), sc[h])

    b_last = b[BLK - 1:BLK, :]
    qe = per_head(qg * jnp.exp(b))
    kd = (kg * jnp.exp(b_last - b)).astype(BF16)
    st = st_scr[...]
    stb = st.astype(BF16)
    new_st = st * jnp.exp(b_last)
    for h in range(GLA_HEADS):
        vh = p_ref[:, C_VG + h * GLA_DV:C_VG + (h + 1) * GLA_DV].astype(BF16)
        o = jnp.dot(sc[h].astype(BF16), vh, preferred_element_type=F32) + _nt_dot(qe[h], stb)
        new_st = new_st + jnp.where(hl == h, _tn_dot(vh, kd), 0.0)
        og = o * lax.rsqrt(jnp.mean(o * o, axis=-1, keepdims=True) + EPS) * gn_ref[...]
        rg = p_ref[:, C_RG + h * GLA_DV:C_RG + (h + 1) * GLA_DV]
        gated = og * (rg * _sigmoid(rg))
        omix_ref[:, ATTN_WIDTH + h * GLA_DV:ATTN_WIDTH + (h + 1) * GLA_DV] = gated.astype(omix_ref.dtype)
    st_scr[...] = new_st
    st_ref[...] = new_st


def _prompt_mixer(p, relb, sinks, gn):
    t = p.shape[0]
    smem = pl.BlockSpec(memory_space=pltpu.SMEM)
    return pl.pallas_call(
        _prompt_mixer_kernel,
        grid=(t // BLK,),
        in_specs=[
            smem, smem,
            pl.BlockSpec((BLK, P_WIDTH), lambda i: (i, 0)),
            _const_spec((2, BLK, 2 * BLK)),
            _const_spec((BLK, BLK)),
            _const_spec((BLK, BLK)),
            _const_spec((N_LEVELS * BLK, BLK)),
            _const_spec((1, GLA_DV)),
        ],
        out_specs=[
            pl.BlockSpec((BLK, ATTN_WIDTH + GLA_WIDTH), lambda i: (i, 0)),
            pl.BlockSpec((GLA_DV, GLA_QK_WIDTH), lambda i: (0, 0)),
        ],
        out_shape=[
            jax.ShapeDtypeStruct((t, ATTN_WIDTH + GLA_WIDTH), BF16),
            jax.ShapeDtypeStruct((GLA_DV, GLA_QK_WIDTH), F32),
        ],
        scratch_shapes=[
            pltpu.VMEM((BLK, KV_WIDTH), F32),
            pltpu.VMEM((BLK, KV_WIDTH), F32),
            pltpu.VMEM((GLA_DV, GLA_QK_WIDTH), F32),
            pltpu.VMEM((2, ATTN_HEADS // 2, BLK, 4 * BLK), F32),
        ],
        compiler_params=pltpu.CompilerParams(
            dimension_semantics=("arbitrary",), vmem_limit_bytes=VMEM_LIMIT),
        name="prompt_mixer",
    )(relb, sinks, p, jnp.asarray(_BUCKET_PROMPT), jnp.asarray(_LEV), jnp.asarray(_TRI, BF16),
      jnp.asarray(_SEL, BF16), gn)


def _sample_mixer_kernel(ps_ref, pfull_ref, ck_ref, cv_ref, st_ref, relbt_ref, sink_ref, bucket_ref,
                         gn_ref, omix_ref, kwin_ref, vwin_ref, stout_ref, at_scr, kt_scr, qt_scr,
                         bias_scr):
    i = pl.program_id(0)

    @pl.when(i == 0)
    def _init():
        at_scr[...] = jnp.exp(pfull_ref[:, C_LA:P_WIDTH]).T
        kt_scr[...] = pfull_ref[:, C_KG:C_VG].T
        qt_scr[...] = pfull_ref[:, C_QG:C_KG].T
        bk = jnp.broadcast_to(bucket_ref[...], (ATTN_HEADS, WINDOW))
        acc = jnp.zeros((ATTN_HEADS, WINDOW), F32)
        for b in range(N_BUCKETS):
            acc = jnp.where(bk == b, relbt_ref[:, b:b + 1], acc)
        bias_scr[...] = acc

    lo = lax.broadcasted_iota(jnp.int32, (1, LANES), 1) < HEAD_DIM
    sub = lax.broadcasted_iota(jnp.int32, (ATTN_HEADS, LANES), 0)
    wrow = lax.broadcasted_iota(jnp.int32, (WINDOW, KV_WIDTH), 0)
    lane_n = lax.broadcasted_iota(jnp.int32, (GLA_QK_WIDTH, LANES), 1)
    heads_per_kv = ATTN_HEADS // ATTN_KV_HEADS
    sink = sink_ref[...]
    for j in range(SAMPLE_BLK):
        k_new = ps_ref[j:j + 1, C_KA:C_VA]
        v_new = ps_ref[j:j + 1, C_VA:C_QG]
        kbuf = jnp.where(wrow == WINDOW - 1, k_new, pltpu.roll(ck_ref[j], WINDOW - 1, 0))
        vbuf = jnp.where(wrow == WINDOW - 1, v_new, pltpu.roll(cv_ref[j], WINDOW - 1, 0))
        kwin_ref[j] = kbuf
        vwin_ref[j] = vbuf
        qexp = jnp.zeros((ATTN_HEADS, LANES), F32)
        for c in range(ATTN_HEADS // 2):
            chunk = ps_ref[j:j + 1, C_QA + c * LANES:C_QA + (c + 1) * LANES]
            swapped = pltpu.roll(chunk, HEAD_DIM, 1)
            if (2 * c) // heads_per_kv == 0:
                rows = (jnp.where(lo, chunk, 0.0), jnp.where(lo, swapped, 0.0))
            else:
                rows = (jnp.where(lo, 0.0, swapped), jnp.where(lo, 0.0, chunk))
            for e in range(2):
                qexp = jnp.where(sub == 2 * c + e, rows[e], qexp)
        s = _nt_dot(qexp.astype(BF16), kbuf.astype(BF16)) * ATTN_SCALE + bias_scr[...]
        m = jnp.maximum(jnp.max(s, axis=-1, keepdims=True), sink)
        pe = jnp.exp(s - m)
        den = jnp.sum(pe, axis=-1, keepdims=True) + jnp.exp(sink - m)
        o = jnp.dot(pe.astype(BF16), vbuf.astype(BF16), preferred_element_type=F32) / den
        oswap = pltpu.roll(o, HEAD_DIM, 1)
        for c in range(ATTN_HEADS // 2):
            if (2 * c) // heads_per_kv == 0:
                piece = jnp.where(lo, o[2 * c:2 * c + 1, :], oswap[2 * c + 1:2 * c + 2, :])
            else:
                piece = jnp.where(lo, oswap[2 * c:2 * c + 1, :], o[2 * c + 1:2 * c + 2, :])
            omix_ref[j:j + 1, c * LANES:(c + 1) * LANES] = piece
        hit = lane_n == i * SAMPLE_BLK + j
        a_col = jnp.sum(jnp.where(hit, at_scr[...], 0.0), axis=1, keepdims=True)
        k_col = jnp.sum(jnp.where(hit, kt_scr[...], 0.0), axis=1, keepdims=True)
        q_col = jnp.sum(jnp.where(hit, qt_scr[...], 0.0), axis=1, keepdims=True)
        for h in range(GLA_HEADS):
            rs = slice(h * GLA_DK, (h + 1) * GLA_DK)
            v_row = ps_ref[j:j + 1, C_VG + h * GLA_DV:C_VG + (h + 1) * GLA_DV]
            s_new = a_col[rs] * st_ref[j, h] + k_col[rs] * v_row
            stout_ref[j, h] = s_new
            og = jnp.sum(q_col[rs] * s_new, axis=0, keepdims=True)
            og = og * lax.rsqrt(jnp.mean(og * og, axis=-1, keepdims=True) + EPS) * gn_ref[...]
            rg = ps_ref[j:j + 1, C_RG + h * GLA_DV:C_RG + (h + 1) * GLA_DV]
            omix_ref[j:j + 1, ATTN_WIDTH + h * GLA_DV:ATTN_WIDTH + (h + 1) * GLA_DV] = (
                og * (rg * _sigmoid(rg)))


def _sample_mixer(ps, cache_k, cache_v, state, relbt, sinks_col, gn):
    nb = ps.shape[0]
    blk3 = lambda i: (i, 0, 0)
    blk4 = lambda i: (i, 0, 0, 0)
    return pl.pallas_call(
        _sample_mixer_kernel,
        grid=(nb // SAMPLE_BLK,),
        in_specs=[
            pl.BlockSpec((SAMPLE_BLK, P_WIDTH), lambda i: (i, 0)),
            _const_spec((nb, P_WIDTH)),
            pl.BlockSpec((SAMPLE_BLK, WINDOW, KV_WIDTH), blk3),
            pl.BlockSpec((SAMPLE_BLK, WINDOW, KV_WIDTH), blk3),
            pl.BlockSpec((SAMPLE_BLK, GLA_HEADS, GLA_DK, GLA_DV), blk4),
            _const_spec((ATTN_HEADS, N_BUCKETS)),
            _const_spec((ATTN_HEADS, 1)),
            _const_spec((1, WINDOW)),
            _const_spec((1, GLA_DV)),
        ],
        out_specs=[
            pl.BlockSpec((SAMPLE_BLK, ATTN_WIDTH + GLA_WIDTH), lambda i: (i, 0)),
            pl.BlockSpec((SAMPLE_BLK, WINDOW, KV_WIDTH), blk3),
            pl.BlockSpec((SAMPLE_BLK, WINDOW, KV_WIDTH), blk3),
            pl.BlockSpec((SAMPLE_BLK, GLA_HEADS, GLA_DK, GLA_DV), blk4),
        ],
        out_shape=[
            jax.ShapeDtypeStruct((nb, ATTN_WIDTH + GLA_WIDTH), F32),
            jax.ShapeDtypeStruct((nb, WINDOW, KV_WIDTH), F32),
            jax.ShapeDtypeStruct((nb, WINDOW, KV_WIDTH), F32),
            jax.ShapeDtypeStruct((nb, GLA_HEADS, GLA_DK, GLA_DV), F32),
        ],
        scratch_shapes=[
            pltpu.VMEM((GLA_QK_WIDTH, nb), F32),
            pltpu.VMEM((GLA_QK_WIDTH, nb), F32),
            pltpu.VMEM((GLA_QK_WIDTH, nb), F32),
            pltpu.VMEM((ATTN_HEADS, WINDOW), F32),
        ],
        compiler_params=pltpu.CompilerParams(
            dimension_semantics=("arbitrary",), vmem_limit_bytes=VMEM_LIMIT),
        name="sample_mixer",
    )(ps, ps, cache_k, cache_v, state, relbt, sinks_col, jnp.asarray(_BUCKET_SAMPLE), gn)


def _finish_kernel(x_ref, mix_ref, wo_ref, g_ref, wg_ref, wu_ref, wd_ref, y_ref, *, ff_chunks):
    h = x_ref[...] + jnp.dot(mix_ref[...].astype(BF16), wo_ref[...], preferred_element_type=F32)
    r = lax.rsqrt(jnp.mean(h * h, axis=-1, keepdims=True) + EPS)
    z = (h * r * g_ref[...]).astype(BF16)
    d_ff = wg_ref.shape[1]
    cw = d_ff // ff_chunks
    acc = h
    for c in range(ff_chunks):
        gate = jnp.dot(z, wg_ref[:, c * cw:(c + 1) * cw], preferred_element_type=F32)
        up = jnp.dot(z, wu_ref[:, c * cw:(c + 1) * cw], preferred_element_type=F32)
        act = ((gate * _sigmoid(gate)) * up).astype(BF16)
        acc = acc + jnp.dot(act, wd_ref[c * cw:(c + 1) * cw, :], preferred_element_type=F32)
    y_ref[...] = acc


def _finish(x, mix, wo, g_ffn, wg, wu, wd, rows):
    t = x.shape[0]
    d_ff = wg.shape[1]
    ff_chunks = 2 if (d_ff // 2) % LANES == 0 else 1
    return pl.pallas_call(
        functools.partial(_finish_kernel, ff_chunks=ff_chunks),
        grid=(t // rows,),
        in_specs=[
            pl.BlockSpec((rows, D_MODEL), lambda i: (i, 0)),
            pl.BlockSpec((rows, ATTN_WIDTH + GLA_WIDTH), lambda i: (i, 0)),
            _const_spec((ATTN_WIDTH + GLA_WIDTH, D_MODEL)),
            _const_spec((1, D_MODEL)),
            _const_spec((D_MODEL, d_ff)),
            _const_spec((D_MODEL, d_ff)),
            _const_spec((d_ff, D_MODEL)),
        ],
        out_specs=pl.BlockSpec((rows, D_MODEL), lambda i: (i, 0)),
        out_shape=jax.ShapeDtypeStruct((t, D_MODEL), F32),
        compiler_params=pltpu.CompilerParams(
            dimension_semantics=("arbitrary",), vmem_limit_bytes=VMEM_LIMIT),
        name="finish",
    )(x, mix, wo, g_ffn, wg, wu, wd)


PROMPT_ROWS = 512


def kernel(x_prompt, x_sample, cache_k, cache_v, state_gla, attn_norm_g, w_in, q_norm_g, k_norm_g,
           attn_sinks, rel_bias, w_gla_gate2, b_gla_gate, gla_norm_g, w_o, ffn_norm_g, w_gate, w_up,
           w_down):
    depth = w_in.shape[0]
    batch, seq, _ = x_prompt.shape
    dec_batch, dec_seq, _ = x_sample.shape
    wb = cache_k.shape[2]
    assert batch == 1 and dec_seq == 1 and wb == WINDOW and seq % PROMPT_ROWS == 0
    assert dec_batch % SAMPLE_BLK == 0 and dec_batch % LANES == 0
    assert rel_bias.shape == (N_BUCKETS, ATTN_HEADS)

    xp = x_prompt.reshape(seq, D_MODEL)
    xs = x_sample.reshape(dec_batch, D_MODEL)
    relb_flat = rel_bias.reshape(-1)
    relb_t = rel_bias.T
    outs = ([], [], [], [], [], [])
    for l in range(depth):
        w_in_p = jnp.concatenate(
            [w_in[l, :, :MAIN_WIDTH],
             jnp.pad(w_in[l, :, MAIN_WIDTH:], ((0, 0), (0, RANK_PAD - GLA_RANK)))], axis=1).astype(BF16)
        w2p = jnp.pad(w_gla_gate2[l], ((0, RANK_PAD - GLA_RANK), (0, 0))).astype(BF16)
        proj_w = (attn_norm_g[l][None, :], w_in_p, jnp.tile(q_norm_g[l], ATTN_HEADS)[None, :],
                  jnp.tile(k_norm_g[l], ATTN_KV_HEADS)[None, :], w2p, b_gla_gate[l][None, :])
        fin_w = (w_o[l].astype(BF16), ffn_norm_g[l][None, :], w_gate[l].astype(BF16),
                 w_up[l].astype(BF16), w_down[l].astype(BF16))
        gn = gla_norm_g[l][None, :]

        pp = _project(xp, *proj_w, rows=PROMPT_ROWS)
        mix_p, st_p = _prompt_mixer(pp, relb_flat, attn_sinks[l], gn)
        xp = _finish(xp, mix_p, *fin_w, rows=PROMPT_ROWS)
        outs[0].append(pp[seq - wb:, C_KA:C_VA].reshape(batch, wb, ATTN_KV_HEADS, HEAD_DIM))
        outs[1].append(pp[seq - wb:, C_VA:C_QG].reshape(batch, wb, ATTN_KV_HEADS, HEAD_DIM))
        outs[2].append(st_p.T.reshape(batch, GLA_HEADS, GLA_DK, GLA_DV).astype(state_gla.dtype))

        ps = _project(xs, *proj_w, rows=dec_batch)
        mix_s, kwin, vwin, st_s = _sample_mixer(
            ps, cache_k[l].reshape(dec_batch, wb, KV_WIDTH), cache_v[l].reshape(dec_batch, wb, KV_WIDTH),
            state_gla[l].astype(F32), relb_t, attn_sinks[l][:, None], gn)
        xs = _finish(xs, mix_s, *fin_w, rows=dec_batch)
        outs[3].append(kwin.reshape(dec_batch, wb, ATTN_KV_HEADS, HEAD_DIM))
        outs[4].append(vwin.reshape(dec_batch, wb, ATTN_KV_HEADS, HEAD_DIM))
        outs[5].append(st_s.astype(state_gla.dtype))

    y_prompt = xp.reshape(batch, seq, D_MODEL)
    y_sample = xs.reshape(dec_batch, dec_seq, D_MODEL)
    return (y_prompt, y_sample) + tuple(jnp.stack(o) for o in outs)
```

```python
import functools
import math

import numpy as np
import jax
import jax.numpy as jnp
from jax import lax
from jax.experimental import pallas as pl
from jax.experimental.pallas import tpu as pltpu

F32 = jnp.float32
BF16 = jnp.bfloat16

D_MODEL = 1024
HEAD_DIM = 64
ATTN_HEADS = 8
ATTN_KV_HEADS = 2
WINDOW = 128
N_BUCKETS = 32
MAX_DISTANCE = 128
GLA_HEADS = 4
GLA_DK = 64
GLA_DV = 128
GLA_RANK = 16
GLA_TAU = 16.0
EPS = 1e-6
ATTN_WIDTH = ATTN_HEADS * HEAD_DIM
KV_WIDTH = ATTN_KV_HEADS * HEAD_DIM
GLA_QK_WIDTH = GLA_HEADS * GLA_DK
GLA_WIDTH = GLA_HEADS * GLA_DV
MIX_WIDTH = ATTN_WIDTH + GLA_WIDTH
MAIN_WIDTH = ATTN_WIDTH + 2 * KV_WIDTH + 2 * GLA_QK_WIDTH + 2 * GLA_WIDTH
LANES = 128
SUBLANES = 8
RANK_PAD = LANES
IN_PAD_WIDTH = MAIN_WIDTH + RANK_PAD

C_QA = 0
C_KA = C_QA + ATTN_WIDTH
C_VA = C_KA + KV_WIDTH
C_QG = C_VA + KV_WIDTH
C_KG = C_QG + GLA_QK_WIDTH
C_VG = C_KG + GLA_QK_WIDTH
C_RG = C_VG + GLA_WIDTH
C_LA = C_RG + GLA_WIDTH
P_WIDTH = C_LA + GLA_QK_WIDTH

W_QA = 0
W_KV = W_QA + ATTN_WIDTH
W_QG = W_KV + 2 * KV_WIDTH + RANK_PAD
W_VG = W_QG + 2 * GLA_QK_WIDTH
W_RG = W_VG + GLA_WIDTH

BLK = 128
N_LEVELS = 7
NEG = -1e30
ATTN_SCALE = HEAD_DIM ** -0.5
SAMPLE_BLK = 8
VMEM_LIMIT = 56 * 1024 * 1024


def _t5_bucket_np(dist):
    n = np.maximum(dist, 0)
    max_exact = N_BUCKETS // 2
    nf = np.maximum(n, 1).astype(np.float64)
    large = max_exact + (np.log(nf / max_exact) / math.log(MAX_DISTANCE / max_exact)
                         * (N_BUCKETS - max_exact)).astype(np.int32)
    large = np.minimum(large, N_BUCKETS - 1)
    return np.where(n < max_exact, n, large).astype(np.int32)


def _prompt_bucket_tables():
    i = np.arange(BLK)[:, None]
    j = np.arange(2 * BLK)[None, :]
    dist = BLK + i - j
    band = (dist >= 0) & (dist < WINDOW)
    bucket = _t5_bucket_np(dist)
    t0 = np.where(band, bucket, -1)
    t1 = np.where(band & (j >= BLK), bucket, -1)
    return np.stack([t0, t1]).astype(np.int32)


def _level_tables():
    t = np.arange(BLK)[:, None]
    s = np.arange(BLK)[None, :]
    x = t ^ s
    lev = np.where(x > 0, np.floor(np.log2(np.maximum(x, 1))).astype(np.int32) + 1, 0)
    lev = np.where(s > t, -1, lev).astype(np.int32)
    tri = (s <= t).astype(np.float32)
    return lev, np.concatenate([tri, tri, tri], axis=1)


_BUCKET_PROMPT = _prompt_bucket_tables()
_LEV, _TRI3 = _level_tables()
_BUCKET_SAMPLE = _t5_bucket_np((WINDOW - 1) - np.arange(WINDOW))[None, :].astype(np.int32)


def _nt_dot(a, b):
    return lax.dot_general(a, b, (((1,), (1,)), ((), ())), preferred_element_type=F32)


def _tn_dot(a, b):
    return lax.dot_general(a, b, (((0,), (0,)), ((), ())), preferred_element_type=F32)


def _head_mean_sq(x):
    lo = lax.broadcasted_iota(jnp.int32, (x.shape[0], LANES), 1) < HEAD_DIM
    outs = []
    for c in range(x.shape[1] // LANES):
        y = x[:, c * LANES:(c + 1) * LANES]
        y = y * y
        s_lo = jnp.sum(jnp.where(lo, y, 0.0), axis=-1, keepdims=True)
        s_hi = jnp.sum(jnp.where(lo, 0.0, y), axis=-1, keepdims=True)
        outs.append(jnp.where(lo, s_lo, s_hi) * (1.0 / HEAD_DIM))
    return outs[0] if len(outs) == 1 else jnp.concatenate(outs, axis=1)


def _sigmoid(x):
    return 1.0 / (1.0 + jnp.exp(-x))


def _split3_rows(x):
    hi = x.astype(BF16)
    r1 = x - hi.astype(F32)
    mid = r1.astype(BF16)
    lo = (r1 - mid.astype(F32)).astype(BF16)
    return jnp.concatenate([hi, mid, lo], axis=0)


def _proj_kernel(x_ref, g_ref, w_ref, qn_ref, kn_ref, w2_ref, b2_ref, tri_ref, out_ref, *,
                 block_cumsum):
    x = x_ref[...]
    r = lax.rsqrt(jnp.mean(x * x, axis=-1, keepdims=True) + EPS)
    n = (x * r * g_ref[...]).astype(BF16)

    def seg(c0, c1):
        return jnp.dot(n, w_ref[:, c0:c1], preferred_element_type=F32)

    q = seg(W_QA, W_KV)
    out_ref[:, C_QA:C_KA] = q * lax.rsqrt(_head_mean_sq(q) + EPS) * qn_ref[...] * ATTN_SCALE
    kvl = seg(W_KV, W_QG)
    k = kvl[:, :KV_WIDTH]
    out_ref[:, C_KA:C_VA] = k * lax.rsqrt(_head_mean_sq(k) + EPS) * kn_ref[...]
    out_ref[:, C_VA:C_QG] = kvl[:, KV_WIDTH:2 * KV_WIDTH]
    lr = kvl[:, 2 * KV_WIDTH:].astype(BF16)
    qk_g = seg(W_QG, W_VG)
    out_ref[:, C_QG:C_KG] = qk_g[:, :GLA_QK_WIDTH] * (GLA_DK ** -0.5)
    out_ref[:, C_KG:C_VG] = qk_g[:, GLA_QK_WIDTH:]
    out_ref[:, C_VG:C_RG] = seg(W_VG, W_RG)
    out_ref[:, C_RG:C_LA] = seg(W_RG, IN_PAD_WIDTH)
    z = jnp.dot(lr, w2_ref[...], preferred_element_type=F32) + b2_ref[...]
    log_a = (jnp.minimum(z, 0.0) - jnp.log1p(jnp.exp(-jnp.abs(z)))) / GLA_TAU
    if block_cumsum:
        for blk in range(x.shape[0] // BLK):
            rows = slice(blk * BLK, (blk + 1) * BLK)
            out_ref[rows, C_LA:P_WIDTH] = jnp.dot(
                tri_ref[...], _split3_rows(log_a[rows]), preferred_element_type=F32)
    else:
        out_ref[:, C_LA:P_WIDTH] = log_a


def _const_spec(shape):
    nd = len(shape)
    return pl.BlockSpec(shape, lambda i: (0,) * nd, pipeline_mode=pl.Buffered(1))


def _project(x, g_attn, w_in_p, qn, kn, w2p, b2, rows, block_cumsum):
    t = x.shape[0]
    return pl.pallas_call(
        functools.partial(_proj_kernel, block_cumsum=block_cumsum),
        grid=(t // rows,),
        in_specs=[
            pl.BlockSpec((rows, D_MODEL), lambda i: (i, 0)),
            _const_spec((1, D_MODEL)),
            _const_spec((D_MODEL, IN_PAD_WIDTH)),
            _const_spec((1, ATTN_WIDTH)),
            _const_spec((1, KV_WIDTH)),
            _const_spec((RANK_PAD, GLA_QK_WIDTH)),
            _const_spec((1, GLA_QK_WIDTH)),
            _const_spec((BLK, 3 * BLK)),
        ],
        out_specs=pl.BlockSpec((rows, P_WIDTH), lambda i: (i, 0)),
        out_shape=jax.ShapeDtypeStruct((t, P_WIDTH), F32),
        compiler_params=pltpu.CompilerParams(
            dimension_semantics=("arbitrary",), vmem_limit_bytes=VMEM_LIMIT),
        name="proj",
    )(x, g_attn, w_in_p, qn, kn, w2p, b2, jnp.asarray(_TRI3, BF16))


def _boundary_rows(b_ref, c0, b, row, level):
    m = 1 << (level - 1)
    if 2 * m >= SUBLANES:
        pieces = [jnp.broadcast_to(b_ref[g * 2 * m + m - 1:g * 2 * m + m, c0:c0 + LANES], (2 * m, LANES))
                  for g in range(BLK // (2 * m))]
        return pieces[0] if len(pieces) == 1 else jnp.concatenate(pieces, axis=0)
    pos = row & (2 * m - 1)
    out = b
    for p in range(2 * m):
        shift = (m - 1) - p
        if shift != 0:
            out = jnp.where(pos == p, pltpu.roll(b, (-shift) % BLK, 0), out)
    return out


def _prompt_mixer_kernel(relb_ref, sink_ref, p_ref, bucket_ref, lev_ref, gn_ref,
                         omix_ref, st_ref, kprev, vprev, st_scr, mb_scr):
    i = pl.program_id(0)

    @pl.when(i == 0)
    def _init():
        kprev[...] = jnp.zeros_like(kprev)
        vprev[...] = jnp.zeros_like(vprev)
        st_scr[...] = jnp.zeros_like(st_scr)
        bk = bucket_ref[0]
        acc = [jnp.zeros(bk.shape, F32) for _ in range(ATTN_HEADS)]
        for b in range(N_BUCKETS):
            hit = bk == b
            for h in range(ATTN_HEADS):
                acc[h] = jnp.where(hit, relb_ref[b * ATTN_HEADS + h], acc[h])
        for tb in range(2):
            masked = bucket_ref[tb] < 0
            for h in range(ATTN_HEADS):
                mb_scr[tb, h // 2, :, (h % 2) * 2 * BLK:(h % 2 + 1) * 2 * BLK] = (
                    jnp.where(masked, NEG, acc[h]))

    lo1 = lax.broadcasted_iota(jnp.int32, (BLK, LANES), 1) < HEAD_DIM
    lo_bf = jnp.where(lo1, 1.0, 0.0).astype(BF16)
    hi_bf = jnp.where(lo1, 0.0, 1.0).astype(BF16)

    k_cur = p_ref[:, C_KA:C_VA]
    v_cur = p_ref[:, C_VA:C_QG]
    kcat = jnp.concatenate([kprev[...], k_cur], axis=0)
    vcat = jnp.concatenate([vprev[...], v_cur], axis=0)
    kroll = pltpu.roll(kcat, HEAD_DIM, 1)
    vroll = pltpu.roll(vcat, HEAD_DIM, 1)
    lo2 = lax.broadcasted_iota(jnp.int32, kcat.shape, 1) < HEAD_DIM

    def block_diag(same, swapped):
        g0 = jnp.concatenate([jnp.where(lo2, same, 0.0), jnp.where(lo2, 0.0, swapped)], axis=0)
        g1 = jnp.concatenate([jnp.where(lo2, swapped, 0.0), jnp.where(lo2, 0.0, same)], axis=0)
        return [g0.astype(BF16), g1.astype(BF16)]

    kbd = block_diag(kcat, kroll)
    vbd = block_diag(vcat, vroll)
    table = jnp.where(i == 0, 1, 0)
    heads_per_kv = ATTN_HEADS // ATTN_KV_HEADS
    for j in range(ATTN_HEADS // 2):
        g = (2 * j) // heads_per_kv
        qp = p_ref[:, C_QA + j * LANES:C_QA + (j + 1) * LANES].astype(BF16)
        s = _nt_dot(qp, kbd[g]) + mb_scr[table, j]
        probs, inv = [], []
        for e in range(2):
            se = s[:, e * 2 * BLK:(e + 1) * 2 * BLK]
            sk = sink_ref[2 * j + e]
            m = jnp.maximum(jnp.max(se, axis=-1, keepdims=True), sk)
            pe = jnp.exp(se - m)
            den = jnp.sum(pe, axis=-1, keepdims=True) + jnp.exp(sk - m)
            probs.append(pe.astype(BF16))
            inv.append(1.0 / den)
        o = jnp.dot(jnp.concatenate(probs, axis=1), vbd[g], preferred_element_type=F32)
        o = o * jnp.where(lo1, inv[0], inv[1])
        omix_ref[:, j * LANES:(j + 1) * LANES] = o.astype(omix_ref.dtype)
    kprev[...] = k_cur
    vprev[...] = v_cur

    row = lax.broadcasted_iota(jnp.int32, (BLK, LANES), 0)
    lev = lev_ref[...]
    at_level = [lev == level for level in range(N_LEVELS + 1)]
    zero_blk = jnp.zeros((BLK, LANES), BF16)
    for c in range(GLA_HEADS // 2):
        c0 = c * LANES
        qc = p_ref[:, C_QG + c0:C_QG + c0 + LANES]
        kc = p_ref[:, C_KG + c0:C_KG + c0 + LANES]
        bc = p_ref[:, C_LA + c0:C_LA + c0 + LANES]

        def pair_scores(qtb, ktb):
            rhs = jnp.concatenate([ktb * lo_bf, ktb * hi_bf], axis=0)
            return _nt_dot(qtb, rhs)

        s0 = pair_scores(qc.astype(BF16), kc.astype(BF16))
        sc = [jnp.where(at_level[0], s0[:, e * BLK:(e + 1) * BLK], 0.0) for e in range(2)]
        for level in range(1, N_LEVELS + 1):
            m = 1 << (level - 1)
            if m >= SUBLANES:
                qs, ks = [], []
                zeros = jnp.zeros((m, LANES), BF16)
                for g in range(BLK // (2 * m)):
                    lo_r = slice(g * 2 * m, g * 2 * m + m)
                    up_r = slice(g * 2 * m + m, (g + 1) * 2 * m)
                    rb = jnp.broadcast_to(
                        p_ref[g * 2 * m + m - 1:g * 2 * m + m, C_LA + c0:C_LA + c0 + LANES], (m, LANES))
                    qs += [zeros, (qc[up_r] * jnp.exp(bc[up_r] - rb)).astype(BF16)]
                    ks += [(kc[lo_r] * jnp.exp(rb - bc[lo_r])).astype(BF16), zeros]
                qtb = jnp.concatenate(qs, axis=0)
                ktb = jnp.concatenate(ks, axis=0)
            else:
                d = bc - _boundary_rows(p_ref, C_LA + c0, bc, row, level)
                upper = ((row >> (level - 1)) & 1) == 1
                qtb = (qc * jnp.exp(jnp.where(upper, d, NEG))).astype(BF16)
                ktb = (kc * jnp.exp(jnp.where(upper, NEG, -d))).astype(BF16)
            sl = pair_scores(qtb, ktb)
            sc = [jnp.where(at_level[level], sl[:, e * BLK:(e + 1) * BLK], sc[e]) for e in range(2)]
        sc = jnp.concatenate(sc, axis=1)

        b_last = bc[BLK - 1:BLK, :]
        v0 = p_ref[:, C_VG + 2 * c0:C_VG + 2 * c0 + LANES].astype(BF16)
        v1 = p_ref[:, C_VG + 2 * c0 + LANES:C_VG + 2 * c0 + 2 * LANES].astype(BF16)
        v_bd = jnp.concatenate([jnp.concatenate([v0, zero_blk], axis=1),
                                jnp.concatenate([zero_blk, v1], axis=1)], axis=0)
        st_c = st_scr[:, c0:c0 + LANES]
        stb = st_c.astype(BF16)
        st_rhs = jnp.concatenate([stb * lo_bf, stb * hi_bf], axis=0)
        o = (jnp.dot(sc.astype(BF16), v_bd, preferred_element_type=F32)
             + _nt_dot((qc * jnp.exp(bc)).astype(BF16), st_rhs))
        kd = (kc * jnp.exp(b_last - bc)).astype(BF16)
        upd = _tn_dot(jnp.concatenate([v0, v1], axis=1), kd)
        new_st = st_c * jnp.exp(b_last) + jnp.where(lo1, upd[:BLK], upd[BLK:])
        st_scr[:, c0:c0 + LANES] = new_st
        st_ref[:, c0:c0 + LANES] = new_st
        for e in range(2):
            h = 2 * c + e
            oh = o[:, e * LANES:(e + 1) * LANES]
            og = oh * lax.rsqrt(jnp.mean(oh * oh, axis=-1, keepdims=True) + EPS) * gn_ref[...]
            rg = p_ref[:, C_RG + h * GLA_DV:C_RG + (h + 1) * GLA_DV]
            gated = og * (rg * _sigmoid(rg))
            omix_ref[:, ATTN_WIDTH + h * GLA_DV:ATTN_WIDTH + (h + 1) * GLA_DV] = (
                gated.astype(omix_ref.dtype))


def _prompt_mixer(p, relb, sinks, gn):
    t = p.shape[0]
    smem = pl.BlockSpec(memory_space=pltpu.SMEM)
    return pl.pallas_call(
        _prompt_mixer_kernel,
        grid=(t // BLK,),
        in_specs=[
            smem, smem,
            pl.BlockSpec((BLK, P_WIDTH), lambda i: (i, 0)),
            _const_spec((2, BLK, 2 * BLK)),
            _const_spec((BLK, BLK)),
            _const_spec((1, GLA_DV)),
        ],
        out_specs=[
            pl.BlockSpec((BLK, MIX_WIDTH), lambda i: (i, 0)),
            pl.BlockSpec((GLA_DV, GLA_QK_WIDTH), lambda i: (0, 0)),
        ],
        out_shape=[
            jax.ShapeDtypeStruct((t, MIX_WIDTH), BF16),
            jax.ShapeDtypeStruct((GLA_DV, GLA_QK_WIDTH), F32),
        ],
        scratch_shapes=[
            pltpu.VMEM((BLK, KV_WIDTH), F32),
            pltpu.VMEM((BLK, KV_WIDTH), F32),
            pltpu.VMEM((GLA_DV, GLA_QK_WIDTH), F32),
            pltpu.VMEM((2, ATTN_HEADS // 2, BLK, 4 * BLK), F32),
        ],
        compiler_params=pltpu.CompilerParams(
            dimension_semantics=("arbitrary",), vmem_limit_bytes=VMEM_LIMIT),
        name="prompt_mixer",
    )(relb, sinks, p, jnp.asarray(_BUCKET_PROMPT), jnp.asarray(_LEV), gn)


def _sample_mixer_kernel(ps_ref, pfull_ref, ck_ref, cv_ref, st_ref, relbt_ref, sink_ref, bucket_ref,
                         gn_ref, omix_ref, kwin_ref, vwin_ref, stout_ref, at_scr, kt_scr, qt_scr,
                         bias_scr):
    i = pl.program_id(0)

    @pl.when(i == 0)
    def _init():
        at_scr[...] = jnp.exp(pfull_ref[:, C_LA:P_WIDTH]).T
        kt_scr[...] = pfull_ref[:, C_KG:C_VG].T
        qt_scr[...] = pfull_ref[:, C_QG:C_KG].T
        bk = jnp.broadcast_to(bucket_ref[...], (ATTN_HEADS, WINDOW))
        acc = jnp.zeros((ATTN_HEADS, WINDOW), F32)
        for b in range(N_BUCKETS):
            acc = jnp.where(bk == b, relbt_ref[:, b:b + 1], acc)
        bias_scr[...] = acc

    lo = lax.broadcasted_iota(jnp.int32, (1, LANES), 1) < HEAD_DIM
    sub = lax.broadcasted_iota(jnp.int32, (ATTN_HEADS, LANES), 0)
    wrow = lax.broadcasted_iota(jnp.int32, (WINDOW, KV_WIDTH), 0)
    lane_n = lax.broadcasted_iota(jnp.int32, (GLA_QK_WIDTH, LANES), 1)
    heads_per_kv = ATTN_HEADS // ATTN_KV_HEADS
    sink = sink_ref[...]
    for j in range(SAMPLE_BLK):
        k_new = ps_ref[j:j + 1, C_KA:C_VA]
        v_new = ps_ref[j:j + 1, C_VA:C_QG]
        kbuf = jnp.where(wrow == WINDOW - 1, k_new, pltpu.roll(ck_ref[j], WINDOW - 1, 0))
        vbuf = jnp.where(wrow == WINDOW - 1, v_new, pltpu.roll(cv_ref[j], WINDOW - 1, 0))
        kwin_ref[j] = kbuf
        vwin_ref[j] = vbuf
        qexp = jnp.zeros((ATTN_HEADS, LANES), F32)
        for c in range(ATTN_HEADS // 2):
            chunk = ps_ref[j:j + 1, C_QA + c * LANES:C_QA + (c + 1) * LANES]
            swapped = pltpu.roll(chunk, HEAD_DIM, 1)
            if (2 * c) // heads_per_kv == 0:
                rows = (jnp.where(lo, chunk, 0.0), jnp.where(lo, swapped, 0.0))
            else:
                rows = (jnp.where(lo, 0.0, swapped), jnp.where(lo, 0.0, chunk))
            for e in range(2):
                qexp = jnp.where(sub == 2 * c + e, rows[e], qexp)
        s = _nt_dot(qexp.astype(BF16), kbuf.astype(BF16)) + bias_scr[...]
        m = jnp.maximum(jnp.max(s, axis=-1, keepdims=True), sink)
        pe = jnp.exp(s - m)
        den = jnp.sum(pe, axis=-1, keepdims=True) + jnp.exp(sink - m)
        o = jnp.dot(pe.astype(BF16), vbuf.astype(BF16), preferred_element_type=F32) / den
        oswap = pltpu.roll(o, HEAD_DIM, 1)
        for c in range(ATTN_HEADS // 2):
            if (2 * c) // heads_per_kv == 0:
                piece = jnp.where(lo, o[2 * c:2 * c + 1, :], oswap[2 * c + 1:2 * c + 2, :])
            else:
                piece = jnp.where(lo, oswap[2 * c:2 * c + 1, :], o[2 * c + 1:2 * c + 2, :])
            omix_ref[j:j + 1, c * LANES:(c + 1) * LANES] = piece
        hit = lane_n == i * SAMPLE_BLK + j
        a_col = jnp.sum(jnp.where(hit, at_scr[...], 0.0), axis=1, keepdims=True)
        k_col = jnp.sum(jnp.where(hit, kt_scr[...], 0.0), axis=1, keepdims=True)
        q_col = jnp.sum(jnp.where(hit, qt_scr[...], 0.0), axis=1, keepdims=True)
        for h in range(GLA_HEADS):
            rs = slice(h * GLA_DK, (h + 1) * GLA_DK)
            v_row = ps_ref[j:j + 1, C_VG + h * GLA_DV:C_VG + (h + 1) * GLA_DV]
            s_new = a_col[rs] * st_ref[j, h] + k_col[rs] * v_row
            stout_ref[j, h] = s_new
            og = jnp.sum(q_col[rs] * s_new, axis=0, keepdims=True)
            og = og * lax.rsqrt(jnp.mean(og * og, axis=-1, keepdims=True) + EPS) * gn_ref[...]
            rg = ps_ref[j:j + 1, C_RG + h * GLA_DV:C_RG + (h + 1) * GLA_DV]
            omix_ref[j:j + 1, ATTN_WIDTH + h * GLA_DV:ATTN_WIDTH + (h + 1) * GLA_DV] = (
                og * (rg * _sigmoid(rg)))


def _sample_mixer(ps, cache_k, cache_v, state, relbt, sinks_col, gn):
    nb = ps.shape[0]
    blk3 = lambda i: (i, 0, 0)
    blk4 = lambda i: (i, 0, 0, 0)
    return pl.pallas_call(
        _sample_mixer_kernel,
        grid=(nb // SAMPLE_BLK,),
        in_specs=[
            pl.BlockSpec((SAMPLE_BLK, P_WIDTH), lambda i: (i, 0)),
            _const_spec((nb, P_WIDTH)),
            pl.BlockSpec((SAMPLE_BLK, WINDOW, KV_WIDTH), blk3),
            pl.BlockSpec((SAMPLE_BLK, WINDOW, KV_WIDTH), blk3),
            pl.BlockSpec((SAMPLE_BLK, GLA_HEADS, GLA_DK, GLA_DV), blk4),
            _const_spec((ATTN_HEADS, N_BUCKETS)),
            _const_spec((ATTN_HEADS, 1)),
            _const_spec((1, WINDOW)),
            _const_spec((1, GLA_DV)),
        ],
        out_specs=[
            pl.BlockSpec((SAMPLE_BLK, MIX_WIDTH), lambda i: (i, 0)),
            pl.BlockSpec((SAMPLE_BLK, WINDOW, KV_WIDTH), blk3),
            pl.BlockSpec((SAMPLE_BLK, WINDOW, KV_WIDTH), blk3),
            pl.BlockSpec((SAMPLE_BLK, GLA_HEADS, GLA_DK, GLA_DV), blk4),
        ],
        out_shape=[
            jax.ShapeDtypeStruct((nb, MIX_WIDTH), F32),
            jax.ShapeDtypeStruct((nb, WINDOW, KV_WIDTH), F32),
            jax.ShapeDtypeStruct((nb, WINDOW, KV_WIDTH), F32),
            jax.ShapeDtypeStruct((nb, GLA_HEADS, GLA_DK, GLA_DV), F32),
        ],
        scratch_shapes=[
            pltpu.VMEM((GLA_QK_WIDTH, nb), F32),
            pltpu.VMEM((GLA_QK_WIDTH, nb), F32),
            pltpu.VMEM((GLA_QK_WIDTH, nb), F32),
            pltpu.VMEM((ATTN_HEADS, WINDOW), F32),
        ],
        compiler_params=pltpu.CompilerParams(
            dimension_semantics=("arbitrary",), vmem_limit_bytes=VMEM_LIMIT),
        name="sample_mixer",
    )(ps, ps, cache_k, cache_v, state, relbt, sinks_col, jnp.asarray(_BUCKET_SAMPLE), gn)


def _finish_kernel(x_ref, mix_ref, wo_ref, g_ref, wg_ref, wu_ref, wd_ref, y_ref, *, ff_chunks):
    h = x_ref[...] + jnp.dot(mix_ref[...].astype(BF16), wo_ref[...], preferred_element_type=F32)
    r = lax.rsqrt(jnp.mean(h * h, axis=-1, keepdims=True) + EPS)
    z = (h * r * g_ref[...]).astype(BF16)
    d_ff = wg_ref.shape[1]
    cw = d_ff // ff_chunks
    acc = h
    for c in range(ff_chunks):
        gate = jnp.dot(z, wg_ref[:, c * cw:(c + 1) * cw], preferred_element_type=F32)
        up = jnp.dot(z, wu_ref[:, c * cw:(c + 1) * cw], preferred_element_type=F32)
        act = ((gate * _sigmoid(gate)) * up).astype(BF16)
        acc = acc + jnp.dot(act, wd_ref[c * cw:(c + 1) * cw, :], preferred_element_type=F32)
    y_ref[...] = acc


def _finish(x, mix, wo, g_ffn, wg, wu, wd, rows):
    t = x.shape[0]
    d_ff = wg.shape[1]
    ff_chunks = 2 if (d_ff // 2) % LANES == 0 else 1
    return pl.pallas_call(
        functools.partial(_finish_kernel, ff_chunks=ff_chunks),
        grid=(t // rows,),
        in_specs=[
            pl.BlockSpec((rows, D_MODEL), lambda i: (i, 0)),
            pl.BlockSpec((rows, MIX_WIDTH), lambda i: (i, 0)),
            _const_spec((MIX_WIDTH, D_MODEL)),
            _const_spec((1, D_MODEL)),
            _const_spec((D_MODEL, d_ff)),
            _const_spec((D_MODEL, d_ff)),
            _const_spec((d_ff, D_MODEL)),
        ],
        out_specs=pl.BlockSpec((rows, D_MODEL), lambda i: (i, 0)),
        out_shape=jax.ShapeDtypeStruct((t, D_MODEL), F32),
        compiler_params=pltpu.CompilerParams(
            dimension_semantics=("arbitrary",), vmem_limit_bytes=VMEM_LIMIT),
        name="finish",
    )(x, mix, wo, g_ffn, wg, wu, wd)


PROMPT_ROWS = 512


def kernel(x_prompt, x_sample, cache_k, cache_v, state_gla, attn_norm_g, w_in, q_norm_g, k_norm_g,
           attn_sinks, rel_bias, w_gla_gate2, b_gla_gate, gla_norm_g, w_o, ffn_norm_g, w_gate, w_up,
           w_down):
    depth = w_in.shape[0]
    batch, seq, _ = x_prompt.shape
    dec_batch, dec_seq, _ = x_sample.shape
    wb = cache_k.shape[2]
    assert batch == 1 and dec_seq == 1 and wb == WINDOW and seq % PROMPT_ROWS == 0
    assert dec_batch % SAMPLE_BLK == 0 and dec_batch % LANES == 0
    assert rel_bias.shape == (N_BUCKETS, ATTN_HEADS)

    xp = x_prompt.reshape(seq, D_MODEL)
    xs = x_sample.reshape(dec_batch, D_MODEL)
    relb_flat = rel_bias.reshape(-1)
    relb_t = rel_bias.T
    outs = ([], [], [], [], [], [])
    for l in range(depth):
        kv_end = ATTN_WIDTH + 2 * KV_WIDTH
        w_in_p = jnp.concatenate(
            [w_in[l, :, :kv_end],
             jnp.pad(w_in[l, :, MAIN_WIDTH:], ((0, 0), (0, RANK_PAD - GLA_RANK))),
             w_in[l, :, kv_end:MAIN_WIDTH]], axis=1).astype(BF16)
        w2p = jnp.pad(w_gla_gate2[l], ((0, RANK_PAD - GLA_RANK), (0, 0))).astype(BF16)
        proj_w = (attn_norm_g[l][None, :], w_in_p, jnp.tile(q_norm_g[l], ATTN_HEADS)[None, :],
                  jnp.tile(k_norm_g[l], ATTN_KV_HEADS)[None, :], w2p, b_gla_gate[l][None, :])
        fin_w = (w_o[l].astype(BF16), ffn_norm_g[l][None, :], w_gate[l].astype(BF16),
                 w_up[l].astype(BF16), w_down[l].astype(BF16))
        gn = gla_norm_g[l][None, :]

        pp = _project(xp, *proj_w, rows=PROMPT_ROWS, block_cumsum=True)
        mix_p, st_p = _prompt_mixer(pp, relb_flat, attn_sinks[l], gn)
        xp = _finish(xp, mix_p, *fin_w, rows=PROMPT_ROWS)
        outs[0].append(pp[seq - wb:, C_KA:C_VA].reshape(batch, wb, ATTN_KV_HEADS, HEAD_DIM))
        outs[1].append(pp[seq - wb:, C_VA:C_QG].reshape(batch, wb, ATTN_KV_HEADS, HEAD_DIM))
        outs[2].append(st_p.T.reshape(batch, GLA_HEADS, GLA_DK, GLA_DV).astype(state_gla.dtype))

        ps = _project(xs, *proj_w, rows=dec_batch, block_cumsum=False)
        mix_s, kwin, vwin, st_s = _sample_mixer(
            ps, cache_k[l].reshape(dec_batch, wb, KV_WIDTH), cache_v[l].reshape(dec_batch, wb, KV_WIDTH),
            state_gla[l].astype(F32), relb_t, attn_sinks[l][:, None], gn)
        xs = _finish(xs, mix_s, *fin_w, rows=dec_batch)
        outs[3].append(kwin.reshape(dec_batch, wb, ATTN_KV_HEADS, HEAD_DIM))
        outs[4].append(vwin.reshape(dec_batch, wb, ATTN_KV_HEADS, HEAD_DIM))
        outs[5].append(st_s.astype(state_gla.dtype))

    y_prompt = xp.reshape(batch, seq, D_MODEL)
    y_sample = xs.reshape(dec_batch, dec_seq, D_MODEL)
    return (y_prompt, y_sample) + tuple(jnp.stack(o) for o in outs)
```

```python
import functools
import math

import numpy as np
import jax
import jax.numpy as jnp
from jax import lax
from jax.experimental import pallas as pl
from jax.experimental.pallas import tpu as pltpu

F32 = jnp.float32
BF16 = jnp.bfloat16

D_MODEL = 1024
HEAD_DIM = 64
ATTN_HEADS = 8
ATTN_KV_HEADS = 2
WINDOW = 128
N_BUCKETS = 32
MAX_DISTANCE = 128
GLA_HEADS = 4
GLA_DK = 64
GLA_DV = 128
GLA_RANK = 16
GLA_TAU = 16.0
EPS = 1e-6
ATTN_WIDTH = ATTN_HEADS * HEAD_DIM
KV_WIDTH = ATTN_KV_HEADS * HEAD_DIM
GLA_QK_WIDTH = GLA_HEADS * GLA_DK
GLA_WIDTH = GLA_HEADS * GLA_DV
MIX_WIDTH = ATTN_WIDTH + GLA_WIDTH
MAIN_WIDTH = ATTN_WIDTH + 2 * KV_WIDTH + 2 * GLA_QK_WIDTH + 2 * GLA_WIDTH
LANES = 128
SUBLANES = 8
RANK_PAD = LANES
IN_PAD_WIDTH = MAIN_WIDTH + RANK_PAD

C_QA = 0
C_KA = C_QA + ATTN_WIDTH
C_VA = C_KA + KV_WIDTH
C_QG = C_VA + KV_WIDTH
C_KG = C_QG + GLA_QK_WIDTH
C_VG = C_KG + GLA_QK_WIDTH
C_RG = C_VG + GLA_WIDTH
C_LA = C_RG + GLA_WIDTH
P_WIDTH = C_LA + GLA_QK_WIDTH

W_QA = 0
W_KV = W_QA + ATTN_WIDTH
W_QG = W_KV + 2 * KV_WIDTH + RANK_PAD
W_VG = W_QG + 2 * GLA_QK_WIDTH
W_RG = W_VG + GLA_WIDTH

BLK = 128
N_LEVELS = 7
NEG = -1e30
ATTN_SCALE = HEAD_DIM ** -0.5
SAMPLE_BLK = 8
VMEM_LIMIT = 56 * 1024 * 1024


def _t5_bucket_np(dist):
    n = np.maximum(dist, 0)
    max_exact = N_BUCKETS // 2
    nf = np.maximum(n, 1).astype(np.float64)
    large = max_exact + (np.log(nf / max_exact) / math.log(MAX_DISTANCE / max_exact)
                         * (N_BUCKETS - max_exact)).astype(np.int32)
    large = np.minimum(large, N_BUCKETS - 1)
    return np.where(n < max_exact, n, large).astype(np.int32)


def _prompt_bucket_tables():
    i = np.arange(BLK)[:, None]
    j = np.arange(2 * BLK)[None, :]
    dist = BLK + i - j
    band = (dist >= 0) & (dist < WINDOW)
    bucket = _t5_bucket_np(dist)
    t0 = np.where(band, bucket, -1)
    t1 = np.where(band & (j >= BLK), bucket, -1)
    return np.stack([t0, t1]).astype(np.int32)


def _level_tables():
    t = np.arange(BLK)[:, None]
    s = np.arange(BLK)[None, :]
    x = t ^ s
    lev = np.where(x > 0, np.floor(np.log2(np.maximum(x, 1))).astype(np.int32) + 1, 0)
    lev = np.where(s > t, -1, lev).astype(np.int32)
    tri = (s <= t).astype(np.float32)
    return lev, np.concatenate([tri, tri, tri], axis=1)


_BUCKET_PROMPT = _prompt_bucket_tables()
_LEV, _TRI3 = _level_tables()
_BUCKET_SAMPLE = _t5_bucket_np((WINDOW - 1) - np.arange(WINDOW))[None, :].astype(np.int32)


def _nt_dot(a, b):
    return lax.dot_general(a, b, (((1,), (1,)), ((), ())), preferred_element_type=F32)


def _tn_dot(a, b):
    return lax.dot_general(a, b, (((0,), (0,)), ((), ())), preferred_element_type=F32)


def _head_mean_sq(x):
    lo = lax.broadcasted_iota(jnp.int32, (x.shape[0], LANES), 1) < HEAD_DIM
    outs = []
    for c in range(x.shape[1] // LANES):
        y = x[:, c * LANES:(c + 1) * LANES]
        y = y * y
        s_lo = jnp.sum(jnp.where(lo, y, 0.0), axis=-1, keepdims=True)
        s_hi = jnp.sum(jnp.where(lo, 0.0, y), axis=-1, keepdims=True)
        outs.append(jnp.where(lo, s_lo, s_hi) * (1.0 / HEAD_DIM))
    return outs[0] if len(outs) == 1 else jnp.concatenate(outs, axis=1)


def _sigmoid(x):
    return 1.0 / (1.0 + jnp.exp(-x))


def _split3_rows(x):
    hi = x.astype(BF16)
    r1 = x - hi.astype(F32)
    mid = r1.astype(BF16)
    lo = (r1 - mid.astype(F32)).astype(BF16)
    return jnp.concatenate([hi, mid, lo], axis=0)


def _proj_kernel(x_ref, g_ref, w_ref, qn_ref, kn_ref, w2_ref, b2_ref, tri_ref, out_ref, *,
                 block_cumsum):
    x = x_ref[...]
    r = lax.rsqrt(jnp.mean(x * x, axis=-1, keepdims=True) + EPS)
    n = (x * r * g_ref[...]).astype(BF16)

    def seg(c0, c1):
        return jnp.dot(n, w_ref[:, c0:c1], preferred_element_type=F32)

    q = seg(W_QA, W_KV)
    out_ref[:, C_QA:C_KA] = q * lax.rsqrt(_head_mean_sq(q) + EPS) * qn_ref[...] * ATTN_SCALE
    kvl = seg(W_KV, W_QG)
    k = kvl[:, :KV_WIDTH]
    out_ref[:, C_KA:C_VA] = k * lax.rsqrt(_head_mean_sq(k) + EPS) * kn_ref[...]
    out_ref[:, C_VA:C_QG] = kvl[:, KV_WIDTH:2 * KV_WIDTH]
    lr = kvl[:, 2 * KV_WIDTH:].astype(BF16)
    qk_g = seg(W_QG, W_VG)
    out_ref[:, C_QG:C_KG] = qk_g[:, :GLA_QK_WIDTH] * (GLA_DK ** -0.5)
    out_ref[:, C_KG:C_VG] = qk_g[:, GLA_QK_WIDTH:]
    out_ref[:, C_VG:C_RG] = seg(W_VG, W_RG)
    out_ref[:, C_RG:C_LA] = seg(W_RG, IN_PAD_WIDTH)
    z = jnp.dot(lr, w2_ref[...], preferred_element_type=F32) + b2_ref[...]
    log_a = (jnp.minimum(z, 0.0) - jnp.log1p(jnp.exp(-jnp.abs(z)))) / GLA_TAU
    if block_cumsum:
        for blk in range(x.shape[0] // BLK):
            rows = slice(blk * BLK, (blk + 1) * BLK)
            out_ref[rows, C_LA:P_WIDTH] = jnp.dot(
                tri_ref[...], _split3_rows(log_a[rows]), preferred_element_type=F32)
    else:
        out_ref[:, C_LA:P_WIDTH] = log_a


def _const_spec(shape):
    nd = len(shape)
    return pl.BlockSpec(shape, lambda i: (0,) * nd, pipeline_mode=pl.Buffered(1))


def _project(x, g_attn, w_in_p, qn, kn, w2p, b2, rows, block_cumsum):
    t = x.shape[0]
    return pl.pallas_call(
        functools.partial(_proj_kernel, block_cumsum=block_cumsum),
        grid=(t // rows,),
        in_specs=[
            pl.BlockSpec((rows, D_MODEL), lambda i: (i, 0)),
            _const_spec((1, D_MODEL)),
            _const_spec((D_MODEL, IN_PAD_WIDTH)),
            _const_spec((1, ATTN_WIDTH)),
            _const_spec((1, KV_WIDTH)),
            _const_spec((RANK_PAD, GLA_QK_WIDTH)),
            _const_spec((1, GLA_QK_WIDTH)),
            _const_spec((BLK, 3 * BLK)),
        ],
        out_specs=pl.BlockSpec((rows, P_WIDTH), lambda i: (i, 0)),
        out_shape=jax.ShapeDtypeStruct((t, P_WIDTH), F32),
        compiler_params=pltpu.CompilerParams(
            dimension_semantics=("arbitrary",), vmem_limit_bytes=VMEM_LIMIT),
        name="proj",
    )(x, g_attn, w_in_p, qn, kn, w2p, b2, jnp.asarray(_TRI3, BF16))


def _boundary_rows(b_ref, r0, c0, b, row, level):
    m = 1 << (level - 1)
    if 2 * m >= SUBLANES:
        pieces = [jnp.broadcast_to(b_ref[r0 + g * 2 * m + m - 1:r0 + g * 2 * m + m, c0:c0 + LANES],
                                   (2 * m, LANES))
                  for g in range(BLK // (2 * m))]
        return pieces[0] if len(pieces) == 1 else jnp.concatenate(pieces, axis=0)
    pos = row & (2 * m - 1)
    out = b
    for p in range(2 * m):
        shift = (m - 1) - p
        if shift != 0:
            out = jnp.where(pos == p, pltpu.roll(b, (-shift) % BLK, 0), out)
    return out


def _mixer_init(relb_ref, bucket_ref, kprev, vprev, st_scr, mb_scr):
    kprev[...] = jnp.zeros_like(kprev)
    vprev[...] = jnp.zeros_like(vprev)
    st_scr[...] = jnp.zeros_like(st_scr)
    bk = bucket_ref[0]
    acc = [jnp.zeros(bk.shape, F32) for _ in range(ATTN_HEADS)]
    for b in range(N_BUCKETS):
        hit = bk == b
        for h in range(ATTN_HEADS):
            acc[h] = jnp.where(hit, relb_ref[b * ATTN_HEADS + h], acc[h])
    for tb in range(2):
        masked = bucket_ref[tb] < 0
        for h in range(ATTN_HEADS):
            mb_scr[tb, h // 2, :, (h % 2) * 2 * BLK:(h % 2 + 1) * 2 * BLK] = (
                jnp.where(masked, NEG, acc[h]))


MIXER_STEPS = ATTN_HEADS // 2 + (GLA_HEADS // 2) * (N_LEVELS // 2 + 1)


def _mixer_block(p_ref, r0, table, valid, sink_ref, lev_ref, gn_ref, omix_ref, st_ref,
                 kprev, vprev, st_scr, mb_scr):
    rows = slice(r0, r0 + BLK)
    lo1 = lax.broadcasted_iota(jnp.int32, (BLK, LANES), 1) < HEAD_DIM
    lo_bf = jnp.where(lo1, 1.0, 0.0).astype(BF16)
    hi_bf = jnp.where(lo1, 0.0, 1.0).astype(BF16)

    k_cur = p_ref[rows,C_KA:C_VA]
    v_cur = p_ref[rows,C_VA:C_QG]
    kcat = jnp.concatenate([kprev[...], k_cur], axis=0)
    vcat = jnp.concatenate([vprev[...], v_cur], axis=0)
    kroll = pltpu.roll(kcat, HEAD_DIM, 1)
    vroll = pltpu.roll(vcat, HEAD_DIM, 1)
    lo2 = lax.broadcasted_iota(jnp.int32, kcat.shape, 1) < HEAD_DIM

    def block_diag(same, swapped):
        g0 = jnp.concatenate([jnp.where(lo2, same, 0.0), jnp.where(lo2, 0.0, swapped)], axis=0)
        g1 = jnp.concatenate([jnp.where(lo2, swapped, 0.0), jnp.where(lo2, 0.0, same)], axis=0)
        return [g0.astype(BF16), g1.astype(BF16)]

    kbd = block_diag(kcat, kroll)
    vbd = block_diag(vcat, vroll)
    heads_per_kv = ATTN_HEADS // ATTN_KV_HEADS
    for j in range(ATTN_HEADS // 2):
        g = (2 * j) // heads_per_kv
        qp = p_ref[rows,C_QA + j * LANES:C_QA + (j + 1) * LANES].astype(BF16)
        s = _nt_dot(qp, kbd[g]) + mb_scr[table, j]
        probs, inv = [], []
        for e in range(2):
            se = s[:, e * 2 * BLK:(e + 1) * 2 * BLK]
            sk = sink_ref[2 * j + e]
            m = jnp.maximum(jnp.max(se, axis=-1, keepdims=True), sk)
            pe = jnp.exp(se - m)
            den = jnp.sum(pe, axis=-1, keepdims=True) + jnp.exp(sk - m)
            probs.append(pe.astype(BF16))
            inv.append(1.0 / den)
        o = jnp.dot(jnp.concatenate(probs, axis=1), vbd[g], preferred_element_type=F32)
        o = o * jnp.where(lo1, inv[0], inv[1])
        omix_ref[rows,j * LANES:(j + 1) * LANES] = o.astype(omix_ref.dtype)
        yield
    kprev[...] = k_cur
    vprev[...] = v_cur

    row = lax.broadcasted_iota(jnp.int32, (BLK, LANES), 0)
    lev = lev_ref[...]
    at_level = [lev == level for level in range(N_LEVELS + 1)]
    zero_blk = jnp.zeros((BLK, LANES), BF16)
    for c in range(GLA_HEADS // 2):
        c0 = c * LANES
        qc = p_ref[rows,C_QG + c0:C_QG + c0 + LANES]
        kc = p_ref[rows,C_KG + c0:C_KG + c0 + LANES]
        bc = p_ref[rows,C_LA + c0:C_LA + c0 + LANES]

        def pair_scores(qtb, ktb):
            rhs = jnp.concatenate([ktb * lo_bf, ktb * hi_bf], axis=0)
            return _nt_dot(qtb, rhs)

        s0 = pair_scores(qc.astype(BF16), kc.astype(BF16))
        sc = [jnp.where(at_level[0], s0[:, e * BLK:(e + 1) * BLK], 0.0) for e in range(2)]
        for level in range(1, N_LEVELS + 1):
            m = 1 << (level - 1)
            if m >= SUBLANES:
                qs, ks = [], []
                zeros = jnp.zeros((m, LANES), BF16)
                for g in range(BLK // (2 * m)):
                    lo_r = slice(g * 2 * m, g * 2 * m + m)
                    up_r = slice(g * 2 * m + m, (g + 1) * 2 * m)
                    rr = r0 + g * 2 * m + m - 1
                    rb = jnp.broadcast_to(p_ref[rr:rr + 1, C_LA + c0:C_LA + c0 + LANES], (m, LANES))
                    qs += [zeros, (qc[up_r] * jnp.exp(bc[up_r] - rb)).astype(BF16)]
                    ks += [(kc[lo_r] * jnp.exp(rb - bc[lo_r])).astype(BF16), zeros]
                qtb = jnp.concatenate(qs, axis=0)
                ktb = jnp.concatenate(ks, axis=0)
            else:
                d = bc - _boundary_rows(p_ref, r0, C_LA + c0, bc, row, level)
                upper = ((row >> (level - 1)) & 1) == 1
                qtb = (qc * jnp.exp(jnp.where(upper, d, NEG))).astype(BF16)
                ktb = (kc * jnp.exp(jnp.where(upper, NEG, -d))).astype(BF16)
            sl = pair_scores(qtb, ktb)
            sc = [jnp.where(at_level[level], sl[:, e * BLK:(e + 1) * BLK], sc[e]) for e in range(2)]
            if level % 2 == 0:
                yield
        sc = jnp.concatenate(sc, axis=1)

        b_last = bc[BLK - 1:BLK, :]
        v0 = p_ref[rows,C_VG + 2 * c0:C_VG + 2 * c0 + LANES].astype(BF16)
        v1 = p_ref[rows,C_VG + 2 * c0 + LANES:C_VG + 2 * c0 + 2 * LANES].astype(BF16)
        v_bd = jnp.concatenate([jnp.concatenate([v0, zero_blk], axis=1),
                                jnp.concatenate([zero_blk, v1], axis=1)], axis=0)
        st_c = st_scr[:, c0:c0 + LANES]
        stb = st_c.astype(BF16)
        st_rhs = jnp.concatenate([stb * lo_bf, stb * hi_bf], axis=0)
        o = (jnp.dot(sc.astype(BF16), v_bd, preferred_element_type=F32)
             + _nt_dot((qc * jnp.exp(bc)).astype(BF16), st_rhs))
        kd = (kc * jnp.exp(b_last - bc)).astype(BF16)
        upd = _tn_dot(jnp.concatenate([v0, v1], axis=1), kd)
        new_st = st_c * jnp.exp(b_last) + jnp.where(lo1, upd[:BLK], upd[BLK:])
        new_st = jnp.where(valid, new_st, st_c)
        st_scr[:, c0:c0 + LANES] = new_st
        st_ref[:, c0:c0 + LANES] = new_st
        for e in range(2):
            h = 2 * c + e
            oh = o[:, e * LANES:(e + 1) * LANES]
            og = oh * lax.rsqrt(jnp.mean(oh * oh, axis=-1, keepdims=True) + EPS) * gn_ref[...]
            rg = p_ref[rows,C_RG + h * GLA_DV:C_RG + (h + 1) * GLA_DV]
            gated = og * (rg * _sigmoid(rg))
            omix_ref[rows,ATTN_WIDTH + h * GLA_DV:ATTN_WIDTH + (h + 1) * GLA_DV] = (
                gated.astype(omix_ref.dtype))
        yield


FF_CHUNK = 256


def _finish_steps(x_ref, mix_ref, wo_ref, g_ref, wg_ref, wu_ref, wd_ref, y_ref, z_scr):
    h = x_ref[...] + jnp.dot(mix_ref[...].astype(BF16), wo_ref[...], preferred_element_type=F32)
    y_ref[...] = h
    r = lax.rsqrt(jnp.mean(h * h, axis=-1, keepdims=True) + EPS)
    z_scr[...] = (h * r * g_ref[...]).astype(BF16)
    yield
    d_ff = wg_ref.shape[1]
    for c0 in range(0, d_ff, FF_CHUNK):
        c1 = min(c0 + FF_CHUNK, d_ff)
        z = z_scr[...]
        gate = jnp.dot(z, wg_ref[:, c0:c1], preferred_element_type=F32)
        up = jnp.dot(z, wu_ref[:, c0:c1], preferred_element_type=F32)
        act = ((gate * _sigmoid(gate)) * up).astype(BF16)
        y_ref[...] += jnp.dot(act, wd_ref[c0:c1, :], preferred_element_type=F32)
        yield


def _finish_step_count(d_ff):
    return 1 + -(-d_ff // FF_CHUNK)


def _run_interleaved(a, n_a, b, n_b):
    i_a = i_b = 0
    while i_a < n_a or i_b < n_b:
        if i_b < n_b and (i_a >= n_a or i_b * n_a <= i_a * n_b):
            next(b, None)
            i_b += 1
        else:
            next(a, None)
            i_a += 1
    for _ in a:
        pass
    for _ in b:
        pass


def _mixfin_kernel(relb_ref, sink_ref, p_ref, x_ref, bucket_ref, lev_ref, gn_ref, wo_ref, gf_ref,
                   wg_ref, wu_ref, wd_ref, y_ref, st_ref, kprev, vprev, st_scr, mb_scr, mix_scr,
                   z_scr):
    i = pl.program_id(0)
    n_tiles = pl.num_programs(0) - 1

    @pl.when(i == 0)
    def _init():
        _mixer_init(relb_ref, bucket_ref, kprev, vprev, st_scr, mb_scr)
        mix_scr[...] = jnp.zeros_like(mix_scr)

    valid = i < n_tiles
    n_blocks = p_ref.shape[0] // BLK

    def mixer_steps():
        for jb in range(n_blocks):
            table = jnp.where(i == 0, 1, 0) if jb == 0 else 0
            yield from _mixer_block(p_ref, jb * BLK, table, valid, sink_ref, lev_ref, gn_ref,
                                    mix_scr, st_ref, kprev, vprev, st_scr, mb_scr)

    finish = _finish_steps(x_ref, mix_scr, wo_ref, gf_ref, wg_ref, wu_ref, wd_ref, y_ref, z_scr)
    next(finish)
    _run_interleaved(mixer_steps(), n_blocks * MIXER_STEPS, finish,
                     _finish_step_count(wg_ref.shape[1]) - 1)


def _prompt_mixfin(p, x, relb, sinks, gn, wo, g_ffn, wg, wu, wd, rows):
    t = p.shape[0]
    n_tiles = t // rows
    d_ff = wg.shape[1]
    smem = pl.BlockSpec(memory_space=pltpu.SMEM)
    cur = lambda i: (jnp.minimum(i, n_tiles - 1), 0)
    prev = lambda i: (jnp.maximum(i - 1, 0), 0)
    return pl.pallas_call(
        _mixfin_kernel,
        grid=(n_tiles + 1,),
        in_specs=[
            smem, smem,
            pl.BlockSpec((rows, P_WIDTH), cur),
            pl.BlockSpec((rows, D_MODEL), prev),
            _const_spec((2, BLK, 2 * BLK)),
            _const_spec((BLK, BLK)),
            _const_spec((1, GLA_DV)),
            _const_spec((MIX_WIDTH, D_MODEL)),
            _const_spec((1, D_MODEL)),
            _const_spec((D_MODEL, d_ff)),
            _const_spec((D_MODEL, d_ff)),
            _const_spec((d_ff, D_MODEL)),
        ],
        out_specs=[
            pl.BlockSpec((rows, D_MODEL), prev),
            pl.BlockSpec((GLA_DV, GLA_QK_WIDTH), lambda i: (0, 0)),
        ],
        out_shape=[
            jax.ShapeDtypeStruct((t, D_MODEL), F32),
            jax.ShapeDtypeStruct((GLA_DV, GLA_QK_WIDTH), F32),
        ],
        scratch_shapes=[
            pltpu.VMEM((BLK, KV_WIDTH), F32),
            pltpu.VMEM((BLK, KV_WIDTH), F32),
            pltpu.VMEM((GLA_DV, GLA_QK_WIDTH), F32),
            pltpu.VMEM((2, ATTN_HEADS // 2, BLK, 4 * BLK), F32),
            pltpu.VMEM((rows, MIX_WIDTH), BF16),
            pltpu.VMEM((rows, D_MODEL), BF16),
        ],
        compiler_params=pltpu.CompilerParams(
            dimension_semantics=("arbitrary",), vmem_limit_bytes=VMEM_LIMIT),
        name="mixfin",
    )(relb, sinks, p, x, jnp.asarray(_BUCKET_PROMPT), jnp.asarray(_LEV), gn, wo, g_ffn, wg, wu, wd)


def _sample_mixer_kernel(ps_ref, pfull_ref, ck_ref, cv_ref, st_ref, relbt_ref, sink_ref, bucket_ref,
                         gn_ref, omix_ref, kwin_ref, vwin_ref, stout_ref, at_scr, kt_scr, qt_scr,
                         bias_scr):
    i = pl.program_id(0)

    @pl.when(i == 0)
    def _init():
        at_scr[...] = jnp.exp(pfull_ref[:, C_LA:P_WIDTH]).T
        kt_scr[...] = pfull_ref[:, C_KG:C_VG].T
        qt_scr[...] = pfull_ref[:, C_QG:C_KG].T
        bk = jnp.broadcast_to(bucket_ref[...], (ATTN_HEADS, WINDOW))
        acc = jnp.zeros((ATTN_HEADS, WINDOW), F32)
        for b in range(N_BUCKETS):
            acc = jnp.where(bk == b, relbt_ref[:, b:b + 1], acc)
        bias_scr[...] = acc

    lo = lax.broadcasted_iota(jnp.int32, (1, LANES), 1) < HEAD_DIM
    sub = lax.broadcasted_iota(jnp.int32, (ATTN_HEADS, LANES), 0)
    wrow = lax.broadcasted_iota(jnp.int32, (WINDOW, KV_WIDTH), 0)
    lane_n = lax.broadcasted_iota(jnp.int32, (GLA_QK_WIDTH, LANES), 1)
    heads_per_kv = ATTN_HEADS // ATTN_KV_HEADS
    sink = sink_ref[...]
    for j in range(SAMPLE_BLK):
        k_new = ps_ref[j:j + 1, C_KA:C_VA]
        v_new = ps_ref[j:j + 1, C_VA:C_QG]
        kbuf = jnp.where(wrow == WINDOW - 1, k_new, pltpu.roll(ck_ref[j], WINDOW - 1, 0))
        vbuf = jnp.where(wrow == WINDOW - 1, v_new, pltpu.roll(cv_ref[j], WINDOW - 1, 0))
        kwin_ref[j] = kbuf
        vwin_ref[j] = vbuf
        qexp = jnp.zeros((ATTN_HEADS, LANES), F32)
        for c in range(ATTN_HEADS // 2):
            chunk = ps_ref[j:j + 1, C_QA + c * LANES:C_QA + (c + 1) * LANES]
            swapped = pltpu.roll(chunk, HEAD_DIM, 1)
            if (2 * c) // heads_per_kv == 0:
                rows = (jnp.where(lo, chunk, 0.0), jnp.where(lo, swapped, 0.0))
            else:
                rows = (jnp.where(lo, 0.0, swapped), jnp.where(lo, 0.0, chunk))
            for e in range(2):
                qexp = jnp.where(sub == 2 * c + e, rows[e], qexp)
        s = _nt_dot(qexp.astype(BF16), kbuf.astype(BF16)) + bias_scr[...]
        m = jnp.maximum(jnp.max(s, axis=-1, keepdims=True), sink)
        pe = jnp.exp(s - m)
        den = jnp.sum(pe, axis=-1, keepdims=True) + jnp.exp(sink - m)
        o = jnp.dot(pe.astype(BF16), vbuf.astype(BF16), preferred_element_type=F32) / den
        oswap = pltpu.roll(o, HEAD_DIM, 1)
        for c in range(ATTN_HEADS // 2):
            if (2 * c) // heads_per_kv == 0:
                piece = jnp.where(lo, o[2 * c:2 * c + 1, :], oswap[2 * c + 1:2 * c + 2, :])
            else:
                piece = jnp.where(lo, oswap[2 * c:2 * c + 1, :], o[2 * c + 1:2 * c + 2, :])
            omix_ref[j:j + 1, c * LANES:(c + 1) * LANES] = piece
        hit = lane_n == i * SAMPLE_BLK + j
        a_col = jnp.sum(jnp.where(hit, at_scr[...], 0.0), axis=1, keepdims=True)
        k_col = jnp.sum(jnp.where(hit, kt_scr[...], 0.0), axis=1, keepdims=True)
        q_col = jnp.sum(jnp.where(hit, qt_scr[...], 0.0), axis=1, keepdims=True)
        for h in range(GLA_HEADS):
            rs = slice(h * GLA_DK, (h + 1) * GLA_DK)
            v_row = ps_ref[j:j + 1, C_VG + h * GLA_DV:C_VG + (h + 1) * GLA_DV]
            s_new = a_col[rs] * st_ref[j, h] + k_col[rs] * v_row
            stout_ref[j, h] = s_new
            og = jnp.sum(q_col[rs] * s_new, axis=0, keepdims=True)
            og = og * lax.rsqrt(jnp.mean(og * og, axis=-1, keepdims=True) + EPS) * gn_ref[...]
            rg = ps_ref[j:j + 1, C_RG + h * GLA_DV:C_RG + (h + 1) * GLA_DV]
            omix_ref[j:j + 1, ATTN_WIDTH + h * GLA_DV:ATTN_WIDTH + (h + 1) * GLA_DV] = (
                og * (rg * _sigmoid(rg)))


def _sample_mixer(ps, cache_k, cache_v, state, relbt, sinks_col, gn):
    nb = ps.shape[0]
    blk3 = lambda i: (i, 0, 0)
    blk4 = lambda i: (i, 0, 0, 0)
    return pl.pallas_call(
        _sample_mixer_kernel,
        grid=(nb // SAMPLE_BLK,),
        in_specs=[
            pl.BlockSpec((SAMPLE_BLK, P_WIDTH), lambda i: (i, 0)),
            _const_spec((nb, P_WIDTH)),
            pl.BlockSpec((SAMPLE_BLK, WINDOW, KV_WIDTH), blk3),
            pl.BlockSpec((SAMPLE_BLK, WINDOW, KV_WIDTH), blk3),
            pl.BlockSpec((SAMPLE_BLK, GLA_HEADS, GLA_DK, GLA_DV), blk4),
            _const_spec((ATTN_HEADS, N_BUCKETS)),
            _const_spec((ATTN_HEADS, 1)),
            _const_spec((1, WINDOW)),
            _const_spec((1, GLA_DV)),
        ],
        out_specs=[
            pl.BlockSpec((SAMPLE_BLK, MIX_WIDTH), lambda i: (i, 0)),
            pl.BlockSpec((SAMPLE_BLK, WINDOW, KV_WIDTH), blk3),
            pl.BlockSpec((SAMPLE_BLK, WINDOW, KV_WIDTH), blk3),
            pl.BlockSpec((SAMPLE_BLK, GLA_HEADS, GLA_DK, GLA_DV), blk4),
        ],
        out_shape=[
            jax.ShapeDtypeStruct((nb, MIX_WIDTH), F32),
            jax.ShapeDtypeStruct((nb, WINDOW, KV_WIDTH), F32),
            jax.ShapeDtypeStruct((nb, WINDOW, KV_WIDTH), F32),
            jax.ShapeDtypeStruct((nb, GLA_HEADS, GLA_DK, GLA_DV), F32),
        ],
        scratch_shapes=[
            pltpu.VMEM((GLA_QK_WIDTH, nb), F32),
            pltpu.VMEM((GLA_QK_WIDTH, nb), F32),
            pltpu.VMEM((GLA_QK_WIDTH, nb), F32),
            pltpu.VMEM((ATTN_HEADS, WINDOW), F32),
        ],
        compiler_params=pltpu.CompilerParams(
            dimension_semantics=("arbitrary",), vmem_limit_bytes=VMEM_LIMIT),
        name="sample_mixer",
    )(ps, ps, cache_k, cache_v, state, relbt, sinks_col, jnp.asarray(_BUCKET_SAMPLE), gn)


def _finish_kernel(x_ref, mix_ref, wo_ref, g_ref, wg_ref, wu_ref, wd_ref, y_ref, z_scr):
    for _ in _finish_steps(x_ref, mix_ref, wo_ref, g_ref, wg_ref, wu_ref, wd_ref, y_ref, z_scr):
        pass


def _finish(x, mix, wo, g_ffn, wg, wu, wd, rows):
    t = x.shape[0]
    d_ff = wg.shape[1]
    return pl.pallas_call(
        _finish_kernel,
        grid=(t // rows,),
        in_specs=[
            pl.BlockSpec((rows, D_MODEL), lambda i: (i, 0)),
            pl.BlockSpec((rows, MIX_WIDTH), lambda i: (i, 0)),
            _const_spec((MIX_WIDTH, D_MODEL)),
            _const_spec((1, D_MODEL)),
            _const_spec((D_MODEL, d_ff)),
            _const_spec((D_MODEL, d_ff)),
            _const_spec((d_ff, D_MODEL)),
        ],
        out_specs=pl.BlockSpec((rows, D_MODEL), lambda i: (i, 0)),
        out_shape=jax.ShapeDtypeStruct((t, D_MODEL), F32),
        scratch_shapes=[pltpu.VMEM((rows, D_MODEL), BF16)],
        compiler_params=pltpu.CompilerParams(
            dimension_semantics=("arbitrary",), vmem_limit_bytes=VMEM_LIMIT),
        name="finish",
    )(x, mix, wo, g_ffn, wg, wu, wd)


PROMPT_ROWS = 512
MIXFIN_ROWS = 256


def kernel(x_prompt, x_sample, cache_k, cache_v, state_gla, attn_norm_g, w_in, q_norm_g, k_norm_g,
           attn_sinks, rel_bias, w_gla_gate2, b_gla_gate, gla_norm_g, w_o, ffn_norm_g, w_gate, w_up,
           w_down):
    depth = w_in.shape[0]
    batch, seq, _ = x_prompt.shape
    dec_batch, dec_seq, _ = x_sample.shape
    wb = cache_k.shape[2]
    assert batch == 1 and dec_seq == 1 and wb == WINDOW and seq % PROMPT_ROWS == 0
    assert dec_batch % SAMPLE_BLK == 0 and dec_batch % LANES == 0
    assert rel_bias.shape == (N_BUCKETS, ATTN_HEADS)

    xp = x_prompt.reshape(seq, D_MODEL)
    xs = x_sample.reshape(dec_batch, D_MODEL)
    relb_flat = rel_bias.reshape(-1)
    relb_t = rel_bias.T
    outs = ([], [], [], [], [], [])
    for l in range(depth):
        kv_end = ATTN_WIDTH + 2 * KV_WIDTH
        w_in_p = jnp.concatenate(
            [w_in[l, :, :kv_end],
             jnp.pad(w_in[l, :, MAIN_WIDTH:], ((0, 0), (0, RANK_PAD - GLA_RANK))),
             w_in[l, :, kv_end:MAIN_WIDTH]], axis=1).astype(BF16)
        w2p = jnp.pad(w_gla_gate2[l], ((0, RANK_PAD - GLA_RANK), (0, 0))).astype(BF16)
        proj_w = (attn_norm_g[l][None, :], w_in_p, jnp.tile(q_norm_g[l], ATTN_HEADS)[None, :],
                  jnp.tile(k_norm_g[l], ATTN_KV_HEADS)[None, :], w2p, b_gla_gate[l][None, :])
        fin_w = (w_o[l].astype(BF16), ffn_norm_g[l][None, :], w_gate[l].astype(BF16),
                 w_up[l].astype(BF16), w_down[l].astype(BF16))
        gn = gla_norm_g[l][None, :]

        pp = _project(xp, *proj_w, rows=PROMPT_ROWS, block_cumsum=True)
        xp, st_p = _prompt_mixfin(pp, xp, relb_flat, attn_sinks[l], gn, *fin_w, rows=MIXFIN_ROWS)
        outs[0].append(pp[seq - wb:, C_KA:C_VA].reshape(batch, wb, ATTN_KV_HEADS, HEAD_DIM))
        outs[1].append(pp[seq - wb:, C_VA:C_QG].reshape(batch, wb, ATTN_KV_HEADS, HEAD_DIM))
        outs[2].append(st_p.T.reshape(batch, GLA_HEADS, GLA_DK, GLA_DV).astype(state_gla.dtype))

        ps = _project(xs, *proj_w, rows=dec_batch, block_cumsum=False)
        mix_s, kwin, vwin, st_s = _sample_mixer(
            ps, cache_k[l].reshape(dec_batch, wb, KV_WIDTH), cache_v[l].reshape(dec_batch, wb, KV_WIDTH),
            state_gla[l].astype(F32), relb_t, attn_sinks[l][:, None], gn)
        xs = _finish(xs, mix_s, *fin_w, rows=dec_batch)
        outs[3].append(kwin.reshape(dec_batch, wb, ATTN_KV_HEADS, HEAD_DIM))
        outs[4].append(vwin.reshape(dec_batch, wb, ATTN_KV_HEADS, HEAD_DIM))
        outs[5].append(st_s.astype(state_gla.dtype))

    y_prompt = xp.reshape(batch, seq, D_MODEL)
    y_sample = xs.reshape(dec_batch, dec_seq, D_MODEL)
    return (y_prompt, y_sample) + tuple(jnp.stack(o) for o in outs)
```

```python
import functools
import math

import numpy as np
import jax
import jax.numpy as jnp
from jax import lax
from jax.experimental import pallas as pl
from jax.experimental.pallas import tpu as pltpu

F32 = jnp.float32
BF16 = jnp.bfloat16

D_MODEL = 1024
HEAD_DIM = 64
ATTN_HEADS = 8
ATTN_KV_HEADS = 2
WINDOW = 128
N_BUCKETS = 32
MAX_DISTANCE = 128
GLA_HEADS = 4
GLA_DK = 64
GLA_DV = 128
GLA_RANK = 16
GLA_TAU = 16.0
EPS = 1e-6
ATTN_WIDTH = ATTN_HEADS * HEAD_DIM
KV_WIDTH = ATTN_KV_HEADS * HEAD_DIM
GLA_QK_WIDTH = GLA_HEADS * GLA_DK
GLA_WIDTH = GLA_HEADS * GLA_DV
MIX_WIDTH = ATTN_WIDTH + GLA_WIDTH
MAIN_WIDTH = ATTN_WIDTH + 2 * KV_WIDTH + 2 * GLA_QK_WIDTH + 2 * GLA_WIDTH
LANES = 128
SUBLANES = 8
MXU_TILE = 256
FF_CHUNKS = 2
RANK_PAD = LANES
IN_PAD_WIDTH = MAIN_WIDTH + RANK_PAD

C_QA = 0
C_KA = C_QA + ATTN_WIDTH
C_VA = C_KA + KV_WIDTH
C_QG = C_VA + KV_WIDTH
C_KG = C_QG + GLA_QK_WIDTH
C_VG = C_KG + GLA_QK_WIDTH
C_RG = C_VG + GLA_WIDTH
C_LA = C_RG + GLA_WIDTH
P_WIDTH = C_LA + GLA_QK_WIDTH

W_QA = 0
W_KV = W_QA + ATTN_WIDTH
W_QG = W_KV + 2 * KV_WIDTH + RANK_PAD
W_VG = W_QG + 2 * GLA_QK_WIDTH
W_RG = W_VG + GLA_WIDTH

BLK = 128
N_LEVELS = 7
NEG = -1e30
ATTN_SCALE = HEAD_DIM ** -0.5
SAMPLE_BLK = 8
VMEM_LIMIT = 56 * 1024 * 1024


def _t5_bucket_np(dist):
    n = np.maximum(dist, 0)
    max_exact = N_BUCKETS // 2
    nf = np.maximum(n, 1).astype(np.float64)
    large = max_exact + (np.log(nf / max_exact) / math.log(MAX_DISTANCE / max_exact)
                         * (N_BUCKETS - max_exact)).astype(np.int32)
    large = np.minimum(large, N_BUCKETS - 1)
    return np.where(n < max_exact, n, large).astype(np.int32)


def _prompt_bucket_tables():
    i = np.arange(BLK)[:, None]
    j = np.arange(2 * BLK)[None, :]
    dist = BLK + i - j
    band = (dist >= 0) & (dist < WINDOW)
    bucket = _t5_bucket_np(dist)
    t0 = np.where(band, bucket, -1)
    t1 = np.where(band & (j >= BLK), bucket, -1)
    return np.stack([t0, t1]).astype(np.int32)


def _level_tables():
    t = np.arange(BLK)[:, None]
    s = np.arange(BLK)[None, :]
    x = t ^ s
    lev = np.where(x > 0, np.floor(np.log2(np.maximum(x, 1))).astype(np.int32) + 1, 0)
    lev = np.where(s > t, -1, lev).astype(np.int32)
    tri = (s <= t).astype(np.float32)
    return lev, np.concatenate([tri, tri, tri], axis=1)


_BUCKET_PROMPT = _prompt_bucket_tables()
_LEV, _TRI3 = _level_tables()
_BUCKET_SAMPLE = _t5_bucket_np((WINDOW - 1) - np.arange(WINDOW))[None, :].astype(np.int32)


def _nt_dot(a, b):
    return lax.dot_general(a, b, (((1,), (1,)), ((), ())), preferred_element_type=F32)


def _tn_dot(a, b):
    return lax.dot_general(a, b, (((0,), (0,)), ((), ())), preferred_element_type=F32)


def _head_mean_sq(x):
    lo = lax.broadcasted_iota(jnp.int32, (x.shape[0], LANES), 1) < HEAD_DIM
    outs = []
    for c in range(x.shape[1] // LANES):
        y = x[:, c * LANES:(c + 1) * LANES]
        y = y * y
        s_lo = jnp.sum(jnp.where(lo, y, 0.0), axis=-1, keepdims=True)
        s_hi = jnp.sum(jnp.where(lo, 0.0, y), axis=-1, keepdims=True)
        outs.append(jnp.where(lo, s_lo, s_hi) * (1.0 / HEAD_DIM))
    return outs[0] if len(outs) == 1 else jnp.concatenate(outs, axis=1)


def _sigmoid(x):
    return 1.0 / (1.0 + jnp.exp(-x))


def _split3_rows(x):
    hi = x.astype(BF16)
    r1 = x - hi.astype(F32)
    mid = r1.astype(BF16)
    lo = (r1 - mid.astype(F32)).astype(BF16)
    return jnp.concatenate([hi, mid, lo], axis=0)


def _proj_kernel(x_ref, g_ref, w_ref, qn_ref, kn_ref, w2_ref, b2_ref, tri_ref, out_ref, *,
                 block_cumsum):
    x = x_ref[...]
    r = lax.rsqrt(jnp.mean(x * x, axis=-1, keepdims=True) + EPS)
    n = (x * r * g_ref[...]).astype(BF16)

    def seg(c0, c1):
        return jnp.dot(n, w_ref[:, c0:c1], preferred_element_type=F32)

    q = seg(W_QA, W_KV)
    out_ref[:, C_QA:C_KA] = q * lax.rsqrt(_head_mean_sq(q) + EPS) * qn_ref[...] * ATTN_SCALE
    kvl = seg(W_KV, W_QG)
    k = kvl[:, :KV_WIDTH]
    out_ref[:, C_KA:C_VA] = k * lax.rsqrt(_head_mean_sq(k) + EPS) * kn_ref[...]
    out_ref[:, C_VA:C_QG] = kvl[:, KV_WIDTH:2 * KV_WIDTH]
    lr = kvl[:, 2 * KV_WIDTH:].astype(BF16)
    qk_g = seg(W_QG, W_VG)
    out_ref[:, C_QG:C_KG] = qk_g[:, :GLA_QK_WIDTH] * (GLA_DK ** -0.5)
    out_ref[:, C_KG:C_VG] = qk_g[:, GLA_QK_WIDTH:]
    out_ref[:, C_VG:C_RG] = seg(W_VG, W_RG)
    out_ref[:, C_RG:C_LA] = seg(W_RG, IN_PAD_WIDTH)
    z = jnp.dot(lr, w2_ref[...], preferred_element_type=F32) + b2_ref[...]
    log_a = (jnp.minimum(z, 0.0) - jnp.log1p(jnp.exp(-jnp.abs(z)))) / GLA_TAU
    if block_cumsum:
        for blk in range(x.shape[0] // BLK):
            rows = slice(blk * BLK, (blk + 1) * BLK)
            out_ref[rows, C_LA:P_WIDTH] = jnp.dot(
                tri_ref[...], _split3_rows(log_a[rows]), preferred_element_type=F32)
    else:
        out_ref[:, C_LA:P_WIDTH] = log_a


def _const_spec(shape):
    nd = len(shape)
    return pl.BlockSpec(shape, lambda i: (0,) * nd, pipeline_mode=pl.Buffered(1))


def _project(x, g_attn, w_in_p, qn, kn, w2p, b2, rows, block_cumsum):
    t = x.shape[0]
    return pl.pallas_call(
        functools.partial(_proj_kernel, block_cumsum=block_cumsum),
        grid=(t // rows,),
        in_specs=[
            pl.BlockSpec((rows, D_MODEL), lambda i: (i, 0)),
            _const_spec((1, D_MODEL)),
            _const_spec((D_MODEL, IN_PAD_WIDTH)),
            _const_spec((1, ATTN_WIDTH)),
            _const_spec((1, KV_WIDTH)),
            _const_spec((RANK_PAD, GLA_QK_WIDTH)),
            _const_spec((1, GLA_QK_WIDTH)),
            _const_spec((BLK, 3 * BLK)),
        ],
        out_specs=pl.BlockSpec((rows, P_WIDTH), lambda i: (i, 0)),
        out_shape=jax.ShapeDtypeStruct((t, P_WIDTH), F32),
        compiler_params=pltpu.CompilerParams(
            dimension_semantics=("arbitrary",), vmem_limit_bytes=VMEM_LIMIT),
        name="proj",
    )(x, g_attn, w_in_p, qn, kn, w2p, b2, jnp.asarray(_TRI3, BF16))


def _boundary_rows(b_ref, r0, c0, b, row, level):
    m = 1 << (level - 1)
    if 2 * m >= SUBLANES:
        pieces = [jnp.broadcast_to(b_ref[r0 + g * 2 * m + m - 1:r0 + g * 2 * m + m, c0:c0 + LANES],
                                   (2 * m, LANES))
                  for g in range(BLK // (2 * m))]
        return pieces[0] if len(pieces) == 1 else jnp.concatenate(pieces, axis=0)
    pos = row & (2 * m - 1)
    out = b
    for p in range(2 * m):
        shift = (m - 1) - p
        if shift != 0:
            out = jnp.where(pos == p, pltpu.roll(b, (-shift) % BLK, 0), out)
    return out


def _mixer_init(relb_ref, bucket_ref, kbd, vbd, st_scr, mb_scr):
    kbd[...] = jnp.zeros_like(kbd)
    vbd[...] = jnp.zeros_like(vbd)
    st_scr[...] = jnp.zeros_like(st_scr)
    bk = bucket_ref[0]
    acc = [jnp.zeros(bk.shape, F32) for _ in range(ATTN_HEADS)]
    for b in range(N_BUCKETS):
        hit = bk == b
        for h in range(ATTN_HEADS):
            acc[h] = jnp.where(hit, relb_ref[b * ATTN_HEADS + h], acc[h])
    for tb in range(2):
        masked = bucket_ref[tb] < 0
        for h in range(ATTN_HEADS):
            mb_scr[tb, h // 2, :, (h % 2) * 2 * BLK:(h % 2 + 1) * 2 * BLK] = (
                jnp.where(masked, NEG, acc[h]))


def _mixer_block(p_ref, r0, table, sink_ref, lev_ref, gn_ref, omix_ref, st_ref,
                 kbd, vbd, st_scr, mb_scr):
    rows = slice(r0, r0 + BLK)
    lo1 = lax.broadcasted_iota(jnp.int32, (BLK, LANES), 1) < HEAD_DIM
    lo_bf = jnp.where(lo1, 1.0, 0.0).astype(BF16)
    hi_bf = jnp.where(lo1, 0.0, 1.0).astype(BF16)

    for src_c, dst in ((C_KA, kbd), (C_VA, vbd)):
        cur = p_ref[rows,src_c:src_c + KV_WIDTH]
        swapped = pltpu.roll(cur, HEAD_DIM, 1)
        dst[0, BLK:2 * BLK] = jnp.where(lo1, cur, 0.0).astype(BF16)
        dst[0, 3 * BLK:4 * BLK] = jnp.where(lo1, 0.0, swapped).astype(BF16)
        dst[1, BLK:2 * BLK] = jnp.where(lo1, swapped, 0.0).astype(BF16)
        dst[1, 3 * BLK:4 * BLK] = jnp.where(lo1, 0.0, cur).astype(BF16)
    heads_per_kv = ATTN_HEADS // ATTN_KV_HEADS
    for j in range(ATTN_HEADS // 2):
        g = (2 * j) // heads_per_kv
        qp = p_ref[rows,C_QA + j * LANES:C_QA + (j + 1) * LANES].astype(BF16)
        s = _nt_dot(qp, kbd[g]) + mb_scr[table, j]
        probs, inv = [], []
        for e in range(2):
            se = s[:, e * 2 * BLK:(e + 1) * 2 * BLK]
            sk = sink_ref[2 * j + e]
            m = jnp.maximum(jnp.max(se, axis=-1, keepdims=True), sk)
            pe = jnp.exp(se - m)
            den = jnp.sum(pe, axis=-1, keepdims=True) + jnp.exp(sk - m)
            probs.append(pe.astype(BF16))
            inv.append(1.0 / den)
        o = jnp.dot(jnp.concatenate(probs, axis=1), vbd[g], preferred_element_type=F32)
        o = o * jnp.where(lo1, inv[0], inv[1])
        omix_ref[rows,j * LANES:(j + 1) * LANES] = o.astype(omix_ref.dtype)
    for dst in (kbd, vbd):
        for g in range(ATTN_KV_HEADS):
            dst[g, 0:BLK] = dst[g, BLK:2 * BLK]
            dst[g, 2 * BLK:3 * BLK] = dst[g, 3 * BLK:4 * BLK]

    row = lax.broadcasted_iota(jnp.int32, (BLK, LANES), 0)
    zero_blk = jnp.zeros((BLK, LANES), BF16)
    for c in range(GLA_HEADS // 2):
        c0 = c * LANES
        q_at = lambda a, z: p_ref[r0 + a:r0 + z, C_QG + c0:C_QG + c0 + LANES]
        k_at = lambda a, z: p_ref[r0 + a:r0 + z, C_KG + c0:C_KG + c0 + LANES]
        b_at = lambda a, z: p_ref[r0 + a:r0 + z, C_LA + c0:C_LA + c0 + LANES]

        def pair_scores(qtb, ktb):
            rhs = jnp.concatenate([ktb * lo_bf, ktb * hi_bf], axis=0)
            return _nt_dot(qtb, rhs)

        s0 = pair_scores(q_at(0, BLK).astype(BF16), k_at(0, BLK).astype(BF16))
        sc = [jnp.where(lev_ref[...] == 0, s0[:, e * BLK:(e + 1) * BLK], 0.0) for e in range(2)]
        for level in range(1, N_LEVELS + 1):
            m = 1 << (level - 1)
            if m >= SUBLANES:
                qs, ks = [], []
                zeros = jnp.zeros((m, LANES), BF16)
                for g in range(BLK // (2 * m)):
                    lo_a, up_a, up_z = g * 2 * m, g * 2 * m + m, (g + 1) * 2 * m
                    rb = jnp.broadcast_to(b_at(up_a - 1, up_a), (m, LANES))
                    qs += [zeros, (q_at(up_a, up_z) * jnp.exp(b_at(up_a, up_z) - rb)).astype(BF16)]
                    ks += [(k_at(lo_a, up_a) * jnp.exp(rb - b_at(lo_a, up_a))).astype(BF16), zeros]
                qtb = jnp.concatenate(qs, axis=0)
                ktb = jnp.concatenate(ks, axis=0)
            else:
                bc = b_at(0, BLK)
                d = bc - _boundary_rows(p_ref, r0, C_LA + c0, bc, row, level)
                upper = ((row >> (level - 1)) & 1) == 1
                qtb = (q_at(0, BLK) * jnp.exp(jnp.where(upper, d, NEG))).astype(BF16)
                ktb = (k_at(0, BLK) * jnp.exp(jnp.where(upper, NEG, -d))).astype(BF16)
            sl = pair_scores(qtb, ktb)
            sc = [jnp.where(lev_ref[...] == level, sl[:, e * BLK:(e + 1) * BLK], sc[e])
                  for e in range(2)]
        sc = jnp.concatenate(sc, axis=1)
        qc, kc, bc = q_at(0, BLK), k_at(0, BLK), b_at(0, BLK)

        b_last = bc[BLK - 1:BLK, :]
        v0 = p_ref[rows,C_VG + 2 * c0:C_VG + 2 * c0 + LANES].astype(BF16)
        v1 = p_ref[rows,C_VG + 2 * c0 + LANES:C_VG + 2 * c0 + 2 * LANES].astype(BF16)
        v_bd = jnp.concatenate([jnp.concatenate([v0, zero_blk], axis=1),
                                jnp.concatenate([zero_blk, v1], axis=1)], axis=0)
        st_c = st_scr[:, c0:c0 + LANES]
        stb = st_c.astype(BF16)
        st_rhs = jnp.concatenate([stb * lo_bf, stb * hi_bf], axis=0)
        o = (jnp.dot(sc.astype(BF16), v_bd, preferred_element_type=F32)
             + _nt_dot((qc * jnp.exp(bc)).astype(BF16), st_rhs))
        kd = (kc * jnp.exp(b_last - bc)).astype(BF16)
        upd = _tn_dot(jnp.concatenate([v0, v1], axis=1), kd)
        new_st = st_c * jnp.exp(b_last) + jnp.where(lo1, upd[:BLK], upd[BLK:])
        st_scr[:, c0:c0 + LANES] = new_st
        st_ref[:, c0:c0 + LANES] = new_st
        for e in range(2):
            h = 2 * c + e
            oh = o[:, e * LANES:(e + 1) * LANES]
            og = oh * lax.rsqrt(jnp.mean(oh * oh, axis=-1, keepdims=True) + EPS) * gn_ref[...]
            rg = p_ref[rows,C_RG + h * GLA_DV:C_RG + (h + 1) * GLA_DV]
            gated = og * (rg * _sigmoid(rg))
            omix_ref[rows,ATTN_WIDTH + h * GLA_DV:ATTN_WIDTH + (h + 1) * GLA_DV] = (
                gated.astype(omix_ref.dtype))


def _prompt_mixer_kernel(relb_ref, sink_ref, p_ref, bucket_ref, lev_ref, gn_ref,
                         omix_ref, st_ref, kbd, vbd, st_scr, mb_scr):
    i = pl.program_id(0)

    @pl.when(i == 0)
    def _init():
        _mixer_init(relb_ref, bucket_ref, kbd, vbd, st_scr, mb_scr)

    _mixer_block(p_ref, 0, jnp.where(i == 0, 1, 0), sink_ref, lev_ref, gn_ref, omix_ref, st_ref,
                 kbd, vbd, st_scr, mb_scr)


def _prompt_mixer(p, relb, sinks, gn):
    t = p.shape[0]
    smem = pl.BlockSpec(memory_space=pltpu.SMEM)
    return pl.pallas_call(
        _prompt_mixer_kernel,
        grid=(t // BLK,),
        in_specs=[
            smem, smem,
            pl.BlockSpec((BLK, P_WIDTH), lambda i: (i, 0)),
            _const_spec((2, BLK, 2 * BLK)),
            _const_spec((BLK, BLK)),
            _const_spec((1, GLA_DV)),
        ],
        out_specs=[
            pl.BlockSpec((BLK, MIX_WIDTH), lambda i: (i, 0)),
            pl.BlockSpec((GLA_DV, GLA_QK_WIDTH), lambda i: (0, 0)),
        ],
        out_shape=[
            jax.ShapeDtypeStruct((t, MIX_WIDTH), BF16),
            jax.ShapeDtypeStruct((GLA_DV, GLA_QK_WIDTH), F32),
        ],
        scratch_shapes=[
            pltpu.VMEM((ATTN_KV_HEADS, 4 * BLK, KV_WIDTH), BF16),
            pltpu.VMEM((ATTN_KV_HEADS, 4 * BLK, KV_WIDTH), BF16),
            pltpu.VMEM((GLA_DV, GLA_QK_WIDTH), F32),
            pltpu.VMEM((2, ATTN_HEADS // 2, BLK, 4 * BLK), F32),
        ],
        compiler_params=pltpu.CompilerParams(
            dimension_semantics=("arbitrary",), vmem_limit_bytes=VMEM_LIMIT),
        name="prompt_mixer",
    )(relb, sinks, p, jnp.asarray(_BUCKET_PROMPT), jnp.asarray(_LEV), gn)


def _sample_mixer_kernel(ps_ref, pfull_ref, ck_ref, cv_ref, st_ref, relbt_ref, sink_ref, bucket_ref,
                         gn_ref, omix_ref, kwin_ref, vwin_ref, stout_ref, lat_scr, kqt_scr,
                         bias_scr, s_scr, o_scr, og_scr):
    i = pl.program_id(0)
    nb = pfull_ref.shape[0]

    @pl.when(i == 0)
    def _init():
        lat_scr[...] = _split3_rows(pfull_ref[:, C_LA:P_WIDTH].T)
        kqt_scr[0:GLA_QK_WIDTH] = pfull_ref[:, C_KG:C_VG].T.astype(BF16)
        kqt_scr[GLA_QK_WIDTH:2 * GLA_QK_WIDTH] = pfull_ref[:, C_QG:C_KG].T.astype(BF16)
        bk = jnp.broadcast_to(bucket_ref[...], (ATTN_HEADS, WINDOW))
        acc = jnp.zeros((ATTN_HEADS, WINDOW), F32)
        for b in range(N_BUCKETS):
            acc = jnp.where(bk == b, relbt_ref[:, b:b + 1], acc)
        bias_scr[...] = acc

    lo = lax.broadcasted_iota(jnp.int32, (1, LANES), 1) < HEAD_DIM
    sub = lax.broadcasted_iota(jnp.int32, (ATTN_HEADS, LANES), 0)
    wrow = lax.broadcasted_iota(jnp.int32, (WINDOW, KV_WIDTH), 0)
    heads_per_kv = ATTN_HEADS // ATTN_KV_HEADS

    n_of_col = i * SAMPLE_BLK + lax.broadcasted_iota(jnp.int32, (nb, SAMPLE_BLK * LANES), 1) // LANES
    pick = jnp.where(lax.broadcasted_iota(jnp.int32, (nb, SAMPLE_BLK * LANES), 0) == n_of_col,
                     1.0, 0.0).astype(BF16)
    la_b = (jnp.dot(lat_scr[0:GLA_QK_WIDTH], pick, preferred_element_type=F32)
            + jnp.dot(lat_scr[GLA_QK_WIDTH:2 * GLA_QK_WIDTH], pick, preferred_element_type=F32)
            + jnp.dot(lat_scr[2 * GLA_QK_WIDTH:3 * GLA_QK_WIDTH], pick, preferred_element_type=F32))
    kq_b = jnp.dot(kqt_scr[...], pick, preferred_element_type=F32)

    for j in range(SAMPLE_BLK):
        k_new = ps_ref[j:j + 1, C_KA:C_VA]
        v_new = ps_ref[j:j + 1, C_VA:C_QG]
        kwin_ref[j] = jnp.where(wrow == WINDOW - 1, k_new, pltpu.roll(ck_ref[j], WINDOW - 1, 0))
        vwin_ref[j] = jnp.where(wrow == WINDOW - 1, v_new, pltpu.roll(cv_ref[j], WINDOW - 1, 0))

    for j in range(SAMPLE_BLK):
        qexp = jnp.zeros((ATTN_HEADS, LANES), F32)
        for c in range(ATTN_HEADS // 2):
            chunk = ps_ref[j:j + 1, C_QA + c * LANES:C_QA + (c + 1) * LANES]
            swapped = pltpu.roll(chunk, HEAD_DIM, 1)
            if (2 * c) // heads_per_kv == 0:
                rows = (jnp.where(lo, chunk, 0.0), jnp.where(lo, swapped, 0.0))
            else:
                rows = (jnp.where(lo, 0.0, swapped), jnp.where(lo, 0.0, chunk))
            for e in range(2):
                qexp = jnp.where(sub == 2 * c + e, rows[e], qexp)
        s_scr[j * ATTN_HEADS:(j + 1) * ATTN_HEADS] = _nt_dot(qexp.astype(BF16), kwin_ref[j].astype(BF16))

    tile = lambda x: jnp.concatenate([x] * SAMPLE_BLK, axis=0)
    sink = tile(sink_ref[...])
    s = s_scr[...] + tile(bias_scr[...])
    m = jnp.maximum(jnp.max(s, axis=-1, keepdims=True), sink)
    pe = jnp.exp(s - m)
    inv_den = 1.0 / (jnp.sum(pe, axis=-1, keepdims=True) + jnp.exp(sink - m))
    peb = pe.astype(BF16)
    for j in range(SAMPLE_BLK):
        o_scr[j * ATTN_HEADS:(j + 1) * ATTN_HEADS] = jnp.dot(
            peb[j * ATTN_HEADS:(j + 1) * ATTN_HEADS], vwin_ref[j].astype(BF16),
            preferred_element_type=F32)
    o_all = o_scr[...] * inv_den
    o_swap = pltpu.roll(o_all, HEAD_DIM, 1)
    for j in range(SAMPLE_BLK):
        r = j * ATTN_HEADS
        for c in range(ATTN_HEADS // 2):
            if (2 * c) // heads_per_kv == 0:
                piece = jnp.where(lo, o_all[r + 2 * c:r + 2 * c + 1, :], o_swap[r + 2 * c + 1:r + 2 * c + 2, :])
            else:
                piece = jnp.where(lo, o_swap[r + 2 * c:r + 2 * c + 1, :], o_all[r + 2 * c + 1:r + 2 * c + 2, :])
            omix_ref[j:j + 1, c * LANES:(c + 1) * LANES] = piece

    for j in range(SAMPLE_BLK):
        cols = slice(j * LANES, (j + 1) * LANES)
        for h in range(GLA_HEADS):
            rs = slice(h * GLA_DK, (h + 1) * GLA_DK)
            qs = slice(GLA_QK_WIDTH + h * GLA_DK, GLA_QK_WIDTH + (h + 1) * GLA_DK)
            v_row = ps_ref[j:j + 1, C_VG + h * GLA_DV:C_VG + (h + 1) * GLA_DV]
            s_new = jnp.exp(la_b[rs, cols]) * st_ref[j, h] + kq_b[rs, cols] * v_row
            stout_ref[j, h] = s_new
            og_scr[j:j + 1, h * GLA_DV:(h + 1) * GLA_DV] = jnp.sum(
                kq_b[qs, cols] * s_new, axis=0, keepdims=True)
    for h in range(GLA_HEADS):
        hs = slice(h * GLA_DV, (h + 1) * GLA_DV)
        og = og_scr[:, hs]
        og = og * lax.rsqrt(jnp.mean(og * og, axis=-1, keepdims=True) + EPS) * gn_ref[...]
        rg = ps_ref[:, C_RG + h * GLA_DV:C_RG + (h + 1) * GLA_DV]
        omix_ref[:, ATTN_WIDTH + h * GLA_DV:ATTN_WIDTH + (h + 1) * GLA_DV] = og * (rg * _sigmoid(rg))


def _sample_mixer(ps, cache_k, cache_v, state, relbt, sinks_col, gn):
    nb = ps.shape[0]
    blk3 = lambda i: (i, 0, 0)
    blk4 = lambda i: (i, 0, 0, 0)
    return pl.pallas_call(
        _sample_mixer_kernel,
        grid=(nb // SAMPLE_BLK,),
        in_specs=[
            pl.BlockSpec((SAMPLE_BLK, P_WIDTH), lambda i: (i, 0)),
            _const_spec((nb, P_WIDTH)),
            pl.BlockSpec((SAMPLE_BLK, WINDOW, KV_WIDTH), blk3),
            pl.BlockSpec((SAMPLE_BLK, WINDOW, KV_WIDTH), blk3),
            pl.BlockSpec((SAMPLE_BLK, GLA_HEADS, GLA_DK, GLA_DV), blk4),
            _const_spec((ATTN_HEADS, N_BUCKETS)),
            _const_spec((ATTN_HEADS, 1)),
            _const_spec((1, WINDOW)),
            _const_spec((1, GLA_DV)),
        ],
        out_specs=[
            pl.BlockSpec((SAMPLE_BLK, MIX_WIDTH), lambda i: (i, 0)),
            pl.BlockSpec((SAMPLE_BLK, WINDOW, KV_WIDTH), blk3),
            pl.BlockSpec((SAMPLE_BLK, WINDOW, KV_WIDTH), blk3),
            pl.BlockSpec((SAMPLE_BLK, GLA_HEADS, GLA_DK, GLA_DV), blk4),
        ],
        out_shape=[
            jax.ShapeDtypeStruct((nb, MIX_WIDTH), F32),
            jax.ShapeDtypeStruct((nb, WINDOW, KV_WIDTH), F32),
            jax.ShapeDtypeStruct((nb, WINDOW, KV_WIDTH), F32),
            jax.ShapeDtypeStruct((nb, GLA_HEADS, GLA_DK, GLA_DV), F32),
        ],
        scratch_shapes=[
            pltpu.VMEM((3 * GLA_QK_WIDTH, nb), BF16),
            pltpu.VMEM((2 * GLA_QK_WIDTH, nb), BF16),
            pltpu.VMEM((ATTN_HEADS, WINDOW), F32),
            pltpu.VMEM((SAMPLE_BLK * ATTN_HEADS, WINDOW), F32),
            pltpu.VMEM((SAMPLE_BLK * ATTN_HEADS, KV_WIDTH), F32),
            pltpu.VMEM((SAMPLE_BLK, GLA_WIDTH), F32),
        ],
        compiler_params=pltpu.CompilerParams(
            dimension_semantics=("arbitrary",), vmem_limit_bytes=VMEM_LIMIT),
        name="sample_mixer",
    )(ps, ps, cache_k, cache_v, state, relbt, sinks_col, jnp.asarray(_BUCKET_SAMPLE), gn)


def _finish_kernel(x_ref, mix_ref, wo_ref, g_ref, wgu_ref, wd_ref, y_ref, *, ff_chunks):
    h = x_ref[...] + jnp.dot(mix_ref[...].astype(BF16), wo_ref[...], preferred_element_type=F32)
    r = lax.rsqrt(jnp.mean(h * h, axis=-1, keepdims=True) + EPS)
    z = (h * r * g_ref[...]).astype(BF16)
    n_tiles = wd_ref.shape[0] // MXU_TILE
    acc = h
    for c in range(ff_chunks):
        t0 = (c * n_tiles) // ff_chunks
        t1 = ((c + 1) * n_tiles) // ff_chunks
        gu = jnp.dot(z, wgu_ref[:, 2 * t0 * MXU_TILE:2 * t1 * MXU_TILE], preferred_element_type=F32)
        acts = []
        for t in range(t1 - t0):
            gate = gu[:, 2 * t * MXU_TILE:(2 * t + 1) * MXU_TILE]
            up = gu[:, (2 * t + 1) * MXU_TILE:(2 * t + 2) * MXU_TILE]
            acts.append(((gate * _sigmoid(gate)) * up).astype(BF16))
        act = jnp.concatenate(acts, axis=1)
        acc = acc + jnp.dot(act, wd_ref[t0 * MXU_TILE:t1 * MXU_TILE, :], preferred_element_type=F32)
    y_ref[...] = acc


def _interleave_gate_up(w_gate, w_up):
    d, d_ff = w_gate.shape
    assert d_ff % MXU_TILE == 0
    n_tiles = d_ff // MXU_TILE
    both = jnp.stack([w_gate.reshape(d, n_tiles, MXU_TILE), w_up.reshape(d, n_tiles, MXU_TILE)], axis=2)
    return both.reshape(d, 2 * d_ff)


def _finish(x, mix, wo, g_ffn, wgu, wd, rows):
    t = x.shape[0]
    d_ff = wd.shape[0]
    return pl.pallas_call(
        functools.partial(_finish_kernel, ff_chunks=FF_CHUNKS),
        grid=(t // rows,),
        in_specs=[
            pl.BlockSpec((rows, D_MODEL), lambda i: (i, 0)),
            pl.BlockSpec((rows, MIX_WIDTH), lambda i: (i, 0)),
            _const_spec((MIX_WIDTH, D_MODEL)),
            _const_spec((1, D_MODEL)),
            _const_spec((D_MODEL, 2 * d_ff)),
            _const_spec((d_ff, D_MODEL)),
        ],
        out_specs=pl.BlockSpec((rows, D_MODEL), lambda i: (i, 0)),
        out_shape=jax.ShapeDtypeStruct((t, D_MODEL), F32),
        compiler_params=pltpu.CompilerParams(
            dimension_semantics=("arbitrary",), vmem_limit_bytes=VMEM_LIMIT),
        name="finish",
    )(x, mix, wo, g_ffn, wgu, wd)


PROMPT_ROWS = 512


def kernel(x_prompt, x_sample, cache_k, cache_v, state_gla, attn_norm_g, w_in, q_norm_g, k_norm_g,
           attn_sinks, rel_bias, w_gla_gate2, b_gla_gate, gla_norm_g, w_o, ffn_norm_g, w_gate, w_up,
           w_down):
    depth = w_in.shape[0]
    batch, seq, _ = x_prompt.shape
    dec_batch, dec_seq, _ = x_sample.shape
    wb = cache_k.shape[2]
    assert batch == 1 and dec_seq == 1 and wb == WINDOW and seq % PROMPT_ROWS == 0
    assert dec_batch % SAMPLE_BLK == 0 and dec_batch % LANES == 0
    assert rel_bias.shape == (N_BUCKETS, ATTN_HEADS)

    xp = x_prompt.reshape(seq, D_MODEL)
    xs = x_sample.reshape(dec_batch, D_MODEL)
    relb_flat = rel_bias.reshape(-1)
    relb_t = rel_bias.T
    outs = ([], [], [], [], [], [])
    for l in range(depth):
        kv_end = ATTN_WIDTH + 2 * KV_WIDTH
        w_in_p = jnp.concatenate(
            [w_in[l, :, :kv_end],
             jnp.pad(w_in[l, :, MAIN_WIDTH:], ((0, 0), (0, RANK_PAD - GLA_RANK))),
             w_in[l, :, kv_end:MAIN_WIDTH]], axis=1).astype(BF16)
        w2p = jnp.pad(w_gla_gate2[l], ((0, RANK_PAD - GLA_RANK), (0, 0))).astype(BF16)
        proj_w = (attn_norm_g[l][None, :], w_in_p, jnp.tile(q_norm_g[l], ATTN_HEADS)[None, :],
                  jnp.tile(k_norm_g[l], ATTN_KV_HEADS)[None, :], w2p, b_gla_gate[l][None, :])
        fin_w = (w_o[l].astype(BF16), ffn_norm_g[l][None, :],
                 _interleave_gate_up(w_gate[l], w_up[l]).astype(BF16), w_down[l].astype(BF16))
        gn = gla_norm_g[l][None, :]

        pp = _project(xp, *proj_w, rows=PROMPT_ROWS, block_cumsum=True)
        mix_p, st_p = _prompt_mixer(pp, relb_flat, attn_sinks[l], gn)
        xp = _finish(xp, mix_p, *fin_w, rows=PROMPT_ROWS)
        outs[0].append(pp[seq - wb:, C_KA:C_VA].reshape(batch, wb, ATTN_KV_HEADS, HEAD_DIM))
        outs[1].append(pp[seq - wb:, C_VA:C_QG].reshape(batch, wb, ATTN_KV_HEADS, HEAD_DIM))
        outs[2].append(st_p.T.reshape(batch, GLA_HEADS, GLA_DK, GLA_DV).astype(state_gla.dtype))

        ps = _project(xs, *proj_w, rows=dec_batch, block_cumsum=False)
        mix_s, kwin, vwin, st_s = _sample_mixer(
            ps, cache_k[l].reshape(dec_batch, wb, KV_WIDTH), cache_v[l].reshape(dec_batch, wb, KV_WIDTH),
            state_gla[l].astype(F32), relb_t, attn_sinks[l][:, None], gn)
        xs = _finish(xs, mix_s, *fin_w, rows=dec_batch)
        outs[3].append(kwin.reshape(dec_batch, wb, ATTN_KV_HEADS, HEAD_DIM))
        outs[4].append(vwin.reshape(dec_batch, wb, ATTN_KV_HEADS, HEAD_DIM))
        outs[5].append(st_s.astype(state_gla.dtype))

    y_prompt = xp.reshape(batch, seq, D_MODEL)
    y_sample = xs.reshape(dec_batch, dec_seq, D_MODEL)
    return (y_prompt, y_sample) + tuple(jnp.stack(o) for o in outs)
```

```python
import functools
import math

import numpy as np
import jax
import jax.numpy as jnp
from jax import lax
from jax.experimental import pallas as pl
from jax.experimental.pallas import tpu as pltpu

F32 = jnp.float32
BF16 = jnp.bfloat16

D_MODEL = 1024
HEAD_DIM = 64
ATTN_HEADS = 8
ATTN_KV_HEADS = 2
WINDOW = 128
N_BUCKETS = 32
MAX_DISTANCE = 128
GLA_HEADS = 4
GLA_DK = 64
GLA_DV = 128
GLA_RANK = 16
GLA_TAU = 16.0
EPS = 1e-6
ATTN_WIDTH = ATTN_HEADS * HEAD_DIM
KV_WIDTH = ATTN_KV_HEADS * HEAD_DIM
GLA_QK_WIDTH = GLA_HEADS * GLA_DK
GLA_WIDTH = GLA_HEADS * GLA_DV
MIX_WIDTH = ATTN_WIDTH + GLA_WIDTH
MAIN_WIDTH = ATTN_WIDTH + 2 * KV_WIDTH + 2 * GLA_QK_WIDTH + 2 * GLA_WIDTH
LANES = 128
SUBLANES = 8
MXU_TILE = 256
FF_CHUNKS = 2
RANK_PAD = LANES
IN_PAD_WIDTH = MAIN_WIDTH + RANK_PAD

C_QA = 0
C_KA = C_QA + ATTN_WIDTH
C_VA = C_KA + KV_WIDTH
C_QG = C_VA + KV_WIDTH
C_KG = C_QG + GLA_QK_WIDTH
C_VG = C_KG + GLA_QK_WIDTH
C_RG = C_VG + GLA_WIDTH
C_LA = C_RG + GLA_WIDTH
P_WIDTH = C_LA + GLA_QK_WIDTH

W_QA = 0
W_KV = W_QA + ATTN_WIDTH
W_QG = W_KV + 2 * KV_WIDTH + RANK_PAD
W_VG = W_QG + 2 * GLA_QK_WIDTH
W_RG = W_VG + GLA_WIDTH

BLK = 128
N_LEVELS = 7
NEG = -1e30
ATTN_SCALE = HEAD_DIM ** -0.5
SAMPLE_BLK = 8
VMEM_LIMIT = 56 * 1024 * 1024


def _t5_bucket_np(dist):
    n = np.maximum(dist, 0)
    max_exact = N_BUCKETS // 2
    nf = np.maximum(n, 1).astype(np.float64)
    large = max_exact + (np.log(nf / max_exact) / math.log(MAX_DISTANCE / max_exact)
                         * (N_BUCKETS - max_exact)).astype(np.int32)
    large = np.minimum(large, N_BUCKETS - 1)
    return np.where(n < max_exact, n, large).astype(np.int32)


def _prompt_bucket_tables():
    i = np.arange(BLK)[:, None]
    j = np.arange(2 * BLK)[None, :]
    dist = BLK + i - j
    band = (dist >= 0) & (dist < WINDOW)
    bucket = _t5_bucket_np(dist)
    t0 = np.where(band, bucket, -1)
    t1 = np.where(band & (j >= BLK), bucket, -1)
    return np.stack([t0, t1]).astype(np.int32)


def _level_tables():
    t = np.arange(BLK)[:, None]
    s = np.arange(BLK)[None, :]
    x = t ^ s
    lev = np.where(x > 0, np.floor(np.log2(np.maximum(x, 1))).astype(np.int32) + 1, 0)
    lev = np.where(s > t, -1, lev).astype(np.int32)
    tri = (s <= t).astype(np.float32)
    return lev, np.concatenate([tri, tri, tri], axis=1)


_BUCKET_PROMPT = _prompt_bucket_tables()
_LEV, _TRI3 = _level_tables()
_BUCKET_SAMPLE = _t5_bucket_np((WINDOW - 1) - np.arange(WINDOW))[None, :].astype(np.int32)


def _nt_dot(a, b):
    return lax.dot_general(a, b, (((1,), (1,)), ((), ())), preferred_element_type=F32)


def _tn_dot(a, b):
    return lax.dot_general(a, b, (((0,), (0,)), ((), ())), preferred_element_type=F32)


def _head_mean_sq(x):
    lo = lax.broadcasted_iota(jnp.int32, (x.shape[0], LANES), 1) < HEAD_DIM
    outs = []
    for c in range(x.shape[1] // LANES):
        y = x[:, c * LANES:(c + 1) * LANES]
        y = y * y
        s_lo = jnp.sum(jnp.where(lo, y, 0.0), axis=-1, keepdims=True)
        s_hi = jnp.sum(jnp.where(lo, 0.0, y), axis=-1, keepdims=True)
        outs.append(jnp.where(lo, s_lo, s_hi) * (1.0 / HEAD_DIM))
    return outs[0] if len(outs) == 1 else jnp.concatenate(outs, axis=1)


def _sigmoid(x):
    return 1.0 / (1.0 + jnp.exp(-x))


def _split3_rows(x):
    hi = x.astype(BF16)
    r1 = x - hi.astype(F32)
    mid = r1.astype(BF16)
    lo = (r1 - mid.astype(F32)).astype(BF16)
    return jnp.concatenate([hi, mid, lo], axis=0)


def _proj_kernel(x_ref, g_ref, w_ref, qn_ref, kn_ref, w2_ref, b2_ref, tri_ref, out_ref, *,
                 block_cumsum):
    x = x_ref[...]
    r = lax.rsqrt(jnp.mean(x * x, axis=-1, keepdims=True) + EPS)
    n = (x * r * g_ref[...]).astype(BF16)

    def seg(c0, c1):
        return jnp.dot(n, w_ref[:, c0:c1], preferred_element_type=F32)

    q = seg(W_QA, W_KV)
    out_ref[:, C_QA:C_KA] = q * lax.rsqrt(_head_mean_sq(q) + EPS) * qn_ref[...] * ATTN_SCALE
    kvl = seg(W_KV, W_QG)
    k = kvl[:, :KV_WIDTH]
    out_ref[:, C_KA:C_VA] = k * lax.rsqrt(_head_mean_sq(k) + EPS) * kn_ref[...]
    out_ref[:, C_VA:C_QG] = kvl[:, KV_WIDTH:2 * KV_WIDTH]
    lr = kvl[:, 2 * KV_WIDTH:].astype(BF16)
    qk_g = seg(W_QG, W_VG)
    out_ref[:, C_QG:C_KG] = qk_g[:, :GLA_QK_WIDTH] * (GLA_DK ** -0.5)
    out_ref[:, C_KG:C_VG] = qk_g[:, GLA_QK_WIDTH:]
    out_ref[:, C_VG:C_RG] = seg(W_VG, W_RG)
    out_ref[:, C_RG:C_LA] = seg(W_RG, IN_PAD_WIDTH)
    z = jnp.dot(lr, w2_ref[...], preferred_element_type=F32) + b2_ref[...]
    log_a = (jnp.minimum(z, 0.0) - jnp.log1p(jnp.exp(-jnp.abs(z)))) / GLA_TAU
    if block_cumsum:
        for blk in range(x.shape[0] // BLK):
            rows = slice(blk * BLK, (blk + 1) * BLK)
            out_ref[rows, C_LA:P_WIDTH] = jnp.dot(
                tri_ref[...], _split3_rows(log_a[rows]), preferred_element_type=F32)
    else:
        out_ref[:, C_LA:P_WIDTH] = log_a


def _const_spec(shape):
    nd = len(shape)
    return pl.BlockSpec(shape, lambda i: (0,) * nd, pipeline_mode=pl.Buffered(1))


def _project(x, g_attn, w_in_p, qn, kn, w2p, b2, rows, block_cumsum):
    t = x.shape[0]
    return pl.pallas_call(
        functools.partial(_proj_kernel, block_cumsum=block_cumsum),
        grid=(t // rows,),
        in_specs=[
            pl.BlockSpec((rows, D_MODEL), lambda i: (i, 0)),
            _const_spec((1, D_MODEL)),
            _const_spec((D_MODEL, IN_PAD_WIDTH)),
            _const_spec((1, ATTN_WIDTH)),
            _const_spec((1, KV_WIDTH)),
            _const_spec((RANK_PAD, GLA_QK_WIDTH)),
            _const_spec((1, GLA_QK_WIDTH)),
            _const_spec((BLK, 3 * BLK)),
        ],
        out_specs=pl.BlockSpec((rows, P_WIDTH), lambda i: (i, 0)),
        out_shape=jax.ShapeDtypeStruct((t, P_WIDTH), F32),
        compiler_params=pltpu.CompilerParams(
            dimension_semantics=("arbitrary",), vmem_limit_bytes=VMEM_LIMIT),
        name="proj",
    )(x, g_attn, w_in_p, qn, kn, w2p, b2, jnp.asarray(_TRI3, BF16))


def _boundary_rows(b_ref, r0, c0, b, row, level):
    m = 1 << (level - 1)
    if 2 * m >= SUBLANES:
        pieces = [jnp.broadcast_to(b_ref[r0 + g * 2 * m + m - 1:r0 + g * 2 * m + m, c0:c0 + LANES],
                                   (2 * m, LANES))
                  for g in range(BLK // (2 * m))]
        return pieces[0] if len(pieces) == 1 else jnp.concatenate(pieces, axis=0)
    pos = row & (2 * m - 1)
    out = b
    for p in range(2 * m):
        shift = (m - 1) - p
        if shift != 0:
            out = jnp.where(pos == p, pltpu.roll(b, (-shift) % BLK, 0), out)
    return out


def _mixer_init(relb_ref, bucket_ref, kbd, vbd, st_scr, mb_scr):
    kbd[...] = jnp.zeros_like(kbd)
    vbd[...] = jnp.zeros_like(vbd)
    st_scr[...] = jnp.zeros_like(st_scr)
    bk = bucket_ref[0]
    acc = [jnp.zeros(bk.shape, F32) for _ in range(ATTN_HEADS)]
    for b in range(N_BUCKETS):
        hit = bk == b
        for h in range(ATTN_HEADS):
            acc[h] = jnp.where(hit, relb_ref[b * ATTN_HEADS + h], acc[h])
    for tb in range(2):
        masked = bucket_ref[tb] < 0
        for h in range(ATTN_HEADS):
            mb_scr[tb, h // 2, :, (h % 2) * 2 * BLK:(h % 2 + 1) * 2 * BLK] = (
                jnp.where(masked, NEG, acc[h]))


def _mixer_block(p_ref, r0, table, sink_ref, lev_ref, gn_ref, omix_ref, st_ref,
                 kbd, vbd, st_scr, mb_scr):
    rows = slice(r0, r0 + BLK)
    lo1 = lax.broadcasted_iota(jnp.int32, (BLK, LANES), 1) < HEAD_DIM
    lo_bf = jnp.where(lo1, 1.0, 0.0).astype(BF16)
    hi_bf = jnp.where(lo1, 0.0, 1.0).astype(BF16)

    for src_c, dst in ((C_KA, kbd), (C_VA, vbd)):
        cur = p_ref[rows,src_c:src_c + KV_WIDTH]
        swapped = pltpu.roll(cur, HEAD_DIM, 1)
        dst[0, BLK:2 * BLK] = jnp.where(lo1, cur, 0.0).astype(BF16)
        dst[0, 3 * BLK:4 * BLK] = jnp.where(lo1, 0.0, swapped).astype(BF16)
        dst[1, BLK:2 * BLK] = jnp.where(lo1, swapped, 0.0).astype(BF16)
        dst[1, 3 * BLK:4 * BLK] = jnp.where(lo1, 0.0, cur).astype(BF16)
    heads_per_kv = ATTN_HEADS // ATTN_KV_HEADS
    for j in range(ATTN_HEADS // 2):
        g = (2 * j) // heads_per_kv
        qp = p_ref[rows,C_QA + j * LANES:C_QA + (j + 1) * LANES].astype(BF16)
        s = _nt_dot(qp, kbd[g]) + mb_scr[table, j]
        probs, inv = [], []
        for e in range(2):
            se = s[:, e * 2 * BLK:(e + 1) * 2 * BLK]
            sk = sink_ref[2 * j + e]
            m = jnp.maximum(jnp.max(se, axis=-1, keepdims=True), sk)
            pe = jnp.exp(se - m)
            den = jnp.sum(pe, axis=-1, keepdims=True) + jnp.exp(sk - m)
            probs.append(pe.astype(BF16))
            inv.append(1.0 / den)
        o = jnp.dot(jnp.concatenate(probs, axis=1), vbd[g], preferred_element_type=F32)
        o = o * jnp.where(lo1, inv[0], inv[1])
        omix_ref[rows,j * LANES:(j + 1) * LANES] = o.astype(omix_ref.dtype)
    for dst in (kbd, vbd):
        for g in range(ATTN_KV_HEADS):
            dst[g, 0:BLK] = dst[g, BLK:2 * BLK]
            dst[g, 2 * BLK:3 * BLK] = dst[g, 3 * BLK:4 * BLK]

    row = lax.broadcasted_iota(jnp.int32, (BLK, LANES), 0)
    zero_blk = jnp.zeros((BLK, LANES), BF16)
    for c in range(GLA_HEADS // 2):
        c0 = c * LANES
        q_at = lambda a, z: p_ref[r0 + a:r0 + z, C_QG + c0:C_QG + c0 + LANES]
        k_at = lambda a, z: p_ref[r0 + a:r0 + z, C_KG + c0:C_KG + c0 + LANES]
        b_at = lambda a, z: p_ref[r0 + a:r0 + z, C_LA + c0:C_LA + c0 + LANES]

        def pair_scores(qtb, ktb):
            rhs = jnp.concatenate([ktb * lo_bf, ktb * hi_bf], axis=0)
            return _nt_dot(qtb, rhs)

        s0 = pair_scores(q_at(0, BLK).astype(BF16), k_at(0, BLK).astype(BF16))
        sc = [jnp.where(lev_ref[...] == 0, s0[:, e * BLK:(e + 1) * BLK], 0.0) for e in range(2)]
        for level in range(1, N_LEVELS + 1):
            m = 1 << (level - 1)
            if m >= SUBLANES:
                qs, ks = [], []
                zeros = jnp.zeros((m, LANES), BF16)
                for g in range(BLK // (2 * m)):
                    lo_a, up_a, up_z = g * 2 * m, g * 2 * m + m, (g + 1) * 2 * m
                    rb = jnp.broadcast_to(b_at(up_a - 1, up_a), (m, LANES))
                    qs += [zeros, (q_at(up_a, up_z) * jnp.exp(b_at(up_a, up_z) - rb)).astype(BF16)]
                    ks += [(k_at(lo_a, up_a) * jnp.exp(rb - b_at(lo_a, up_a))).astype(BF16), zeros]
                qtb = jnp.concatenate(qs, axis=0)
                ktb = jnp.concatenate(ks, axis=0)
            else:
                bc = b_at(0, BLK)
                d = bc - _boundary_rows(p_ref, r0, C_LA + c0, bc, row, level)
                upper = ((row >> (level - 1)) & 1) == 1
                qtb = (q_at(0, BLK) * jnp.exp(jnp.where(upper, d, NEG))).astype(BF16)
                ktb = (k_at(0, BLK) * jnp.exp(jnp.where(upper, NEG, -d))).astype(BF16)
            sl = pair_scores(qtb, ktb)
            sc = [jnp.where(lev_ref[...] == level, sl[:, e * BLK:(e + 1) * BLK], sc[e])
                  for e in range(2)]
        sc = jnp.concatenate(sc, axis=1)
        qc, kc, bc = q_at(0, BLK), k_at(0, BLK), b_at(0, BLK)

        b_last = bc[BLK - 1:BLK, :]
        v0 = p_ref[rows,C_VG + 2 * c0:C_VG + 2 * c0 + LANES].astype(BF16)
        v1 = p_ref[rows,C_VG + 2 * c0 + LANES:C_VG + 2 * c0 + 2 * LANES].astype(BF16)
        v_bd = jnp.concatenate([jnp.concatenate([v0, zero_blk], axis=1),
                                jnp.concatenate([zero_blk, v1], axis=1)], axis=0)
        st_c = st_scr[:, c0:c0 + LANES]
        stb = st_c.astype(BF16)
        st_rhs = jnp.concatenate([stb * lo_bf, stb * hi_bf], axis=0)
        o = (jnp.dot(sc.astype(BF16), v_bd, preferred_element_type=F32)
             + _nt_dot((qc * jnp.exp(bc)).astype(BF16), st_rhs))
        kd = (kc * jnp.exp(b_last - bc)).astype(BF16)
        upd = _tn_dot(jnp.concatenate([v0, v1], axis=1), kd)
        new_st = st_c * jnp.exp(b_last) + jnp.where(lo1, upd[:BLK], upd[BLK:])
        st_scr[:, c0:c0 + LANES] = new_st
        st_ref[:, c0:c0 + LANES] = new_st
        for e in range(2):
            h = 2 * c + e
            oh = o[:, e * LANES:(e + 1) * LANES]
            og = oh * lax.rsqrt(jnp.mean(oh * oh, axis=-1, keepdims=True) + EPS) * gn_ref[...]
            rg = p_ref[rows,C_RG + h * GLA_DV:C_RG + (h + 1) * GLA_DV]
            gated = og * (rg * _sigmoid(rg))
            omix_ref[rows,ATTN_WIDTH + h * GLA_DV:ATTN_WIDTH + (h + 1) * GLA_DV] = (
                gated.astype(omix_ref.dtype))


def _prompt_mixer_kernel(relb_ref, sink_ref, p_ref, bucket_ref, lev_ref, gn_ref,
                         omix_ref, st_ref, kbd, vbd, st_scr, mb_scr):
    i = pl.program_id(0)

    @pl.when(i == 0)
    def _init():
        _mixer_init(relb_ref, bucket_ref, kbd, vbd, st_scr, mb_scr)

    _mixer_block(p_ref, 0, jnp.where(i == 0, 1, 0), sink_ref, lev_ref, gn_ref, omix_ref, st_ref,
                 kbd, vbd, st_scr, mb_scr)


def _prompt_mixer(p, relb, sinks, gn):
    t = p.shape[0]
    smem = pl.BlockSpec(memory_space=pltpu.SMEM)
    return pl.pallas_call(
        _prompt_mixer_kernel,
        grid=(t // BLK,),
        in_specs=[
            smem, smem,
            pl.BlockSpec((BLK, P_WIDTH), lambda i: (i, 0)),
            _const_spec((2, BLK, 2 * BLK)),
            _const_spec((BLK, BLK)),
            _const_spec((1, GLA_DV)),
        ],
        out_specs=[
            pl.BlockSpec((BLK, MIX_WIDTH), lambda i: (i, 0)),
            pl.BlockSpec((GLA_DV, GLA_QK_WIDTH), lambda i: (0, 0)),
        ],
        out_shape=[
            jax.ShapeDtypeStruct((t, MIX_WIDTH), BF16),
            jax.ShapeDtypeStruct((GLA_DV, GLA_QK_WIDTH), F32),
        ],
        scratch_shapes=[
            pltpu.VMEM((ATTN_KV_HEADS, 4 * BLK, KV_WIDTH), BF16),
            pltpu.VMEM((ATTN_KV_HEADS, 4 * BLK, KV_WIDTH), BF16),
            pltpu.VMEM((GLA_DV, GLA_QK_WIDTH), F32),
            pltpu.VMEM((2, ATTN_HEADS // 2, BLK, 4 * BLK), F32),
        ],
        compiler_params=pltpu.CompilerParams(
            dimension_semantics=("arbitrary",), vmem_limit_bytes=VMEM_LIMIT),
        name="prompt_mixer",
    )(relb, sinks, p, jnp.asarray(_BUCKET_PROMPT), jnp.asarray(_LEV), gn)


def _sample_mixer_kernel(ps_ref, pfull_ref, ck_ref, cv_ref, st_ref, relbt_ref, sink_ref, bucket_ref,
                         gn_ref, omix_ref, kwin_ref, vwin_ref, stout_ref, lat_scr, kqt_scr,
                         bias_scr, s_scr, o_scr, og_scr):
    i = pl.program_id(0)
    nb = pfull_ref.shape[0]

    @pl.when(i == 0)
    def _init():
        lat_scr[...] = _split3_rows(pfull_ref[:, C_LA:P_WIDTH].T)
        kqt_scr[0:GLA_QK_WIDTH] = pfull_ref[:, C_KG:C_VG].T.astype(BF16)
        kqt_scr[GLA_QK_WIDTH:2 * GLA_QK_WIDTH] = pfull_ref[:, C_QG:C_KG].T.astype(BF16)
        bk = jnp.broadcast_to(bucket_ref[...], (ATTN_HEADS, WINDOW))
        acc = jnp.zeros((ATTN_HEADS, WINDOW), F32)
        for b in range(N_BUCKETS):
            acc = jnp.where(bk == b, relbt_ref[:, b:b + 1], acc)
        bias_scr[...] = acc

    lo = lax.broadcasted_iota(jnp.int32, (1, LANES), 1) < HEAD_DIM
    sub = lax.broadcasted_iota(jnp.int32, (ATTN_HEADS, LANES), 0)
    wrow = lax.broadcasted_iota(jnp.int32, (WINDOW, KV_WIDTH), 0)
    heads_per_kv = ATTN_HEADS // ATTN_KV_HEADS

    n_of_col = i * SAMPLE_BLK + lax.broadcasted_iota(jnp.int32, (nb, SAMPLE_BLK * LANES), 1) // LANES
    pick = jnp.where(lax.broadcasted_iota(jnp.int32, (nb, SAMPLE_BLK * LANES), 0) == n_of_col,
                     1.0, 0.0).astype(BF16)
    la_b = (jnp.dot(lat_scr[0:GLA_QK_WIDTH], pick, preferred_element_type=F32)
            + jnp.dot(lat_scr[GLA_QK_WIDTH:2 * GLA_QK_WIDTH], pick, preferred_element_type=F32)
            + jnp.dot(lat_scr[2 * GLA_QK_WIDTH:3 * GLA_QK_WIDTH], pick, preferred_element_type=F32))
    kq_b = jnp.dot(kqt_scr[...], pick, preferred_element_type=F32)

    for j in range(SAMPLE_BLK):
        k_new = ps_ref[j:j + 1, C_KA:C_VA]
        v_new = ps_ref[j:j + 1, C_VA:C_QG]
        kwin_ref[j] = jnp.where(wrow == WINDOW - 1, k_new, pltpu.roll(ck_ref[j], WINDOW - 1, 0))
        vwin_ref[j] = jnp.where(wrow == WINDOW - 1, v_new, pltpu.roll(cv_ref[j], WINDOW - 1, 0))

    for j in range(SAMPLE_BLK):
        qexp = jnp.zeros((ATTN_HEADS, LANES), F32)
        for c in range(ATTN_HEADS // 2):
            chunk = ps_ref[j:j + 1, C_QA + c * LANES:C_QA + (c + 1) * LANES]
            swapped = pltpu.roll(chunk, HEAD_DIM, 1)
            if (2 * c) // heads_per_kv == 0:
                rows = (jnp.where(lo, chunk, 0.0), jnp.where(lo, swapped, 0.0))
            else:
                rows = (jnp.where(lo, 0.0, swapped), jnp.where(lo, 0.0, chunk))
            for e in range(2):
                qexp = jnp.where(sub == 2 * c + e, rows[e], qexp)
        s_scr[j * ATTN_HEADS:(j + 1) * ATTN_HEADS] = _nt_dot(qexp.astype(BF16), kwin_ref[j].astype(BF16))

    tile = lambda x: jnp.concatenate([x] * SAMPLE_BLK, axis=0)
    sink = tile(sink_ref[...])
    s = s_scr[...] + tile(bias_scr[...])
    m = jnp.maximum(jnp.max(s, axis=-1, keepdims=True), sink)
    pe = jnp.exp(s - m)
    inv_den = 1.0 / (jnp.sum(pe, axis=-1, keepdims=True) + jnp.exp(sink - m))
    peb = pe.astype(BF16)
    for j in range(SAMPLE_BLK):
        o_scr[j * ATTN_HEADS:(j + 1) * ATTN_HEADS] = jnp.dot(
            peb[j * ATTN_HEADS:(j + 1) * ATTN_HEADS], vwin_ref[j].astype(BF16),
            preferred_element_type=F32)
    o_all = o_scr[...] * inv_den
    o_swap = pltpu.roll(o_all, HEAD_DIM, 1)
    for j in range(SAMPLE_BLK):
        r = j * ATTN_HEADS
        for c in range(ATTN_HEADS // 2):
            if (2 * c) // heads_per_kv == 0:
                piece = jnp.where(lo, o_all[r + 2 * c:r + 2 * c + 1, :], o_swap[r + 2 * c + 1:r + 2 * c + 2, :])
            else:
                piece = jnp.where(lo, o_swap[r + 2 * c:r + 2 * c + 1, :], o_all[r + 2 * c + 1:r + 2 * c + 2, :])
            omix_ref[j:j + 1, c * LANES:(c + 1) * LANES] = piece

    for j in range(SAMPLE_BLK):
        cols = slice(j * LANES, (j + 1) * LANES)
        for h in range(GLA_HEADS):
            rs = slice(h * GLA_DK, (h + 1) * GLA_DK)
            qs = slice(GLA_QK_WIDTH + h * GLA_DK, GLA_QK_WIDTH + (h + 1) * GLA_DK)
            v_row = ps_ref[j:j + 1, C_VG + h * GLA_DV:C_VG + (h + 1) * GLA_DV]
            s_new = jnp.exp(la_b[rs, cols]) * st_ref[j, h] + kq_b[rs, cols] * v_row
            stout_ref[j, h] = s_new
            og_scr[j:j + 1, h * GLA_DV:(h + 1) * GLA_DV] = jnp.sum(
                kq_b[qs, cols] * s_new, axis=0, keepdims=True)
    for h in range(GLA_HEADS):
        hs = slice(h * GLA_DV, (h + 1) * GLA_DV)
        og = og_scr[:, hs]
        og = og * lax.rsqrt(jnp.mean(og * og, axis=-1, keepdims=True) + EPS) * gn_ref[...]
        rg = ps_ref[:, C_RG + h * GLA_DV:C_RG + (h + 1) * GLA_DV]
        omix_ref[:, ATTN_WIDTH + h * GLA_DV:ATTN_WIDTH + (h + 1) * GLA_DV] = og * (rg * _sigmoid(rg))


def _sample_mixer(ps, cache_k, cache_v, state, relbt, sinks_col, gn):
    nb = ps.shape[0]
    blk3 = lambda i: (i, 0, 0)
    blk4 = lambda i: (i, 0, 0, 0)
    return pl.pallas_call(
        _sample_mixer_kernel,
        grid=(nb // SAMPLE_BLK,),
        in_specs=[
            pl.BlockSpec((SAMPLE_BLK, P_WIDTH), lambda i: (i, 0)),
            _const_spec((nb, P_WIDTH)),
            pl.BlockSpec((SAMPLE_BLK, WINDOW, KV_WIDTH), blk3),
            pl.BlockSpec((SAMPLE_BLK, WINDOW, KV_WIDTH), blk3),
            pl.BlockSpec((SAMPLE_BLK, GLA_HEADS, GLA_DK, GLA_DV), blk4),
            _const_spec((ATTN_HEADS, N_BUCKETS)),
            _const_spec((ATTN_HEADS, 1)),
            _const_spec((1, WINDOW)),
            _const_spec((1, GLA_DV)),
        ],
        out_specs=[
            pl.BlockSpec((SAMPLE_BLK, MIX_WIDTH), lambda i: (i, 0)),
            pl.BlockSpec((SAMPLE_BLK, WINDOW, KV_WIDTH), blk3),
            pl.BlockSpec((SAMPLE_BLK, WINDOW, KV_WIDTH), blk3),
            pl.BlockSpec((SAMPLE_BLK, GLA_HEADS, GLA_DK, GLA_DV), blk4),
        ],
        out_shape=[
            jax.ShapeDtypeStruct((nb, MIX_WIDTH), F32),
            jax.ShapeDtypeStruct((nb, WINDOW, KV_WIDTH), F32),
            jax.ShapeDtypeStruct((nb, WINDOW, KV_WIDTH), F32),
            jax.ShapeDtypeStruct((nb, GLA_HEADS, GLA_DK, GLA_DV), F32),
        ],
        scratch_shapes=[
            pltpu.VMEM((3 * GLA_QK_WIDTH, nb), BF16),
            pltpu.VMEM((2 * GLA_QK_WIDTH, nb), BF16),
            pltpu.VMEM((ATTN_HEADS, WINDOW), F32),
            pltpu.VMEM((SAMPLE_BLK * ATTN_HEADS, WINDOW), F32),
            pltpu.VMEM((SAMPLE_BLK * ATTN_HEADS, KV_WIDTH), F32),
            pltpu.VMEM((SAMPLE_BLK, GLA_WIDTH), F32),
        ],
        compiler_params=pltpu.CompilerParams(
            dimension_semantics=("arbitrary",), vmem_limit_bytes=VMEM_LIMIT),
        name="sample_mixer",
    )(ps, ps, cache_k, cache_v, state, relbt, sinks_col, jnp.asarray(_BUCKET_SAMPLE), gn)


def _finish_kernel(x_ref, mix_ref, wo_ref, g_ref, wg_ref, wu_ref, wd_ref, y_ref, *, ff_chunks):
    h = x_ref[...] + jnp.dot(mix_ref[...].astype(BF16), wo_ref[...], preferred_element_type=F32)
    r = lax.rsqrt(jnp.mean(h * h, axis=-1, keepdims=True) + EPS)
    z = (h * r * g_ref[...]).astype(BF16)
    n_tiles = wd_ref.shape[0] // MXU_TILE
    acc = h
    for c in range(ff_chunks):
        c0 = ((c * n_tiles) // ff_chunks) * MXU_TILE
        c1 = (((c + 1) * n_tiles) // ff_chunks) * MXU_TILE
        gate = jnp.dot(z, wg_ref[:, c0:c1], preferred_element_type=F32)
        up = jnp.dot(z, wu_ref[:, c0:c1], preferred_element_type=F32)
        act = ((gate * _sigmoid(gate)) * up).astype(BF16)
        acc = acc + jnp.dot(act, wd_ref[c0:c1, :], preferred_element_type=F32)
    y_ref[...] = acc


def _finish(x, mix, wo, g_ffn, wg, wu, wd, rows):
    t = x.shape[0]
    d_ff = wd.shape[0]
    assert d_ff % MXU_TILE == 0
    return pl.pallas_call(
        functools.partial(_finish_kernel, ff_chunks=FF_CHUNKS),
        grid=(t // rows,),
        in_specs=[
            pl.BlockSpec((rows, D_MODEL), lambda i: (i, 0)),
            pl.BlockSpec((rows, MIX_WIDTH), lambda i: (i, 0)),
            _const_spec((MIX_WIDTH, D_MODEL)),
            _const_spec((1, D_MODEL)),
            _const_spec((D_MODEL, d_ff)),
            _const_spec((D_MODEL, d_ff)),
            _const_spec((d_ff, D_MODEL)),
        ],
        out_specs=pl.BlockSpec((rows, D_MODEL), lambda i: (i, 0)),
        out_shape=jax.ShapeDtypeStruct((t, D_MODEL), F32),
        compiler_params=pltpu.CompilerParams(
            dimension_semantics=("arbitrary",), vmem_limit_bytes=VMEM_LIMIT),
        name="finish",
    )(x, mix, wo, g_ffn, wg, wu, wd)


PROMPT_ROWS = 512


def kernel(x_prompt, x_sample, cache_k, cache_v, state_gla, attn_norm_g, w_in, q_norm_g, k_norm_g,
           attn_sinks, rel_bias, w_gla_gate2, b_gla_gate, gla_norm_g, w_o, ffn_norm_g, w_gate, w_up,
           w_down):
    depth = w_in.shape[0]
    batch, seq, _ = x_prompt.shape
    dec_batch, dec_seq, _ = x_sample.shape
    wb = cache_k.shape[2]
    assert batch == 1 and dec_seq == 1 and wb == WINDOW and seq % PROMPT_ROWS == 0
    assert dec_batch % SAMPLE_BLK == 0 and dec_batch % LANES == 0
    assert rel_bias.shape == (N_BUCKETS, ATTN_HEADS)

    xp = x_prompt.reshape(seq, D_MODEL)
    xs = x_sample.reshape(dec_batch, D_MODEL)
    relb_flat = rel_bias.reshape(-1)
    relb_t = rel_bias.T
    outs = ([], [], [], [], [], [])
    for l in range(depth):
        kv_end = ATTN_WIDTH + 2 * KV_WIDTH
        w_in_p = jnp.concatenate(
            [w_in[l, :, :kv_end],
             jnp.pad(w_in[l, :, MAIN_WIDTH:], ((0, 0), (0, RANK_PAD - GLA_RANK))),
             w_in[l, :, kv_end:MAIN_WIDTH]], axis=1).astype(BF16)
        w2p = jnp.pad(w_gla_gate2[l], ((0, RANK_PAD - GLA_RANK), (0, 0))).astype(BF16)
        proj_w = (attn_norm_g[l][None, :], w_in_p, jnp.tile(q_norm_g[l], ATTN_HEADS)[None, :],
                  jnp.tile(k_norm_g[l], ATTN_KV_HEADS)[None, :], w2p, b_gla_gate[l][None, :])
        fin_w = (w_o[l].astype(BF16), ffn_norm_g[l][None, :], w_gate[l].astype(BF16),
                 w_up[l].astype(BF16), w_down[l].astype(BF16))
        gn = gla_norm_g[l][None, :]

        pp = _project(xp, *proj_w, rows=PROMPT_ROWS, block_cumsum=True)
        mix_p, st_p = _prompt_mixer(pp, relb_flat, attn_sinks[l], gn)
        xp = _finish(xp, mix_p, *fin_w, rows=PROMPT_ROWS)
        outs[0].append(pp[seq - wb:, C_KA:C_VA].reshape(batch, wb, ATTN_KV_HEADS, HEAD_DIM))
        outs[1].append(pp[seq - wb:, C_VA:C_QG].reshape(batch, wb, ATTN_KV_HEADS, HEAD_DIM))
        outs[2].append(st_p.T.reshape(batch, GLA_HEADS, GLA_DK, GLA_DV).astype(state_gla.dtype))

        ps = _project(xs, *proj_w, rows=dec_batch, block_cumsum=False)
        mix_s, kwin, vwin, st_s = _sample_mixer(
            ps, cache_k[l].reshape(dec_batch, wb, KV_WIDTH), cache_v[l].reshape(dec_batch, wb, KV_WIDTH),
            state_gla[l].astype(F32), relb_t, attn_sinks[l][:, None], gn)
        xs = _finish(xs, mix_s, *fin_w, rows=dec_batch)
        outs[3].append(kwin.reshape(dec_batch, wb, ATTN_KV_HEADS, HEAD_DIM))
        outs[4].append(vwin.reshape(dec_batch, wb, ATTN_KV_HEADS, HEAD_DIM))
        outs[5].append(st_s.astype(state_gla.dtype))

    y_prompt = xp.reshape(batch, seq, D_MODEL)
    y_sample = xs.reshape(dec_batch, dec_seq, D_MODEL)
    return (y_prompt, y_sample) + tuple(jnp.stack(o) for o in outs)
```

```python
import functools
import math

import numpy as np
import jax
import jax.numpy as jnp
from jax import lax
from jax.experimental import pallas as pl
from jax.experimental.pallas import tpu as pltpu

F32 = jnp.float32
BF16 = jnp.bfloat16

D_MODEL = 1024
HEAD_DIM = 64
ATTN_HEADS = 8
ATTN_KV_HEADS = 2
WINDOW = 128
N_BUCKETS = 32
MAX_DISTANCE = 128
GLA_HEADS = 4
GLA_DK = 64
GLA_DV = 128
GLA_RANK = 16
GLA_TAU = 16.0
EPS = 1e-6
ATTN_WIDTH = ATTN_HEADS * HEAD_DIM
KV_WIDTH = ATTN_KV_HEADS * HEAD_DIM
GLA_QK_WIDTH = GLA_HEADS * GLA_DK
GLA_WIDTH = GLA_HEADS * GLA_DV
MIX_WIDTH = ATTN_WIDTH + GLA_WIDTH
MAIN_WIDTH = ATTN_WIDTH + 2 * KV_WIDTH + 2 * GLA_QK_WIDTH + 2 * GLA_WIDTH
LANES = 128
SUBLANES = 8
MXU_TILE = 256
FF_CHUNKS = 2
RANK_PAD = LANES
IN_PAD_WIDTH = MAIN_WIDTH + RANK_PAD

C_QA = 0
C_KA = C_QA + ATTN_WIDTH
C_VA = C_KA + KV_WIDTH
C_QG = C_VA + KV_WIDTH
C_KG = C_QG + GLA_QK_WIDTH
C_VG = C_KG + GLA_QK_WIDTH
C_RG = C_VG + GLA_WIDTH
C_LA = C_RG + GLA_WIDTH
P_WIDTH = C_LA + GLA_QK_WIDTH

W_QA = 0
W_KV = W_QA + ATTN_WIDTH
W_QG = W_KV + 2 * KV_WIDTH + RANK_PAD
W_VG = W_QG + 2 * GLA_QK_WIDTH
W_RG = W_VG + GLA_WIDTH

BLK = 128
N_LEVELS = 7
NEG = -1e30
ATTN_SCALE = HEAD_DIM ** -0.5
SAMPLE_BLK = 8
VMEM_LIMIT = 56 * 1024 * 1024


def _t5_bucket_np(dist):
    n = np.maximum(dist, 0)
    max_exact = N_BUCKETS // 2
    nf = np.maximum(n, 1).astype(np.float64)
    large = max_exact + (np.log(nf / max_exact) / math.log(MAX_DISTANCE / max_exact)
                         * (N_BUCKETS - max_exact)).astype(np.int32)
    large = np.minimum(large, N_BUCKETS - 1)
    return np.where(n < max_exact, n, large).astype(np.int32)


def _prompt_bucket_tables():
    i = np.arange(BLK)[:, None]
    j = np.arange(2 * BLK)[None, :]
    dist = BLK + i - j
    band = (dist >= 0) & (dist < WINDOW)
    bucket = _t5_bucket_np(dist)
    t0 = np.where(band, bucket, -1)
    t1 = np.where(band & (j >= BLK), bucket, -1)
    return np.stack([t0, t1]).astype(np.int32)


def _level_tables():
    t = np.arange(BLK)[:, None]
    s = np.arange(BLK)[None, :]
    x = t ^ s
    lev = np.where(x > 0, np.floor(np.log2(np.maximum(x, 1))).astype(np.int32) + 1, 0)
    lev = np.where(s > t, -1, lev).astype(np.int32)
    tri = (s <= t).astype(np.float32)
    return lev, np.concatenate([tri, tri, tri], axis=1)


_BUCKET_PROMPT = _prompt_bucket_tables()
_LEV, _TRI3 = _level_tables()
_BUCKET_SAMPLE = _t5_bucket_np((WINDOW - 1) - np.arange(WINDOW))[None, :].astype(np.int32)


def _nt_dot(a, b):
    return lax.dot_general(a, b, (((1,), (1,)), ((), ())), preferred_element_type=F32)


def _tn_dot(a, b):
    return lax.dot_general(a, b, (((0,), (0,)), ((), ())), preferred_element_type=F32)


def _head_mean_sq(x):
    lo = lax.broadcasted_iota(jnp.int32, (x.shape[0], LANES), 1) < HEAD_DIM
    outs = []
    for c in range(x.shape[1] // LANES):
        y = x[:, c * LANES:(c + 1) * LANES]
        y = y * y
        s_lo = jnp.sum(jnp.where(lo, y, 0.0), axis=-1, keepdims=True)
        s_hi = jnp.sum(jnp.where(lo, 0.0, y), axis=-1, keepdims=True)
        outs.append(jnp.where(lo, s_lo, s_hi) * (1.0 / HEAD_DIM))
    return outs[0] if len(outs) == 1 else jnp.concatenate(outs, axis=1)


def _sigmoid(x):
    return 1.0 / (1.0 + jnp.exp(-x))


def _split3_rows(x):
    hi = x.astype(BF16)
    r1 = x - hi.astype(F32)
    mid = r1.astype(BF16)
    lo = (r1 - mid.astype(F32)).astype(BF16)
    return jnp.concatenate([hi, mid, lo], axis=0)


def _proj_kernel(x_ref, g_ref, w_ref, qn_ref, kn_ref, w2_ref, b2_ref, tri_ref, out_ref, *,
                 block_cumsum):
    x = x_ref[...]
    r = lax.rsqrt(jnp.mean(x * x, axis=-1, keepdims=True) + EPS)
    n = (x * g_ref[...]).astype(BF16)

    def seg(c0, c1):
        return jnp.dot(n, w_ref[:, c0:c1], preferred_element_type=F32) * r

    kvl = seg(W_KV, W_QG)
    lr = kvl[:, 2 * KV_WIDTH:].astype(BF16)
    z = jnp.dot(lr, w2_ref[...], preferred_element_type=F32) + b2_ref[...]
    log_a = (jnp.minimum(z, 0.0) - jnp.log1p(jnp.exp(-jnp.abs(z)))) / GLA_TAU
    if block_cumsum:
        for blk in range(x.shape[0] // BLK):
            rows = slice(blk * BLK, (blk + 1) * BLK)
            out_ref[rows, C_LA:P_WIDTH] = jnp.dot(
                tri_ref[...], _split3_rows(log_a[rows]), preferred_element_type=F32)
    else:
        out_ref[:, C_LA:P_WIDTH] = log_a
    k = kvl[:, :KV_WIDTH]
    out_ref[:, C_KA:C_VA] = k * lax.rsqrt(_head_mean_sq(k) + EPS) * kn_ref[...]
    out_ref[:, C_VA:C_QG] = kvl[:, KV_WIDTH:2 * KV_WIDTH]
    q = seg(W_QA, W_KV)
    out_ref[:, C_QA:C_KA] = q * lax.rsqrt(_head_mean_sq(q) + EPS) * qn_ref[...] * ATTN_SCALE
    qk_g = seg(W_QG, W_VG)
    out_ref[:, C_QG:C_KG] = qk_g[:, :GLA_QK_WIDTH] * (GLA_DK ** -0.5)
    out_ref[:, C_KG:C_VG] = qk_g[:, GLA_QK_WIDTH:]
    out_ref[:, C_VG:C_RG] = seg(W_VG, W_RG)
    out_ref[:, C_RG:C_LA] = seg(W_RG, IN_PAD_WIDTH)


def _const_spec(shape):
    nd = len(shape)
    return pl.BlockSpec(shape, lambda i: (0,) * nd, pipeline_mode=pl.Buffered(1))


def _project(x, g_attn, w_in_p, qn, kn, w2p, b2, rows, block_cumsum):
    t = x.shape[0]
    return pl.pallas_call(
        functools.partial(_proj_kernel, block_cumsum=block_cumsum),
        grid=(t // rows,),
        in_specs=[
            pl.BlockSpec((rows, D_MODEL), lambda i: (i, 0)),
            _const_spec((1, D_MODEL)),
            _const_spec((D_MODEL, IN_PAD_WIDTH)),
            _const_spec((1, ATTN_WIDTH)),
            _const_spec((1, KV_WIDTH)),
            _const_spec((RANK_PAD, GLA_QK_WIDTH)),
            _const_spec((1, GLA_QK_WIDTH)),
            _const_spec((BLK, 3 * BLK)),
        ],
        out_specs=pl.BlockSpec((rows, P_WIDTH), lambda i: (i, 0)),
        out_shape=jax.ShapeDtypeStruct((t, P_WIDTH), F32),
        compiler_params=pltpu.CompilerParams(
            dimension_semantics=("arbitrary",), vmem_limit_bytes=VMEM_LIMIT),
        name="proj",
    )(x, g_attn, w_in_p, qn, kn, w2p, b2, jnp.asarray(_TRI3, BF16))


def _boundary_rows(b_ref, r0, c0, b, row, level):
    m = 1 << (level - 1)
    if 2 * m >= SUBLANES:
        pieces = [jnp.broadcast_to(b_ref[r0 + g * 2 * m + m - 1:r0 + g * 2 * m + m, c0:c0 + LANES],
                                   (2 * m, LANES))
                  for g in range(BLK // (2 * m))]
        return pieces[0] if len(pieces) == 1 else jnp.concatenate(pieces, axis=0)
    pos = row & (2 * m - 1)
    out = b
    for p in range(2 * m):
        shift = (m - 1) - p
        if shift != 0:
            out = jnp.where(pos == p, pltpu.roll(b, (-shift) % BLK, 0), out)
    return out


def _mixer_init(relb_ref, bucket_ref, kbd, vbd, kprev_t, st_scr, mb_scr):
    kbd[...] = jnp.zeros_like(kbd)
    vbd[...] = jnp.zeros_like(vbd)
    kprev_t[...] = jnp.zeros_like(kprev_t)
    st_scr[...] = jnp.zeros_like(st_scr)
    bk = bucket_ref[0]
    acc = [jnp.zeros(bk.shape, F32) for _ in range(ATTN_HEADS)]
    for b in range(N_BUCKETS):
        hit = bk == b
        for h in range(ATTN_HEADS):
            acc[h] = jnp.where(hit, relb_ref[b * ATTN_HEADS + h], acc[h])
    for tb in range(2):
        masked = bucket_ref[tb] < 0
        for h in range(ATTN_HEADS):
            mb_scr[tb, h // 2, :, (h % 2) * 2 * BLK:(h % 2 + 1) * 2 * BLK] = (
                jnp.where(masked, NEG, acc[h]))


def _mixer_attention(p_ref, r0, table, sink_ref, omix_ref, kbd, vbd, kprev_t, mb_scr):
    rows = slice(r0, r0 + BLK)
    lo1 = lax.broadcasted_iota(jnp.int32, (BLK, LANES), 1) < HEAD_DIM
    k_t = p_ref[rows,C_KA:C_VA].T.astype(BF16)
    k_prev_t = kprev_t[...]
    for g in range(ATTN_KV_HEADS):
        hd = slice(g * HEAD_DIM, (g + 1) * HEAD_DIM)
        kbd[g, 0:HEAD_DIM, 0:BLK] = k_prev_t[hd]
        kbd[g, 0:HEAD_DIM, BLK:2 * BLK] = k_t[hd]
        kbd[g, HEAD_DIM:2 * HEAD_DIM, 2 * BLK:3 * BLK] = k_prev_t[hd]
        kbd[g, HEAD_DIM:2 * HEAD_DIM, 3 * BLK:4 * BLK] = k_t[hd]
    kprev_t[...] = k_t
    v_cur = p_ref[rows,C_VA:C_QG]
    v_swap = pltpu.roll(v_cur, HEAD_DIM, 1)
    vbd[0, BLK:2 * BLK] = jnp.where(lo1, v_cur, 0.0).astype(BF16)
    vbd[0, 3 * BLK:4 * BLK] = jnp.where(lo1, 0.0, v_swap).astype(BF16)
    vbd[1, BLK:2 * BLK] = jnp.where(lo1, v_swap, 0.0).astype(BF16)
    vbd[1, 3 * BLK:4 * BLK] = jnp.where(lo1, 0.0, v_cur).astype(BF16)
    chunks_per_kv = ATTN_HEADS // ATTN_KV_HEADS // 2
    for g in range(ATTN_KV_HEADS):
        c_first = g * chunks_per_kv
        qs = jnp.concatenate(
            [p_ref[rows,C_QA + (c_first + c) * LANES:C_QA + (c_first + c + 1) * LANES].astype(BF16)
             for c in range(chunks_per_kv)], axis=0)
        mb = jnp.concatenate([mb_scr[table, c_first + c] for c in range(chunks_per_kv)], axis=0)
        s = jnp.dot(qs, kbd[g], preferred_element_type=F32) + mb
        probs, inv = [], []
        for e in range(2):
            se = s[:, e * 2 * BLK:(e + 1) * 2 * BLK]
            sk = jnp.concatenate(
                [jnp.full((BLK, 1), sink_ref[2 * (c_first + c) + e], F32) for c in range(chunks_per_kv)],
                axis=0)
            m = jnp.maximum(jnp.max(se, axis=-1, keepdims=True), sk)
            pe = jnp.exp(se - m)
            den = jnp.sum(pe, axis=-1, keepdims=True) + jnp.exp(sk - m)
            probs.append(pe.astype(BF16))
            inv.append(1.0 / den)
        o = jnp.dot(jnp.concatenate(probs, axis=1), vbd[g], preferred_element_type=F32)
        lo_rows = lax.broadcasted_iota(jnp.int32, o.shape, 1) < HEAD_DIM
        o = o * jnp.where(lo_rows, inv[0], inv[1])
        for c in range(chunks_per_kv):
            omix_ref[rows,(c_first + c) * LANES:(c_first + c + 1) * LANES] = (
                o[c * BLK:(c + 1) * BLK].astype(omix_ref.dtype))
    for g in range(ATTN_KV_HEADS):
        vbd[g, 0:BLK] = vbd[g, BLK:2 * BLK]
        vbd[g, 2 * BLK:3 * BLK] = vbd[g, 3 * BLK:4 * BLK]


def _mixer_gla(pairs, p_ref, r0, lev_ref, gn_ref, omix_ref, st_ref, st_scr):
    rows = slice(r0, r0 + BLK)
    lo1 = lax.broadcasted_iota(jnp.int32, (BLK, LANES), 1) < HEAD_DIM
    lo_bf = jnp.where(lo1, 1.0, 0.0).astype(BF16)
    hi_bf = jnp.where(lo1, 0.0, 1.0).astype(BF16)
    row = lax.broadcasted_iota(jnp.int32, (BLK, LANES), 0)
    zero_blk = jnp.zeros((BLK, LANES), BF16)
    for c in pairs:
        c0 = c * LANES
        q_at = lambda a, z: p_ref[r0 + a:r0 + z, C_QG + c0:C_QG + c0 + LANES]
        k_at = lambda a, z: p_ref[r0 + a:r0 + z, C_KG + c0:C_KG + c0 + LANES]
        b_at = lambda a, z: p_ref[r0 + a:r0 + z, C_LA + c0:C_LA + c0 + LANES]

        def pair_scores(qtb, ktb):
            rhs = jnp.concatenate([ktb * lo_bf, ktb * hi_bf], axis=0)
            return _nt_dot(qtb, rhs)

        s0 = pair_scores(q_at(0, BLK).astype(BF16), k_at(0, BLK).astype(BF16))
        sc = [jnp.where(lev_ref[...] == 0, s0[:, e * BLK:(e + 1) * BLK], 0.0) for e in range(2)]
        for level in range(1, N_LEVELS + 1):
            m = 1 << (level - 1)
            if m >= SUBLANES:
                qs, ks = [], []
                zeros = jnp.zeros((m, LANES), BF16)
                for g in range(BLK // (2 * m)):
                    lo_a, up_a, up_z = g * 2 * m, g * 2 * m + m, (g + 1) * 2 * m
                    rb = jnp.broadcast_to(b_at(up_a - 1, up_a), (m, LANES))
                    qs += [zeros, (q_at(up_a, up_z) * jnp.exp(b_at(up_a, up_z) - rb)).astype(BF16)]
                    ks += [(k_at(lo_a, up_a) * jnp.exp(rb - b_at(lo_a, up_a))).astype(BF16), zeros]
                qtb = jnp.concatenate(qs, axis=0)
                ktb = jnp.concatenate(ks, axis=0)
            else:
                bc = b_at(0, BLK)
                d = bc - _boundary_rows(p_ref, r0, C_LA + c0, bc, row, level)
                upper = ((row >> (level - 1)) & 1) == 1
                qtb = (q_at(0, BLK) * jnp.exp(jnp.where(upper, d, NEG))).astype(BF16)
                ktb = (k_at(0, BLK) * jnp.exp(jnp.where(upper, NEG, -d))).astype(BF16)
            sl = pair_scores(qtb, ktb)
            sc = [jnp.where(lev_ref[...] == level, sl[:, e * BLK:(e + 1) * BLK], sc[e])
                  for e in range(2)]
        sc = jnp.concatenate(sc, axis=1)
        qc, kc, bc = q_at(0, BLK), k_at(0, BLK), b_at(0, BLK)

        b_last = bc[BLK - 1:BLK, :]
        v0 = p_ref[rows,C_VG + 2 * c0:C_VG + 2 * c0 + LANES].astype(BF16)
        v1 = p_ref[rows,C_VG + 2 * c0 + LANES:C_VG + 2 * c0 + 2 * LANES].astype(BF16)
        v_bd = jnp.concatenate([jnp.concatenate([v0, zero_blk], axis=1),
                                jnp.concatenate([zero_blk, v1], axis=1)], axis=0)
        st_c = st_scr[:, c0:c0 + LANES]
        stb = st_c.astype(BF16)
        st_rhs = jnp.concatenate([stb * lo_bf, stb * hi_bf], axis=0)
        o = (jnp.dot(sc.astype(BF16), v_bd, preferred_element_type=F32)
             + _nt_dot((qc * jnp.exp(bc)).astype(BF16), st_rhs))
        kd = (kc * jnp.exp(b_last - bc)).astype(BF16)
        upd = _tn_dot(jnp.concatenate([v0, v1], axis=1), kd)
        new_st = st_c * jnp.exp(b_last) + jnp.where(lo1, upd[:BLK], upd[BLK:])
        st_scr[:, c0:c0 + LANES] = new_st
        st_ref[:, c0:c0 + LANES] = new_st
        for e in range(2):
            h = 2 * c + e
            oh = o[:, e * LANES:(e + 1) * LANES]
            og = oh * lax.rsqrt(jnp.mean(oh * oh, axis=-1, keepdims=True) + EPS) * gn_ref[...]
            rg = p_ref[rows,C_RG + h * GLA_DV:C_RG + (h + 1) * GLA_DV]
            gated = og * (rg * _sigmoid(rg))
            omix_ref[rows,ATTN_WIDTH + h * GLA_DV:ATTN_WIDTH + (h + 1) * GLA_DV] = (
                gated.astype(omix_ref.dtype))


def _prompt_mixer_kernel(relb_ref, sink_ref, p_ref, bucket_ref, lev_ref, gn_ref,
                         omix_ref, st_ref, kbd, vbd, kprev_t, st_scr, mb_scr):
    i = pl.program_id(0)

    @pl.when(i == 0)
    def _init():
        _mixer_init(relb_ref, bucket_ref, kbd, vbd, kprev_t, st_scr, mb_scr)

    _mixer_attention(p_ref, 0, jnp.where(i == 0, 1, 0), sink_ref, omix_ref, kbd, vbd, kprev_t,
                     mb_scr)
    _mixer_gla(range(GLA_HEADS // 2), p_ref, 0, lev_ref, gn_ref, omix_ref, st_ref, st_scr)


def _prompt_mixer(p, relb, sinks, gn):
    t = p.shape[0]
    smem = pl.BlockSpec(memory_space=pltpu.SMEM)
    return pl.pallas_call(
        _prompt_mixer_kernel,
        grid=(t // BLK,),
        in_specs=[
            smem, smem,
            pl.BlockSpec((BLK, P_WIDTH), lambda i: (i, 0)),
            _const_spec((2, BLK, 2 * BLK)),
            _const_spec((BLK, BLK)),
            _const_spec((1, GLA_DV)),
        ],
        out_specs=[
            pl.BlockSpec((BLK, MIX_WIDTH), lambda i: (i, 0)),
            pl.BlockSpec((GLA_DV, GLA_QK_WIDTH), lambda i: (0, 0)),
        ],
        out_shape=[
            jax.ShapeDtypeStruct((t, MIX_WIDTH), BF16),
            jax.ShapeDtypeStruct((GLA_DV, GLA_QK_WIDTH), F32),
        ],
        scratch_shapes=[
            pltpu.VMEM((ATTN_KV_HEADS, 2 * HEAD_DIM, 4 * BLK), BF16),
            pltpu.VMEM((ATTN_KV_HEADS, 4 * BLK, 2 * HEAD_DIM), BF16),
            pltpu.VMEM((KV_WIDTH, BLK), BF16),
            pltpu.VMEM((GLA_DV, GLA_QK_WIDTH), F32),
            pltpu.VMEM((2, ATTN_HEADS // 2, BLK, 4 * BLK), F32),
        ],
        compiler_params=pltpu.CompilerParams(
            dimension_semantics=("arbitrary",), vmem_limit_bytes=VMEM_LIMIT),
        name="prompt_mixer",
    )(relb, sinks, p, jnp.asarray(_BUCKET_PROMPT), jnp.asarray(_LEV), gn)


def _sample_mixer_kernel(ps_ref, pfull_ref, ck_ref, cv_ref, st_ref, relbt_ref, sink_ref, bucket_ref,
                         gn_ref, omix_ref, kwin_ref, vwin_ref, stout_ref, lat_scr, kqt_scr,
                         bias_scr, s_scr, o_scr, og_scr):
    i = pl.program_id(0)
    nb = pfull_ref.shape[0]

    @pl.when(i == 0)
    def _init():
        lat_scr[...] = _split3_rows(pfull_ref[:, C_LA:P_WIDTH].T)
        kqt_scr[0:GLA_QK_WIDTH] = pfull_ref[:, C_KG:C_VG].T.astype(BF16)
        kqt_scr[GLA_QK_WIDTH:2 * GLA_QK_WIDTH] = pfull_ref[:, C_QG:C_KG].T.astype(BF16)
        bk = jnp.broadcast_to(bucket_ref[...], (ATTN_HEADS, WINDOW))
        acc = jnp.zeros((ATTN_HEADS, WINDOW), F32)
        for b in range(N_BUCKETS):
            acc = jnp.where(bk == b, relbt_ref[:, b:b + 1], acc)
        bias_scr[...] = acc

    lo = lax.broadcasted_iota(jnp.int32, (1, LANES), 1) < HEAD_DIM
    sub = lax.broadcasted_iota(jnp.int32, (ATTN_HEADS, LANES), 0)
    wrow = lax.broadcasted_iota(jnp.int32, (WINDOW, KV_WIDTH), 0)
    heads_per_kv = ATTN_HEADS // ATTN_KV_HEADS

    n_of_col = i * SAMPLE_BLK + lax.broadcasted_iota(jnp.int32, (nb, SAMPLE_BLK * LANES), 1) // LANES
    pick = jnp.where(lax.broadcasted_iota(jnp.int32, (nb, SAMPLE_BLK * LANES), 0) == n_of_col,
                     1.0, 0.0).astype(BF16)
    la_b = (jnp.dot(lat_scr[0:GLA_QK_WIDTH], pick, preferred_element_type=F32)
            + jnp.dot(lat_scr[GLA_QK_WIDTH:2 * GLA_QK_WIDTH], pick, preferred_element_type=F32)
            + jnp.dot(lat_scr[2 * GLA_QK_WIDTH:3 * GLA_QK_WIDTH], pick, preferred_element_type=F32))
    kq_b = jnp.dot(kqt_scr[...], pick, preferred_element_type=F32)

    for j in range(SAMPLE_BLK):
        k_new = ps_ref[j:j + 1, C_KA:C_VA]
        v_new = ps_ref[j:j + 1, C_VA:C_QG]
        kwin_ref[j] = jnp.where(wrow == WINDOW - 1, k_new, pltpu.roll(ck_ref[j], WINDOW - 1, 0))
        vwin_ref[j] = jnp.where(wrow == WINDOW - 1, v_new, pltpu.roll(cv_ref[j], WINDOW - 1, 0))

    for j in range(SAMPLE_BLK):
        qexp = jnp.zeros((ATTN_HEADS, LANES), F32)
        for c in range(ATTN_HEADS // 2):
            chunk = ps_ref[j:j + 1, C_QA + c * LANES:C_QA + (c + 1) * LANES]
            swapped = pltpu.roll(chunk, HEAD_DIM, 1)
            if (2 * c) // heads_per_kv == 0:
                rows = (jnp.where(lo, chunk, 0.0), jnp.where(lo, swapped, 0.0))
            else:
                rows = (jnp.where(lo, 0.0, swapped), jnp.where(lo, 0.0, chunk))
            for e in range(2):
                qexp = jnp.where(sub == 2 * c + e, rows[e], qexp)
        s_scr[j * ATTN_HEADS:(j + 1) * ATTN_HEADS] = _nt_dot(qexp.astype(BF16), kwin_ref[j].astype(BF16))

    tile = lambda x: jnp.concatenate([x] * SAMPLE_BLK, axis=0)
    sink = tile(sink_ref[...])
    s = s_scr[...] + tile(bias_scr[...])
    m = jnp.maximum(jnp.max(s, axis=-1, keepdims=True), sink)
    pe = jnp.exp(s - m)
    inv_den = 1.0 / (jnp.sum(pe, axis=-1, keepdims=True) + jnp.exp(sink - m))
    peb = pe.astype(BF16)
    for j in range(SAMPLE_BLK):
        o_scr[j * ATTN_HEADS:(j + 1) * ATTN_HEADS] = jnp.dot(
            peb[j * ATTN_HEADS:(j + 1) * ATTN_HEADS], vwin_ref[j].astype(BF16),
            preferred_element_type=F32)
    o_all = o_scr[...] * inv_den
    o_swap = pltpu.roll(o_all, HEAD_DIM, 1)
    for j in range(SAMPLE_BLK):
        r = j * ATTN_HEADS
        for c in range(ATTN_HEADS // 2):
            if (2 * c) // heads_per_kv == 0:
                piece = jnp.where(lo, o_all[r + 2 * c:r + 2 * c + 1, :], o_swap[r + 2 * c + 1:r + 2 * c + 2, :])
            else:
                piece = jnp.where(lo, o_swap[r + 2 * c:r + 2 * c + 1, :], o_all[r + 2 * c + 1:r + 2 * c + 2, :])
            omix_ref[j:j + 1, c * LANES:(c + 1) * LANES] = piece

    for j in range(SAMPLE_BLK):
        cols = slice(j * LANES, (j + 1) * LANES)
        for h in range(GLA_HEADS):
            rs = slice(h * GLA_DK, (h + 1) * GLA_DK)
            qs = slice(GLA_QK_WIDTH + h * GLA_DK, GLA_QK_WIDTH + (h + 1) * GLA_DK)
            v_row = ps_ref[j:j + 1, C_VG + h * GLA_DV:C_VG + (h + 1) * GLA_DV]
            s_new = jnp.exp(la_b[rs, cols]) * st_ref[j, h] + kq_b[rs, cols] * v_row
            stout_ref[j, h] = s_new
            og_scr[j:j + 1, h * GLA_DV:(h + 1) * GLA_DV] = jnp.sum(
                kq_b[qs, cols] * s_new, axis=0, keepdims=True)
    for h in range(GLA_HEADS):
        hs = slice(h * GLA_DV, (h + 1) * GLA_DV)
        og = og_scr[:, hs]
        og = og * lax.rsqrt(jnp.mean(og * og, axis=-1, keepdims=True) + EPS) * gn_ref[...]
        rg = ps_ref[:, C_RG + h * GLA_DV:C_RG + (h + 1) * GLA_DV]
        omix_ref[:, ATTN_WIDTH + h * GLA_DV:ATTN_WIDTH + (h + 1) * GLA_DV] = og * (rg * _sigmoid(rg))


def _sample_mixer(ps, cache_k, cache_v, state, relbt, sinks_col, gn):
    nb = ps.shape[0]
    blk3 = lambda i: (i, 0, 0)
    blk4 = lambda i: (i, 0, 0, 0)
    return pl.pallas_call(
        _sample_mixer_kernel,
        grid=(nb // SAMPLE_BLK,),
        in_specs=[
            pl.BlockSpec((SAMPLE_BLK, P_WIDTH), lambda i: (i, 0)),
            _const_spec((nb, P_WIDTH)),
            pl.BlockSpec((SAMPLE_BLK, WINDOW, KV_WIDTH), blk3),
            pl.BlockSpec((SAMPLE_BLK, WINDOW, KV_WIDTH), blk3),
            pl.BlockSpec((SAMPLE_BLK, GLA_HEADS, GLA_DK, GLA_DV), blk4),
            _const_spec((ATTN_HEADS, N_BUCKETS)),
            _const_spec((ATTN_HEADS, 1)),
            _const_spec((1, WINDOW)),
            _const_spec((1, GLA_DV)),
        ],
        out_specs=[
            pl.BlockSpec((SAMPLE_BLK, MIX_WIDTH), lambda i: (i, 0)),
            pl.BlockSpec((SAMPLE_BLK, WINDOW, KV_WIDTH), blk3),
            pl.BlockSpec((SAMPLE_BLK, WINDOW, KV_WIDTH), blk3),
            pl.BlockSpec((SAMPLE_BLK, GLA_HEADS, GLA_DK, GLA_DV), blk4),
        ],
        out_shape=[
            jax.ShapeDtypeStruct((nb, MIX_WIDTH), F32),
            jax.ShapeDtypeStruct((nb, WINDOW, KV_WIDTH), F32),
            jax.ShapeDtypeStruct((nb, WINDOW, KV_WIDTH), F32),
            jax.ShapeDtypeStruct((nb, GLA_HEADS, GLA_DK, GLA_DV), F32),
        ],
        scratch_shapes=[
            pltpu.VMEM((3 * GLA_QK_WIDTH, nb), BF16),
            pltpu.VMEM((2 * GLA_QK_WIDTH, nb), BF16),
            pltpu.VMEM((ATTN_HEADS, WINDOW), F32),
            pltpu.VMEM((SAMPLE_BLK * ATTN_HEADS, WINDOW), F32),
            pltpu.VMEM((SAMPLE_BLK * ATTN_HEADS, KV_WIDTH), F32),
            pltpu.VMEM((SAMPLE_BLK, GLA_WIDTH), F32),
        ],
        compiler_params=pltpu.CompilerParams(
            dimension_semantics=("arbitrary",), vmem_limit_bytes=VMEM_LIMIT),
        name="sample_mixer",
    )(ps, ps, cache_k, cache_v, state, relbt, sinks_col, jnp.asarray(_BUCKET_SAMPLE), gn)


def _finish_kernel(x_ref, mix_ref, wo_ref, g_ref, wg_ref, wu_ref, wd_ref, y_ref, *, ff_chunks):
    h = x_ref[...] + jnp.dot(mix_ref[...].astype(BF16), wo_ref[...], preferred_element_type=F32)
    r = lax.rsqrt(jnp.mean(h * h, axis=-1, keepdims=True) + EPS)
    z = (h * g_ref[...]).astype(BF16)
    n_tiles = wd_ref.shape[0] // MXU_TILE
    acc = h
    for c in range(ff_chunks):
        c0 = ((c * n_tiles) // ff_chunks) * MXU_TILE
        c1 = (((c + 1) * n_tiles) // ff_chunks) * MXU_TILE
        gate = jnp.dot(z, wg_ref[:, c0:c1], preferred_element_type=F32) * r
        up = jnp.dot(z, wu_ref[:, c0:c1], preferred_element_type=F32) * r
        act = ((gate * _sigmoid(gate)) * up).astype(BF16)
        acc = acc + jnp.dot(act, wd_ref[c0:c1, :], preferred_element_type=F32)
    y_ref[...] = acc


def _finish(x, mix, wo, g_ffn, wg, wu, wd, rows):
    t = x.shape[0]
    d_ff = wd.shape[0]
    assert d_ff % MXU_TILE == 0
    return pl.pallas_call(
        functools.partial(_finish_kernel, ff_chunks=FF_CHUNKS),
        grid=(t // rows,),
        in_specs=[
            pl.BlockSpec((rows, D_MODEL), lambda i: (i, 0)),
            pl.BlockSpec((rows, MIX_WIDTH), lambda i: (i, 0)),
            _const_spec((MIX_WIDTH, D_MODEL)),
            _const_spec((1, D_MODEL)),
            _const_spec((D_MODEL, d_ff)),
            _const_spec((D_MODEL, d_ff)),
            _const_spec((d_ff, D_MODEL)),
        ],
        out_specs=pl.BlockSpec((rows, D_MODEL), lambda i: (i, 0)),
        out_shape=jax.ShapeDtypeStruct((t, D_MODEL), F32),
        compiler_params=pltpu.CompilerParams(
            dimension_semantics=("arbitrary",), vmem_limit_bytes=VMEM_LIMIT),
        name="finish",
    )(x, mix, wo, g_ffn, wg, wu, wd)


PROMPT_ROWS = 512


def kernel(x_prompt, x_sample, cache_k, cache_v, state_gla, attn_norm_g, w_in, q_norm_g, k_norm_g,
           attn_sinks, rel_bias, w_gla_gate2, b_gla_gate, gla_norm_g, w_o, ffn_norm_g, w_gate, w_up,
           w_down):
    depth = w_in.shape[0]
    batch, seq, _ = x_prompt.shape
    dec_batch, dec_seq, _ = x_sample.shape
    wb = cache_k.shape[2]
    assert batch == 1 and dec_seq == 1 and wb == WINDOW and seq % PROMPT_ROWS == 0
    assert dec_batch % SAMPLE_BLK == 0 and dec_batch % LANES == 0
    assert rel_bias.shape == (N_BUCKETS, ATTN_HEADS)

    xp = x_prompt.reshape(seq, D_MODEL)
    xs = x_sample.reshape(dec_batch, D_MODEL)
    relb_flat = rel_bias.reshape(-1)
    relb_t = rel_bias.T
    outs = ([], [], [], [], [], [])
    for l in range(depth):
        kv_end = ATTN_WIDTH + 2 * KV_WIDTH
        w_in_p = jnp.concatenate(
            [w_in[l, :, :kv_end],
             jnp.pad(w_in[l, :, MAIN_WIDTH:], ((0, 0), (0, RANK_PAD - GLA_RANK))),
             w_in[l, :, kv_end:MAIN_WIDTH]], axis=1).astype(BF16)
        w2p = jnp.pad(w_gla_gate2[l], ((0, RANK_PAD - GLA_RANK), (0, 0))).astype(BF16)
        proj_w = (attn_norm_g[l][None, :], w_in_p, jnp.tile(q_norm_g[l], ATTN_HEADS)[None, :],
                  jnp.tile(k_norm_g[l], ATTN_KV_HEADS)[None, :], w2p, b_gla_gate[l][None, :])
        fin_w = (w_o[l].astype(BF16), ffn_norm_g[l][None, :], w_gate[l].astype(BF16),
                 w_up[l].astype(BF16), w_down[l].astype(BF16))
        gn = gla_norm_g[l][None, :]

        pp = _project(xp, *proj_w, rows=PROMPT_ROWS, block_cumsum=True)
        mix_p, st_p = _prompt_mixer(pp, relb_flat, attn_sinks[l], gn)
        xp = _finish(xp, mix_p, *fin_w, rows=PROMPT_ROWS)
        outs[0].append(pp[seq - wb:, C_KA:C_VA].reshape(batch, wb, ATTN_KV_HEADS, HEAD_DIM))
        outs[1].append(pp[seq - wb:, C_VA:C_QG].reshape(batch, wb, ATTN_KV_HEADS, HEAD_DIM))
        outs[2].append(st_p.T.reshape(batch, GLA_HEADS, GLA_DK, GLA_DV).astype(state_gla.dtype))

        ps = _project(xs, *proj_w, rows=dec_batch, block_cumsum=False)
        mix_s, kwin, vwin, st_s = _sample_mixer(
            ps, cache_k[l].reshape(dec_batch, wb, KV_WIDTH), cache_v[l].reshape(dec_batch, wb, KV_WIDTH),
            state_gla[l].astype(F32), relb_t, attn_sinks[l][:, None], gn)
        xs = _finish(xs, mix_s, *fin_w, rows=dec_batch)
        outs[3].append(kwin.reshape(dec_batch, wb, ATTN_KV_HEADS, HEAD_DIM))
        outs[4].append(vwin.reshape(dec_batch, wb, ATTN_KV_HEADS, HEAD_DIM))
        outs[5].append(st_s.astype(state_gla.dtype))

    y_prompt = xp.reshape(batch, seq, D_MODEL)
    y_sample = xs.reshape(dec_batch, dec_seq, D_MODEL)
    return (y_prompt, y_sample) + tuple(jnp.stack(o) for o in outs)
```

```python
import functools
import math

import numpy as np
import jax
import jax.numpy as jnp
from jax import lax
from jax.experimental import pallas as pl
from jax.experimental.pallas import tpu as pltpu

F32 = jnp.float32
BF16 = jnp.bfloat16

D_MODEL = 1024
HEAD_DIM = 64
ATTN_HEADS = 8
ATTN_KV_HEADS = 2
WINDOW = 128
N_BUCKETS = 32
MAX_DISTANCE = 128
GLA_HEADS = 4
GLA_DK = 64
GLA_DV = 128
GLA_RANK = 16
GLA_TAU = 16.0
EPS = 1e-6
ATTN_WIDTH = ATTN_HEADS * HEAD_DIM
KV_WIDTH = ATTN_KV_HEADS * HEAD_DIM
GLA_QK_WIDTH = GLA_HEADS * GLA_DK
GLA_WIDTH = GLA_HEADS * GLA_DV
MIX_WIDTH = ATTN_WIDTH + GLA_WIDTH
MAIN_WIDTH = ATTN_WIDTH + 2 * KV_WIDTH + 2 * GLA_QK_WIDTH + 2 * GLA_WIDTH
LANES = 128
SUBLANES = 8
MXU_TILE = 256
FF_CHUNKS = 2
RANK_PAD = LANES
IN_PAD_WIDTH = MAIN_WIDTH + RANK_PAD

C_QA = 0
C_KA = C_QA + ATTN_WIDTH
C_VA = C_KA + KV_WIDTH
C_QG = C_VA + KV_WIDTH
C_KG = C_QG + GLA_QK_WIDTH
C_VG = C_KG + GLA_QK_WIDTH
C_RG = C_VG + GLA_WIDTH
C_LA = C_RG + GLA_WIDTH
P_WIDTH = C_LA + GLA_QK_WIDTH

W_QA = 0
W_KV = W_QA + ATTN_WIDTH
W_QG = W_KV + 2 * KV_WIDTH + RANK_PAD
W_VG = W_QG + 2 * GLA_QK_WIDTH
W_RG = W_VG + GLA_WIDTH

BLK = 128
N_LEVELS = 7
NEG = -1e30
ATTN_SCALE = HEAD_DIM ** -0.5
SAMPLE_BLK = 8
VMEM_LIMIT = 56 * 1024 * 1024


def _t5_bucket_np(dist):
    n = np.maximum(dist, 0)
    max_exact = N_BUCKETS // 2
    nf = np.maximum(n, 1).astype(np.float64)
    large = max_exact + (np.log(nf / max_exact) / math.log(MAX_DISTANCE / max_exact)
                         * (N_BUCKETS - max_exact)).astype(np.int32)
    large = np.minimum(large, N_BUCKETS - 1)
    return np.where(n < max_exact, n, large).astype(np.int32)


def _prompt_bucket_tables():
    i = np.arange(BLK)[:, None]
    j = np.arange(2 * BLK)[None, :]
    dist = BLK + i - j
    band = (dist >= 0) & (dist < WINDOW)
    bucket = _t5_bucket_np(dist)
    t0 = np.where(band, bucket, -1)
    t1 = np.where(band & (j >= BLK), bucket, -1)
    return np.stack([t0, t1]).astype(np.int32)


def _level_tables():
    t = np.arange(BLK)[:, None]
    s = np.arange(BLK)[None, :]
    x = t ^ s
    lev = np.where(x > 0, np.floor(np.log2(np.maximum(x, 1))).astype(np.int32) + 1, 0)
    lev = np.where(s > t, -1, lev).astype(np.int32)
    tri = (s <= t).astype(np.float32)
    return lev, np.concatenate([tri, tri, tri], axis=1)


_BUCKET_PROMPT = _prompt_bucket_tables()
_LEV, _TRI3 = _level_tables()
_BUCKET_SAMPLE = _t5_bucket_np((WINDOW - 1) - np.arange(WINDOW))[None, :].astype(np.int32)


def _nt_dot(a, b):
    return lax.dot_general(a, b, (((1,), (1,)), ((), ())), preferred_element_type=F32)


def _tn_dot(a, b):
    return lax.dot_general(a, b, (((0,), (0,)), ((), ())), preferred_element_type=F32)


def _head_mean_sq(x):
    lo = lax.broadcasted_iota(jnp.int32, (x.shape[0], LANES), 1) < HEAD_DIM
    outs = []
    for c in range(x.shape[1] // LANES):
        y = x[:, c * LANES:(c + 1) * LANES]
        y = y * y
        s_lo = jnp.sum(jnp.where(lo, y, 0.0), axis=-1, keepdims=True)
        s_hi = jnp.sum(jnp.where(lo, 0.0, y), axis=-1, keepdims=True)
        outs.append(jnp.where(lo, s_lo, s_hi) * (1.0 / HEAD_DIM))
    return outs[0] if len(outs) == 1 else jnp.concatenate(outs, axis=1)


def _sigmoid(x):
    return 1.0 / (1.0 + jnp.exp(-x))


def _split3_rows(x):
    hi = x.astype(BF16)
    r1 = x - hi.astype(F32)
    mid = r1.astype(BF16)
    lo = (r1 - mid.astype(F32)).astype(BF16)
    return jnp.concatenate([hi, mid, lo], axis=0)


def _proj_kernel(x_ref, xs_ref, g_ref, w_ref, qn_ref, kn_ref, w2_ref, b2_ref, tri_ref,
                 out_ref, kt_ref, outs_ref):
    weights = (g_ref, w_ref, qn_ref, kn_ref, w2_ref, b2_ref, tri_ref)
    _proj_rows(x_ref, *weights, out_ref, kt_ref, block_cumsum=True)

    @pl.when(pl.program_id(0) == pl.num_programs(0) - 1)
    def _samples():
        _proj_rows(xs_ref, *weights, outs_ref, None, block_cumsum=False)


def _proj_rows(x_ref, g_ref, w_ref, qn_ref, kn_ref, w2_ref, b2_ref, tri_ref, out_ref, kt_ref, *,
               block_cumsum):
    x = x_ref[...]
    r = lax.rsqrt(jnp.mean(x * x, axis=-1, keepdims=True) + EPS)
    n = (x * g_ref[...]).astype(BF16)

    def seg(c0, c1):
        return jnp.dot(n, w_ref[:, c0:c1], preferred_element_type=F32) * r

    kvl = seg(W_KV, W_QG)
    lr = kvl[:, 2 * KV_WIDTH:].astype(BF16)
    z = jnp.dot(lr, w2_ref[...], preferred_element_type=F32) + b2_ref[...]
    log_a = (jnp.minimum(z, 0.0) - jnp.log1p(jnp.exp(-jnp.abs(z)))) / GLA_TAU
    if block_cumsum:
        for blk in range(x.shape[0] // BLK):
            rows = slice(blk * BLK, (blk + 1) * BLK)
            out_ref[rows, C_LA:P_WIDTH] = jnp.dot(
                tri_ref[...], _split3_rows(log_a[rows]), preferred_element_type=F32)
    else:
        out_ref[:, C_LA:P_WIDTH] = log_a
    k = kvl[:, :KV_WIDTH]
    k = k * lax.rsqrt(_head_mean_sq(k) + EPS) * kn_ref[...]
    out_ref[:, C_KA:C_VA] = k
    if kt_ref is not None:
        kt_ref[...] = k.T.astype(BF16)
    out_ref[:, C_VA:C_QG] = kvl[:, KV_WIDTH:2 * KV_WIDTH]
    q = seg(W_QA, W_KV)
    out_ref[:, C_QA:C_KA] = q * lax.rsqrt(_head_mean_sq(q) + EPS) * qn_ref[...] * ATTN_SCALE
    qk_g = seg(W_QG, W_VG)
    out_ref[:, C_QG:C_KG] = qk_g[:, :GLA_QK_WIDTH] * (GLA_DK ** -0.5)
    out_ref[:, C_KG:C_VG] = qk_g[:, GLA_QK_WIDTH:]
    out_ref[:, C_VG:C_RG] = seg(W_VG, W_RG)
    out_ref[:, C_RG:C_LA] = seg(W_RG, IN_PAD_WIDTH)


def _const_spec(shape):
    nd = len(shape)
    return pl.BlockSpec(shape, lambda i: (0,) * nd, pipeline_mode=pl.Buffered(1))


def _project(x, xs, g_attn, w_in_p, qn, kn, w2p, b2, rows):
    t = x.shape[0]
    ns = xs.shape[0]
    return pl.pallas_call(
        _proj_kernel,
        grid=(t // rows,),
        in_specs=[
            pl.BlockSpec((rows, D_MODEL), lambda i: (i, 0)),
            _const_spec((ns, D_MODEL)),
            _const_spec((1, D_MODEL)),
            _const_spec((D_MODEL, IN_PAD_WIDTH)),
            _const_spec((1, ATTN_WIDTH)),
            _const_spec((1, KV_WIDTH)),
            _const_spec((RANK_PAD, GLA_QK_WIDTH)),
            _const_spec((1, GLA_QK_WIDTH)),
            _const_spec((BLK, 3 * BLK)),
        ],
        out_specs=[pl.BlockSpec((rows, P_WIDTH), lambda i: (i, 0)),
                   pl.BlockSpec((KV_WIDTH, rows), lambda i: (0, i)),
                   pl.BlockSpec((ns, P_WIDTH), lambda i: (0, 0))],
        out_shape=[jax.ShapeDtypeStruct((t, P_WIDTH), F32),
                   jax.ShapeDtypeStruct((KV_WIDTH, t), BF16),
                   jax.ShapeDtypeStruct((ns, P_WIDTH), F32)],
        compiler_params=pltpu.CompilerParams(
            dimension_semantics=("arbitrary",), vmem_limit_bytes=VMEM_LIMIT),
        name="proj",
    )(x, xs, g_attn, w_in_p, qn, kn, w2p, b2, jnp.asarray(_TRI3, BF16))


def _boundary_rows(b_ref, r0, c0, b, row, level):
    m = 1 << (level - 1)
    if 2 * m >= SUBLANES:
        pieces = [jnp.broadcast_to(b_ref[r0 + g * 2 * m + m - 1:r0 + g * 2 * m + m, c0:c0 + LANES],
                                   (2 * m, LANES))
                  for g in range(BLK // (2 * m))]
        return pieces[0] if len(pieces) == 1 else jnp.concatenate(pieces, axis=0)
    pos = row & (2 * m - 1)
    tiles = b.reshape(BLK // SUBLANES, SUBLANES, LANES)
    out = b
    for p in range(2 * m):
        shift = (m - 1) - p
        if shift != 0:
            rolled = pltpu.roll(tiles, (-shift) % SUBLANES, 1).reshape(BLK, LANES)
            out = jnp.where(pos == p, rolled, out)
    return out


def _mixer_init(relb_ref, bucket_ref, kbd, vbd, kprev_t, st_scr, mb_scr):
    kbd[...] = jnp.zeros_like(kbd)
    vbd[...] = jnp.zeros_like(vbd)
    kprev_t[...] = jnp.zeros_like(kprev_t)
    st_scr[...] = jnp.zeros_like(st_scr)
    bk = bucket_ref[0]
    acc = [jnp.zeros(bk.shape, F32) for _ in range(ATTN_HEADS)]
    for b in range(N_BUCKETS):
        hit = bk == b
        for h in range(ATTN_HEADS):
            acc[h] = jnp.where(hit, relb_ref[b * ATTN_HEADS + h], acc[h])
    for tb in range(2):
        masked = bucket_ref[tb] < 0
        for h in range(ATTN_HEADS):
            mb_scr[tb, h // 2, :, (h % 2) * 2 * BLK:(h % 2 + 1) * 2 * BLK] = (
                jnp.where(masked, NEG, acc[h]))


def _mixer_attention(p_ref, kt_ref, r0, table, sink_ref, omix_ref, kbd, vbd, kprev_t, mb_scr):
    rows = slice(r0, r0 + BLK)
    lo1 = lax.broadcasted_iota(jnp.int32, (BLK, LANES), 1) < HEAD_DIM
    k_t = kt_ref[:, r0:r0 + BLK]
    k_prev_t = kprev_t[...]
    for g in range(ATTN_KV_HEADS):
        hd = slice(g * HEAD_DIM, (g + 1) * HEAD_DIM)
        kbd[g, 0:HEAD_DIM, 0:BLK] = k_prev_t[hd]
        kbd[g, 0:HEAD_DIM, BLK:2 * BLK] = k_t[hd]
        kbd[g, HEAD_DIM:2 * HEAD_DIM, 2 * BLK:3 * BLK] = k_prev_t[hd]
        kbd[g, HEAD_DIM:2 * HEAD_DIM, 3 * BLK:4 * BLK] = k_t[hd]
    kprev_t[...] = k_t
    v_cur = p_ref[rows,C_VA:C_QG]
    v_swap = pltpu.roll(v_cur, HEAD_DIM, 1)
    vbd[0, BLK:2 * BLK] = jnp.where(lo1, v_cur, 0.0).astype(BF16)
    vbd[0, 3 * BLK:4 * BLK] = jnp.where(lo1, 0.0, v_swap).astype(BF16)
    vbd[1, BLK:2 * BLK] = jnp.where(lo1, v_swap, 0.0).astype(BF16)
    vbd[1, 3 * BLK:4 * BLK] = jnp.where(lo1, 0.0, v_cur).astype(BF16)
    chunks_per_kv = ATTN_HEADS // ATTN_KV_HEADS // 2
    for g in range(ATTN_KV_HEADS):
        c_first = g * chunks_per_kv
        qs = jnp.concatenate(
            [p_ref[rows,C_QA + (c_first + c) * LANES:C_QA + (c_first + c + 1) * LANES].astype(BF16)
             for c in range(chunks_per_kv)], axis=0)
        mb = jnp.concatenate([mb_scr[table, c_first + c] for c in range(chunks_per_kv)], axis=0)
        s = jnp.dot(qs, kbd[g], preferred_element_type=F32) + mb
        probs, inv = [], []
        for e in range(2):
            se = s[:, e * 2 * BLK:(e + 1) * 2 * BLK]
            sk = jnp.concatenate(
                [jnp.full((BLK, 1), sink_ref[2 * (c_first + c) + e], F32) for c in range(chunks_per_kv)],
                axis=0)
            m = jnp.maximum(jnp.max(se, axis=-1, keepdims=True), sk)
            pe = jnp.exp(se - m)
            den = jnp.sum(pe, axis=-1, keepdims=True) + jnp.exp(sk - m)
            probs.append(pe.astype(BF16))
            inv.append(1.0 / den)
        o = jnp.dot(jnp.concatenate(probs, axis=1), vbd[g], preferred_element_type=F32)
        lo_rows = lax.broadcasted_iota(jnp.int32, o.shape, 1) < HEAD_DIM
        o = o * jnp.where(lo_rows, inv[0], inv[1])
        for c in range(chunks_per_kv):
            omix_ref[rows,(c_first + c) * LANES:(c_first + c + 1) * LANES] = (
                o[c * BLK:(c + 1) * BLK].astype(omix_ref.dtype))
    for g in range(ATTN_KV_HEADS):
        vbd[g, 0:BLK] = vbd[g, BLK:2 * BLK]
        vbd[g, 2 * BLK:3 * BLK] = vbd[g, 3 * BLK:4 * BLK]


def _mixer_gla(pairs, p_ref, r0, lev_ref, gn_ref, omix_ref, st_ref, st_scr):
    rows = slice(r0, r0 + BLK)
    lo1 = lax.broadcasted_iota(jnp.int32, (BLK, LANES), 1) < HEAD_DIM
    lo_bf = jnp.where(lo1, 1.0, 0.0).astype(BF16)
    hi_bf = jnp.where(lo1, 0.0, 1.0).astype(BF16)
    row = lax.broadcasted_iota(jnp.int32, (BLK, LANES), 0)
    zero_blk = jnp.zeros((BLK, LANES), BF16)
    for c in pairs:
        c0 = c * LANES
        q_at = lambda a, z: p_ref[r0 + a:r0 + z, C_QG + c0:C_QG + c0 + LANES]
        k_at = lambda a, z: p_ref[r0 + a:r0 + z, C_KG + c0:C_KG + c0 + LANES]
        b_at = lambda a, z: p_ref[r0 + a:r0 + z, C_LA + c0:C_LA + c0 + LANES]

        def pair_scores(qtb, ktb):
            rhs = jnp.concatenate([ktb * lo_bf, ktb * hi_bf], axis=0)
            return _nt_dot(qtb, rhs)

        s0 = pair_scores(q_at(0, BLK).astype(BF16), k_at(0, BLK).astype(BF16))
        sc = [jnp.where(lev_ref[...] == 0, s0[:, e * BLK:(e + 1) * BLK], 0.0) for e in range(2)]
        for level in range(1, N_LEVELS + 1):
            m = 1 << (level - 1)
            if m >= SUBLANES:
                qs, ks = [], []
                zeros = jnp.zeros((m, LANES), BF16)
                for g in range(BLK // (2 * m)):
                    lo_a, up_a, up_z = g * 2 * m, g * 2 * m + m, (g + 1) * 2 * m
                    rb = jnp.broadcast_to(b_at(up_a - 1, up_a), (m, LANES))
                    qs += [zeros, (q_at(up_a, up_z) * jnp.exp(b_at(up_a, up_z) - rb)).astype(BF16)]
                    ks += [(k_at(lo_a, up_a) * jnp.exp(rb - b_at(lo_a, up_a))).astype(BF16), zeros]
                qtb = jnp.concatenate(qs, axis=0)
                ktb = jnp.concatenate(ks, axis=0)
            else:
                bc = b_at(0, BLK)
                d = bc - _boundary_rows(p_ref, r0, C_LA + c0, bc, row, level)
                upper = ((row >> (level - 1)) & 1) == 1
                qtb = (q_at(0, BLK) * jnp.exp(jnp.where(upper, d, NEG))).astype(BF16)
                ktb = (k_at(0, BLK) * jnp.exp(jnp.where(upper, NEG, -d))).astype(BF16)
            sl = pair_scores(qtb, ktb)
            sc = [jnp.where(lev_ref[...] == level, sl[:, e * BLK:(e + 1) * BLK], sc[e])
                  for e in range(2)]
        sc = jnp.concatenate(sc, axis=1)
        qc, kc, bc = q_at(0, BLK), k_at(0, BLK), b_at(0, BLK)

        b_last = bc[BLK - 1:BLK, :]
        v0 = p_ref[rows,C_VG + 2 * c0:C_VG + 2 * c0 + LANES].astype(BF16)
        v1 = p_ref[rows,C_VG + 2 * c0 + LANES:C_VG + 2 * c0 + 2 * LANES].astype(BF16)
        v_bd = jnp.concatenate([jnp.concatenate([v0, zero_blk], axis=1),
                                jnp.concatenate([zero_blk, v1], axis=1)], axis=0)
        st_c = st_scr[:, c0:c0 + LANES]
        stb = st_c.astype(BF16)
        st_rhs = jnp.concatenate([stb * lo_bf, stb * hi_bf], axis=0)
        o = (jnp.dot(sc.astype(BF16), v_bd, preferred_element_type=F32)
             + _nt_dot((qc * jnp.exp(bc)).astype(BF16), st_rhs))
        kd = (kc * jnp.exp(b_last - bc)).astype(BF16)
        upd = _tn_dot(jnp.concatenate([v0, v1], axis=1), kd)
        new_st = st_c * jnp.exp(b_last) + jnp.where(lo1, upd[:BLK], upd[BLK:])
        st_scr[:, c0:c0 + LANES] = new_st
        st_ref[:, c0:c0 + LANES] = new_st
        for e in range(2):
            h = 2 * c + e
            oh = o[:, e * LANES:(e + 1) * LANES]
            og = oh * lax.rsqrt(jnp.mean(oh * oh, axis=-1, keepdims=True) + EPS) * gn_ref[...]
            rg = p_ref[rows,C_RG + h * GLA_DV:C_RG + (h + 1) * GLA_DV]
            gated = og * (rg * _sigmoid(rg))
            omix_ref[rows,ATTN_WIDTH + h * GLA_DV:ATTN_WIDTH + (h + 1) * GLA_DV] = (
                gated.astype(omix_ref.dtype))


def _prompt_mixer_kernel(relb_ref, sink_ref, p_ref, kt_ref, bucket_ref, lev_ref, gn_ref,
                         omix_ref, st_ref, kbd, vbd, kprev_t, st_scr, mb_scr):
    i = pl.program_id(0)

    @pl.when(i == 0)
    def _init():
        _mixer_init(relb_ref, bucket_ref, kbd, vbd, kprev_t, st_scr, mb_scr)

    _mixer_attention(p_ref, kt_ref, 0, jnp.where(i == 0, 1, 0), sink_ref, omix_ref, kbd, vbd,
                     kprev_t, mb_scr)
    _mixer_gla(range(GLA_HEADS // 2), p_ref, 0, lev_ref, gn_ref, omix_ref, st_ref, st_scr)


def _prompt_mixer(p, kt, relb, sinks, gn):
    t = p.shape[0]
    smem = pl.BlockSpec(memory_space=pltpu.SMEM)
    return pl.pallas_call(
        _prompt_mixer_kernel,
        grid=(t // BLK,),
        in_specs=[
            smem, smem,
            pl.BlockSpec((BLK, P_WIDTH), lambda i: (i, 0)),
            pl.BlockSpec((KV_WIDTH, BLK), lambda i: (0, i)),
            _const_spec((2, BLK, 2 * BLK)),
            _const_spec((BLK, BLK)),
            _const_spec((1, GLA_DV)),
        ],
        out_specs=[
            pl.BlockSpec((BLK, MIX_WIDTH), lambda i: (i, 0)),
            pl.BlockSpec((GLA_DV, GLA_QK_WIDTH), lambda i: (0, 0)),
        ],
        out_shape=[
            jax.ShapeDtypeStruct((t, MIX_WIDTH), BF16),
            jax.ShapeDtypeStruct((GLA_DV, GLA_QK_WIDTH), F32),
        ],
        scratch_shapes=[
            pltpu.VMEM((ATTN_KV_HEADS, 2 * HEAD_DIM, 4 * BLK), BF16),
            pltpu.VMEM((ATTN_KV_HEADS, 4 * BLK, 2 * HEAD_DIM), BF16),
            pltpu.VMEM((KV_WIDTH, BLK), BF16),
            pltpu.VMEM((GLA_DV, GLA_QK_WIDTH), F32),
            pltpu.VMEM((2, ATTN_HEADS // 2, BLK, 4 * BLK), F32),
        ],
        compiler_params=pltpu.CompilerParams(
            dimension_semantics=("arbitrary",), vmem_limit_bytes=VMEM_LIMIT),
        name="prompt_mixer",
    )(relb, sinks, p, kt, jnp.asarray(_BUCKET_PROMPT), jnp.asarray(_LEV), gn)


def _sample_mixer_kernel(ps_ref, pfull_ref, ck_ref, cv_ref, st_ref, relbt_ref, sink_ref, bucket_ref,
                         gn_ref, omix_ref, kwin_ref, vwin_ref, stout_ref, lat_scr, kqt_scr,
                         bias_scr, s_scr, o_scr, og_scr):
    i = pl.program_id(0)
    nb = pfull_ref.shape[0]

    @pl.when(i == 0)
    def _init():
        lat_scr[...] = _split3_rows(pfull_ref[:, C_LA:P_WIDTH].T)
        kqt_scr[0:GLA_QK_WIDTH] = pfull_ref[:, C_KG:C_VG].T.astype(BF16)
        kqt_scr[GLA_QK_WIDTH:2 * GLA_QK_WIDTH] = pfull_ref[:, C_QG:C_KG].T.astype(BF16)
        bk = jnp.broadcast_to(bucket_ref[...], (ATTN_HEADS, WINDOW))
        acc = jnp.zeros((ATTN_HEADS, WINDOW), F32)
        for b in range(N_BUCKETS):
            acc = jnp.where(bk == b, relbt_ref[:, b:b + 1], acc)
        bias_scr[...] = acc

    lo = lax.broadcasted_iota(jnp.int32, (1, LANES), 1) < HEAD_DIM
    sub = lax.broadcasted_iota(jnp.int32, (ATTN_HEADS, LANES), 0)
    wrow = lax.broadcasted_iota(jnp.int32, (WINDOW, KV_WIDTH), 0)
    heads_per_kv = ATTN_HEADS // ATTN_KV_HEADS

    n_of_col = i * SAMPLE_BLK + lax.broadcasted_iota(jnp.int32, (nb, SAMPLE_BLK * LANES), 1) // LANES
    pick = jnp.where(lax.broadcasted_iota(jnp.int32, (nb, SAMPLE_BLK * LANES), 0) == n_of_col,
                     1.0, 0.0).astype(BF16)
    la_b = (jnp.dot(lat_scr[0:GLA_QK_WIDTH], pick, preferred_element_type=F32)
            + jnp.dot(lat_scr[GLA_QK_WIDTH:2 * GLA_QK_WIDTH], pick, preferred_element_type=F32)
            + jnp.dot(lat_scr[2 * GLA_QK_WIDTH:3 * GLA_QK_WIDTH], pick, preferred_element_type=F32))
    kq_b = jnp.dot(kqt_scr[...], pick, preferred_element_type=F32)

    for j in range(SAMPLE_BLK):
        k_new = ps_ref[j:j + 1, C_KA:C_VA]
        v_new = ps_ref[j:j + 1, C_VA:C_QG]
        kwin_ref[j] = jnp.where(wrow == WINDOW - 1, k_new, pltpu.roll(ck_ref[j], WINDOW - 1, 0))
        vwin_ref[j] = jnp.where(wrow == WINDOW - 1, v_new, pltpu.roll(cv_ref[j], WINDOW - 1, 0))

    for j in range(SAMPLE_BLK):
        qexp = jnp.zeros((ATTN_HEADS, LANES), F32)
        for c in range(ATTN_HEADS // 2):
            chunk = ps_ref[j:j + 1, C_QA + c * LANES:C_QA + (c + 1) * LANES]
            swapped = pltpu.roll(chunk, HEAD_DIM, 1)
            if (2 * c) // heads_per_kv == 0:
                rows = (jnp.where(lo, chunk, 0.0), jnp.where(lo, swapped, 0.0))
            else:
                rows = (jnp.where(lo, 0.0, swapped), jnp.where(lo, 0.0, chunk))
            for e in range(2):
                qexp = jnp.where(sub == 2 * c + e, rows[e], qexp)
        s_scr[j * ATTN_HEADS:(j + 1) * ATTN_HEADS] = _nt_dot(qexp.astype(BF16), kwin_ref[j].astype(BF16))

    tile = lambda x: jnp.concatenate([x] * SAMPLE_BLK, axis=0)
    sink = tile(sink_ref[...])
    s = s_scr[...] + tile(bias_scr[...])
    m = jnp.maximum(jnp.max(s, axis=-1, keepdims=True), sink)
    pe = jnp.exp(s - m)
    inv_den = 1.0 / (jnp.sum(pe, axis=-1, keepdims=True) + jnp.exp(sink - m))
    peb = pe.astype(BF16)
    for j in range(SAMPLE_BLK):
        o_scr[j * ATTN_HEADS:(j + 1) * ATTN_HEADS] = jnp.dot(
            peb[j * ATTN_HEADS:(j + 1) * ATTN_HEADS], vwin_ref[j].astype(BF16),
            preferred_element_type=F32)
    o_all = o_scr[...] * inv_den
    o_swap = pltpu.roll(o_all, HEAD_DIM, 1)
    for j in range(SAMPLE_BLK):
        r = j * ATTN_HEADS
        for c in range(ATTN_HEADS // 2):
            if (2 * c) // heads_per_kv == 0:
                piece = jnp.where(lo, o_all[r + 2 * c:r + 2 * c + 1, :], o_swap[r + 2 * c + 1:r + 2 * c + 2, :])
            else:
                piece = jnp.where(lo, o_swap[r + 2 * c:r + 2 * c + 1, :], o_all[r + 2 * c + 1:r + 2 * c + 2, :])
            omix_ref[j:j + 1, c * LANES:(c + 1) * LANES] = piece

    for j in range(SAMPLE_BLK):
        cols = slice(j * LANES, (j + 1) * LANES)
        for h in range(GLA_HEADS):
            rs = slice(h * GLA_DK, (h + 1) * GLA_DK)
            qs = slice(GLA_QK_WIDTH + h * GLA_DK, GLA_QK_WIDTH + (h + 1) * GLA_DK)
            v_row = ps_ref[j:j + 1, C_VG + h * GLA_DV:C_VG + (h + 1) * GLA_DV]
            s_new = jnp.exp(la_b[rs, cols]) * st_ref[j, h] + kq_b[rs, cols] * v_row
            stout_ref[j, h] = s_new
            og_scr[j:j + 1, h * GLA_DV:(h + 1) * GLA_DV] = jnp.sum(
                kq_b[qs, cols] * s_new, axis=0, keepdims=True)
    for h in range(GLA_HEADS):
        hs = slice(h * GLA_DV, (h + 1) * GLA_DV)
        og = og_scr[:, hs]
        og = og * lax.rsqrt(jnp.mean(og * og, axis=-1, keepdims=True) + EPS) * gn_ref[...]
        rg = ps_ref[:, C_RG + h * GLA_DV:C_RG + (h + 1) * GLA_DV]
        omix_ref[:, ATTN_WIDTH + h * GLA_DV:ATTN_WIDTH + (h + 1) * GLA_DV] = og * (rg * _sigmoid(rg))


def _sample_mixer(ps, cache_k, cache_v, state, relbt, sinks_col, gn):
    nb = ps.shape[0]
    blk3 = lambda i: (i, 0, 0)
    blk4 = lambda i: (i, 0, 0, 0)
    return pl.pallas_call(
        _sample_mixer_kernel,
        grid=(nb // SAMPLE_BLK,),
        in_specs=[
            pl.BlockSpec((SAMPLE_BLK, P_WIDTH), lambda i: (i, 0)),
            _const_spec((nb, P_WIDTH)),
            pl.BlockSpec((SAMPLE_BLK, WINDOW, KV_WIDTH), blk3),
            pl.BlockSpec((SAMPLE_BLK, WINDOW, KV_WIDTH), blk3),
            pl.BlockSpec((SAMPLE_BLK, GLA_HEADS, GLA_DK, GLA_DV), blk4),
            _const_spec((ATTN_HEADS, N_BUCKETS)),
            _const_spec((ATTN_HEADS, 1)),
            _const_spec((1, WINDOW)),
            _const_spec((1, GLA_DV)),
        ],
        out_specs=[
            pl.BlockSpec((SAMPLE_BLK, MIX_WIDTH), lambda i: (i, 0)),
            pl.BlockSpec((SAMPLE_BLK, WINDOW, KV_WIDTH), blk3),
            pl.BlockSpec((SAMPLE_BLK, WINDOW, KV_WIDTH), blk3),
            pl.BlockSpec((SAMPLE_BLK, GLA_HEADS, GLA_DK, GLA_DV), blk4),
        ],
        out_shape=[
            jax.ShapeDtypeStruct((nb, MIX_WIDTH), F32),
            jax.ShapeDtypeStruct((nb, WINDOW, KV_WIDTH), F32),
            jax.ShapeDtypeStruct((nb, WINDOW, KV_WIDTH), F32),
            jax.ShapeDtypeStruct((nb, GLA_HEADS, GLA_DK, GLA_DV), F32),
        ],
        scratch_shapes=[
            pltpu.VMEM((3 * GLA_QK_WIDTH, nb), BF16),
            pltpu.VMEM((2 * GLA_QK_WIDTH, nb), BF16),
            pltpu.VMEM((ATTN_HEADS, WINDOW), F32),
            pltpu.VMEM((SAMPLE_BLK * ATTN_HEADS, WINDOW), F32),
            pltpu.VMEM((SAMPLE_BLK * ATTN_HEADS, KV_WIDTH), F32),
            pltpu.VMEM((SAMPLE_BLK, GLA_WIDTH), F32),
        ],
        compiler_params=pltpu.CompilerParams(
            dimension_semantics=("arbitrary",), vmem_limit_bytes=VMEM_LIMIT),
        name="sample_mixer",
    )(ps, ps, cache_k, cache_v, state, relbt, sinks_col, jnp.asarray(_BUCKET_SAMPLE), gn)


def _finish_kernel(x_ref, mix_ref, xs_ref, mixs_ref, wo_ref, g_ref, wg_ref, wu_ref, wd_ref,
                   y_ref, ys_ref, *, ff_chunks):
    weights = (wo_ref, g_ref, wg_ref, wu_ref, wd_ref)
    _finish_rows(x_ref, mix_ref, *weights, y_ref, ff_chunks=ff_chunks)

    @pl.when(pl.program_id(0) == pl.num_programs(0) - 1)
    def _samples():
        _finish_rows(xs_ref, mixs_ref, *weights, ys_ref, ff_chunks=ff_chunks)


def _finish_rows(x_ref, mix_ref, wo_ref, g_ref, wg_ref, wu_ref, wd_ref, y_ref, *, ff_chunks):
    h = x_ref[...] + jnp.dot(mix_ref[...].astype(BF16), wo_ref[...], preferred_element_type=F32)
    r = lax.rsqrt(jnp.mean(h * h, axis=-1, keepdims=True) + EPS)
    z = (h * g_ref[...]).astype(BF16)
    n_tiles = wd_ref.shape[0] // MXU_TILE
    acc = h
    for c in range(ff_chunks):
        c0 = ((c * n_tiles) // ff_chunks) * MXU_TILE
        c1 = (((c + 1) * n_tiles) // ff_chunks) * MXU_TILE
        gate = jnp.dot(z, wg_ref[:, c0:c1], preferred_element_type=F32) * r
        up = jnp.dot(z, wu_ref[:, c0:c1], preferred_element_type=F32) * r
        act = ((gate * _sigmoid(gate)) * up).astype(BF16)
        acc = acc + jnp.dot(act, wd_ref[c0:c1, :], preferred_element_type=F32)
    y_ref[...] = acc


def _finish(x, mix, xs, mixs, wo, g_ffn, wg, wu, wd, rows):
    t = x.shape[0]
    ns = xs.shape[0]
    d_ff = wd.shape[0]
    assert d_ff % MXU_TILE == 0
    return pl.pallas_call(
        functools.partial(_finish_kernel, ff_chunks=FF_CHUNKS),
        grid=(t // rows,),
        in_specs=[
            pl.BlockSpec((rows, D_MODEL), lambda i: (i, 0)),
            pl.BlockSpec((rows, MIX_WIDTH), lambda i: (i, 0)),
            _const_spec((ns, D_MODEL)),
            _const_spec((ns, MIX_WIDTH)),
            _const_spec((MIX_WIDTH, D_MODEL)),
            _const_spec((1, D_MODEL)),
            _const_spec((D_MODEL, d_ff)),
            _const_spec((D_MODEL, d_ff)),
            _const_spec((d_ff, D_MODEL)),
        ],
        out_specs=[pl.BlockSpec((rows, D_MODEL), lambda i: (i, 0)),
                   pl.BlockSpec((ns, D_MODEL), lambda i: (0, 0))],
        out_shape=[jax.ShapeDtypeStruct((t, D_MODEL), F32),
                   jax.ShapeDtypeStruct((ns, D_MODEL), F32)],
        compiler_params=pltpu.CompilerParams(
            dimension_semantics=("arbitrary",), vmem_limit_bytes=VMEM_LIMIT),
        name="finish",
    )(x, mix, xs, mixs, wo, g_ffn, wg, wu, wd)


PROMPT_ROWS = 512
PROJ_ROWS = 1024


def kernel(x_prompt, x_sample, cache_k, cache_v, state_gla, attn_norm_g, w_in, q_norm_g, k_norm_g,
           attn_sinks, rel_bias, w_gla_gate2, b_gla_gate, gla_norm_g, w_o, ffn_norm_g, w_gate, w_up,
           w_down):
    depth = w_in.shape[0]
    batch, seq, _ = x_prompt.shape
    dec_batch, dec_seq, _ = x_sample.shape
    wb = cache_k.shape[2]
    assert batch == 1 and dec_seq == 1 and wb == WINDOW
    assert seq % PROMPT_ROWS == 0 and seq % PROJ_ROWS == 0
    assert dec_batch % SAMPLE_BLK == 0 and dec_batch % LANES == 0
    assert rel_bias.shape == (N_BUCKETS, ATTN_HEADS)

    xp = x_prompt.reshape(seq, D_MODEL)
    xs = x_sample.reshape(dec_batch, D_MODEL)
    relb_flat = rel_bias.reshape(-1)
    relb_t = rel_bias.T
    outs = ([], [], [], [], [], [])
    for l in range(depth):
        kv_end = ATTN_WIDTH + 2 * KV_WIDTH
        w_in_p = jnp.concatenate(
            [w_in[l, :, :kv_end],
             jnp.pad(w_in[l, :, MAIN_WIDTH:], ((0, 0), (0, RANK_PAD - GLA_RANK))),
             w_in[l, :, kv_end:MAIN_WIDTH]], axis=1).astype(BF16)
        w2p = jnp.pad(w_gla_gate2[l], ((0, RANK_PAD - GLA_RANK), (0, 0))).astype(BF16)
        proj_w = (attn_norm_g[l][None, :], w_in_p, jnp.tile(q_norm_g[l], ATTN_HEADS)[None, :],
                  jnp.tile(k_norm_g[l], ATTN_KV_HEADS)[None, :], w2p, b_gla_gate[l][None, :])
        fin_w = (w_o[l].astype(BF16), ffn_norm_g[l][None, :], w_gate[l].astype(BF16),
                 w_up[l].astype(BF16), w_down[l].astype(BF16))
        gn = gla_norm_g[l][None, :]

        pp, kt_p, ps = _project(xp, xs, *proj_w, rows=PROJ_ROWS)
        mix_p, st_p = _prompt_mixer(pp, kt_p, relb_flat, attn_sinks[l], gn)
        mix_s, kwin, vwin, st_s = _sample_mixer(
            ps, cache_k[l].reshape(dec_batch, wb, KV_WIDTH), cache_v[l].reshape(dec_batch, wb, KV_WIDTH),
            state_gla[l].astype(F32), relb_t, attn_sinks[l][:, None], gn)
        xp_in = xp
        xp, xs = _finish(xp_in, mix_p, xs, mix_s, *fin_w, rows=PROMPT_ROWS)
        outs[0].append(pp[seq - wb:, C_KA:C_VA].reshape(batch, wb, ATTN_KV_HEADS, HEAD_DIM))
        outs[1].append(pp[seq - wb:, C_VA:C_QG].reshape(batch, wb, ATTN_KV_HEADS, HEAD_DIM))
        outs[2].append(st_p.T.reshape(batch, GLA_HEADS, GLA_DK, GLA_DV).astype(state_gla.dtype))
        outs[3].append(kwin.reshape(dec_batch, wb, ATTN_KV_HEADS, HEAD_DIM))
        outs[4].append(vwin.reshape(dec_batch, wb, ATTN_KV_HEADS, HEAD_DIM))
        outs[5].append(st_s.astype(state_gla.dtype))

    y_prompt = xp.reshape(batch, seq, D_MODEL)
    y_sample = xs.reshape(dec_batch, dec_seq, D_MODEL)
    return (y_prompt, y_sample) + tuple(jnp.stack(o) for o in outs)
```

```python
import functools
import math

import numpy as np
import jax
import jax.numpy as jnp
from jax import lax
from jax.experimental import pallas as pl
from jax.experimental.pallas import tpu as pltpu

F32 = jnp.float32
BF16 = jnp.bfloat16

D_MODEL = 1024
HEAD_DIM = 64
ATTN_HEADS = 8
ATTN_KV_HEADS = 2
WINDOW = 128
N_BUCKETS = 32
MAX_DISTANCE = 128
GLA_HEADS = 4
GLA_DK = 64
GLA_DV = 128
GLA_RANK = 16
GLA_TAU = 16.0
EPS = 1e-6
ATTN_WIDTH = ATTN_HEADS * HEAD_DIM
KV_WIDTH = ATTN_KV_HEADS * HEAD_DIM
GLA_QK_WIDTH = GLA_HEADS * GLA_DK
GLA_WIDTH = GLA_HEADS * GLA_DV
MIX_WIDTH = ATTN_WIDTH + GLA_WIDTH
MAIN_WIDTH = ATTN_WIDTH + 2 * KV_WIDTH + 2 * GLA_QK_WIDTH + 2 * GLA_WIDTH
LANES = 128
SUBLANES = 8
MXU_TILE = 256
FF_CHUNKS = 2
RANK_PAD = LANES
IN_PAD_WIDTH = MAIN_WIDTH + RANK_PAD

C_QA = 0
C_KA = C_QA + ATTN_WIDTH
C_VA = C_KA + KV_WIDTH
C_QG = C_VA + KV_WIDTH
C_KG = C_QG + GLA_QK_WIDTH
C_VG = C_KG + GLA_QK_WIDTH
C_RG = C_VG + GLA_WIDTH
C_LA = C_RG + GLA_WIDTH
P_WIDTH = C_LA + GLA_QK_WIDTH

W_QA = 0
W_KV = W_QA + ATTN_WIDTH
W_QG = W_KV + 2 * KV_WIDTH + RANK_PAD
W_VG = W_QG + 2 * GLA_QK_WIDTH
W_RG = W_VG + GLA_WIDTH

BLK = 128
N_LEVELS = 7
NEG = -1e30
ATTN_SCALE = HEAD_DIM ** -0.5
SAMPLE_BLK = 8
VMEM_LIMIT = 56 * 1024 * 1024


def _t5_bucket_np(dist):
    n = np.maximum(dist, 0)
    max_exact = N_BUCKETS // 2
    nf = np.maximum(n, 1).astype(np.float64)
    large = max_exact + (np.log(nf / max_exact) / math.log(MAX_DISTANCE / max_exact)
                         * (N_BUCKETS - max_exact)).astype(np.int32)
    large = np.minimum(large, N_BUCKETS - 1)
    return np.where(n < max_exact, n, large).astype(np.int32)


def _prompt_bucket_tables():
    i = np.arange(BLK)[:, None]
    j = np.arange(2 * BLK)[None, :]
    dist = BLK + i - j
    band = (dist >= 0) & (dist < WINDOW)
    bucket = _t5_bucket_np(dist)
    t0 = np.where(band, bucket, -1)
    t1 = np.where(band & (j >= BLK), bucket, -1)
    return np.stack([t0, t1]).astype(np.int32)


def _level_tables():
    t = np.arange(BLK)[:, None]
    s = np.arange(BLK)[None, :]
    x = t ^ s
    lev = np.where(x > 0, np.floor(np.log2(np.maximum(x, 1))).astype(np.int32) + 1, 0)
    lev = np.where(s > t, -1, lev).astype(np.int32)
    tri = (s <= t).astype(np.float32)
    return lev, np.concatenate([tri, tri, tri], axis=1)


_BUCKET_PROMPT = _prompt_bucket_tables()
_LEV, _TRI3 = _level_tables()
_BUCKET_SAMPLE = _t5_bucket_np((WINDOW - 1) - np.arange(WINDOW))[None, :].astype(np.int32)


def _nt_dot(a, b):
    return lax.dot_general(a, b, (((1,), (1,)), ((), ())), preferred_element_type=F32)


def _tn_dot(a, b):
    return lax.dot_general(a, b, (((0,), (0,)), ((), ())), preferred_element_type=F32)


def _head_mean_sq(x):
    lo = lax.broadcasted_iota(jnp.int32, (x.shape[0], LANES), 1) < HEAD_DIM
    outs = []
    for c in range(x.shape[1] // LANES):
        y = x[:, c * LANES:(c + 1) * LANES]
        y = y * y
        s_lo = jnp.sum(jnp.where(lo, y, 0.0), axis=-1, keepdims=True)
        s_hi = jnp.sum(jnp.where(lo, 0.0, y), axis=-1, keepdims=True)
        outs.append(jnp.where(lo, s_lo, s_hi) * (1.0 / HEAD_DIM))
    return outs[0] if len(outs) == 1 else jnp.concatenate(outs, axis=1)


def _sigmoid(x):
    return 1.0 / (1.0 + jnp.exp(-x))


def _split3_rows(x):
    hi = x.astype(BF16)
    r1 = x - hi.astype(F32)
    mid = r1.astype(BF16)
    lo = (r1 - mid.astype(F32)).astype(BF16)
    return jnp.concatenate([hi, mid, lo], axis=0)


def _proj_kernel(x_ref, xs_ref, g_ref, w_ref, qn_ref, kn_ref, w2_ref, b2_ref, tri_ref,
                 out_ref, kt_ref, outs_ref):
    weights = (g_ref, w_ref, qn_ref, kn_ref, w2_ref, b2_ref, tri_ref)
    _proj_rows(x_ref, *weights, out_ref, kt_ref, block_cumsum=True)

    @pl.when(pl.program_id(0) == pl.num_programs(0) - 1)
    def _samples():
        _proj_rows(xs_ref, *weights, outs_ref, None, block_cumsum=False)


def _proj_rows(x_ref, g_ref, w_ref, qn_ref, kn_ref, w2_ref, b2_ref, tri_ref, out_ref, kt_ref, *,
               block_cumsum):
    x = x_ref[...]
    r = lax.rsqrt(jnp.mean(x * x, axis=-1, keepdims=True) + EPS)
    n = (x * g_ref[...]).astype(BF16)

    def seg(c0, c1):
        return jnp.dot(n, w_ref[:, c0:c1], preferred_element_type=F32) * r

    kvl = seg(W_KV, W_QG)
    lr = kvl[:, 2 * KV_WIDTH:].astype(BF16)
    z = jnp.dot(lr, w2_ref[...], preferred_element_type=F32) + b2_ref[...]
    log_a = (jnp.minimum(z, 0.0) - jnp.log1p(jnp.exp(-jnp.abs(z)))) / GLA_TAU
    if block_cumsum:
        for blk in range(x.shape[0] // BLK):
            rows = slice(blk * BLK, (blk + 1) * BLK)
            out_ref[rows, C_LA:P_WIDTH] = jnp.dot(
                tri_ref[...], _split3_rows(log_a[rows]), preferred_element_type=F32)
    else:
        out_ref[:, C_LA:P_WIDTH] = log_a
    k = kvl[:, :KV_WIDTH]
    k = k * lax.rsqrt(_head_mean_sq(k) + EPS) * kn_ref[...]
    out_ref[:, C_KA:C_VA] = k
    if kt_ref is not None:
        kt_ref[...] = k.T.astype(BF16)
    out_ref[:, C_VA:C_QG] = kvl[:, KV_WIDTH:2 * KV_WIDTH]
    q = seg(W_QA, W_KV)
    out_ref[:, C_QA:C_KA] = q * lax.rsqrt(_head_mean_sq(q) + EPS) * qn_ref[...] * ATTN_SCALE
    qk_g = seg(W_QG, W_VG)
    out_ref[:, C_QG:C_KG] = qk_g[:, :GLA_QK_WIDTH] * (GLA_DK ** -0.5)
    out_ref[:, C_KG:C_VG] = qk_g[:, GLA_QK_WIDTH:]
    out_ref[:, C_VG:C_RG] = seg(W_VG, W_RG)
    out_ref[:, C_RG:C_LA] = seg(W_RG, IN_PAD_WIDTH)


def _const_spec(shape):
    nd = len(shape)
    return pl.BlockSpec(shape, lambda i: (0,) * nd, pipeline_mode=pl.Buffered(1))


def _project(x, xs, g_attn, w_in_p, qn, kn, w2p, b2, rows):
    t = x.shape[0]
    ns = xs.shape[0]
    return pl.pallas_call(
        _proj_kernel,
        grid=(t // rows,),
        in_specs=[
            pl.BlockSpec((rows, D_MODEL), lambda i: (i, 0)),
            _const_spec((ns, D_MODEL)),
            _const_spec((1, D_MODEL)),
            _const_spec((D_MODEL, IN_PAD_WIDTH)),
            _const_spec((1, ATTN_WIDTH)),
            _const_spec((1, KV_WIDTH)),
            _const_spec((RANK_PAD, GLA_QK_WIDTH)),
            _const_spec((1, GLA_QK_WIDTH)),
            _const_spec((BLK, 3 * BLK)),
        ],
        out_specs=[pl.BlockSpec((rows, P_WIDTH), lambda i: (i, 0)),
                   pl.BlockSpec((KV_WIDTH, rows), lambda i: (0, i)),
                   pl.BlockSpec((ns, P_WIDTH), lambda i: (0, 0))],
        out_shape=[jax.ShapeDtypeStruct((t, P_WIDTH), F32),
                   jax.ShapeDtypeStruct((KV_WIDTH, t), BF16),
                   jax.ShapeDtypeStruct((ns, P_WIDTH), F32)],
        compiler_params=pltpu.CompilerParams(
            dimension_semantics=("arbitrary",), vmem_limit_bytes=VMEM_LIMIT),
        name="proj",
    )(x, xs, g_attn, w_in_p, qn, kn, w2p, b2, jnp.asarray(_TRI3, BF16))


def _boundary_rows(b_ref, r0, c0, b, row, level):
    m = 1 << (level - 1)
    if 2 * m >= SUBLANES:
        pieces = [jnp.broadcast_to(b_ref[r0 + g * 2 * m + m - 1:r0 + g * 2 * m + m, c0:c0 + LANES],
                                   (2 * m, LANES))
                  for g in range(BLK // (2 * m))]
        return pieces[0] if len(pieces) == 1 else jnp.concatenate(pieces, axis=0)
    pos = row & (2 * m - 1)
    tiles = b.reshape(BLK // SUBLANES, SUBLANES, LANES)
    out = b
    for p in range(2 * m):
        shift = (m - 1) - p
        if shift != 0:
            rolled = pltpu.roll(tiles, (-shift) % SUBLANES, 1).reshape(BLK, LANES)
            out = jnp.where(pos == p, rolled, out)
    return out


def _mixer_init(relb_ref, bucket_ref, kbd, vbd, kprev_t, vprev, st_scr, mb_scr):
    kbd[...] = jnp.zeros_like(kbd)
    vbd[...] = jnp.zeros_like(vbd)
    for g in range(ATTN_KV_HEADS):
        for half in range(2):
            vbd[g, 2 * half * BLK:(2 * half + 2) * BLK,
                KV_WIDTH + half * HEAD_DIM:KV_WIDTH + (half + 1) * HEAD_DIM] = (
                    jnp.ones((2 * BLK, HEAD_DIM), BF16))
    kprev_t[...] = jnp.zeros_like(kprev_t)
    vprev[...] = jnp.zeros_like(vprev)
    st_scr[...] = jnp.zeros_like(st_scr)
    bk = bucket_ref[0]
    acc = [jnp.zeros(bk.shape, F32) for _ in range(ATTN_HEADS)]
    for b in range(N_BUCKETS):
        hit = bk == b
        for h in range(ATTN_HEADS):
            acc[h] = jnp.where(hit, relb_ref[b * ATTN_HEADS + h], acc[h])
    for tb in range(2):
        masked = bucket_ref[tb] < 0
        for h in range(ATTN_HEADS):
            mb_scr[tb, h // 2, :, (h % 2) * 2 * BLK:(h % 2 + 1) * 2 * BLK] = (
                jnp.where(masked, NEG, acc[h]))


def _mixer_attention(p_ref, kt_ref, r0, table, sink_ref, omix_ref, kbd, vbd, kprev_t, vprev,
                     mb_scr):
    rows = slice(r0, r0 + BLK)
    lo1 = lax.broadcasted_iota(jnp.int32, (BLK, LANES), 1) < HEAD_DIM
    k_t = kt_ref[:, r0:r0 + BLK]
    k_prev_t = kprev_t[...]
    for g in range(ATTN_KV_HEADS):
        hd = slice(g * HEAD_DIM, (g + 1) * HEAD_DIM)
        kbd[g, 0:HEAD_DIM, 0:BLK] = k_prev_t[hd]
        kbd[g, 0:HEAD_DIM, BLK:2 * BLK] = k_t[hd]
        kbd[g, HEAD_DIM:2 * HEAD_DIM, 2 * BLK:3 * BLK] = k_prev_t[hd]
        kbd[g, HEAD_DIM:2 * HEAD_DIM, 3 * BLK:4 * BLK] = k_t[hd]
    kprev_t[...] = k_t
    v_cur = p_ref[rows,C_VA:C_QG]
    v_swap = pltpu.roll(v_cur, HEAD_DIM, 1)
    v_parts = (jnp.where(lo1, v_cur, 0.0).astype(BF16), jnp.where(lo1, 0.0, v_swap).astype(BF16),
               jnp.where(lo1, v_swap, 0.0).astype(BF16), jnp.where(lo1, 0.0, v_cur).astype(BF16))
    for n, part in enumerate(v_parts):
        g, half = divmod(n, 2)
        vbd[g, 2 * half * BLK:(2 * half + 1) * BLK, 0:KV_WIDTH] = vprev[n]
        vbd[g, (2 * half + 1) * BLK:(2 * half + 2) * BLK, 0:KV_WIDTH] = part
        vprev[n] = part
    chunks_per_kv = ATTN_HEADS // ATTN_KV_HEADS // 2
    for g in range(ATTN_KV_HEADS):
        c_first = g * chunks_per_kv
        qs = jnp.concatenate(
            [p_ref[rows,C_QA + (c_first + c) * LANES:C_QA + (c_first + c + 1) * LANES].astype(BF16)
             for c in range(chunks_per_kv)], axis=0)
        mb = jnp.concatenate([mb_scr[table, c_first + c] for c in range(chunks_per_kv)], axis=0)
        s = jnp.dot(qs, kbd[g], preferred_element_type=F32) + mb
        prob_rows, sink_rows = [], []
        for c in range(chunks_per_kv):
            probs, sinks = [], []
            for e in range(2):
                se = s[c * BLK:(c + 1) * BLK, e * 2 * BLK:(e + 1) * 2 * BLK]
                sk = sink_ref[2 * (c_first + c) + e]
                m = jnp.maximum(jnp.max(se, axis=-1, keepdims=True), sk)
                probs.append(jnp.exp(se - m).astype(BF16))
                sinks.append(jnp.exp(sk - m))
            prob_rows.append(jnp.concatenate(probs, axis=1))
            sink_rows.append(jnp.where(lo1, sinks[0], sinks[1]))
        o_den = jnp.dot(jnp.concatenate(prob_rows, axis=0), vbd[g], preferred_element_type=F32)
        o = o_den[:, :KV_WIDTH] / (o_den[:, KV_WIDTH:] + jnp.concatenate(sink_rows, axis=0))
        for c in range(chunks_per_kv):
            omix_ref[rows,(c_first + c) * LANES:(c_first + c + 1) * LANES] = (
                o[c * BLK:(c + 1) * BLK].astype(omix_ref.dtype))


def _mixer_gla(pairs, p_ref, r0, lev_ref, gn_ref, omix_ref, st_ref, st_scr):
    rows = slice(r0, r0 + BLK)
    lo1 = lax.broadcasted_iota(jnp.int32, (BLK, LANES), 1) < HEAD_DIM
    lo_bf = jnp.where(lo1, 1.0, 0.0).astype(BF16)
    hi_bf = jnp.where(lo1, 0.0, 1.0).astype(BF16)
    row = lax.broadcasted_iota(jnp.int32, (BLK, LANES), 0)
    zero_blk = jnp.zeros((BLK, LANES), BF16)
    for c in pairs:
        c0 = c * LANES
        q_at = lambda a, z: p_ref[r0 + a:r0 + z, C_QG + c0:C_QG + c0 + LANES]
        k_at = lambda a, z: p_ref[r0 + a:r0 + z, C_KG + c0:C_KG + c0 + LANES]
        b_at = lambda a, z: p_ref[r0 + a:r0 + z, C_LA + c0:C_LA + c0 + LANES]

        def pair_scores(qtb, ktb):
            rhs = jnp.concatenate([ktb * lo_bf, ktb * hi_bf], axis=0)
            return _nt_dot(qtb, rhs)

        s0 = pair_scores(q_at(0, BLK).astype(BF16), k_at(0, BLK).astype(BF16))
        sc = [jnp.where(lev_ref[...] == 0, s0[:, e * BLK:(e + 1) * BLK], 0.0) for e in range(2)]
        for level in range(1, N_LEVELS + 1):
            m = 1 << (level - 1)
            if m >= SUBLANES:
                qs, ks = [], []
                zeros = jnp.zeros((m, LANES), BF16)
                for g in range(BLK // (2 * m)):
                    lo_a, up_a, up_z = g * 2 * m, g * 2 * m + m, (g + 1) * 2 * m
                    rb = jnp.broadcast_to(b_at(up_a - 1, up_a), (m, LANES))
                    qs += [zeros, (q_at(up_a, up_z) * jnp.exp(b_at(up_a, up_z) - rb)).astype(BF16)]
                    ks += [(k_at(lo_a, up_a) * jnp.exp(rb - b_at(lo_a, up_a))).astype(BF16), zeros]
                qtb = jnp.concatenate(qs, axis=0)
                ktb = jnp.concatenate(ks, axis=0)
            else:
                bc = b_at(0, BLK)
                d = bc - _boundary_rows(p_ref, r0, C_LA + c0, bc, row, level)
                upper = ((row >> (level - 1)) & 1) == 1
                qtb = (q_at(0, BLK) * jnp.exp(jnp.where(upper, d, NEG))).astype(BF16)
                ktb = (k_at(0, BLK) * jnp.exp(jnp.where(upper, NEG, -d))).astype(BF16)
            sl = pair_scores(qtb, ktb)
            sc = [jnp.where(lev_ref[...] == level, sl[:, e * BLK:(e + 1) * BLK], sc[e])
                  for e in range(2)]
        sc = jnp.concatenate(sc, axis=1)
        qc, kc, bc = q_at(0, BLK), k_at(0, BLK), b_at(0, BLK)

        b_last = bc[BLK - 1:BLK, :]
        v0 = p_ref[rows,C_VG + 2 * c0:C_VG + 2 * c0 + LANES].astype(BF16)
        v1 = p_ref[rows,C_VG + 2 * c0 + LANES:C_VG + 2 * c0 + 2 * LANES].astype(BF16)
        v_bd = jnp.concatenate([jnp.concatenate([v0, zero_blk], axis=1),
                                jnp.concatenate([zero_blk, v1], axis=1)], axis=0)
        st_c = st_scr[:, c0:c0 + LANES]
        stb = st_c.astype(BF16)
        st_rhs = jnp.concatenate([stb * lo_bf, stb * hi_bf], axis=0)
        o = (jnp.dot(sc.astype(BF16), v_bd, preferred_element_type=F32)
             + _nt_dot((qc * jnp.exp(bc)).astype(BF16), st_rhs))
        kd = (kc * jnp.exp(b_last - bc)).astype(BF16)
        upd = _tn_dot(jnp.concatenate([v0, v1], axis=1), kd)
        new_st = st_c * jnp.exp(b_last) + jnp.where(lo1, upd[:BLK], upd[BLK:])
        st_scr[:, c0:c0 + LANES] = new_st
        st_ref[:, c0:c0 + LANES] = new_st
        for e in range(2):
            h = 2 * c + e
            oh = o[:, e * LANES:(e + 1) * LANES]
            og = oh * lax.rsqrt(jnp.mean(oh * oh, axis=-1, keepdims=True) + EPS) * gn_ref[...]
            rg = p_ref[rows,C_RG + h * GLA_DV:C_RG + (h + 1) * GLA_DV]
            gated = og * (rg * _sigmoid(rg))
            omix_ref[rows,ATTN_WIDTH + h * GLA_DV:ATTN_WIDTH + (h + 1) * GLA_DV] = (
                gated.astype(omix_ref.dtype))


def _prompt_mixer_kernel(relb_ref, sink_ref, p_ref, kt_ref, bucket_ref, lev_ref, gn_ref,
                         omix_ref, st_ref, kbd, vbd, kprev_t, vprev, st_scr, mb_scr):
    i = pl.program_id(0)

    @pl.when(i == 0)
    def _init():
        _mixer_init(relb_ref, bucket_ref, kbd, vbd, kprev_t, vprev, st_scr, mb_scr)

    _mixer_attention(p_ref, kt_ref, 0, jnp.where(i == 0, 1, 0), sink_ref, omix_ref, kbd, vbd,
                     kprev_t, vprev, mb_scr)
    _mixer_gla(range(GLA_HEADS // 2), p_ref, 0, lev_ref, gn_ref, omix_ref, st_ref, st_scr)


def _prompt_mixer(p, kt, relb, sinks, gn):
    t = p.shape[0]
    smem = pl.BlockSpec(memory_space=pltpu.SMEM)
    return pl.pallas_call(
        _prompt_mixer_kernel,
        grid=(t // BLK,),
        in_specs=[
            smem, smem,
            pl.BlockSpec((BLK, P_WIDTH), lambda i: (i, 0)),
            pl.BlockSpec((KV_WIDTH, BLK), lambda i: (0, i)),
            _const_spec((2, BLK, 2 * BLK)),
            _const_spec((BLK, BLK)),
            _const_spec((1, GLA_DV)),
        ],
        out_specs=[
            pl.BlockSpec((BLK, MIX_WIDTH), lambda i: (i, 0)),
            pl.BlockSpec((GLA_DV, GLA_QK_WIDTH), lambda i: (0, 0)),
        ],
        out_shape=[
            jax.ShapeDtypeStruct((t, MIX_WIDTH), BF16),
            jax.ShapeDtypeStruct((GLA_DV, GLA_QK_WIDTH), F32),
        ],
        scratch_shapes=[
            pltpu.VMEM((ATTN_KV_HEADS, 2 * HEAD_DIM, 4 * BLK), BF16),
            pltpu.VMEM((ATTN_KV_HEADS, 4 * BLK, 2 * KV_WIDTH), BF16),
            pltpu.VMEM((KV_WIDTH, BLK), BF16),
            pltpu.VMEM((2 * ATTN_KV_HEADS, BLK, KV_WIDTH), BF16),
            pltpu.VMEM((GLA_DV, GLA_QK_WIDTH), F32),
            pltpu.VMEM((2, ATTN_HEADS // 2, BLK, 4 * BLK), F32),
        ],
        compiler_params=pltpu.CompilerParams(
            dimension_semantics=("arbitrary",), vmem_limit_bytes=VMEM_LIMIT),
        name="prompt_mixer",
    )(relb, sinks, p, kt, jnp.asarray(_BUCKET_PROMPT), jnp.asarray(_LEV), gn)


def _sample_mixer_kernel(ps_ref, pfull_ref, ck_ref, cv_ref, st_ref, relbt_ref, sink_ref, bucket_ref,
                         gn_ref, omix_ref, kwin_ref, vwin_ref, stout_ref, lat_scr, kqt_scr,
                         bias_scr, s_scr, o_scr, og_scr):
    i = pl.program_id(0)
    nb = pfull_ref.shape[0]

    @pl.when(i == 0)
    def _init():
        lat_scr[...] = _split3_rows(pfull_ref[:, C_LA:P_WIDTH].T)
        kqt_scr[0:GLA_QK_WIDTH] = pfull_ref[:, C_KG:C_VG].T.astype(BF16)
        kqt_scr[GLA_QK_WIDTH:2 * GLA_QK_WIDTH] = pfull_ref[:, C_QG:C_KG].T.astype(BF16)
        bk = jnp.broadcast_to(bucket_ref[...], (ATTN_HEADS, WINDOW))
        acc = jnp.zeros((ATTN_HEADS, WINDOW), F32)
        for b in range(N_BUCKETS):
            acc = jnp.where(bk == b, relbt_ref[:, b:b + 1], acc)
        bias_scr[...] = acc

    lo = lax.broadcasted_iota(jnp.int32, (1, LANES), 1) < HEAD_DIM
    sub = lax.broadcasted_iota(jnp.int32, (ATTN_HEADS, LANES), 0)
    wrow = lax.broadcasted_iota(jnp.int32, (WINDOW, KV_WIDTH), 0)
    heads_per_kv = ATTN_HEADS // ATTN_KV_HEADS

    n_of_col = i * SAMPLE_BLK + lax.broadcasted_iota(jnp.int32, (nb, SAMPLE_BLK * LANES), 1) // LANES
    pick = jnp.where(lax.broadcasted_iota(jnp.int32, (nb, SAMPLE_BLK * LANES), 0) == n_of_col,
                     1.0, 0.0).astype(BF16)
    la_b = (jnp.dot(lat_scr[0:GLA_QK_WIDTH], pick, preferred_element_type=F32)
            + jnp.dot(lat_scr[GLA_QK_WIDTH:2 * GLA_QK_WIDTH], pick, preferred_element_type=F32)
            + jnp.dot(lat_scr[2 * GLA_QK_WIDTH:3 * GLA_QK_WIDTH], pick, preferred_element_type=F32))
    kq_b = jnp.dot(kqt_scr[...], pick, preferred_element_type=F32)

    for j in range(SAMPLE_BLK):
        k_new = ps_ref[j:j + 1, C_KA:C_VA]
        v_new = ps_ref[j:j + 1, C_VA:C_QG]
        kwin_ref[j] = jnp.where(wrow == WINDOW - 1, k_new, pltpu.roll(ck_ref[j], WINDOW - 1, 0))
        vwin_ref[j] = jnp.where(wrow == WINDOW - 1, v_new, pltpu.roll(cv_ref[j], WINDOW - 1, 0))

    for j in range(SAMPLE_BLK):
        qexp = jnp.zeros((ATTN_HEADS, LANES), F32)
        for c in range(ATTN_HEADS // 2):
            chunk = ps_ref[j:j + 1, C_QA + c * LANES:C_QA + (c + 1) * LANES]
            swapped = pltpu.roll(chunk, HEAD_DIM, 1)
            if (2 * c) // heads_per_kv == 0:
                rows = (jnp.where(lo, chunk, 0.0), jnp.where(lo, swapped, 0.0))
            else:
                rows = (jnp.where(lo, 0.0, swapped), jnp.where(lo, 0.0, chunk))
            for e in range(2):
                qexp = jnp.where(sub == 2 * c + e, rows[e], qexp)
        s_scr[j * ATTN_HEADS:(j + 1) * ATTN_HEADS] = _nt_dot(qexp.astype(BF16), kwin_ref[j].astype(BF16))

    tile = lambda x: jnp.concatenate([x] * SAMPLE_BLK, axis=0)
    sink = tile(sink_ref[...])
    s = s_scr[...] + tile(bias_scr[...])
    m = jnp.maximum(jnp.max(s, axis=-1, keepdims=True), sink)
    pe = jnp.exp(s - m)
    inv_den = 1.0 / (jnp.sum(pe, axis=-1, keepdims=True) + jnp.exp(sink - m))
    peb = pe.astype(BF16)
    for j in range(SAMPLE_BLK):
        o_scr[j * ATTN_HEADS:(j + 1) * ATTN_HEADS] = jnp.dot(
            peb[j * ATTN_HEADS:(j + 1) * ATTN_HEADS], vwin_ref[j].astype(BF16),
            preferred_element_type=F32)
    o_all = o_scr[...] * inv_den
    o_swap = pltpu.roll(o_all, HEAD_DIM, 1)
    for j in range(SAMPLE_BLK):
        r = j * ATTN_HEADS
        for c in range(ATTN_HEADS // 2):
            if (2 * c) // heads_per_kv == 0:
                piece = jnp.where(lo, o_all[r + 2 * c:r + 2 * c + 1, :], o_swap[r + 2 * c + 1:r + 2 * c + 2, :])
            else:
                piece = jnp.where(lo, o_swap[r + 2 * c:r + 2 * c + 1, :], o_all[r + 2 * c + 1:r + 2 * c + 2, :])
            omix_ref[j:j + 1, c * LANES:(c + 1) * LANES] = piece

    for j in range(SAMPLE_BLK):
        cols = slice(j * LANES, (j + 1) * LANES)
        for h in range(GLA_HEADS):
            rs = slice(h * GLA_DK, (h + 1) * GLA_DK)
            qs = slice(GLA_QK_WIDTH + h * GLA_DK, GLA_QK_WIDTH + (h + 1) * GLA_DK)
            v_row = ps_ref[j:j + 1, C_VG + h * GLA_DV:C_VG + (h + 1) * GLA_DV]
            s_new = jnp.exp(la_b[rs, cols]) * st_ref[j, h] + kq_b[rs, cols] * v_row
            stout_ref[j, h] = s_new
            og_scr[j:j + 1, h * GLA_DV:(h + 1) * GLA_DV] = jnp.sum(
                kq_b[qs, cols] * s_new, axis=0, keepdims=True)
    for h in range(GLA_HEADS):
        hs = slice(h * GLA_DV, (h + 1) * GLA_DV)
        og = og_scr[:, hs]
        og = og * lax.rsqrt(jnp.mean(og * og, axis=-1, keepdims=True) + EPS) * gn_ref[...]
        rg = ps_ref[:, C_RG + h * GLA_DV:C_RG + (h + 1) * GLA_DV]
        omix_ref[:, ATTN_WIDTH + h * GLA_DV:ATTN_WIDTH + (h + 1) * GLA_DV] = og * (rg * _sigmoid(rg))


def _sample_mixer(ps, cache_k, cache_v, state, relbt, sinks_col, gn):
    nb = ps.shape[0]
    blk3 = lambda i: (i, 0, 0)
    blk4 = lambda i: (i, 0, 0, 0)
    return pl.pallas_call(
        _sample_mixer_kernel,
        grid=(nb // SAMPLE_BLK,),
        in_specs=[
            pl.BlockSpec((SAMPLE_BLK, P_WIDTH), lambda i: (i, 0)),
            _const_spec((nb, P_WIDTH)),
            pl.BlockSpec((SAMPLE_BLK, WINDOW, KV_WIDTH), blk3),
            pl.BlockSpec((SAMPLE_BLK, WINDOW, KV_WIDTH), blk3),
            pl.BlockSpec((SAMPLE_BLK, GLA_HEADS, GLA_DK, GLA_DV), blk4),
            _const_spec((ATTN_HEADS, N_BUCKETS)),
            _const_spec((ATTN_HEADS, 1)),
            _const_spec((1, WINDOW)),
            _const_spec((1, GLA_DV)),
        ],
        out_specs=[
            pl.BlockSpec((SAMPLE_BLK, MIX_WIDTH), lambda i: (i, 0)),
            pl.BlockSpec((SAMPLE_BLK, WINDOW, KV_WIDTH), blk3),
            pl.BlockSpec((SAMPLE_BLK, WINDOW, KV_WIDTH), blk3),
            pl.BlockSpec((SAMPLE_BLK, GLA_HEADS, GLA_DK, GLA_DV), blk4),
        ],
        out_shape=[
            jax.ShapeDtypeStruct((nb, MIX_WIDTH), F32),
            jax.ShapeDtypeStruct((nb, WINDOW, KV_WIDTH), F32),
            jax.ShapeDtypeStruct((nb, WINDOW, KV_WIDTH), F32),
            jax.ShapeDtypeStruct((nb, GLA_HEADS, GLA_DK, GLA_DV), F32),
        ],
        scratch_shapes=[
            pltpu.VMEM((3 * GLA_QK_WIDTH, nb), BF16),
            pltpu.VMEM((2 * GLA_QK_WIDTH, nb), BF16),
            pltpu.VMEM((ATTN_HEADS, WINDOW), F32),
            pltpu.VMEM((SAMPLE_BLK * ATTN_HEADS, WINDOW), F32),
            pltpu.VMEM((SAMPLE_BLK * ATTN_HEADS, KV_WIDTH), F32),
            pltpu.VMEM((SAMPLE_BLK, GLA_WIDTH), F32),
        ],
        compiler_params=pltpu.CompilerParams(
            dimension_semantics=("arbitrary",), vmem_limit_bytes=VMEM_LIMIT),
        name="sample_mixer",
    )(ps, ps, cache_k, cache_v, state, relbt, sinks_col, jnp.asarray(_BUCKET_SAMPLE), gn)


def _finish_kernel(x_ref, mix_ref, xs_ref, mixs_ref, wo_ref, g_ref, wg_ref, wu_ref, wd_ref,
                   y_ref, ys_ref, *, ff_chunks):
    weights = (wo_ref, g_ref, wg_ref, wu_ref, wd_ref)
    _finish_rows(x_ref, mix_ref, *weights, y_ref, ff_chunks=ff_chunks)

    @pl.when(pl.program_id(0) == pl.num_programs(0) - 1)
    def _samples():
        _finish_rows(xs_ref, mixs_ref, *weights, ys_ref, ff_chunks=ff_chunks)


def _finish_rows(x_ref, mix_ref, wo_ref, g_ref, wg_ref, wu_ref, wd_ref, y_ref, *, ff_chunks):
    h = x_ref[...] + jnp.dot(mix_ref[...].astype(BF16), wo_ref[...], preferred_element_type=F32)
    r = lax.rsqrt(jnp.mean(h * h, axis=-1, keepdims=True) + EPS)
    z = (h * g_ref[...]).astype(BF16)
    n_tiles = wd_ref.shape[0] // MXU_TILE
    acc = h
    for c in range(ff_chunks):
        c0 = ((c * n_tiles) // ff_chunks) * MXU_TILE
        c1 = (((c + 1) * n_tiles) // ff_chunks) * MXU_TILE
        gate = jnp.dot(z, wg_ref[:, c0:c1], preferred_element_type=F32) * r
        up = jnp.dot(z, wu_ref[:, c0:c1], preferred_element_type=F32) * r
        act = ((gate * _sigmoid(gate)) * up).astype(BF16)
        acc = acc + jnp.dot(act, wd_ref[c0:c1, :], preferred_element_type=F32)
    y_ref[...] = acc


def _finish(x, mix, xs, mixs, wo, g_ffn, wg, wu, wd, rows):
    t = x.shape[0]
    ns = xs.shape[0]
    d_ff = wd.shape[0]
    assert d_ff % MXU_TILE == 0
    return pl.pallas_call(
        functools.partial(_finish_kernel, ff_chunks=FF_CHUNKS),
        grid=(t // rows,),
        in_specs=[
            pl.BlockSpec((rows, D_MODEL), lambda i: (i, 0)),
            pl.BlockSpec((rows, MIX_WIDTH), lambda i: (i, 0)),
            _const_spec((ns, D_MODEL)),
            _const_spec((ns, MIX_WIDTH)),
            _const_spec((MIX_WIDTH, D_MODEL)),
            _const_spec((1, D_MODEL)),
            _const_spec((D_MODEL, d_ff)),
            _const_spec((D_MODEL, d_ff)),
            _const_spec((d_ff, D_MODEL)),
        ],
        out_specs=[pl.BlockSpec((rows, D_MODEL), lambda i: (i, 0)),
                   pl.BlockSpec((ns, D_MODEL), lambda i: (0, 0))],
        out_shape=[jax.ShapeDtypeStruct((t, D_MODEL), F32),
                   jax.ShapeDtypeStruct((ns, D_MODEL), F32)],
        compiler_params=pltpu.CompilerParams(
            dimension_semantics=("arbitrary",), vmem_limit_bytes=VMEM_LIMIT),
        name="finish",
    )(x, mix, xs, mixs, wo, g_ffn, wg, wu, wd)


PROMPT_ROWS = 512
PROJ_ROWS = 1024


def kernel(x_prompt, x_sample, cache_k, cache_v, state_gla, attn_norm_g, w_in, q_norm_g, k_norm_g,
           attn_sinks, rel_bias, w_gla_gate2, b_gla_gate, gla_norm_g, w_o, ffn_norm_g, w_gate, w_up,
           w_down):
    depth = w_in.shape[0]
    batch, seq, _ = x_prompt.shape
    dec_batch, dec_seq, _ = x_sample.shape
    wb = cache_k.shape[2]
    assert batch == 1 and dec_seq == 1 and wb == WINDOW
    assert seq % PROMPT_ROWS == 0 and seq % PROJ_ROWS == 0
    assert dec_batch % SAMPLE_BLK == 0 and dec_batch % LANES == 0
    assert rel_bias.shape == (N_BUCKETS, ATTN_HEADS)

    xp = x_prompt.reshape(seq, D_MODEL)
    xs = x_sample.reshape(dec_batch, D_MODEL)
    relb_flat = rel_bias.reshape(-1)
    relb_t = rel_bias.T
    outs = ([], [], [], [], [], [])
    for l in range(depth):
        kv_end = ATTN_WIDTH + 2 * KV_WIDTH
        w_in_p = jnp.concatenate(
            [w_in[l, :, :kv_end],
             jnp.pad(w_in[l, :, MAIN_WIDTH:], ((0, 0), (0, RANK_PAD - GLA_RANK))),
             w_in[l, :, kv_end:MAIN_WIDTH]], axis=1).astype(BF16)
        w2p = jnp.pad(w_gla_gate2[l], ((0, RANK_PAD - GLA_RANK), (0, 0))).astype(BF16)
        proj_w = (attn_norm_g[l][None, :], w_in_p, jnp.tile(q_norm_g[l], ATTN_HEADS)[None, :],
                  jnp.tile(k_norm_g[l], ATTN_KV_HEADS)[None, :], w2p, b_gla_gate[l][None, :])
        fin_w = (w_o[l].astype(BF16), ffn_norm_g[l][None, :], w_gate[l].astype(BF16),
                 w_up[l].astype(BF16), w_down[l].astype(BF16))
        gn = gla_norm_g[l][None, :]

        pp, kt_p, ps = _project(xp, xs, *proj_w, rows=PROJ_ROWS)
        mix_p, st_p = _prompt_mixer(pp, kt_p, relb_flat, attn_sinks[l], gn)
        mix_s, kwin, vwin, st_s = _sample_mixer(
            ps, cache_k[l].reshape(dec_batch, wb, KV_WIDTH), cache_v[l].reshape(dec_batch, wb, KV_WIDTH),
            state_gla[l].astype(F32), relb_t, attn_sinks[l][:, None], gn)
        xp_in = xp
        xp, xs = _finish(xp_in, mix_p, xs, mix_s, *fin_w, rows=PROMPT_ROWS)
        outs[0].append(pp[seq - wb:, C_KA:C_VA].reshape(batch, wb, ATTN_KV_HEADS, HEAD_DIM))
        outs[1].append(pp[seq - wb:, C_VA:C_QG].reshape(batch, wb, ATTN_KV_HEADS, HEAD_DIM))
        outs[2].append(st_p.T.reshape(batch, GLA_HEADS, GLA_DK, GLA_DV).astype(state_gla.dtype))
        outs[3].append(kwin.reshape(dec_batch, wb, ATTN_KV_HEADS, HEAD_DIM))
        outs[4].append(vwin.reshape(dec_batch, wb, ATTN_KV_HEADS, HEAD_DIM))
        outs[5].append(st_s.astype(state_gla.dtype))

    y_prompt = xp.reshape(batch, seq, D_MODEL)
    y_sample = xs.reshape(dec_batch, dec_seq, D_MODEL)
    return (y_prompt, y_sample) + tuple(jnp.stack(o) for o in outs)
```

```python
import functools
import math

import numpy as np
import jax
import jax.numpy as jnp
from jax import lax
from jax.experimental import pallas as pl
from jax.experimental.pallas import tpu as pltpu

F32 = jnp.float32
BF16 = jnp.bfloat16

D_MODEL = 1024
HEAD_DIM = 64
ATTN_HEADS = 8
ATTN_KV_HEADS = 2
WINDOW = 128
N_BUCKETS = 32
MAX_DISTANCE = 128
GLA_HEADS = 4
GLA_DK = 64
GLA_DV = 128
GLA_RANK = 16
GLA_TAU = 16.0
EPS = 1e-6
ATTN_WIDTH = ATTN_HEADS * HEAD_DIM
KV_WIDTH = ATTN_KV_HEADS * HEAD_DIM
GLA_QK_WIDTH = GLA_HEADS * GLA_DK
GLA_WIDTH = GLA_HEADS * GLA_DV
MIX_WIDTH = ATTN_WIDTH + GLA_WIDTH
MAIN_WIDTH = ATTN_WIDTH + 2 * KV_WIDTH + 2 * GLA_QK_WIDTH + 2 * GLA_WIDTH
LANES = 128
SUBLANES = 8
MXU_TILE = 256
FF_CHUNKS = 2
RANK_PAD = LANES

C_QA = 0
C_KA = C_QA + ATTN_WIDTH
C_VA = C_KA + KV_WIDTH
C_QG = C_VA + KV_WIDTH
C_KG = C_QG + GLA_QK_WIDTH
C_VG = C_KG + GLA_QK_WIDTH
C_RG = C_VG + GLA_WIDTH
C_LA = C_RG + GLA_WIDTH
P_WIDTH = C_LA + GLA_QK_WIDTH

BLK = 128
N_LEVELS = 7
NEG = -1e30
ATTN_SCALE = HEAD_DIM ** -0.5
SAMPLE_BLK = 8
VMEM_LIMIT = 56 * 1024 * 1024


def _t5_bucket_np(dist):
    n = np.maximum(dist, 0)
    max_exact = N_BUCKETS // 2
    nf = np.maximum(n, 1).astype(np.float64)
    large = max_exact + (np.log(nf / max_exact) / math.log(MAX_DISTANCE / max_exact)
                         * (N_BUCKETS - max_exact)).astype(np.int32)
    large = np.minimum(large, N_BUCKETS - 1)
    return np.where(n < max_exact, n, large).astype(np.int32)


def _prompt_bucket_tables():
    i = np.arange(BLK)[:, None]
    j = np.arange(2 * BLK)[None, :]
    dist = BLK + i - j
    band = (dist >= 0) & (dist < WINDOW)
    bucket = _t5_bucket_np(dist)
    t0 = np.where(band, bucket, -1)
    t1 = np.where(band & (j >= BLK), bucket, -1)
    return np.stack([t0, t1]).astype(np.int32)


def _level_tables():
    t = np.arange(BLK)[:, None]
    s = np.arange(BLK)[None, :]
    x = t ^ s
    lev = np.where(x > 0, np.floor(np.log2(np.maximum(x, 1))).astype(np.int32) + 1, 0)
    lev = np.where(s > t, -1, lev).astype(np.int32)
    tri = (s <= t).astype(np.float32)
    return lev, np.concatenate([tri, tri, tri], axis=1)


_BUCKET_PROMPT = _prompt_bucket_tables()
_LEV, _TRI3 = _level_tables()
_BUCKET_SAMPLE = _t5_bucket_np((WINDOW - 1) - np.arange(WINDOW))[None, :].astype(np.int32)


def _nt_dot(a, b):
    return lax.dot_general(a, b, (((1,), (1,)), ((), ())), preferred_element_type=F32)


def _tn_dot(a, b):
    return lax.dot_general(a, b, (((0,), (0,)), ((), ())), preferred_element_type=F32)


def _head_mean_sq(x):
    lo = lax.broadcasted_iota(jnp.int32, (x.shape[0], LANES), 1) < HEAD_DIM
    outs = []
    for c in range(x.shape[1] // LANES):
        y = x[:, c * LANES:(c + 1) * LANES]
        y = y * y
        s_lo = jnp.sum(jnp.where(lo, y, 0.0), axis=-1, keepdims=True)
        s_hi = jnp.sum(jnp.where(lo, 0.0, y), axis=-1, keepdims=True)
        outs.append(jnp.where(lo, s_lo, s_hi) * (1.0 / HEAD_DIM))
    return outs[0] if len(outs) == 1 else jnp.concatenate(outs, axis=1)


def _sigmoid(x):
    return 1.0 / (1.0 + jnp.exp(-x))


def _split3_rows(x):
    hi = x.astype(BF16)
    r1 = x - hi.astype(F32)
    mid = r1.astype(BF16)
    lo = (r1 - mid.astype(F32)).astype(BF16)
    return jnp.concatenate([hi, mid, lo], axis=0)


def _proj_kernel(x_ref, xs_ref, g_ref, w_ref, wlr_ref, qn_ref, kn_ref, w2_ref, b2_ref, tri_ref,
                 out_ref, kt_ref, outs_ref):
    weights = (g_ref, w_ref, wlr_ref, qn_ref, kn_ref, w2_ref, b2_ref, tri_ref)
    _proj_rows(x_ref, *weights, out_ref, kt_ref, block_cumsum=True)

    @pl.when(pl.program_id(0) == pl.num_programs(0) - 1)
    def _samples():
        _proj_rows(xs_ref, *weights, outs_ref, None, block_cumsum=False)


def _proj_rows(x_ref, g_ref, w_ref, wlr_ref, qn_ref, kn_ref, w2_ref, b2_ref, tri_ref, out_ref,
               kt_ref, *,
               block_cumsum):
    x = x_ref[...]
    r = lax.rsqrt(jnp.mean(x * x, axis=-1, keepdims=True) + EPS)
    n = (x * g_ref[...]).astype(BF16)

    def seg(c0, c1):
        return jnp.dot(n, w_ref[:, c0:c1], preferred_element_type=F32) * r

    lr = (jnp.dot(n, wlr_ref[...], preferred_element_type=F32) * r).astype(BF16)
    z = jnp.dot(lr, w2_ref[...], preferred_element_type=F32) + b2_ref[...]
    log_a = (jnp.minimum(z, 0.0) - jnp.log1p(jnp.exp(-jnp.abs(z)))) / GLA_TAU
    if block_cumsum:
        for blk in range(x.shape[0] // BLK):
            rows = slice(blk * BLK, (blk + 1) * BLK)
            out_ref[rows, C_LA:P_WIDTH] = jnp.dot(
                tri_ref[...], _split3_rows(log_a[rows]), preferred_element_type=F32)
    else:
        out_ref[:, C_LA:P_WIDTH] = log_a
    kv = seg(C_KA, C_QG)
    k = kv[:, :KV_WIDTH]
    k = k * lax.rsqrt(_head_mean_sq(k) + EPS) * kn_ref[...]
    out_ref[:, C_KA:C_VA] = k
    if kt_ref is not None:
        kt_ref[...] = k.T.astype(BF16)
    out_ref[:, C_VA:C_QG] = kv[:, KV_WIDTH:]
    q = seg(C_QA, C_KA)
    out_ref[:, C_QA:C_KA] = q * lax.rsqrt(_head_mean_sq(q) + EPS) * qn_ref[...] * ATTN_SCALE
    qk_g = seg(C_QG, C_VG)
    out_ref[:, C_QG:C_KG] = qk_g[:, :GLA_QK_WIDTH] * (GLA_DK ** -0.5)
    out_ref[:, C_KG:C_VG] = qk_g[:, GLA_QK_WIDTH:]
    out_ref[:, C_VG:C_RG] = seg(C_VG, C_RG)
    out_ref[:, C_RG:C_LA] = seg(C_RG, C_LA)


def _const_spec(shape):
    nd = len(shape)
    return pl.BlockSpec(shape, lambda i: (0,) * nd, pipeline_mode=pl.Buffered(1))


def _project(x, xs, g_attn, w_main, w_lr, qn, kn, w2p, b2, rows):
    t = x.shape[0]
    ns = xs.shape[0]
    return pl.pallas_call(
        _proj_kernel,
        grid=(t // rows,),
        in_specs=[
            pl.BlockSpec((rows, D_MODEL), lambda i: (i, 0)),
            _const_spec((ns, D_MODEL)),
            _const_spec((1, D_MODEL)),
            _const_spec((D_MODEL, MAIN_WIDTH)),
            _const_spec((D_MODEL, RANK_PAD)),
            _const_spec((1, ATTN_WIDTH)),
            _const_spec((1, KV_WIDTH)),
            _const_spec((RANK_PAD, GLA_QK_WIDTH)),
            _const_spec((1, GLA_QK_WIDTH)),
            _const_spec((BLK, 3 * BLK)),
        ],
        out_specs=[pl.BlockSpec((rows, P_WIDTH), lambda i: (i, 0)),
                   pl.BlockSpec((KV_WIDTH, rows), lambda i: (0, i)),
                   pl.BlockSpec((ns, P_WIDTH), lambda i: (0, 0))],
        out_shape=[jax.ShapeDtypeStruct((t, P_WIDTH), F32),
                   jax.ShapeDtypeStruct((KV_WIDTH, t), BF16),
                   jax.ShapeDtypeStruct((ns, P_WIDTH), F32)],
        compiler_params=pltpu.CompilerParams(
            dimension_semantics=("arbitrary",), vmem_limit_bytes=VMEM_LIMIT),
        name="proj",
    )(x, xs, g_attn, w_main, w_lr, qn, kn, w2p, b2, jnp.asarray(_TRI3, BF16))


def _boundary_rows(b_ref, r0, c0, b, row, level):
    m = 1 << (level - 1)
    if 2 * m >= SUBLANES:
        pieces = [jnp.broadcast_to(b_ref[r0 + g * 2 * m + m - 1:r0 + g * 2 * m + m, c0:c0 + LANES],
                                   (2 * m, LANES))
                  for g in range(BLK // (2 * m))]
        return pieces[0] if len(pieces) == 1 else jnp.concatenate(pieces, axis=0)
    pos = row & (2 * m - 1)
    tiles = b.reshape(BLK // SUBLANES, SUBLANES, LANES)
    out = b
    for p in range(2 * m):
        shift = (m - 1) - p
        if shift != 0:
            rolled = pltpu.roll(tiles, (-shift) % SUBLANES, 1).reshape(BLK, LANES)
            out = jnp.where(pos == p, rolled, out)
    return out


def _mixer_init(relb_ref, bucket_ref, kbd, vbd, kprev_t, vprev, st_scr, mb_scr):
    kbd[...] = jnp.zeros_like(kbd)
    vbd[...] = jnp.zeros_like(vbd)
    for g in range(ATTN_KV_HEADS):
        for half in range(2):
            vbd[g, 2 * half * BLK:(2 * half + 2) * BLK,
                KV_WIDTH + half * HEAD_DIM:KV_WIDTH + (half + 1) * HEAD_DIM] = (
                    jnp.ones((2 * BLK, HEAD_DIM), BF16))
    kprev_t[...] = jnp.zeros_like(kprev_t)
    vprev[...] = jnp.zeros_like(vprev)
    st_scr[...] = jnp.zeros_like(st_scr)
    bk = bucket_ref[0]
    acc = [jnp.zeros(bk.shape, F32) for _ in range(ATTN_HEADS)]
    for b in range(N_BUCKETS):
        hit = bk == b
        for h in range(ATTN_HEADS):
            acc[h] = jnp.where(hit, relb_ref[b * ATTN_HEADS + h], acc[h])
    for tb in range(2):
        masked = bucket_ref[tb] < 0
        for h in range(ATTN_HEADS):
            mb_scr[tb, h // 2, :, (h % 2) * 2 * BLK:(h % 2 + 1) * 2 * BLK] = (
                jnp.where(masked, NEG, acc[h]))


def _mixer_attention(p_ref, kt_ref, r0, table, sink_ref, omix_ref, kbd, vbd, kprev_t, vprev,
                     mb_scr):
    rows = slice(r0, r0 + BLK)
    lo1 = lax.broadcasted_iota(jnp.int32, (BLK, LANES), 1) < HEAD_DIM
    k_t = kt_ref[:, r0:r0 + BLK]
    k_prev_t = kprev_t[...]
    for g in range(ATTN_KV_HEADS):
        hd = slice(g * HEAD_DIM, (g + 1) * HEAD_DIM)
        kbd[g, 0:HEAD_DIM, 0:BLK] = k_prev_t[hd]
        kbd[g, 0:HEAD_DIM, BLK:2 * BLK] = k_t[hd]
        kbd[g, HEAD_DIM:2 * HEAD_DIM, 2 * BLK:3 * BLK] = k_prev_t[hd]
        kbd[g, HEAD_DIM:2 * HEAD_DIM, 3 * BLK:4 * BLK] = k_t[hd]
    kprev_t[...] = k_t
    v_cur = p_ref[rows,C_VA:C_QG]
    v_swap = pltpu.roll(v_cur, HEAD_DIM, 1)
    v_parts = (jnp.where(lo1, v_cur, 0.0).astype(BF16), jnp.where(lo1, 0.0, v_swap).astype(BF16),
               jnp.where(lo1, v_swap, 0.0).astype(BF16), jnp.where(lo1, 0.0, v_cur).astype(BF16))
    for n, part in enumerate(v_parts):
        g, half = divmod(n, 2)
        vbd[g, 2 * half * BLK:(2 * half + 1) * BLK, 0:KV_WIDTH] = vprev[n]
        vbd[g, (2 * half + 1) * BLK:(2 * half + 2) * BLK, 0:KV_WIDTH] = part
        vprev[n] = part
    chunks_per_kv = ATTN_HEADS // ATTN_KV_HEADS // 2
    for g in range(ATTN_KV_HEADS):
        c_first = g * chunks_per_kv
        qs = jnp.concatenate(
            [p_ref[rows,C_QA + (c_first + c) * LANES:C_QA + (c_first + c + 1) * LANES].astype(BF16)
             for c in range(chunks_per_kv)], axis=0)
        mb = jnp.concatenate([mb_scr[table, c_first + c] for c in range(chunks_per_kv)], axis=0)
        s = jnp.dot(qs, kbd[g], preferred_element_type=F32) + mb
        prob_rows, sink_rows = [], []
        for c in range(chunks_per_kv):
            probs, sinks = [], []
            for e in range(2):
                se = s[c * BLK:(c + 1) * BLK, e * 2 * BLK:(e + 1) * 2 * BLK]
                sk = sink_ref[2 * (c_first + c) + e]
                m = jnp.maximum(jnp.max(se, axis=-1, keepdims=True), sk)
                probs.append(jnp.exp(se - m).astype(BF16))
                sinks.append(jnp.exp(sk - m))
            prob_rows.append(jnp.concatenate(probs, axis=1))
            sink_rows.append(jnp.where(lo1, sinks[0], sinks[1]))
        o_den = jnp.dot(jnp.concatenate(prob_rows, axis=0), vbd[g], preferred_element_type=F32)
        o = o_den[:, :KV_WIDTH] / (o_den[:, KV_WIDTH:] + jnp.concatenate(sink_rows, axis=0))
        for c in range(chunks_per_kv):
            omix_ref[rows,(c_first + c) * LANES:(c_first + c + 1) * LANES] = (
                o[c * BLK:(c + 1) * BLK].astype(omix_ref.dtype))


def _mixer_gla(pairs, p_ref, r0, lev_ref, gn_ref, omix_ref, st_ref, st_scr):
    rows = slice(r0, r0 + BLK)
    lo1 = lax.broadcasted_iota(jnp.int32, (BLK, LANES), 1) < HEAD_DIM
    lo_bf = jnp.where(lo1, 1.0, 0.0).astype(BF16)
    hi_bf = jnp.where(lo1, 0.0, 1.0).astype(BF16)
    row = lax.broadcasted_iota(jnp.int32, (BLK, LANES), 0)
    zero_blk = jnp.zeros((BLK, LANES), BF16)
    for c in pairs:
        c0 = c * LANES
        q_at = lambda a, z: p_ref[r0 + a:r0 + z, C_QG + c0:C_QG + c0 + LANES]
        k_at = lambda a, z: p_ref[r0 + a:r0 + z, C_KG + c0:C_KG + c0 + LANES]
        b_at = lambda a, z: p_ref[r0 + a:r0 + z, C_LA + c0:C_LA + c0 + LANES]

        def pair_scores(qtb, ktb):
            rhs = jnp.concatenate([ktb * lo_bf, ktb * hi_bf], axis=0)
            return _nt_dot(qtb, rhs)

        s0 = pair_scores(q_at(0, BLK).astype(BF16), k_at(0, BLK).astype(BF16))
        sc = [jnp.where(lev_ref[...] == 0, s0[:, e * BLK:(e + 1) * BLK], 0.0) for e in range(2)]
        for level in range(1, N_LEVELS + 1):
            m = 1 << (level - 1)
            if m >= SUBLANES:
                qs, ks = [], []
                zeros = jnp.zeros((m, LANES), BF16)
                for g in range(BLK // (2 * m)):
                    lo_a, up_a, up_z = g * 2 * m, g * 2 * m + m, (g + 1) * 2 * m
                    rb = jnp.broadcast_to(b_at(up_a - 1, up_a), (m, LANES))
                    qs += [zeros, (q_at(up_a, up_z) * jnp.exp(b_at(up_a, up_z) - rb)).astype(BF16)]
                    ks += [(k_at(lo_a, up_a) * jnp.exp(rb - b_at(lo_a, up_a))).astype(BF16), zeros]
                qtb = jnp.concatenate(qs, axis=0)
                ktb = jnp.concatenate(ks, axis=0)
            else:
                bc = b_at(0, BLK)
                d = bc - _boundary_rows(p_ref, r0, C_LA + c0, bc, row, level)
                upper = ((row >> (level - 1)) & 1) == 1
                qtb = (q_at(0, BLK) * jnp.exp(jnp.where(upper, d, NEG))).astype(BF16)
                ktb = (k_at(0, BLK) * jnp.exp(jnp.where(upper, NEG, -d))).astype(BF16)
            sl = pair_scores(qtb, ktb)
            sc = [jnp.where(lev_ref[...] == level, sl[:, e * BLK:(e + 1) * BLK], sc[e])
                  for e in range(2)]
        sc = jnp.concatenate(sc, axis=1)
        qc, kc, bc = q_at(0, BLK), k_at(0, BLK), b_at(0, BLK)

        b_last = bc[BLK - 1:BLK, :]
        v0 = p_ref[rows,C_VG + 2 * c0:C_VG + 2 * c0 + LANES].astype(BF16)
        v1 = p_ref[rows,C_VG + 2 * c0 + LANES:C_VG + 2 * c0 + 2 * LANES].astype(BF16)
        v_bd = jnp.concatenate([jnp.concatenate([v0, zero_blk], axis=1),
                                jnp.concatenate([zero_blk, v1], axis=1)], axis=0)
        st_c = st_scr[:, c0:c0 + LANES]
        stb = st_c.astype(BF16)
        st_rhs = jnp.concatenate([stb * lo_bf, stb * hi_bf], axis=0)
        o = (jnp.dot(sc.astype(BF16), v_bd, preferred_element_type=F32)
             + _nt_dot((qc * jnp.exp(bc)).astype(BF16), st_rhs))
        kd = (kc * jnp.exp(b_last - bc)).astype(BF16)
        upd = _tn_dot(jnp.concatenate([v0, v1], axis=1), kd)
        new_st = st_c * jnp.exp(b_last) + jnp.where(lo1, upd[:BLK], upd[BLK:])
        st_scr[:, c0:c0 + LANES] = new_st
        st_ref[:, c0:c0 + LANES] = new_st
        for e in range(2):
            h = 2 * c + e
            oh = o[:, e * LANES:(e + 1) * LANES]
            og = oh * lax.rsqrt(jnp.mean(oh * oh, axis=-1, keepdims=True) + EPS) * gn_ref[...]
            rg = p_ref[rows,C_RG + h * GLA_DV:C_RG + (h + 1) * GLA_DV]
            gated = og * (rg * _sigmoid(rg))
            omix_ref[rows,ATTN_WIDTH + h * GLA_DV:ATTN_WIDTH + (h + 1) * GLA_DV] = (
                gated.astype(omix_ref.dtype))


def _prompt_mixer_kernel(relb_ref, sink_ref, p_ref, kt_ref, bucket_ref, lev_ref, gn_ref,
                         omix_ref, st_ref, kbd, vbd, kprev_t, vprev, st_scr, mb_scr):
    i = pl.program_id(0)

    @pl.when(i == 0)
    def _init():
        _mixer_init(relb_ref, bucket_ref, kbd, vbd, kprev_t, vprev, st_scr, mb_scr)

    _mixer_attention(p_ref, kt_ref, 0, jnp.where(i == 0, 1, 0), sink_ref, omix_ref, kbd, vbd,
                     kprev_t, vprev, mb_scr)
    _mixer_gla(range(GLA_HEADS // 2), p_ref, 0, lev_ref, gn_ref, omix_ref, st_ref, st_scr)


def _prompt_mixer(p, kt, relb, sinks, gn):
    t = p.shape[0]
    smem = pl.BlockSpec(memory_space=pltpu.SMEM)
    return pl.pallas_call(
        _prompt_mixer_kernel,
        grid=(t // BLK,),
        in_specs=[
            smem, smem,
            pl.BlockSpec((BLK, P_WIDTH), lambda i: (i, 0)),
            pl.BlockSpec((KV_WIDTH, BLK), lambda i: (0, i)),
            _const_spec((2, BLK, 2 * BLK)),
            _const_spec((BLK, BLK)),
            _const_spec((1, GLA_DV)),
        ],
        out_specs=[
            pl.BlockSpec((BLK, MIX_WIDTH), lambda i: (i, 0)),
            pl.BlockSpec((GLA_DV, GLA_QK_WIDTH), lambda i: (0, 0)),
        ],
        out_shape=[
            jax.ShapeDtypeStruct((t, MIX_WIDTH), BF16),
            jax.ShapeDtypeStruct((GLA_DV, GLA_QK_WIDTH), F32),
        ],
        scratch_shapes=[
            pltpu.VMEM((ATTN_KV_HEADS, 2 * HEAD_DIM, 4 * BLK), BF16),
            pltpu.VMEM((ATTN_KV_HEADS, 4 * BLK, 2 * KV_WIDTH), BF16),
            pltpu.VMEM((KV_WIDTH, BLK), BF16),
            pltpu.VMEM((2 * ATTN_KV_HEADS, BLK, KV_WIDTH), BF16),
            pltpu.VMEM((GLA_DV, GLA_QK_WIDTH), F32),
            pltpu.VMEM((2, ATTN_HEADS // 2, BLK, 4 * BLK), F32),
        ],
        compiler_params=pltpu.CompilerParams(
            dimension_semantics=("arbitrary",), vmem_limit_bytes=VMEM_LIMIT),
        name="prompt_mixer",
    )(relb, sinks, p, kt, jnp.asarray(_BUCKET_PROMPT), jnp.asarray(_LEV), gn)


def _sample_mixer_kernel(ps_ref, pfull_ref, ck_ref, cv_ref, st_ref, relbt_ref, sink_ref, bucket_ref,
                         gn_ref, omix_ref, kwin_ref, vwin_ref, stout_ref, lat_scr, kqt_scr,
                         bias_scr, s_scr, o_scr, og_scr):
    i = pl.program_id(0)
    nb = pfull_ref.shape[0]

    @pl.when(i == 0)
    def _init():
        lat_scr[...] = _split3_rows(pfull_ref[:, C_LA:P_WIDTH].T)
        kqt_scr[0:GLA_QK_WIDTH] = pfull_ref[:, C_KG:C_VG].T.astype(BF16)
        kqt_scr[GLA_QK_WIDTH:2 * GLA_QK_WIDTH] = pfull_ref[:, C_QG:C_KG].T.astype(BF16)
        bk = jnp.broadcast_to(bucket_ref[...], (ATTN_HEADS, WINDOW))
        acc = jnp.zeros((ATTN_HEADS, WINDOW), F32)
        for b in range(N_BUCKETS):
            acc = jnp.where(bk == b, relbt_ref[:, b:b + 1], acc)
        bias_scr[...] = acc

    lo = lax.broadcasted_iota(jnp.int32, (1, LANES), 1) < HEAD_DIM
    sub = lax.broadcasted_iota(jnp.int32, (ATTN_HEADS, LANES), 0)
    wrow = lax.broadcasted_iota(jnp.int32, (WINDOW, KV_WIDTH), 0)
    heads_per_kv = ATTN_HEADS // ATTN_KV_HEADS

    n_of_col = i * SAMPLE_BLK + lax.broadcasted_iota(jnp.int32, (nb, SAMPLE_BLK * LANES), 1) // LANES
    pick = jnp.where(lax.broadcasted_iota(jnp.int32, (nb, SAMPLE_BLK * LANES), 0) == n_of_col,
                     1.0, 0.0).astype(BF16)
    la_b = (jnp.dot(lat_scr[0:GLA_QK_WIDTH], pick, preferred_element_type=F32)
            + jnp.dot(lat_scr[GLA_QK_WIDTH:2 * GLA_QK_WIDTH], pick, preferred_element_type=F32)
            + jnp.dot(lat_scr[2 * GLA_QK_WIDTH:3 * GLA_QK_WIDTH], pick, preferred_element_type=F32))
    kq_b = jnp.dot(kqt_scr[...], pick, preferred_element_type=F32)

    for j in range(SAMPLE_BLK):
        k_new = ps_ref[j:j + 1, C_KA:C_VA]
        v_new = ps_ref[j:j + 1, C_VA:C_QG]
        kwin_ref[j] = jnp.where(wrow == WINDOW - 1, k_new, pltpu.roll(ck_ref[j], WINDOW - 1, 0))
        vwin_ref[j] = jnp.where(wrow == WINDOW - 1, v_new, pltpu.roll(cv_ref[j], WINDOW - 1, 0))

    for j in range(SAMPLE_BLK):
        qexp = jnp.zeros((ATTN_HEADS, LANES), F32)
        for c in range(ATTN_HEADS // 2):
            chunk = ps_ref[j:j + 1, C_QA + c * LANES:C_QA + (c + 1) * LANES]
            swapped = pltpu.roll(chunk, HEAD_DIM, 1)
            if (2 * c) // heads_per_kv == 0:
                rows = (jnp.where(lo, chunk, 0.0), jnp.where(lo, swapped, 0.0))
            else:
                rows = (jnp.where(lo, 0.0, swapped), jnp.where(lo, 0.0, chunk))
            for e in range(2):
                qexp = jnp.where(sub == 2 * c + e, rows[e], qexp)
        s_scr[j * ATTN_HEADS:(j + 1) * ATTN_HEADS] = _nt_dot(qexp.astype(BF16), kwin_ref[j].astype(BF16))

    tile = lambda x: jnp.concatenate([x] * SAMPLE_BLK, axis=0)
    sink = tile(sink_ref[...])
    s = s_scr[...] + tile(bias_scr[...])
    m = jnp.maximum(jnp.max(s, axis=-1, keepdims=True), sink)
    pe = jnp.exp(s - m)
    inv_den = 1.0 / (jnp.sum(pe, axis=-1, keepdims=True) + jnp.exp(sink - m))
    peb = pe.astype(BF16)
    for j in range(SAMPLE_BLK):
        o_scr[j * ATTN_HEADS:(j + 1) * ATTN_HEADS] = jnp.dot(
            peb[j * ATTN_HEADS:(j + 1) * ATTN_HEADS], vwin_ref[j].astype(BF16),
            preferred_element_type=F32)
    o_all = o_scr[...] * inv_den
    o_swap = pltpu.roll(o_all, HEAD_DIM, 1)
    for j in range(SAMPLE_BLK):
        r = j * ATTN_HEADS
        for c in range(ATTN_HEADS // 2):
            if (2 * c) // heads_per_kv == 0:
                piece = jnp.where(lo, o_all[r + 2 * c:r + 2 * c + 1, :], o_swap[r + 2 * c + 1:r + 2 * c + 2, :])
            else:
                piece = jnp.where(lo, o_swap[r + 2 * c:r + 2 * c + 1, :], o_all[r + 2 * c + 1:r + 2 * c + 2, :])
            omix_ref[j:j + 1, c * LANES:(c + 1) * LANES] = piece

    for j in range(SAMPLE_BLK):
        cols = slice(j * LANES, (j + 1) * LANES)
        for h in range(GLA_HEADS):
            rs = slice(h * GLA_DK, (h + 1) * GLA_DK)
            qs = slice(GLA_QK_WIDTH + h * GLA_DK, GLA_QK_WIDTH + (h + 1) * GLA_DK)
            v_row = ps_ref[j:j + 1, C_VG + h * GLA_DV:C_VG + (h + 1) * GLA_DV]
            s_new = jnp.exp(la_b[rs, cols]) * st_ref[j, h] + kq_b[rs, cols] * v_row
            stout_ref[j, h] = s_new
            og_scr[j:j + 1, h * GLA_DV:(h + 1) * GLA_DV] = jnp.sum(
                kq_b[qs, cols] * s_new, axis=0, keepdims=True)
    for h in range(GLA_HEADS):
        hs = slice(h * GLA_DV, (h + 1) * GLA_DV)
        og = og_scr[:, hs]
        og = og * lax.rsqrt(jnp.mean(og * og, axis=-1, keepdims=True) + EPS) * gn_ref[...]
        rg = ps_ref[:, C_RG + h * GLA_DV:C_RG + (h + 1) * GLA_DV]
        omix_ref[:, ATTN_WIDTH + h * GLA_DV:ATTN_WIDTH + (h + 1) * GLA_DV] = og * (rg * _sigmoid(rg))


def _sample_mixer(ps, cache_k, cache_v, state, relbt, sinks_col, gn):
    nb = ps.shape[0]
    blk3 = lambda i: (i, 0, 0)
    blk4 = lambda i: (i, 0, 0, 0)
    return pl.pallas_call(
        _sample_mixer_kernel,
        grid=(nb // SAMPLE_BLK,),
        in_specs=[
            pl.BlockSpec((SAMPLE_BLK, P_WIDTH), lambda i: (i, 0)),
            _const_spec((nb, P_WIDTH)),
            pl.BlockSpec((SAMPLE_BLK, WINDOW, KV_WIDTH), blk3),
            pl.BlockSpec((SAMPLE_BLK, WINDOW, KV_WIDTH), blk3),
            pl.BlockSpec((SAMPLE_BLK, GLA_HEADS, GLA_DK, GLA_DV), blk4),
            _const_spec((ATTN_HEADS, N_BUCKETS)),
            _const_spec((ATTN_HEADS, 1)),
            _const_spec((1, WINDOW)),
            _const_spec((1, GLA_DV)),
        ],
        out_specs=[
            pl.BlockSpec((SAMPLE_BLK, MIX_WIDTH), lambda i: (i, 0)),
            pl.BlockSpec((SAMPLE_BLK, WINDOW, KV_WIDTH), blk3),
            pl.BlockSpec((SAMPLE_BLK, WINDOW, KV_WIDTH), blk3),
            pl.BlockSpec((SAMPLE_BLK, GLA_HEADS, GLA_DK, GLA_DV), blk4),
        ],
        out_shape=[
            jax.ShapeDtypeStruct((nb, MIX_WIDTH), F32),
            jax.ShapeDtypeStruct((nb, WINDOW, KV_WIDTH), F32),
            jax.ShapeDtypeStruct((nb, WINDOW, KV_WIDTH), F32),
            jax.ShapeDtypeStruct((nb, GLA_HEADS, GLA_DK, GLA_DV), F32),
        ],
        scratch_shapes=[
            pltpu.VMEM((3 * GLA_QK_WIDTH, nb), BF16),
            pltpu.VMEM((2 * GLA_QK_WIDTH, nb), BF16),
            pltpu.VMEM((ATTN_HEADS, WINDOW), F32),
            pltpu.VMEM((SAMPLE_BLK * ATTN_HEADS, WINDOW), F32),
            pltpu.VMEM((SAMPLE_BLK * ATTN_HEADS, KV_WIDTH), F32),
            pltpu.VMEM((SAMPLE_BLK, GLA_WIDTH), F32),
        ],
        compiler_params=pltpu.CompilerParams(
            dimension_semantics=("arbitrary",), vmem_limit_bytes=VMEM_LIMIT),
        name="sample_mixer",
    )(ps, ps, cache_k, cache_v, state, relbt, sinks_col, jnp.asarray(_BUCKET_SAMPLE), gn)


def _finish_kernel(x_ref, mix_ref, xs_ref, mixs_ref, wo_ref, g_ref, wg_ref, wu_ref, wd_ref,
                   y_ref, ys_ref, *, ff_chunks):
    weights = (wo_ref, g_ref, wg_ref, wu_ref, wd_ref)
    _finish_rows(x_ref, mix_ref, *weights, y_ref, ff_chunks=ff_chunks)

    @pl.when(pl.program_id(0) == pl.num_programs(0) - 1)
    def _samples():
        _finish_rows(xs_ref, mixs_ref, *weights, ys_ref, ff_chunks=ff_chunks)


def _finish_rows(x_ref, mix_ref, wo_ref, g_ref, wg_ref, wu_ref, wd_ref, y_ref, *, ff_chunks):
    h = x_ref[...] + jnp.dot(mix_ref[...].astype(BF16), wo_ref[...], preferred_element_type=F32)
    r = lax.rsqrt(jnp.mean(h * h, axis=-1, keepdims=True) + EPS)
    z = (h * g_ref[...]).astype(BF16)
    n_tiles = wd_ref.shape[0] // MXU_TILE
    acc = h
    for c in range(ff_chunks):
        c0 = ((c * n_tiles) // ff_chunks) * MXU_TILE
        c1 = (((c + 1) * n_tiles) // ff_chunks) * MXU_TILE
        gate = jnp.dot(z, wg_ref[:, c0:c1], preferred_element_type=F32) * r
        up = jnp.dot(z, wu_ref[:, c0:c1], preferred_element_type=F32) * r
        act = ((gate * _sigmoid(gate)) * up).astype(BF16)
        acc = acc + jnp.dot(act, wd_ref[c0:c1, :], preferred_element_type=F32)
    y_ref[...] = acc


def _finish(x, mix, xs, mixs, wo, g_ffn, wg, wu, wd, rows):
    t = x.shape[0]
    ns = xs.shape[0]
    d_ff = wd.shape[0]
    assert d_ff % MXU_TILE == 0
    return pl.pallas_call(
        functools.partial(_finish_kernel, ff_chunks=FF_CHUNKS),
        grid=(t // rows,),
        in_specs=[
            pl.BlockSpec((rows, D_MODEL), lambda i: (i, 0)),
            pl.BlockSpec((rows, MIX_WIDTH), lambda i: (i, 0)),
            _const_spec((ns, D_MODEL)),
            _const_spec((ns, MIX_WIDTH)),
            _const_spec((MIX_WIDTH, D_MODEL)),
            _const_spec((1, D_MODEL)),
            _const_spec((D_MODEL, d_ff)),
            _const_spec((D_MODEL, d_ff)),
            _const_spec((d_ff, D_MODEL)),
        ],
        out_specs=[pl.BlockSpec((rows, D_MODEL), lambda i: (i, 0)),
                   pl.BlockSpec((ns, D_MODEL), lambda i: (0, 0))],
        out_shape=[jax.ShapeDtypeStruct((t, D_MODEL), F32),
                   jax.ShapeDtypeStruct((ns, D_MODEL), F32)],
        compiler_params=pltpu.CompilerParams(
            dimension_semantics=("arbitrary",), vmem_limit_bytes=VMEM_LIMIT),
        name="finish",
    )(x, mix, xs, mixs, wo, g_ffn, wg, wu, wd)


PROMPT_ROWS = 512
PROJ_ROWS = 1024


def kernel(x_prompt, x_sample, cache_k, cache_v, state_gla, attn_norm_g, w_in, q_norm_g, k_norm_g,
           attn_sinks, rel_bias, w_gla_gate2, b_gla_gate, gla_norm_g, w_o, ffn_norm_g, w_gate, w_up,
           w_down):
    depth = w_in.shape[0]
    batch, seq, _ = x_prompt.shape
    dec_batch, dec_seq, _ = x_sample.shape
    wb = cache_k.shape[2]
    assert batch == 1 and dec_seq == 1 and wb == WINDOW
    assert seq % PROMPT_ROWS == 0 and seq % PROJ_ROWS == 0
    assert dec_batch % SAMPLE_BLK == 0 and dec_batch % LANES == 0
    assert rel_bias.shape == (N_BUCKETS, ATTN_HEADS)

    xp = x_prompt.reshape(seq, D_MODEL)
    xs = x_sample.reshape(dec_batch, D_MODEL)
    relb_flat = rel_bias.reshape(-1)
    relb_t = rel_bias.T
    outs = ([], [], [], [], [], [])
    for l in range(depth):
        w_main = w_in[l, :, :MAIN_WIDTH].astype(BF16)
        w_lr = jnp.pad(w_in[l, :, MAIN_WIDTH:], ((0, 0), (0, RANK_PAD - GLA_RANK))).astype(BF16)
        w2p = jnp.pad(w_gla_gate2[l], ((0, RANK_PAD - GLA_RANK), (0, 0))).astype(BF16)
        proj_w = (attn_norm_g[l][None, :], w_main, w_lr, jnp.tile(q_norm_g[l], ATTN_HEADS)[None, :],
                  jnp.tile(k_norm_g[l], ATTN_KV_HEADS)[None, :], w2p, b_gla_gate[l][None, :])
        fin_w = (w_o[l].astype(BF16), ffn_norm_g[l][None, :], w_gate[l].astype(BF16),
                 w_up[l].astype(BF16), w_down[l].astype(BF16))
        gn = gla_norm_g[l][None, :]

        pp, kt_p, ps = _project(xp, xs, *proj_w, rows=PROJ_ROWS)
        mix_p, st_p = _prompt_mixer(pp, kt_p, relb_flat, attn_sinks[l], gn)
        mix_s, kwin, vwin, st_s = _sample_mixer(
            ps, cache_k[l].reshape(dec_batch, wb, KV_WIDTH), cache_v[l].reshape(dec_batch, wb, KV_WIDTH),
            state_gla[l].astype(F32), relb_t, attn_sinks[l][:, None], gn)
        xp_in = xp
        xp, xs = _finish(xp_in, mix_p, xs, mix_s, *fin_w, rows=PROMPT_ROWS)
        outs[0].append(pp[seq - wb:, C_KA:C_VA].reshape(batch, wb, ATTN_KV_HEADS, HEAD_DIM))
        outs[1].append(pp[seq - wb:, C_VA:C_QG].reshape(batch, wb, ATTN_KV_HEADS, HEAD_DIM))
        outs[2].append(st_p.T.reshape(batch, GLA_HEADS, GLA_DK, GLA_DV).astype(state_gla.dtype))
        outs[3].append(kwin.reshape(dec_batch, wb, ATTN_KV_HEADS, HEAD_DIM))
        outs[4].append(vwin.reshape(dec_batch, wb, ATTN_KV_HEADS, HEAD_DIM))
        outs[5].append(st_s.astype(state_gla.dtype))

    y_prompt = xp.reshape(batch, seq, D_MODEL)
    y_sample = xs.reshape(dec_batch, dec_seq, D_MODEL)
    return (y_prompt, y_sample) + tuple(jnp.stack(o) for o in outs)
```

```python
import functools
import math

import numpy as np
import jax
import jax.numpy as jnp
from jax import lax
from jax.experimental import pallas as pl
from jax.experimental.pallas import tpu as pltpu

F32 = jnp.float32
BF16 = jnp.bfloat16

D_MODEL = 1024
HEAD_DIM = 64
ATTN_HEADS = 8
ATTN_KV_HEADS = 2
WINDOW = 128
N_BUCKETS = 32
MAX_DISTANCE = 128
GLA_HEADS = 4
GLA_DK = 64
GLA_DV = 128
GLA_RANK = 16
GLA_TAU = 16.0
EPS = 1e-6
ATTN_WIDTH = ATTN_HEADS * HEAD_DIM
KV_WIDTH = ATTN_KV_HEADS * HEAD_DIM
GLA_QK_WIDTH = GLA_HEADS * GLA_DK
GLA_WIDTH = GLA_HEADS * GLA_DV
MIX_WIDTH = ATTN_WIDTH + GLA_WIDTH
MAIN_WIDTH = ATTN_WIDTH + 2 * KV_WIDTH + 2 * GLA_QK_WIDTH + 2 * GLA_WIDTH
LANES = 128
SUBLANES = 8
MXU_TILE = 256
FF_CHUNKS = 4
RANK_PAD = LANES
IN_PAD_WIDTH = MAIN_WIDTH + RANK_PAD

C_QA = 0
C_KA = C_QA + ATTN_WIDTH
C_VA = C_KA + KV_WIDTH
C_QG = C_VA + KV_WIDTH
C_KG = C_QG + GLA_QK_WIDTH
C_VG = C_KG + GLA_QK_WIDTH
C_RG = C_VG + GLA_WIDTH
C_LA = C_RG + GLA_WIDTH
P_WIDTH = C_LA + GLA_QK_WIDTH

W_QA = 0
W_KV = W_QA + ATTN_WIDTH
W_QG = W_KV + 2 * KV_WIDTH + RANK_PAD
W_VG = W_QG + 2 * GLA_QK_WIDTH
W_RG = W_VG + GLA_WIDTH

BLK = 128
N_LEVELS = 7
NEG = -1e30
ATTN_SCALE = HEAD_DIM ** -0.5
SAMPLE_BLK = 8
VMEM_LIMIT = 56 * 1024 * 1024


def _t5_bucket_np(dist):
    n = np.maximum(dist, 0)
    max_exact = N_BUCKETS // 2
    nf = np.maximum(n, 1).astype(np.float64)
    large = max_exact + (np.log(nf / max_exact) / math.log(MAX_DISTANCE / max_exact)
                         * (N_BUCKETS - max_exact)).astype(np.int32)
    large = np.minimum(large, N_BUCKETS - 1)
    return np.where(n < max_exact, n, large).astype(np.int32)


def _prompt_bucket_tables():
    i = np.arange(BLK)[:, None]
    j = np.arange(2 * BLK)[None, :]
    dist = BLK + i - j
    band = (dist >= 0) & (dist < WINDOW)
    bucket = _t5_bucket_np(dist)
    t0 = np.where(band, bucket, -1)
    t1 = np.where(band & (j >= BLK), bucket, -1)
    return np.stack([t0, t1]).astype(np.int32)


def _level_tables():
    t = np.arange(BLK)[:, None]
    s = np.arange(BLK)[None, :]
    x = t ^ s
    lev = np.where(x > 0, np.floor(np.log2(np.maximum(x, 1))).astype(np.int32) + 1, 0)
    lev = np.where(s > t, -1, lev).astype(np.int32)
    tri = (s <= t).astype(np.float32)
    return lev, np.concatenate([tri, tri, tri], axis=1)


_BUCKET_PROMPT = _prompt_bucket_tables()
_LEV, _TRI3 = _level_tables()
_BUCKET_SAMPLE = _t5_bucket_np((WINDOW - 1) - np.arange(WINDOW))[None, :].astype(np.int32)


def _nt_dot(a, b):
    return lax.dot_general(a, b, (((1,), (1,)), ((), ())), preferred_element_type=F32)


def _tn_dot(a, b):
    return lax.dot_general(a, b, (((0,), (0,)), ((), ())), preferred_element_type=F32)


def _head_mean_sq(x):
    lo = lax.broadcasted_iota(jnp.int32, (x.shape[0], LANES), 1) < HEAD_DIM
    outs = []
    for c in range(x.shape[1] // LANES):
        y = x[:, c * LANES:(c + 1) * LANES]
        y = y * y
        s_lo = jnp.sum(jnp.where(lo, y, 0.0), axis=-1, keepdims=True)
        s_hi = jnp.sum(jnp.where(lo, 0.0, y), axis=-1, keepdims=True)
        outs.append(jnp.where(lo, s_lo, s_hi) * (1.0 / HEAD_DIM))
    return outs[0] if len(outs) == 1 else jnp.concatenate(outs, axis=1)


def _sigmoid(x):
    return 1.0 / (1.0 + jnp.exp(-x))


def _split3_rows(x):
    hi = x.astype(BF16)
    r1 = x - hi.astype(F32)
    mid = r1.astype(BF16)
    lo = (r1 - mid.astype(F32)).astype(BF16)
    return jnp.concatenate([hi, mid, lo], axis=0)


def _proj_kernel(x_ref, xs_ref, g_ref, win_ref, wlr_ref, qn_ref, kn_ref, w2_ref, b2_ref, tri_ref,
                 *rest):
    n_cast = (len(rest) - 4) // 2
    cast_in, (out_ref, kt_ref, outs_ref) = rest[:n_cast], rest[n_cast:n_cast + 3]
    cast_out, w_ref = rest[n_cast + 3:-1], rest[-1]
    for src, dst in zip(cast_in, cast_out):
        dst[...] = src[...].astype(BF16)

    @pl.when(pl.program_id(0) == 0)
    def _stage_weight():
        kv_end = ATTN_WIDTH + 2 * KV_WIDTH
        w_ref[:, 0:kv_end] = win_ref[:, 0:kv_end].astype(BF16)
        w_ref[:, kv_end:W_QG] = wlr_ref[...]
        w_ref[:, W_QG:IN_PAD_WIDTH] = win_ref[:, kv_end:MAIN_WIDTH].astype(BF16)

    weights = (g_ref, w_ref, qn_ref, kn_ref, w2_ref, b2_ref, tri_ref)
    _proj_rows(x_ref, *weights, out_ref, kt_ref, block_cumsum=True)

    @pl.when(pl.program_id(0) == pl.num_programs(0) - 1)
    def _samples():
        _proj_rows(xs_ref, *weights, outs_ref, None, block_cumsum=False)


def _proj_rows(x_ref, g_ref, w_ref, qn_ref, kn_ref, w2_ref, b2_ref, tri_ref, out_ref, kt_ref, *,
               block_cumsum):
    x = x_ref[...]
    r = lax.rsqrt(jnp.mean(x * x, axis=-1, keepdims=True) + EPS)
    n = (x * g_ref[...]).astype(BF16)

    def seg(c0, c1):
        return jnp.dot(n, w_ref[:, c0:c1], preferred_element_type=F32) * r

    kvl = seg(W_KV, W_QG)
    lr = kvl[:, 2 * KV_WIDTH:].astype(BF16)
    z = jnp.dot(lr, w2_ref[...], preferred_element_type=F32) + b2_ref[...]
    log_a = (jnp.minimum(z, 0.0) - jnp.log1p(jnp.exp(-jnp.abs(z)))) / GLA_TAU
    if block_cumsum:
        for blk in range(x.shape[0] // BLK):
            rows = slice(blk * BLK, (blk + 1) * BLK)
            out_ref[rows, C_LA:P_WIDTH] = jnp.dot(
                tri_ref[...], _split3_rows(log_a[rows]), preferred_element_type=F32)
    else:
        out_ref[:, C_LA:P_WIDTH] = log_a
    k = kvl[:, :KV_WIDTH]
    k = k * lax.rsqrt(_head_mean_sq(k) + EPS) * kn_ref[...]
    out_ref[:, C_KA:C_VA] = k
    if kt_ref is not None:
        kt_ref[...] = k.T.astype(BF16)
    out_ref[:, C_VA:C_QG] = kvl[:, KV_WIDTH:2 * KV_WIDTH]
    q = seg(W_QA, W_KV)
    out_ref[:, C_QA:C_KA] = q * lax.rsqrt(_head_mean_sq(q) + EPS) * qn_ref[...] * ATTN_SCALE
    qk_g = seg(W_QG, W_VG)
    out_ref[:, C_QG:C_KG] = qk_g[:, :GLA_QK_WIDTH] * (GLA_DK ** -0.5)
    out_ref[:, C_KG:C_VG] = qk_g[:, GLA_QK_WIDTH:]
    out_ref[:, C_VG:C_RG] = seg(W_VG, W_RG)
    out_ref[:, C_RG:C_LA] = seg(W_RG, IN_PAD_WIDTH)


def _const_spec(shape):
    nd = len(shape)
    return pl.BlockSpec(shape, lambda i: (0,) * nd, pipeline_mode=pl.Buffered(1))


def _project(x, xs, g_attn, w_in, w_lr, qn, kn, w2p, b2, rows, to_cast):
    t = x.shape[0]
    ns = xs.shape[0]
    steps = t // rows
    n_slabs = max(steps // CAST_STRIDE, 1)
    cast_specs = []
    for w in to_cast:
        slab = w.shape[0] // n_slabs
        assert w.shape[0] % n_slabs == 0 and slab % (2 * SUBLANES) == 0
        cast_specs.append(pl.BlockSpec((slab, w.shape[1]), lambda i: (i // CAST_STRIDE, 0)))
    return pl.pallas_call(
        _proj_kernel,
        grid=(steps,),
        in_specs=[
            pl.BlockSpec((rows, D_MODEL), lambda i: (i, 0)),
            _const_spec((ns, D_MODEL)),
            _const_spec((1, D_MODEL)),
            _const_spec(w_in.shape),
            _const_spec((D_MODEL, RANK_PAD)),
            _const_spec((1, ATTN_WIDTH)),
            _const_spec((1, KV_WIDTH)),
            _const_spec((RANK_PAD, GLA_QK_WIDTH)),
            _const_spec((1, GLA_QK_WIDTH)),
            _const_spec((BLK, 3 * BLK)),
        ] + cast_specs,
        out_specs=[pl.BlockSpec((rows, P_WIDTH), lambda i: (i, 0)),
                   pl.BlockSpec((KV_WIDTH, rows), lambda i: (0, i)),
                   pl.BlockSpec((ns, P_WIDTH), lambda i: (0, 0))] + cast_specs,
        out_shape=[jax.ShapeDtypeStruct((t, P_WIDTH), F32),
                   jax.ShapeDtypeStruct((KV_WIDTH, t), BF16),
                   jax.ShapeDtypeStruct((ns, P_WIDTH), F32)]
        + [jax.ShapeDtypeStruct(w.shape, BF16) for w in to_cast],
        scratch_shapes=[pltpu.VMEM((D_MODEL, IN_PAD_WIDTH), BF16)],
        compiler_params=pltpu.CompilerParams(
            dimension_semantics=("arbitrary",), vmem_limit_bytes=VMEM_LIMIT),
        name="proj",
    )(x, xs, g_attn, w_in, w_lr, qn, kn, w2p, b2, jnp.asarray(_TRI3, BF16), *to_cast)


def _boundary_rows(b_ref, r0, c0, b, row, level):
    m = 1 << (level - 1)
    if 2 * m >= SUBLANES:
        pieces = [jnp.broadcast_to(b_ref[r0 + g * 2 * m + m - 1:r0 + g * 2 * m + m, c0:c0 + LANES],
                                   (2 * m, LANES))
                  for g in range(BLK // (2 * m))]
        return pieces[0] if len(pieces) == 1 else jnp.concatenate(pieces, axis=0)
    pos = row & (2 * m - 1)
    tiles = b.reshape(BLK // SUBLANES, SUBLANES, LANES)
    out = b
    for p in range(2 * m):
        shift = (m - 1) - p
        if shift != 0:
            rolled = pltpu.roll(tiles, (-shift) % SUBLANES, 1).reshape(BLK, LANES)
            out = jnp.where(pos == p, rolled, out)
    return out


def _mixer_init(relb_ref, bucket_ref, kbd, vbd, kprev_t, vprev, st_scr, mb_scr):
    kbd[...] = jnp.zeros_like(kbd)
    vbd[...] = jnp.zeros_like(vbd)
    for g in range(ATTN_KV_HEADS):
        for half in range(2):
            vbd[g, 2 * half * BLK:(2 * half + 2) * BLK,
                KV_WIDTH + half * HEAD_DIM:KV_WIDTH + (half + 1) * HEAD_DIM] = (
                    jnp.ones((2 * BLK, HEAD_DIM), BF16))
    kprev_t[...] = jnp.zeros_like(kprev_t)
    vprev[...] = jnp.zeros_like(vprev)
    st_scr[...] = jnp.zeros_like(st_scr)
    bk = bucket_ref[0]
    acc = [jnp.zeros(bk.shape, F32) for _ in range(ATTN_HEADS)]
    for b in range(N_BUCKETS):
        hit = bk == b
        for h in range(ATTN_HEADS):
            acc[h] = jnp.where(hit, relb_ref[b * ATTN_HEADS + h], acc[h])
    for tb in range(2):
        masked = bucket_ref[tb] < 0
        for h in range(ATTN_HEADS):
            mb_scr[tb, h // 2, :, (h % 2) * 2 * BLK:(h % 2 + 1) * 2 * BLK] = (
                jnp.where(masked, NEG, acc[h]))


def _mixer_attention(p_ref, kt_ref, r0, table, sink_ref, omix_ref, kbd, vbd, kprev_t, vprev,
                     mb_scr):
    rows = slice(r0, r0 + BLK)
    lo1 = lax.broadcasted_iota(jnp.int32, (BLK, LANES), 1) < HEAD_DIM
    k_t = kt_ref[:, r0:r0 + BLK]
    k_prev_t = kprev_t[...]
    for g in range(ATTN_KV_HEADS):
        hd = slice(g * HEAD_DIM, (g + 1) * HEAD_DIM)
        kbd[g, 0:HEAD_DIM, 0:BLK] = k_prev_t[hd]
        kbd[g, 0:HEAD_DIM, BLK:2 * BLK] = k_t[hd]
        kbd[g, HEAD_DIM:2 * HEAD_DIM, 2 * BLK:3 * BLK] = k_prev_t[hd]
        kbd[g, HEAD_DIM:2 * HEAD_DIM, 3 * BLK:4 * BLK] = k_t[hd]
    kprev_t[...] = k_t
    v_cur = p_ref[rows,C_VA:C_QG]
    v_swap = pltpu.roll(v_cur, HEAD_DIM, 1)
    v_parts = (jnp.where(lo1, v_cur, 0.0).astype(BF16), jnp.where(lo1, 0.0, v_swap).astype(BF16),
               jnp.where(lo1, v_swap, 0.0).astype(BF16), jnp.where(lo1, 0.0, v_cur).astype(BF16))
    for n, part in enumerate(v_parts):
        g, half = divmod(n, 2)
        vbd[g, 2 * half * BLK:(2 * half + 1) * BLK, 0:KV_WIDTH] = vprev[n]
        vbd[g, (2 * half + 1) * BLK:(2 * half + 2) * BLK, 0:KV_WIDTH] = part
        vprev[n] = part
    chunks_per_kv = ATTN_HEADS // ATTN_KV_HEADS // 2
    for g in range(ATTN_KV_HEADS):
        c_first = g * chunks_per_kv
        qs = jnp.concatenate(
            [p_ref[rows,C_QA + (c_first + c) * LANES:C_QA + (c_first + c + 1) * LANES].astype(BF16)
             for c in range(chunks_per_kv)], axis=0)
        mb = jnp.concatenate([mb_scr[table, c_first + c] for c in range(chunks_per_kv)], axis=0)
        s = jnp.dot(qs, kbd[g], preferred_element_type=F32) + mb
        prob_rows, sink_rows = [], []
        for c in range(chunks_per_kv):
            probs, sinks = [], []
            for e in range(2):
                se = s[c * BLK:(c + 1) * BLK, e * 2 * BLK:(e + 1) * 2 * BLK]
                sk = sink_ref[2 * (c_first + c) + e]
                m = jnp.maximum(jnp.max(se, axis=-1, keepdims=True), sk)
                probs.append(jnp.exp(se - m).astype(BF16))
                sinks.append(jnp.exp(sk - m))
            prob_rows.append(jnp.concatenate(probs, axis=1))
            sink_rows.append(jnp.where(lo1, sinks[0], sinks[1]))
        o_den = jnp.dot(jnp.concatenate(prob_rows, axis=0), vbd[g], preferred_element_type=F32)
        o = o_den[:, :KV_WIDTH] / (o_den[:, KV_WIDTH:] + jnp.concatenate(sink_rows, axis=0))
        for c in range(chunks_per_kv):
            omix_ref[rows,(c_first + c) * LANES:(c_first + c + 1) * LANES] = (
                o[c * BLK:(c + 1) * BLK].astype(omix_ref.dtype))


def _mixer_gla(pairs, p_ref, r0, lev_ref, gn_ref, omix_ref, st_ref, st_scr):
    rows = slice(r0, r0 + BLK)
    lo1 = lax.broadcasted_iota(jnp.int32, (BLK, LANES), 1) < HEAD_DIM
    lo_bf = jnp.where(lo1, 1.0, 0.0).astype(BF16)
    hi_bf = jnp.where(lo1, 0.0, 1.0).astype(BF16)
    row = lax.broadcasted_iota(jnp.int32, (BLK, LANES), 0)
    zero_blk = jnp.zeros((BLK, LANES), BF16)
    for c in pairs:
        c0 = c * LANES
        q_at = lambda a, z: p_ref[r0 + a:r0 + z, C_QG + c0:C_QG + c0 + LANES]
        k_at = lambda a, z: p_ref[r0 + a:r0 + z, C_KG + c0:C_KG + c0 + LANES]
        b_at = lambda a, z: p_ref[r0 + a:r0 + z, C_LA + c0:C_LA + c0 + LANES]

        def pair_scores(qtb, ktb):
            rhs = jnp.concatenate([ktb * lo_bf, ktb * hi_bf], axis=0)
            return _nt_dot(qtb, rhs)

        s0 = pair_scores(q_at(0, BLK).astype(BF16), k_at(0, BLK).astype(BF16))
        sc = [jnp.where(lev_ref[...] == 0, s0[:, e * BLK:(e + 1) * BLK], 0.0) for e in range(2)]
        for level in range(1, N_LEVELS + 1):
            m = 1 << (level - 1)
            if m >= SUBLANES:
                qs, ks = [], []
                zeros = jnp.zeros((m, LANES), BF16)
                for g in range(BLK // (2 * m)):
                    lo_a, up_a, up_z = g * 2 * m, g * 2 * m + m, (g + 1) * 2 * m
                    rb = jnp.broadcast_to(b_at(up_a - 1, up_a), (m, LANES))
                    qs += [zeros, (q_at(up_a, up_z) * jnp.exp(b_at(up_a, up_z) - rb)).astype(BF16)]
                    ks += [(k_at(lo_a, up_a) * jnp.exp(rb - b_at(lo_a, up_a))).astype(BF16), zeros]
                qtb = jnp.concatenate(qs, axis=0)
                ktb = jnp.concatenate(ks, axis=0)
            else:
                bc = b_at(0, BLK)
                d = bc - _boundary_rows(p_ref, r0, C_LA + c0, bc, row, level)
                upper = ((row >> (level - 1)) & 1) == 1
                qtb = (q_at(0, BLK) * jnp.exp(jnp.where(upper, d, NEG))).astype(BF16)
                ktb = (k_at(0, BLK) * jnp.exp(jnp.where(upper, NEG, -d))).astype(BF16)
            sl = pair_scores(qtb, ktb)
            sc = [jnp.where(lev_ref[...] == level, sl[:, e * BLK:(e + 1) * BLK], sc[e])
                  for e in range(2)]
        sc = jnp.concatenate(sc, axis=1)
        qc, kc, bc = q_at(0, BLK), k_at(0, BLK), b_at(0, BLK)

        b_last = bc[BLK - 1:BLK, :]
        v0 = p_ref[rows,C_VG + 2 * c0:C_VG + 2 * c0 + LANES].astype(BF16)
        v1 = p_ref[rows,C_VG + 2 * c0 + LANES:C_VG + 2 * c0 + 2 * LANES].astype(BF16)
        v_bd = jnp.concatenate([jnp.concatenate([v0, zero_blk], axis=1),
                                jnp.concatenate([zero_blk, v1], axis=1)], axis=0)
        st_c = st_scr[:, c0:c0 + LANES]
        stb = st_c.astype(BF16)
        st_rhs = jnp.concatenate([stb * lo_bf, stb * hi_bf], axis=0)
        o = (jnp.dot(sc.astype(BF16), v_bd, preferred_element_type=F32)
             + _nt_dot((qc * jnp.exp(bc)).astype(BF16), st_rhs))
        kd = (kc * jnp.exp(b_last - bc)).astype(BF16)
        upd = _tn_dot(jnp.concatenate([v0, v1], axis=1), kd)
        new_st = st_c * jnp.exp(b_last) + jnp.where(lo1, upd[:BLK], upd[BLK:])
        st_scr[:, c0:c0 + LANES] = new_st
        st_ref[:, c0:c0 + LANES] = new_st
        for e in range(2):
            h = 2 * c + e
            oh = o[:, e * LANES:(e + 1) * LANES]
            og = oh * lax.rsqrt(jnp.mean(oh * oh, axis=-1, keepdims=True) + EPS) * gn_ref[...]
            rg = p_ref[rows,C_RG + h * GLA_DV:C_RG + (h + 1) * GLA_DV]
            gated = og * (rg * _sigmoid(rg))
            omix_ref[rows,ATTN_WIDTH + h * GLA_DV:ATTN_WIDTH + (h + 1) * GLA_DV] = (
                gated.astype(omix_ref.dtype))


def _prompt_mixer_kernel(relb_ref, sink_ref, p_ref, kt_ref, bucket_ref, lev_ref, gn_ref,
                         omix_ref, st_ref, kbd, vbd, kprev_t, vprev, st_scr, mb_scr):
    i = pl.program_id(0)

    @pl.when(i == 0)
    def _init():
        _mixer_init(relb_ref, bucket_ref, kbd, vbd, kprev_t, vprev, st_scr, mb_scr)

    _mixer_attention(p_ref, kt_ref, 0, jnp.where(i == 0, 1, 0), sink_ref, omix_ref, kbd, vbd,
                     kprev_t, vprev, mb_scr)
    _mixer_gla(range(GLA_HEADS // 2), p_ref, 0, lev_ref, gn_ref, omix_ref, st_ref, st_scr)


def _prompt_mixer(p, kt, relb, sinks, gn):
    t = p.shape[0]
    smem = pl.BlockSpec(memory_space=pltpu.SMEM)
    return pl.pallas_call(
        _prompt_mixer_kernel,
        grid=(t // BLK,),
        in_specs=[
            smem, smem,
            pl.BlockSpec((BLK, P_WIDTH), lambda i: (i, 0)),
            pl.BlockSpec((KV_WIDTH, BLK), lambda i: (0, i)),
            _const_spec((2, BLK, 2 * BLK)),
            _const_spec((BLK, BLK)),
            _const_spec((1, GLA_DV)),
        ],
        out_specs=[
            pl.BlockSpec((BLK, MIX_WIDTH), lambda i: (i, 0)),
            pl.BlockSpec((GLA_DV, GLA_QK_WIDTH), lambda i: (0, 0)),
        ],
        out_shape=[
            jax.ShapeDtypeStruct((t, MIX_WIDTH), BF16),
            jax.ShapeDtypeStruct((GLA_DV, GLA_QK_WIDTH), F32),
        ],
        scratch_shapes=[
            pltpu.VMEM((ATTN_KV_HEADS, 2 * HEAD_DIM, 4 * BLK), BF16),
            pltpu.VMEM((ATTN_KV_HEADS, 4 * BLK, 2 * KV_WIDTH), BF16),
            pltpu.VMEM((KV_WIDTH, BLK), BF16),
            pltpu.VMEM((2 * ATTN_KV_HEADS, BLK, KV_WIDTH), BF16),
            pltpu.VMEM((GLA_DV, GLA_QK_WIDTH), F32),
            pltpu.VMEM((2, ATTN_HEADS // 2, BLK, 4 * BLK), F32),
        ],
        compiler_params=pltpu.CompilerParams(
            dimension_semantics=("arbitrary",), vmem_limit_bytes=VMEM_LIMIT),
        name="prompt_mixer",
    )(relb, sinks, p, kt, jnp.asarray(_BUCKET_PROMPT), jnp.asarray(_LEV), gn)


def _sample_mixer_kernel(ps_ref, pfull_ref, ck_ref, cv_ref, st_ref, relbt_ref, sink_ref, bucket_ref,
                         gn_ref, omix_ref, kwin_ref, vwin_ref, stout_ref, lat_scr, kqt_scr,
                         bias_scr, s_scr, o_scr, og_scr):
    i = pl.program_id(0)
    nb = pfull_ref.shape[0]

    @pl.when(i == 0)
    def _init():
        lat_scr[...] = _split3_rows(pfull_ref[:, C_LA:P_WIDTH].T)
        kqt_scr[0:GLA_QK_WIDTH] = pfull_ref[:, C_KG:C_VG].T.astype(BF16)
        kqt_scr[GLA_QK_WIDTH:2 * GLA_QK_WIDTH] = pfull_ref[:, C_QG:C_KG].T.astype(BF16)
        bk = jnp.broadcast_to(bucket_ref[...], (ATTN_HEADS, WINDOW))
        acc = jnp.zeros((ATTN_HEADS, WINDOW), F32)
        for b in range(N_BUCKETS):
            acc = jnp.where(bk == b, relbt_ref[:, b:b + 1], acc)
        bias_scr[...] = acc

    lo = lax.broadcasted_iota(jnp.int32, (1, LANES), 1) < HEAD_DIM
    sub = lax.broadcasted_iota(jnp.int32, (ATTN_HEADS, LANES), 0)
    wrow = lax.broadcasted_iota(jnp.int32, (WINDOW, KV_WIDTH), 0)
    heads_per_kv = ATTN_HEADS // ATTN_KV_HEADS

    n_of_col = i * SAMPLE_BLK + lax.broadcasted_iota(jnp.int32, (nb, SAMPLE_BLK * LANES), 1) // LANES
    pick = jnp.where(lax.broadcasted_iota(jnp.int32, (nb, SAMPLE_BLK * LANES), 0) == n_of_col,
                     1.0, 0.0).astype(BF16)
    la_b = (jnp.dot(lat_scr[0:GLA_QK_WIDTH], pick, preferred_element_type=F32)
            + jnp.dot(lat_scr[GLA_QK_WIDTH:2 * GLA_QK_WIDTH], pick, preferred_element_type=F32)
            + jnp.dot(lat_scr[2 * GLA_QK_WIDTH:3 * GLA_QK_WIDTH], pick, preferred_element_type=F32))
    kq_b = jnp.dot(kqt_scr[...], pick, preferred_element_type=F32)

    for j in range(SAMPLE_BLK):
        k_new = ps_ref[j:j + 1, C_KA:C_VA]
        v_new = ps_ref[j:j + 1, C_VA:C_QG]
        kwin_ref[j] = jnp.where(wrow == WINDOW - 1, k_new, pltpu.roll(ck_ref[j], WINDOW - 1, 0))
        vwin_ref[j] = jnp.where(wrow == WINDOW - 1, v_new, pltpu.roll(cv_ref[j], WINDOW - 1, 0))

    for j in range(SAMPLE_BLK):
        qexp = jnp.zeros((ATTN_HEADS, LANES), F32)
        for c in range(ATTN_HEADS // 2):
            chunk = ps_ref[j:j + 1, C_QA + c * LANES:C_QA + (c + 1) * LANES]
            swapped = pltpu.roll(chunk, HEAD_DIM, 1)
            if (2 * c) // heads_per_kv == 0:
                rows = (jnp.where(lo, chunk, 0.0), jnp.where(lo, swapped, 0.0))
            else:
                rows = (jnp.where(lo, 0.0, swapped), jnp.where(lo, 0.0, chunk))
            for e in range(2):
                qexp = jnp.where(sub == 2 * c + e, rows[e], qexp)
        s_scr[j * ATTN_HEADS:(j + 1) * ATTN_HEADS] = _nt_dot(qexp.astype(BF16), kwin_ref[j].astype(BF16))

    tile = lambda x: jnp.concatenate([x] * SAMPLE_BLK, axis=0)
    sink = tile(sink_ref[...])
    s = s_scr[...] + tile(bias_scr[...])
    m = jnp.maximum(jnp.max(s, axis=-1, keepdims=True), sink)
    pe = jnp.exp(s - m)
    inv_den = 1.0 / (jnp.sum(pe, axis=-1, keepdims=True) + jnp.exp(sink - m))
    peb = pe.astype(BF16)
    for j in range(SAMPLE_BLK):
        o_scr[j * ATTN_HEADS:(j + 1) * ATTN_HEADS] = jnp.dot(
            peb[j * ATTN_HEADS:(j + 1) * ATTN_HEADS], vwin_ref[j].astype(BF16),
            preferred_element_type=F32)
    o_all = o_scr[...] * inv_den
    o_swap = pltpu.roll(o_all, HEAD_DIM, 1)
    for j in range(SAMPLE_BLK):
        r = j * ATTN_HEADS
        for c in range(ATTN_HEADS // 2):
            if (2 * c) // heads_per_kv == 0:
                piece = jnp.where(lo, o_all[r + 2 * c:r + 2 * c + 1, :], o_swap[r + 2 * c + 1:r + 2 * c + 2, :])
            else:
                piece = jnp.where(lo, o_swap[r + 2 * c:r + 2 * c + 1, :], o_all[r + 2 * c + 1:r + 2 * c + 2, :])
            omix_ref[j:j + 1, c * LANES:(c + 1) * LANES] = piece

    for j in range(SAMPLE_BLK):
        cols = slice(j * LANES, (j + 1) * LANES)
        for h in range(GLA_HEADS):
            rs = slice(h * GLA_DK, (h + 1) * GLA_DK)
            qs = slice(GLA_QK_WIDTH + h * GLA_DK, GLA_QK_WIDTH + (h + 1) * GLA_DK)
            v_row = ps_ref[j:j + 1, C_VG + h * GLA_DV:C_VG + (h + 1) * GLA_DV]
            s_new = jnp.exp(la_b[rs, cols]) * st_ref[j, h] + kq_b[rs, cols] * v_row
            stout_ref[j, h] = s_new
            og_scr[j:j + 1, h * GLA_DV:(h + 1) * GLA_DV] = jnp.sum(
                kq_b[qs, cols] * s_new, axis=0, keepdims=True)
    for h in range(GLA_HEADS):
        hs = slice(h * GLA_DV, (h + 1) * GLA_DV)
        og = og_scr[:, hs]
        og = og * lax.rsqrt(jnp.mean(og * og, axis=-1, keepdims=True) + EPS) * gn_ref[...]
        rg = ps_ref[:, C_RG + h * GLA_DV:C_RG + (h + 1) * GLA_DV]
        omix_ref[:, ATTN_WIDTH + h * GLA_DV:ATTN_WIDTH + (h + 1) * GLA_DV] = og * (rg * _sigmoid(rg))


def _sample_mixer(ps, cache_k, cache_v, state, relbt, sinks_col, gn):
    nb = ps.shape[0]
    blk3 = lambda i: (i, 0, 0)
    blk4 = lambda i: (i, 0, 0, 0)
    return pl.pallas_call(
        _sample_mixer_kernel,
        grid=(nb // SAMPLE_BLK,),
        in_specs=[
            pl.BlockSpec((SAMPLE_BLK, P_WIDTH), lambda i: (i, 0)),
            _const_spec((nb, P_WIDTH)),
            pl.BlockSpec((SAMPLE_BLK, WINDOW, KV_WIDTH), blk3),
            pl.BlockSpec((SAMPLE_BLK, WINDOW, KV_WIDTH), blk3),
            pl.BlockSpec((SAMPLE_BLK, GLA_HEADS, GLA_DK, GLA_DV), blk4),
            _const_spec((ATTN_HEADS, N_BUCKETS)),
            _const_spec((ATTN_HEADS, 1)),
            _const_spec((1, WINDOW)),
            _const_spec((1, GLA_DV)),
        ],
        out_specs=[
            pl.BlockSpec((SAMPLE_BLK, MIX_WIDTH), lambda i: (i, 0)),
            pl.BlockSpec((SAMPLE_BLK, WINDOW, KV_WIDTH), blk3),
            pl.BlockSpec((SAMPLE_BLK, WINDOW, KV_WIDTH), blk3),
            pl.BlockSpec((SAMPLE_BLK, GLA_HEADS, GLA_DK, GLA_DV), blk4),
        ],
        out_shape=[
            jax.ShapeDtypeStruct((nb, MIX_WIDTH), F32),
            jax.ShapeDtypeStruct((nb, WINDOW, KV_WIDTH), F32),
            jax.ShapeDtypeStruct((nb, WINDOW, KV_WIDTH), F32),
            jax.ShapeDtypeStruct((nb, GLA_HEADS, GLA_DK, GLA_DV), F32),
        ],
        scratch_shapes=[
            pltpu.VMEM((3 * GLA_QK_WIDTH, nb), BF16),
            pltpu.VMEM((2 * GLA_QK_WIDTH, nb), BF16),
            pltpu.VMEM((ATTN_HEADS, WINDOW), F32),
            pltpu.VMEM((SAMPLE_BLK * ATTN_HEADS, WINDOW), F32),
            pltpu.VMEM((SAMPLE_BLK * ATTN_HEADS, KV_WIDTH), F32),
            pltpu.VMEM((SAMPLE_BLK, GLA_WIDTH), F32),
        ],
        compiler_params=pltpu.CompilerParams(
            dimension_semantics=("arbitrary",), vmem_limit_bytes=VMEM_LIMIT),
        name="sample_mixer",
    )(ps, ps, cache_k, cache_v, state, relbt, sinks_col, jnp.asarray(_BUCKET_SAMPLE), gn)


def _finish_kernel(x_ref, mix_ref, xs_ref, mixs_ref, wo_ref, g_ref, wg_ref, wu_ref, wd_ref,
                   y_ref, ys_ref, *, ff_chunks):
    weights = (wo_ref, g_ref, wg_ref, wu_ref, wd_ref)
    _finish_rows(x_ref, mix_ref, *weights, y_ref, ff_chunks=ff_chunks)

    @pl.when(pl.program_id(0) == pl.num_programs(0) - 1)
    def _samples():
        _finish_rows(xs_ref, mixs_ref, *weights, ys_ref, ff_chunks=ff_chunks)


def _finish_rows(x_ref, mix_ref, wo_ref, g_ref, wg_ref, wu_ref, wd_ref, y_ref, *, ff_chunks):
    h = x_ref[...] + jnp.dot(mix_ref[...].astype(BF16), wo_ref[...], preferred_element_type=F32)
    r = lax.rsqrt(jnp.mean(h * h, axis=-1, keepdims=True) + EPS)
    z = (h * g_ref[...]).astype(BF16)
    n_tiles = wd_ref.shape[0] // MXU_TILE
    acc = h
    for c in range(ff_chunks):
        c0 = ((c * n_tiles) // ff_chunks) * MXU_TILE
        c1 = (((c + 1) * n_tiles) // ff_chunks) * MXU_TILE
        gate = jnp.dot(z, wg_ref[:, c0:c1], preferred_element_type=F32) * r
        up = jnp.dot(z, wu_ref[:, c0:c1], preferred_element_type=F32) * r
        act = ((gate * _sigmoid(gate)) * up).astype(BF16)
        acc = acc + jnp.dot(act, wd_ref[c0:c1, :], preferred_element_type=F32)
    y_ref[...] = acc


def _finish(x, mix, xs, mixs, wo, g_ffn, wg, wu, wd, rows):
    t = x.shape[0]
    ns = xs.shape[0]
    d_ff = wd.shape[0]
    assert d_ff % MXU_TILE == 0
    return pl.pallas_call(
        functools.partial(_finish_kernel, ff_chunks=FF_CHUNKS),
        grid=(t // rows,),
        in_specs=[
            pl.BlockSpec((rows, D_MODEL), lambda i: (i, 0)),
            pl.BlockSpec((rows, MIX_WIDTH), lambda i: (i, 0)),
            _const_spec((ns, D_MODEL)),
            _const_spec((ns, MIX_WIDTH)),
            _const_spec((MIX_WIDTH, D_MODEL)),
            _const_spec((1, D_MODEL)),
            _const_spec((D_MODEL, d_ff)),
            _const_spec((D_MODEL, d_ff)),
            _const_spec((d_ff, D_MODEL)),
        ],
        out_specs=[pl.BlockSpec((rows, D_MODEL), lambda i: (i, 0)),
                   pl.BlockSpec((ns, D_MODEL), lambda i: (0, 0))],
        out_shape=[jax.ShapeDtypeStruct((t, D_MODEL), F32),
                   jax.ShapeDtypeStruct((ns, D_MODEL), F32)],
        compiler_params=pltpu.CompilerParams(
            dimension_semantics=("arbitrary",), vmem_limit_bytes=VMEM_LIMIT),
        name="finish",
    )(x, mix, xs, mixs, wo, g_ffn, wg, wu, wd)


PROMPT_ROWS = 1024
PROJ_ROWS = 512
CAST_STRIDE = 2


def kernel(x_prompt, x_sample, cache_k, cache_v, state_gla, attn_norm_g, w_in, q_norm_g, k_norm_g,
           attn_sinks, rel_bias, w_gla_gate2, b_gla_gate, gla_norm_g, w_o, ffn_norm_g, w_gate, w_up,
           w_down):
    depth = w_in.shape[0]
    batch, seq, _ = x_prompt.shape
    dec_batch, dec_seq, _ = x_sample.shape
    wb = cache_k.shape[2]
    assert batch == 1 and dec_seq == 1 and wb == WINDOW
    assert seq % PROMPT_ROWS == 0 and seq % PROJ_ROWS == 0
    assert dec_batch % SAMPLE_BLK == 0 and dec_batch % LANES == 0
    assert rel_bias.shape == (N_BUCKETS, ATTN_HEADS)

    xp = x_prompt.reshape(seq, D_MODEL)
    xs = x_sample.reshape(dec_batch, D_MODEL)
    relb_flat = rel_bias.reshape(-1)
    relb_t = rel_bias.T
    outs = ([], [], [], [], [], [])
    for l in range(depth):
        w_lr = jnp.pad(w_in[l, :, MAIN_WIDTH:], ((0, 0), (0, RANK_PAD - GLA_RANK))).astype(BF16)
        w2p = jnp.pad(w_gla_gate2[l], ((0, RANK_PAD - GLA_RANK), (0, 0))).astype(BF16)
        proj_w = (attn_norm_g[l][None, :], w_in[l], w_lr, jnp.tile(q_norm_g[l], ATTN_HEADS)[None, :],
                  jnp.tile(k_norm_g[l], ATTN_KV_HEADS)[None, :], w2p, b_gla_gate[l][None, :])
        gn = gla_norm_g[l][None, :]

        pp, kt_p, ps, wo_b, wg_b, wu_b, wd_b = _project(
            xp, xs, *proj_w, rows=PROJ_ROWS, to_cast=(w_o[l], w_gate[l], w_up[l], w_down[l]))
        fin_w = (wo_b, ffn_norm_g[l][None, :], wg_b, wu_b, wd_b)
        mix_p, st_p = _prompt_mixer(pp, kt_p, relb_flat, attn_sinks[l], gn)
        mix_s, kwin, vwin, st_s = _sample_mixer(
            ps, cache_k[l].reshape(dec_batch, wb, KV_WIDTH), cache_v[l].reshape(dec_batch, wb, KV_WIDTH),
            state_gla[l].astype(F32), relb_t, attn_sinks[l][:, None], gn)
        xp_in = xp
        xp, xs = _finish(xp_in, mix_p, xs, mix_s, *fin_w, rows=PROMPT_ROWS)
        outs[0].append(pp[seq - wb:, C_KA:C_VA].reshape(batch, wb, ATTN_KV_HEADS, HEAD_DIM))
        outs[1].append(pp[seq - wb:, C_VA:C_QG].reshape(batch, wb, ATTN_KV_HEADS, HEAD_DIM))
        outs[2].append(st_p.T.reshape(batch, GLA_HEADS, GLA_DK, GLA_DV).astype(state_gla.dtype))
        outs[3].append(kwin.reshape(dec_batch, wb, ATTN_KV_HEADS, HEAD_DIM))
        outs[4].append(vwin.reshape(dec_batch, wb, ATTN_KV_HEADS, HEAD_DIM))
        outs[5].append(st_s.astype(state_gla.dtype))

    y_prompt = xp.reshape(batch, seq, D_MODEL)
    y_sample = xs.reshape(dec_batch, dec_seq, D_MODEL)
    return (y_prompt, y_sample) + tuple(jnp.stack(o) for o in outs)
```

```python
import functools
import math

import numpy as np
import jax
import jax.numpy as jnp
from jax import lax
from jax.experimental import pallas as pl
from jax.experimental.pallas import tpu as pltpu

F32 = jnp.float32
BF16 = jnp.bfloat16

D_MODEL = 1024
HEAD_DIM = 64
ATTN_HEADS = 8
ATTN_KV_HEADS = 2
WINDOW = 128
N_BUCKETS = 32
MAX_DISTANCE = 128
GLA_HEADS = 4
GLA_DK = 64
GLA_DV = 128
GLA_RANK = 16
GLA_TAU = 16.0
EPS = 1e-6
ATTN_WIDTH = ATTN_HEADS * HEAD_DIM
KV_WIDTH = ATTN_KV_HEADS * HEAD_DIM
GLA_QK_WIDTH = GLA_HEADS * GLA_DK
GLA_WIDTH = GLA_HEADS * GLA_DV
MIX_WIDTH = ATTN_WIDTH + GLA_WIDTH
MAIN_WIDTH = ATTN_WIDTH + 2 * KV_WIDTH + 2 * GLA_QK_WIDTH + 2 * GLA_WIDTH
LANES = 128
SUBLANES = 8
MXU_TILE = 256
FF_CHUNKS = 4
RANK_PAD = LANES
IN_PAD_WIDTH = MAIN_WIDTH + RANK_PAD

C_QA = 0
C_KA = C_QA + ATTN_WIDTH
C_VA = C_KA + KV_WIDTH
C_QG = C_VA + KV_WIDTH
C_KG = C_QG + GLA_QK_WIDTH
C_VG = C_KG + GLA_QK_WIDTH
C_RG = C_VG + GLA_WIDTH
C_LA = C_RG + GLA_WIDTH
P_WIDTH = C_LA + GLA_QK_WIDTH

W_QA = 0
W_KV = W_QA + ATTN_WIDTH
W_QG = W_KV + 2 * KV_WIDTH + RANK_PAD
W_VG = W_QG + 2 * GLA_QK_WIDTH
W_RG = W_VG + GLA_WIDTH

BLK = 128
N_LEVELS = 7
NEG = -1e30
ATTN_SCALE = HEAD_DIM ** -0.5
SAMPLE_BLK = 8
VMEM_LIMIT = 56 * 1024 * 1024


def _t5_bucket_np(dist):
    n = np.maximum(dist, 0)
    max_exact = N_BUCKETS // 2
    nf = np.maximum(n, 1).astype(np.float64)
    large = max_exact + (np.log(nf / max_exact) / math.log(MAX_DISTANCE / max_exact)
                         * (N_BUCKETS - max_exact)).astype(np.int32)
    large = np.minimum(large, N_BUCKETS - 1)
    return np.where(n < max_exact, n, large).astype(np.int32)


def _prompt_bucket_tables():
    i = np.arange(BLK)[:, None]
    j = np.arange(2 * BLK)[None, :]
    dist = BLK + i - j
    band = (dist >= 0) & (dist < WINDOW)
    bucket = _t5_bucket_np(dist)
    t0 = np.where(band, bucket, -1)
    t1 = np.where(band & (j >= BLK), bucket, -1)
    return np.stack([t0, t1]).astype(np.int32)


def _level_tables():
    t = np.arange(BLK)[:, None]
    s = np.arange(BLK)[None, :]
    x = t ^ s
    lev = np.where(x > 0, np.floor(np.log2(np.maximum(x, 1))).astype(np.int32) + 1, 0)
    lev = np.where(s > t, -1, lev).astype(np.int32)
    tri = (s <= t).astype(np.float32)
    return lev, np.concatenate([tri, tri, tri], axis=1)


_BUCKET_PROMPT = _prompt_bucket_tables()
_LEV, _TRI3 = _level_tables()
_BUCKET_SAMPLE = _t5_bucket_np((WINDOW - 1) - np.arange(WINDOW))[None, :].astype(np.int32)


def _nt_dot(a, b):
    return lax.dot_general(a, b, (((1,), (1,)), ((), ())), preferred_element_type=F32)


def _tn_dot(a, b):
    return lax.dot_general(a, b, (((0,), (0,)), ((), ())), preferred_element_type=F32)


def _head_mean_sq(x):
    lo = lax.broadcasted_iota(jnp.int32, (x.shape[0], LANES), 1) < HEAD_DIM
    outs = []
    for c in range(x.shape[1] // LANES):
        y = x[:, c * LANES:(c + 1) * LANES]
        y = y * y
        s_lo = jnp.sum(jnp.where(lo, y, 0.0), axis=-1, keepdims=True)
        s_hi = jnp.sum(jnp.where(lo, 0.0, y), axis=-1, keepdims=True)
        outs.append(jnp.where(lo, s_lo, s_hi) * (1.0 / HEAD_DIM))
    return outs[0] if len(outs) == 1 else jnp.concatenate(outs, axis=1)


def _sigmoid(x):
    return 1.0 / (1.0 + jnp.exp(-x))


def _split3_rows(x):
    hi = x.astype(BF16)
    r1 = x - hi.astype(F32)
    mid = r1.astype(BF16)
    lo = (r1 - mid.astype(F32)).astype(BF16)
    return jnp.concatenate([hi, mid, lo], axis=0)


def _proj_kernel(x_ref, xs_ref, g_ref, w_ref, qn_ref, kn_ref, w2_ref, b2_ref, tri_ref, *rest):
    n_cast = (len(rest) - 3) // 2
    cast_in, (out_ref, kt_ref, outs_ref), cast_out = rest[:n_cast], rest[n_cast:n_cast + 3], rest[n_cast + 3:]
    for src, dst in zip(cast_in, cast_out):
        dst[...] = src[...].astype(BF16)
    weights = (g_ref, w_ref, qn_ref, kn_ref, w2_ref, b2_ref, tri_ref)
    _proj_rows(x_ref, *weights, out_ref, kt_ref, block_cumsum=True)

    @pl.when(pl.program_id(0) == pl.num_programs(0) - 1)
    def _samples():
        _proj_rows(xs_ref, *weights, outs_ref, None, block_cumsum=False)


def _proj_rows(x_ref, g_ref, w_ref, qn_ref, kn_ref, w2_ref, b2_ref, tri_ref, out_ref, kt_ref, *,
               block_cumsum):
    x = x_ref[...]
    r = lax.rsqrt(jnp.mean(x * x, axis=-1, keepdims=True) + EPS)
    n = (x * g_ref[...]).astype(BF16)

    def seg(c0, c1):
        return jnp.dot(n, w_ref[:, c0:c1], preferred_element_type=F32) * r

    kvl = seg(W_KV, W_QG)
    lr = kvl[:, 2 * KV_WIDTH:].astype(BF16)
    z = jnp.dot(lr, w2_ref[...], preferred_element_type=F32) + b2_ref[...]
    log_a = (jnp.minimum(z, 0.0) - jnp.log1p(jnp.exp(-jnp.abs(z)))) / GLA_TAU
    if block_cumsum:
        for blk in range(x.shape[0] // BLK):
            rows = slice(blk * BLK, (blk + 1) * BLK)
            out_ref[rows, C_LA:P_WIDTH] = jnp.dot(
                tri_ref[...], _split3_rows(log_a[rows]), preferred_element_type=F32)
    else:
        out_ref[:, C_LA:P_WIDTH] = log_a
    k = kvl[:, :KV_WIDTH]
    k = k * lax.rsqrt(_head_mean_sq(k) + EPS) * kn_ref[...]
    out_ref[:, C_KA:C_VA] = k
    if kt_ref is not None:
        kt_ref[...] = k.T.astype(BF16)
    out_ref[:, C_VA:C_QG] = kvl[:, KV_WIDTH:2 * KV_WIDTH]
    q = seg(W_QA, W_KV)
    out_ref[:, C_QA:C_KA] = q * lax.rsqrt(_head_mean_sq(q) + EPS) * qn_ref[...] * ATTN_SCALE
    qk_g = seg(W_QG, W_VG)
    out_ref[:, C_QG:C_KG] = qk_g[:, :GLA_QK_WIDTH] * (GLA_DK ** -0.5)
    out_ref[:, C_KG:C_VG] = qk_g[:, GLA_QK_WIDTH:]
    out_ref[:, C_VG:C_RG] = seg(W_VG, W_RG)
    out_ref[:, C_RG:C_LA] = seg(W_RG, IN_PAD_WIDTH)


def _const_spec(shape):
    nd = len(shape)
    return pl.BlockSpec(shape, lambda i: (0,) * nd, pipeline_mode=pl.Buffered(1))


def _project(x, xs, g_attn, w_in_p, qn, kn, w2p, b2, rows, to_cast):
    t = x.shape[0]
    ns = xs.shape[0]
    steps = t // rows
    cast_specs = []
    for w in to_cast:
        slab = w.shape[0] // steps
        assert w.shape[0] % steps == 0 and slab % (2 * SUBLANES) == 0
        cast_specs.append(pl.BlockSpec((slab, w.shape[1]), lambda i: (i, 0)))
    return pl.pallas_call(
        _proj_kernel,
        grid=(steps,),
        in_specs=[
            pl.BlockSpec((rows, D_MODEL), lambda i: (i, 0)),
            _const_spec((ns, D_MODEL)),
            _const_spec((1, D_MODEL)),
            _const_spec((D_MODEL, IN_PAD_WIDTH)),
            _const_spec((1, ATTN_WIDTH)),
            _const_spec((1, KV_WIDTH)),
            _const_spec((RANK_PAD, GLA_QK_WIDTH)),
            _const_spec((1, GLA_QK_WIDTH)),
            _const_spec((BLK, 3 * BLK)),
        ] + cast_specs,
        out_specs=[pl.BlockSpec((rows, P_WIDTH), lambda i: (i, 0)),
                   pl.BlockSpec((KV_WIDTH, rows), lambda i: (0, i)),
                   pl.BlockSpec((ns, P_WIDTH), lambda i: (0, 0))] + cast_specs,
        out_shape=[jax.ShapeDtypeStruct((t, P_WIDTH), F32),
                   jax.ShapeDtypeStruct((KV_WIDTH, t), BF16),
                   jax.ShapeDtypeStruct((ns, P_WIDTH), F32)]
        + [jax.ShapeDtypeStruct(w.shape, BF16) for w in to_cast],
        compiler_params=pltpu.CompilerParams(
            dimension_semantics=("arbitrary",), vmem_limit_bytes=VMEM_LIMIT),
        name="proj",
    )(x, xs, g_attn, w_in_p, qn, kn, w2p, b2, jnp.asarray(_TRI3, BF16), *to_cast)


def _boundary_rows(b_ref, r0, c0, b, row, level):
    m = 1 << (level - 1)
    if 2 * m >= SUBLANES:
        pieces = [jnp.broadcast_to(b_ref[r0 + g * 2 * m + m - 1:r0 + g * 2 * m + m, c0:c0 + LANES],
                                   (2 * m, LANES))
                  for g in range(BLK // (2 * m))]
        return pieces[0] if len(pieces) == 1 else jnp.concatenate(pieces, axis=0)
    pos = row & (2 * m - 1)
    tiles = b.reshape(BLK // SUBLANES, SUBLANES, LANES)
    out = b
    for p in range(2 * m):
        shift = (m - 1) - p
        if shift != 0:
            rolled = pltpu.roll(tiles, (-shift) % SUBLANES, 1).reshape(BLK, LANES)
            out = jnp.where(pos == p, rolled, out)
    return out


def _mixer_init(relb_ref, bucket_ref, kbd, vbd, kprev_t, vprev, st_scr, mb_scr):
    kbd[...] = jnp.zeros_like(kbd)
    vbd[...] = jnp.zeros_like(vbd)
    for slot in range(vbd.shape[0]):
        for g in range(ATTN_KV_HEADS):
            for half in range(2):
                vbd[slot, g, 2 * half * BLK:(2 * half + 2) * BLK,
                    KV_WIDTH + half * HEAD_DIM:KV_WIDTH + (half + 1) * HEAD_DIM] = (
                        jnp.ones((2 * BLK, HEAD_DIM), BF16))
    kprev_t[...] = jnp.zeros_like(kprev_t)
    vprev[...] = jnp.zeros_like(vprev)
    st_scr[...] = jnp.zeros_like(st_scr)
    bk = bucket_ref[0]
    acc = [jnp.zeros(bk.shape, F32) for _ in range(ATTN_HEADS)]
    for b in range(N_BUCKETS):
        hit = bk == b
        for h in range(ATTN_HEADS):
            acc[h] = jnp.where(hit, relb_ref[b * ATTN_HEADS + h], acc[h])
    for tb in range(2):
        masked = bucket_ref[tb] < 0
        for h in range(ATTN_HEADS):
            mb_scr[tb, h // 2, :, (h % 2) * 2 * BLK:(h % 2 + 1) * 2 * BLK] = (
                jnp.where(masked, NEG, acc[h]))


def _mixer_attention(p_ref, kt_ref, r0, table, sink_ref, omix_ref, kbd, vbd, k_prev_t, v_prev,
                     mb_scr):
    rows = slice(r0, r0 + BLK)
    lo1 = lax.broadcasted_iota(jnp.int32, (BLK, LANES), 1) < HEAD_DIM
    k_t = kt_ref[:, r0:r0 + BLK]
    for g in range(ATTN_KV_HEADS):
        hd = slice(g * HEAD_DIM, (g + 1) * HEAD_DIM)
        kbd[g, 0:HEAD_DIM, 0:BLK] = k_prev_t[hd]
        kbd[g, 0:HEAD_DIM, BLK:2 * BLK] = k_t[hd]
        kbd[g, HEAD_DIM:2 * HEAD_DIM, 2 * BLK:3 * BLK] = k_prev_t[hd]
        kbd[g, HEAD_DIM:2 * HEAD_DIM, 3 * BLK:4 * BLK] = k_t[hd]
    v_cur = p_ref[rows,C_VA:C_QG]
    v_swap = pltpu.roll(v_cur, HEAD_DIM, 1)
    v_parts = (jnp.where(lo1, v_cur, 0.0).astype(BF16), jnp.where(lo1, 0.0, v_swap).astype(BF16),
               jnp.where(lo1, v_swap, 0.0).astype(BF16), jnp.where(lo1, 0.0, v_cur).astype(BF16))
    for n, part in enumerate(v_parts):
        g, half = divmod(n, 2)
        vbd[g, 2 * half * BLK:(2 * half + 1) * BLK, 0:KV_WIDTH] = v_prev[n]
        vbd[g, (2 * half + 1) * BLK:(2 * half + 2) * BLK, 0:KV_WIDTH] = part
    chunks_per_kv = ATTN_HEADS // ATTN_KV_HEADS // 2
    for g in range(ATTN_KV_HEADS):
        c_first = g * chunks_per_kv
        qs = jnp.concatenate(
            [p_ref[rows,C_QA + (c_first + c) * LANES:C_QA + (c_first + c + 1) * LANES].astype(BF16)
             for c in range(chunks_per_kv)], axis=0)
        mb = jnp.concatenate([mb_scr[table, c_first + c] for c in range(chunks_per_kv)], axis=0)
        s = jnp.dot(qs, kbd[g], preferred_element_type=F32) + mb
        prob_rows, sink_rows = [], []
        for c in range(chunks_per_kv):
            probs, sinks = [], []
            for e in range(2):
                se = s[c * BLK:(c + 1) * BLK, e * 2 * BLK:(e + 1) * 2 * BLK]
                sk = sink_ref[2 * (c_first + c) + e]
                m = jnp.maximum(jnp.max(se, axis=-1, keepdims=True), sk)
                probs.append(jnp.exp(se - m).astype(BF16))
                sinks.append(jnp.exp(sk - m))
            prob_rows.append(jnp.concatenate(probs, axis=1))
            sink_rows.append(jnp.where(lo1, sinks[0], sinks[1]))
        o_den = jnp.dot(jnp.concatenate(prob_rows, axis=0), vbd[g], preferred_element_type=F32)
        o = o_den[:, :KV_WIDTH] / (o_den[:, KV_WIDTH:] + jnp.concatenate(sink_rows, axis=0))
        for c in range(chunks_per_kv):
            omix_ref[rows,(c_first + c) * LANES:(c_first + c + 1) * LANES] = (
                o[c * BLK:(c + 1) * BLK].astype(omix_ref.dtype))
    return k_t, v_parts


def _mixer_gla(state, p_ref, r0, lev_ref, gn_ref, omix_ref):
    new_state = []
    pairs = range(len(state))
    rows = slice(r0, r0 + BLK)
    lo1 = lax.broadcasted_iota(jnp.int32, (BLK, LANES), 1) < HEAD_DIM
    lo_bf = jnp.where(lo1, 1.0, 0.0).astype(BF16)
    hi_bf = jnp.where(lo1, 0.0, 1.0).astype(BF16)
    row = lax.broadcasted_iota(jnp.int32, (BLK, LANES), 0)
    zero_blk = jnp.zeros((BLK, LANES), BF16)
    for c in pairs:
        c0 = c * LANES
        q_at = lambda a, z: p_ref[r0 + a:r0 + z, C_QG + c0:C_QG + c0 + LANES]
        k_at = lambda a, z: p_ref[r0 + a:r0 + z, C_KG + c0:C_KG + c0 + LANES]
        b_at = lambda a, z: p_ref[r0 + a:r0 + z, C_LA + c0:C_LA + c0 + LANES]

        def pair_scores(qtb, ktb):
            rhs = jnp.concatenate([ktb * lo_bf, ktb * hi_bf], axis=0)
            return _nt_dot(qtb, rhs)

        s0 = pair_scores(q_at(0, BLK).astype(BF16), k_at(0, BLK).astype(BF16))
        sc = [jnp.where(lev_ref[...] == 0, s0[:, e * BLK:(e + 1) * BLK], 0.0) for e in range(2)]
        for level in range(1, N_LEVELS + 1):
            m = 1 << (level - 1)
            if m >= SUBLANES:
                qs, ks = [], []
                zeros = jnp.zeros((m, LANES), BF16)
                for g in range(BLK // (2 * m)):
                    lo_a, up_a, up_z = g * 2 * m, g * 2 * m + m, (g + 1) * 2 * m
                    rb = jnp.broadcast_to(b_at(up_a - 1, up_a), (m, LANES))
                    qs += [zeros, (q_at(up_a, up_z) * jnp.exp(b_at(up_a, up_z) - rb)).astype(BF16)]
                    ks += [(k_at(lo_a, up_a) * jnp.exp(rb - b_at(lo_a, up_a))).astype(BF16), zeros]
                qtb = jnp.concatenate(qs, axis=0)
                ktb = jnp.concatenate(ks, axis=0)
            else:
                bc = b_at(0, BLK)
                d = bc - _boundary_rows(p_ref, r0, C_LA + c0, bc, row, level)
                upper = ((row >> (level - 1)) & 1) == 1
                qtb = (q_at(0, BLK) * jnp.exp(jnp.where(upper, d, NEG))).astype(BF16)
                ktb = (k_at(0, BLK) * jnp.exp(jnp.where(upper, NEG, -d))).astype(BF16)
            sl = pair_scores(qtb, ktb)
            sc = [jnp.where(lev_ref[...] == level, sl[:, e * BLK:(e + 1) * BLK], sc[e])
                  for e in range(2)]
        sc = jnp.concatenate(sc, axis=1)
        qc, kc, bc = q_at(0, BLK), k_at(0, BLK), b_at(0, BLK)

        b_last = bc[BLK - 1:BLK, :]
        v0 = p_ref[rows,C_VG + 2 * c0:C_VG + 2 * c0 + LANES].astype(BF16)
        v1 = p_ref[rows,C_VG + 2 * c0 + LANES:C_VG + 2 * c0 + 2 * LANES].astype(BF16)
        v_bd = jnp.concatenate([jnp.concatenate([v0, zero_blk], axis=1),
                                jnp.concatenate([zero_blk, v1], axis=1)], axis=0)
        st_c = state[c]
        stb = st_c.astype(BF16)
        st_rhs = jnp.concatenate([stb * lo_bf, stb * hi_bf], axis=0)
        o = (jnp.dot(sc.astype(BF16), v_bd, preferred_element_type=F32)
             + _nt_dot((qc * jnp.exp(bc)).astype(BF16), st_rhs))
        kd = (kc * jnp.exp(b_last - bc)).astype(BF16)
        upd = _tn_dot(jnp.concatenate([v0, v1], axis=1), kd)
        new_state.append(st_c * jnp.exp(b_last) + jnp.where(lo1, upd[:BLK], upd[BLK:]))
        for e in range(2):
            h = 2 * c + e
            oh = o[:, e * LANES:(e + 1) * LANES]
            og = oh * lax.rsqrt(jnp.mean(oh * oh, axis=-1, keepdims=True) + EPS) * gn_ref[...]
            rg = p_ref[rows,C_RG + h * GLA_DV:C_RG + (h + 1) * GLA_DV]
            gated = og * (rg * _sigmoid(rg))
            omix_ref[rows,ATTN_WIDTH + h * GLA_DV:ATTN_WIDTH + (h + 1) * GLA_DV] = (
                gated.astype(omix_ref.dtype))
    return new_state


def _prompt_mixer_kernel(relb_ref, sink_ref, p_ref, kt_ref, bucket_ref, lev_ref, gn_ref,
                         omix_ref, st_ref, kbd, vbd, kprev_t, vprev, st_scr, mb_scr):
    i = pl.program_id(0)

    @pl.when(i == 0)
    def _init():
        _mixer_init(relb_ref, bucket_ref, kbd, vbd, kprev_t, vprev, st_scr, mb_scr)

    k_prev_t = kprev_t[...]
    v_prev = [vprev[n] for n in range(2 * ATTN_KV_HEADS)]
    state = [st_scr[:, c * LANES:(c + 1) * LANES] for c in range(GLA_HEADS // 2)]
    n_blocks = p_ref.shape[0] // BLK
    for jb in range(n_blocks):
        table = jnp.where(i == 0, 1, 0) if jb == 0 else 0
        k_prev_t, v_prev = _mixer_attention(p_ref, kt_ref, jb * BLK, table, sink_ref, omix_ref,
                                            kbd.at[jb], vbd.at[jb], k_prev_t, v_prev, mb_scr)
        state = _mixer_gla(state, p_ref, jb * BLK, lev_ref, gn_ref, omix_ref)
    kprev_t[...] = k_prev_t
    for n, part in enumerate(v_prev):
        vprev[n] = part
    for c, st_c in enumerate(state):
        st_scr[:, c * LANES:(c + 1) * LANES] = st_c
        st_ref[:, c * LANES:(c + 1) * LANES] = st_c


def _prompt_mixer(p, kt, relb, sinks, gn):
    t = p.shape[0]
    smem = pl.BlockSpec(memory_space=pltpu.SMEM)
    return pl.pallas_call(
        _prompt_mixer_kernel,
        grid=(t // MIX_ROWS,),
        in_specs=[
            smem, smem,
            pl.BlockSpec((MIX_ROWS, P_WIDTH), lambda i: (i, 0)),
            pl.BlockSpec((KV_WIDTH, MIX_ROWS), lambda i: (0, i)),
            _const_spec((2, BLK, 2 * BLK)),
            _const_spec((BLK, BLK)),
            _const_spec((1, GLA_DV)),
        ],
        out_specs=[
            pl.BlockSpec((MIX_ROWS, MIX_WIDTH), lambda i: (i, 0)),
            pl.BlockSpec((GLA_DV, GLA_QK_WIDTH), lambda i: (0, 0)),
        ],
        out_shape=[
            jax.ShapeDtypeStruct((t, MIX_WIDTH), BF16),
            jax.ShapeDtypeStruct((GLA_DV, GLA_QK_WIDTH), F32),
        ],
        scratch_shapes=[
            pltpu.VMEM((MIX_ROWS // BLK, ATTN_KV_HEADS, 2 * HEAD_DIM, 4 * BLK), BF16),
            pltpu.VMEM((MIX_ROWS // BLK, ATTN_KV_HEADS, 4 * BLK, 2 * KV_WIDTH), BF16),
            pltpu.VMEM((KV_WIDTH, BLK), BF16),
            pltpu.VMEM((2 * ATTN_KV_HEADS, BLK, KV_WIDTH), BF16),
            pltpu.VMEM((GLA_DV, GLA_QK_WIDTH), F32),
            pltpu.VMEM((2, ATTN_HEADS // 2, BLK, 4 * BLK), F32),
        ],
        compiler_params=pltpu.CompilerParams(
            dimension_semantics=("arbitrary",), vmem_limit_bytes=VMEM_LIMIT),
        name="prompt_mixer",
    )(relb, sinks, p, kt, jnp.asarray(_BUCKET_PROMPT), jnp.asarray(_LEV), gn)


def _sample_mixer_kernel(ps_ref, pfull_ref, ck_ref, cv_ref, st_ref, relbt_ref, sink_ref, bucket_ref,
                         gn_ref, omix_ref, kwin_ref, vwin_ref, stout_ref, lat_scr, kqt_scr,
                         bias_scr, s_scr, o_scr, og_scr):
    i = pl.program_id(0)
    nb = pfull_ref.shape[0]

    @pl.when(i == 0)
    def _init():
        lat_scr[...] = _split3_rows(pfull_ref[:, C_LA:P_WIDTH].T)
        kqt_scr[0:GLA_QK_WIDTH] = pfull_ref[:, C_KG:C_VG].T.astype(BF16)
        kqt_scr[GLA_QK_WIDTH:2 * GLA_QK_WIDTH] = pfull_ref[:, C_QG:C_KG].T.astype(BF16)
        bk = jnp.broadcast_to(bucket_ref[...], (ATTN_HEADS, WINDOW))
        acc = jnp.zeros((ATTN_HEADS, WINDOW), F32)
        for b in range(N_BUCKETS):
            acc = jnp.where(bk == b, relbt_ref[:, b:b + 1], acc)
        bias_scr[...] = acc

    lo = lax.broadcasted_iota(jnp.int32, (1, LANES), 1) < HEAD_DIM
    sub = lax.broadcasted_iota(jnp.int32, (ATTN_HEADS, LANES), 0)
    wrow = lax.broadcasted_iota(jnp.int32, (WINDOW, KV_WIDTH), 0)
    heads_per_kv = ATTN_HEADS // ATTN_KV_HEADS

    n_of_col = i * SAMPLE_BLK + lax.broadcasted_iota(jnp.int32, (nb, SAMPLE_BLK * LANES), 1) // LANES
    pick = jnp.where(lax.broadcasted_iota(jnp.int32, (nb, SAMPLE_BLK * LANES), 0) == n_of_col,
                     1.0, 0.0).astype(BF16)
    la_b = (jnp.dot(lat_scr[0:GLA_QK_WIDTH], pick, preferred_element_type=F32)
            + jnp.dot(lat_scr[GLA_QK_WIDTH:2 * GLA_QK_WIDTH], pick, preferred_element_type=F32)
            + jnp.dot(lat_scr[2 * GLA_QK_WIDTH:3 * GLA_QK_WIDTH], pick, preferred_element_type=F32))
    kq_b = jnp.dot(kqt_scr[...], pick, preferred_element_type=F32)

    for j in range(SAMPLE_BLK):
        k_new = ps_ref[j:j + 1, C_KA:C_VA]
        v_new = ps_ref[j:j + 1, C_VA:C_QG]
        kwin_ref[j] = jnp.where(wrow == WINDOW - 1, k_new, pltpu.roll(ck_ref[j], WINDOW - 1, 0))
        vwin_ref[j] = jnp.where(wrow == WINDOW - 1, v_new, pltpu.roll(cv_ref[j], WINDOW - 1, 0))

    for j in range(SAMPLE_BLK):
        qexp = jnp.zeros((ATTN_HEADS, LANES), F32)
        for c in range(ATTN_HEADS // 2):
            chunk = ps_ref[j:j + 1, C_QA + c * LANES:C_QA + (c + 1) * LANES]
            swapped = pltpu.roll(chunk, HEAD_DIM, 1)
            if (2 * c) // heads_per_kv == 0:
                rows = (jnp.where(lo, chunk, 0.0), jnp.where(lo, swapped, 0.0))
            else:
                rows = (jnp.where(lo, 0.0, swapped), jnp.where(lo, 0.0, chunk))
            for e in range(2):
                qexp = jnp.where(sub == 2 * c + e, rows[e], qexp)
        s_scr[j * ATTN_HEADS:(j + 1) * ATTN_HEADS] = _nt_dot(qexp.astype(BF16), kwin_ref[j].astype(BF16))

    tile = lambda x: jnp.concatenate([x] * SAMPLE_BLK, axis=0)
    sink = tile(sink_ref[...])
    s = s_scr[...] + tile(bias_scr[...])
    m = jnp.maximum(jnp.max(s, axis=-1, keepdims=True), sink)
    pe = jnp.exp(s - m)
    inv_den = 1.0 / (jnp.sum(pe, axis=-1, keepdims=True) + jnp.exp(sink - m))
    peb = pe.astype(BF16)
    for j in range(SAMPLE_BLK):
        o_scr[j * ATTN_HEADS:(j + 1) * ATTN_HEADS] = jnp.dot(
            peb[j * ATTN_HEADS:(j + 1) * ATTN_HEADS], vwin_ref[j].astype(BF16),
            preferred_element_type=F32)
    o_all = o_scr[...] * inv_den
    o_swap = pltpu.roll(o_all, HEAD_DIM, 1)
    for j in range(SAMPLE_BLK):
        r = j * ATTN_HEADS
        for c in range(ATTN_HEADS // 2):
            if (2 * c) // heads_per_kv == 0:
                piece = jnp.where(lo, o_all[r + 2 * c:r + 2 * c + 1, :], o_swap[r + 2 * c + 1:r + 2 * c + 2, :])
            else:
                piece = jnp.where(lo, o_swap[r + 2 * c:r + 2 * c + 1, :], o_all[r + 2 * c + 1:r + 2 * c + 2, :])
            omix_ref[j:j + 1, c * LANES:(c + 1) * LANES] = piece

    for j in range(SAMPLE_BLK):
        cols = slice(j * LANES, (j + 1) * LANES)
        for h in range(GLA_HEADS):
            rs = slice(h * GLA_DK, (h + 1) * GLA_DK)
            qs = slice(GLA_QK_WIDTH + h * GLA_DK, GLA_QK_WIDTH + (h + 1) * GLA_DK)
            v_row = ps_ref[j:j + 1, C_VG + h * GLA_DV:C_VG + (h + 1) * GLA_DV]
            s_new = jnp.exp(la_b[rs, cols]) * st_ref[j, h] + kq_b[rs, cols] * v_row
            stout_ref[j, h] = s_new
            og_scr[j:j + 1, h * GLA_DV:(h + 1) * GLA_DV] = jnp.sum(
                kq_b[qs, cols] * s_new, axis=0, keepdims=True)
    for h in range(GLA_HEADS):
        hs = slice(h * GLA_DV, (h + 1) * GLA_DV)
        og = og_scr[:, hs]
        og = og * lax.rsqrt(jnp.mean(og * og, axis=-1, keepdims=True) + EPS) * gn_ref[...]
        rg = ps_ref[:, C_RG + h * GLA_DV:C_RG + (h + 1) * GLA_DV]
        omix_ref[:, ATTN_WIDTH + h * GLA_DV:ATTN_WIDTH + (h + 1) * GLA_DV] = og * (rg * _sigmoid(rg))


def _sample_mixer(ps, cache_k, cache_v, state, relbt, sinks_col, gn):
    nb = ps.shape[0]
    blk3 = lambda i: (i, 0, 0)
    blk4 = lambda i: (i, 0, 0, 0)
    return pl.pallas_call(
        _sample_mixer_kernel,
        grid=(nb // SAMPLE_BLK,),
        in_specs=[
            pl.BlockSpec((SAMPLE_BLK, P_WIDTH), lambda i: (i, 0)),
            _const_spec((nb, P_WIDTH)),
            pl.BlockSpec((SAMPLE_BLK, WINDOW, KV_WIDTH), blk3),
            pl.BlockSpec((SAMPLE_BLK, WINDOW, KV_WIDTH), blk3),
            pl.BlockSpec((SAMPLE_BLK, GLA_HEADS, GLA_DK, GLA_DV), blk4),
            _const_spec((ATTN_HEADS, N_BUCKETS)),
            _const_spec((ATTN_HEADS, 1)),
            _const_spec((1, WINDOW)),
            _const_spec((1, GLA_DV)),
        ],
        out_specs=[
            pl.BlockSpec((SAMPLE_BLK, MIX_WIDTH), lambda i: (i, 0)),
            pl.BlockSpec((SAMPLE_BLK, WINDOW, KV_WIDTH), blk3),
            pl.BlockSpec((SAMPLE_BLK, WINDOW, KV_WIDTH), blk3),
            pl.BlockSpec((SAMPLE_BLK, GLA_HEADS, GLA_DK, GLA_DV), blk4),
        ],
        out_shape=[
            jax.ShapeDtypeStruct((nb, MIX_WIDTH), F32),
            jax.ShapeDtypeStruct((nb, WINDOW, KV_WIDTH), F32),
            jax.ShapeDtypeStruct((nb, WINDOW, KV_WIDTH), F32),
            jax.ShapeDtypeStruct((nb, GLA_HEADS, GLA_DK, GLA_DV), F32),
        ],
        scratch_shapes=[
            pltpu.VMEM((3 * GLA_QK_WIDTH, nb), BF16),
            pltpu.VMEM((2 * GLA_QK_WIDTH, nb), BF16),
            pltpu.VMEM((ATTN_HEADS, WINDOW), F32),
            pltpu.VMEM((SAMPLE_BLK * ATTN_HEADS, WINDOW), F32),
            pltpu.VMEM((SAMPLE_BLK * ATTN_HEADS, KV_WIDTH), F32),
            pltpu.VMEM((SAMPLE_BLK, GLA_WIDTH), F32),
        ],
        compiler_params=pltpu.CompilerParams(
            dimension_semantics=("arbitrary",), vmem_limit_bytes=VMEM_LIMIT),
        name="sample_mixer",
    )(ps, ps, cache_k, cache_v, state, relbt, sinks_col, jnp.asarray(_BUCKET_SAMPLE), gn)


def _finish_kernel(x_ref, mix_ref, xs_ref, mixs_ref, wo_ref, g_ref, wg_ref, wu_ref, wd_ref,
                   y_ref, ys_ref, *, ff_chunks):
    weights = (wo_ref, g_ref, wg_ref, wu_ref, wd_ref)
    _finish_rows(x_ref, mix_ref, *weights, y_ref, ff_chunks=ff_chunks)

    @pl.when(pl.program_id(0) == pl.num_programs(0) - 1)
    def _samples():
        _finish_rows(xs_ref, mixs_ref, *weights, ys_ref, ff_chunks=ff_chunks)


def _finish_rows(x_ref, mix_ref, wo_ref, g_ref, wg_ref, wu_ref, wd_ref, y_ref, *, ff_chunks):
    h = x_ref[...] + jnp.dot(mix_ref[...].astype(BF16), wo_ref[...], preferred_element_type=F32)
    r = lax.rsqrt(jnp.mean(h * h, axis=-1, keepdims=True) + EPS)
    z = (h * g_ref[...]).astype(BF16)
    n_tiles = wd_ref.shape[0] // MXU_TILE
    acc = h
    for c in range(ff_chunks):
        c0 = ((c * n_tiles) // ff_chunks) * MXU_TILE
        c1 = (((c + 1) * n_tiles) // ff_chunks) * MXU_TILE
        gate = jnp.dot(z, wg_ref[:, c0:c1], preferred_element_type=F32) * r
        up = jnp.dot(z, wu_ref[:, c0:c1], preferred_element_type=F32) * r
        act = ((gate * _sigmoid(gate)) * up).astype(BF16)
        acc = acc + jnp.dot(act, wd_ref[c0:c1, :], preferred_element_type=F32)
    y_ref[...] = acc


def _finish(x, mix, xs, mixs, wo, g_ffn, wg, wu, wd, rows):
    t = x.shape[0]
    ns = xs.shape[0]
    d_ff = wd.shape[0]
    assert d_ff % MXU_TILE == 0
    return pl.pallas_call(
        functools.partial(_finish_kernel, ff_chunks=FF_CHUNKS),
        grid=(t // rows,),
        in_specs=[
            pl.BlockSpec((rows, D_MODEL), lambda i: (i, 0)),
            pl.BlockSpec((rows, MIX_WIDTH), lambda i: (i, 0)),
            _const_spec((ns, D_MODEL)),
            _const_spec((ns, MIX_WIDTH)),
            _const_spec((MIX_WIDTH, D_MODEL)),
            _const_spec((1, D_MODEL)),
            _const_spec((D_MODEL, d_ff)),
            _const_spec((D_MODEL, d_ff)),
            _const_spec((d_ff, D_MODEL)),
        ],
        out_specs=[pl.BlockSpec((rows, D_MODEL), lambda i: (i, 0)),
                   pl.BlockSpec((ns, D_MODEL), lambda i: (0, 0))],
        out_shape=[jax.ShapeDtypeStruct((t, D_MODEL), F32),
                   jax.ShapeDtypeStruct((ns, D_MODEL), F32)],
        compiler_params=pltpu.CompilerParams(
            dimension_semantics=("arbitrary",), vmem_limit_bytes=VMEM_LIMIT),
        name="finish",
    )(x, mix, xs, mixs, wo, g_ffn, wg, wu, wd)


PROMPT_ROWS = 1024
PROJ_ROWS = 1024
MIX_ROWS = 512


def kernel(x_prompt, x_sample, cache_k, cache_v, state_gla, attn_norm_g, w_in, q_norm_g, k_norm_g,
           attn_sinks, rel_bias, w_gla_gate2, b_gla_gate, gla_norm_g, w_o, ffn_norm_g, w_gate, w_up,
           w_down):
    depth = w_in.shape[0]
    batch, seq, _ = x_prompt.shape
    dec_batch, dec_seq, _ = x_sample.shape
    wb = cache_k.shape[2]
    assert batch == 1 and dec_seq == 1 and wb == WINDOW
    assert seq % PROMPT_ROWS == 0 and seq % PROJ_ROWS == 0
    assert dec_batch % SAMPLE_BLK == 0 and dec_batch % LANES == 0
    assert rel_bias.shape == (N_BUCKETS, ATTN_HEADS)

    xp = x_prompt.reshape(seq, D_MODEL)
    xs = x_sample.reshape(dec_batch, D_MODEL)
    relb_flat = rel_bias.reshape(-1)
    relb_t = rel_bias.T
    outs = ([], [], [], [], [], [])
    for l in range(depth):
        kv_end = ATTN_WIDTH + 2 * KV_WIDTH
        w_in_p = jnp.concatenate(
            [w_in[l, :, :kv_end],
             jnp.pad(w_in[l, :, MAIN_WIDTH:], ((0, 0), (0, RANK_PAD - GLA_RANK))),
             w_in[l, :, kv_end:MAIN_WIDTH]], axis=1).astype(BF16)
        w2p = jnp.pad(w_gla_gate2[l], ((0, RANK_PAD - GLA_RANK), (0, 0))).astype(BF16)
        proj_w = (attn_norm_g[l][None, :], w_in_p, jnp.tile(q_norm_g[l], ATTN_HEADS)[None, :],
                  jnp.tile(k_norm_g[l], ATTN_KV_HEADS)[None, :], w2p, b_gla_gate[l][None, :])
        gn = gla_norm_g[l][None, :]

        pp, kt_p, ps, wo_b, wg_b, wu_b, wd_b = _project(
            xp, xs, *proj_w, rows=PROJ_ROWS, to_cast=(w_o[l], w_gate[l], w_up[l], w_down[l]))
        fin_w = (wo_b, ffn_norm_g[l][None, :], wg_b, wu_b, wd_b)
        mix_p, st_p = _prompt_mixer(pp, kt_p, relb_flat, attn_sinks[l], gn)
        mix_s, kwin, vwin, st_s = _sample_mixer(
            ps, cache_k[l].reshape(dec_batch, wb, KV_WIDTH), cache_v[l].reshape(dec_batch, wb, KV_WIDTH),
            state_gla[l].astype(F32), relb_t, attn_sinks[l][:, None], gn)
        xp_in = xp
        xp, xs = _finish(xp_in, mix_p, xs, mix_s, *fin_w, rows=PROMPT_ROWS)
        outs[0].append(pp[seq - wb:, C_KA:C_VA].reshape(batch, wb, ATTN_KV_HEADS, HEAD_DIM))
        outs[1].append(pp[seq - wb:, C_VA:C_QG].reshape(batch, wb, ATTN_KV_HEADS, HEAD_DIM))
        outs[2].append(st_p.T.reshape(batch, GLA_HEADS, GLA_DK, GLA_DV).astype(state_gla.dtype))
        outs[3].append(kwin.reshape(dec_batch, wb, ATTN_KV_HEADS, HEAD_DIM))
        outs[4].append(vwin.reshape(dec_batch, wb, ATTN_KV_HEADS, HEAD_DIM))
        outs[5].append(st_s.astype(state_gla.dtype))

    y_prompt = xp.reshape(batch, seq, D_MODEL)
    y_sample = xs.reshape(dec_batch, dec_seq, D_MODEL)
    return (y_prompt, y_sample) + tuple(jnp.stack(o) for o in outs)
```

```python
import functools
import math

import numpy as np
import jax
import jax.numpy as jnp
from jax import lax
from jax.experimental import pallas as pl
from jax.experimental.pallas import tpu as pltpu

F32 = jnp.float32
BF16 = jnp.bfloat16

D_MODEL = 1024
HEAD_DIM = 64
ATTN_HEADS = 8
ATTN_KV_HEADS = 2
WINDOW = 128
N_BUCKETS = 32
MAX_DISTANCE = 128
GLA_HEADS = 4
GLA_DK = 64
GLA_DV = 128
GLA_RANK = 16
GLA_TAU = 16.0
EPS = 1e-6
ATTN_WIDTH = ATTN_HEADS * HEAD_DIM
KV_WIDTH = ATTN_KV_HEADS * HEAD_DIM
GLA_QK_WIDTH = GLA_HEADS * GLA_DK
GLA_WIDTH = GLA_HEADS * GLA_DV
MIX_WIDTH = ATTN_WIDTH + GLA_WIDTH
MAIN_WIDTH = ATTN_WIDTH + 2 * KV_WIDTH + 2 * GLA_QK_WIDTH + 2 * GLA_WIDTH
LANES = 128
SUBLANES = 8
MXU_TILE = 256
FF_CHUNKS = 4
RANK_PAD = LANES
IN_PAD_WIDTH = MAIN_WIDTH + RANK_PAD

F_QG = 0
F_KG = F_QG + GLA_QK_WIDTH
F_RG = F_KG + GLA_QK_WIDTH
F_LA = F_RG + GLA_WIDTH
F_WIDTH = F_LA + GLA_QK_WIDTH
H_QA = 0
H_VA = H_QA + ATTN_WIDTH
H_VS = H_VA + KV_WIDTH
H_VG = H_VS + KV_WIDTH
H_WIDTH = H_VG + GLA_WIDTH

W_QA = 0
W_KV = W_QA + ATTN_WIDTH
W_QG = W_KV + 2 * KV_WIDTH + RANK_PAD
W_VG = W_QG + 2 * GLA_QK_WIDTH
W_RG = W_VG + GLA_WIDTH

BLK = 128
N_LEVELS = 7
NEG = -1e30
ATTN_SCALE = HEAD_DIM ** -0.5
SAMPLE_BLK = 8
VMEM_LIMIT = 56 * 1024 * 1024


def _t5_bucket_np(dist):
    n = np.maximum(dist, 0)
    max_exact = N_BUCKETS // 2
    nf = np.maximum(n, 1).astype(np.float64)
    large = max_exact + (np.log(nf / max_exact) / math.log(MAX_DISTANCE / max_exact)
                         * (N_BUCKETS - max_exact)).astype(np.int32)
    large = np.minimum(large, N_BUCKETS - 1)
    return np.where(n < max_exact, n, large).astype(np.int32)


def _prompt_bucket_tables():
    i = np.arange(BLK)[:, None]
    j = np.arange(2 * BLK)[None, :]
    dist = BLK + i - j
    band = (dist >= 0) & (dist < WINDOW)
    bucket = _t5_bucket_np(dist)
    t0 = np.where(band, bucket, -1)
    t1 = np.where(band & (j >= BLK), bucket, -1)
    return np.stack([t0, t1]).astype(np.int32)


def _level_tables():
    t = np.arange(BLK)[:, None]
    s = np.arange(BLK)[None, :]
    x = t ^ s
    lev = np.where(x > 0, np.floor(np.log2(np.maximum(x, 1))).astype(np.int32) + 1, 0)
    lev = np.where(s > t, -1, lev).astype(np.int32)
    tri = (s <= t).astype(np.float32)
    return lev, np.concatenate([tri, tri, tri], axis=1)


_BUCKET_PROMPT = _prompt_bucket_tables()
_LEV, _TRI3 = _level_tables()
_BUCKET_SAMPLE = _t5_bucket_np((WINDOW - 1) - np.arange(WINDOW))[None, :].astype(np.int32)


def _nt_dot(a, b):
    return lax.dot_general(a, b, (((1,), (1,)), ((), ())), preferred_element_type=F32)


def _tn_dot(a, b):
    return lax.dot_general(a, b, (((0,), (0,)), ((), ())), preferred_element_type=F32)


def _head_mean_sq(x):
    lo = lax.broadcasted_iota(jnp.int32, (x.shape[0], LANES), 1) < HEAD_DIM
    outs = []
    for c in range(x.shape[1] // LANES):
        y = x[:, c * LANES:(c + 1) * LANES]
        y = y * y
        s_lo = jnp.sum(jnp.where(lo, y, 0.0), axis=-1, keepdims=True)
        s_hi = jnp.sum(jnp.where(lo, 0.0, y), axis=-1, keepdims=True)
        outs.append(jnp.where(lo, s_lo, s_hi) * (1.0 / HEAD_DIM))
    return outs[0] if len(outs) == 1 else jnp.concatenate(outs, axis=1)


def _sigmoid(x):
    return 1.0 / (1.0 + jnp.exp(-x))


def _split3_rows(x):
    hi = x.astype(BF16)
    r1 = x - hi.astype(F32)
    mid = r1.astype(BF16)
    lo = (r1 - mid.astype(F32)).astype(BF16)
    return jnp.concatenate([hi, mid, lo], axis=0)


def _proj_kernel(x_ref, xs_ref, g_ref, w_ref, qn_ref, kn_ref, w2_ref, b2_ref, tri_ref, *rest):
    n_out = 7
    n_cast = (len(rest) - n_out) // 2
    cast_in, cast_out = rest[:n_cast], rest[n_cast + n_out:]
    f32_ref, bf_ref, kt_ref, kvw_ref, f32s_ref, bfs_ref, kvs_ref = rest[n_cast:n_cast + n_out]
    for src, dst in zip(cast_in, cast_out):
        dst[...] = src[...].astype(BF16)
    weights = (g_ref, w_ref, qn_ref, kn_ref, w2_ref, b2_ref, tri_ref)
    _proj_rows(x_ref, *weights, f32_ref, bf_ref, kt_ref, kvw_ref, block_cumsum=True)

    @pl.when(pl.program_id(0) == pl.num_programs(0) - 1)
    def _samples():
        _proj_rows(xs_ref, *weights, f32s_ref, bfs_ref, None, kvs_ref, block_cumsum=False)


def _proj_rows(x_ref, g_ref, w_ref, qn_ref, kn_ref, w2_ref, b2_ref, tri_ref, f32_ref, bf_ref,
               kt_ref, kv_ref, *, block_cumsum):
    x = x_ref[...]
    r = lax.rsqrt(jnp.mean(x * x, axis=-1, keepdims=True) + EPS)
    n = (x * g_ref[...]).astype(BF16)

    def seg(c0, c1):
        return jnp.dot(n, w_ref[:, c0:c1], preferred_element_type=F32) * r

    kvl = seg(W_KV, W_QG)
    lr = kvl[:, 2 * KV_WIDTH:].astype(BF16)
    z = jnp.dot(lr, w2_ref[...], preferred_element_type=F32) + b2_ref[...]
    log_a = (jnp.minimum(z, 0.0) - jnp.log1p(jnp.exp(-jnp.abs(z)))) / GLA_TAU
    if block_cumsum:
        for blk in range(x.shape[0] // BLK):
            rows = slice(blk * BLK, (blk + 1) * BLK)
            f32_ref[rows, F_LA:F_WIDTH] = jnp.dot(
                tri_ref[...], _split3_rows(log_a[rows]), preferred_element_type=F32)
    else:
        f32_ref[:, F_LA:F_WIDTH] = log_a
    k = kvl[:, :KV_WIDTH]
    k = k * lax.rsqrt(_head_mean_sq(k) + EPS) * kn_ref[...]
    v = kvl[:, KV_WIDTH:2 * KV_WIDTH]
    n_kv = kv_ref.shape[0]
    kv_ref[:, 0:KV_WIDTH] = k[x.shape[0] - n_kv:]
    kv_ref[:, KV_WIDTH:2 * KV_WIDTH] = v[x.shape[0] - n_kv:]
    if kt_ref is not None:
        kt_ref[...] = k.T.astype(BF16)
    hdt = bf_ref.dtype
    bf_ref[:, H_VA:H_VS] = v.astype(hdt)
    bf_ref[:, H_VS:H_VG] = pltpu.roll(v, HEAD_DIM, 1).astype(hdt)
    q = seg(W_QA, W_KV)
    bf_ref[:, H_QA:H_VA] = (q * lax.rsqrt(_head_mean_sq(q) + EPS) * qn_ref[...] * ATTN_SCALE
                            ).astype(hdt)
    qk_g = seg(W_QG, W_VG)
    f32_ref[:, F_QG:F_KG] = qk_g[:, :GLA_QK_WIDTH] * (GLA_DK ** -0.5)
    f32_ref[:, F_KG:F_RG] = qk_g[:, GLA_QK_WIDTH:]
    bf_ref[:, H_VG:H_WIDTH] = seg(W_VG, W_RG).astype(hdt)
    f32_ref[:, F_RG:F_LA] = seg(W_RG, IN_PAD_WIDTH)


def _const_spec(shape):
    nd = len(shape)
    return pl.BlockSpec(shape, lambda i: (0,) * nd, pipeline_mode=pl.Buffered(1))


def _project(x, xs, g_attn, w_in_p, qn, kn, w2p, b2, rows, to_cast):
    t = x.shape[0]
    ns = xs.shape[0]
    steps = t // rows
    cast_specs = []
    for w in to_cast:
        slab = w.shape[0] // steps
        assert w.shape[0] % steps == 0 and slab % (2 * SUBLANES) == 0
        cast_specs.append(pl.BlockSpec((slab, w.shape[1]), lambda i: (i, 0)))
    return pl.pallas_call(
        _proj_kernel,
        grid=(steps,),
        in_specs=[
            pl.BlockSpec((rows, D_MODEL), lambda i: (i, 0)),
            _const_spec((ns, D_MODEL)),
            _const_spec((1, D_MODEL)),
            _const_spec((D_MODEL, IN_PAD_WIDTH)),
            _const_spec((1, ATTN_WIDTH)),
            _const_spec((1, KV_WIDTH)),
            _const_spec((RANK_PAD, GLA_QK_WIDTH)),
            _const_spec((1, GLA_QK_WIDTH)),
            _const_spec((BLK, 3 * BLK)),
        ] + cast_specs,
        out_specs=[pl.BlockSpec((rows, F_WIDTH), lambda i: (i, 0)),
                   pl.BlockSpec((rows, H_WIDTH), lambda i: (i, 0)),
                   pl.BlockSpec((KV_WIDTH, rows), lambda i: (0, i)),
                   pl.BlockSpec((WINDOW, 2 * KV_WIDTH), lambda i: (0, 0)),
                   pl.BlockSpec((ns, F_WIDTH), lambda i: (0, 0)),
                   pl.BlockSpec((ns, H_WIDTH), lambda i: (0, 0)),
                   pl.BlockSpec((ns, 2 * KV_WIDTH), lambda i: (0, 0))] + cast_specs,
        out_shape=[jax.ShapeDtypeStruct((t, F_WIDTH), F32),
                   jax.ShapeDtypeStruct((t, H_WIDTH), BF16),
                   jax.ShapeDtypeStruct((KV_WIDTH, t), BF16),
                   jax.ShapeDtypeStruct((WINDOW, 2 * KV_WIDTH), F32),
                   jax.ShapeDtypeStruct((ns, F_WIDTH), F32),
                   jax.ShapeDtypeStruct((ns, H_WIDTH), F32),
                   jax.ShapeDtypeStruct((ns, 2 * KV_WIDTH), F32)]
        + [jax.ShapeDtypeStruct(w.shape, BF16) for w in to_cast],
        compiler_params=pltpu.CompilerParams(
            dimension_semantics=("arbitrary",), vmem_limit_bytes=VMEM_LIMIT),
        name="proj",
    )(x, xs, g_attn, w_in_p, qn, kn, w2p, b2, jnp.asarray(_TRI3, BF16), *to_cast)


def _boundary_rows(b_ref, r0, c0, b, row, level):
    m = 1 << (level - 1)
    if 2 * m >= SUBLANES:
        pieces = [jnp.broadcast_to(b_ref[r0 + g * 2 * m + m - 1:r0 + g * 2 * m + m, c0:c0 + LANES],
                                   (2 * m, LANES))
                  for g in range(BLK // (2 * m))]
        return pieces[0] if len(pieces) == 1 else jnp.concatenate(pieces, axis=0)
    pos = row & (2 * m - 1)
    tiles = b.reshape(BLK // SUBLANES, SUBLANES, LANES)
    out = b
    for p in range(2 * m):
        shift = (m - 1) - p
        if shift != 0:
            rolled = pltpu.roll(tiles, (-shift) % SUBLANES, 1).reshape(BLK, LANES)
            out = jnp.where(pos == p, rolled, out)
    return out


def _mixer_init(relb_ref, bucket_ref, kbd, vbd, kprev_t, vprev, st_scr, mb_scr):
    kbd[...] = jnp.zeros_like(kbd)
    vbd[...] = jnp.zeros_like(vbd)
    for slot in range(vbd.shape[0]):
        for g in range(ATTN_KV_HEADS):
            for half in range(2):
                vbd[slot, g, 2 * half * BLK:(2 * half + 2) * BLK,
                    KV_WIDTH + half * HEAD_DIM:KV_WIDTH + (half + 1) * HEAD_DIM] = (
                        jnp.ones((2 * BLK, HEAD_DIM), BF16))
    kprev_t[...] = jnp.zeros_like(kprev_t)
    vprev[...] = jnp.zeros_like(vprev)
    st_scr[...] = jnp.zeros_like(st_scr)
    bk = bucket_ref[0]
    acc = [jnp.zeros(bk.shape, F32) for _ in range(ATTN_HEADS)]
    for b in range(N_BUCKETS):
        hit = bk == b
        for h in range(ATTN_HEADS):
            acc[h] = jnp.where(hit, relb_ref[b * ATTN_HEADS + h], acc[h])
    for tb in range(2):
        masked = bucket_ref[tb] < 0
        for h in range(ATTN_HEADS):
            mb_scr[tb, h // 2, :, (h % 2) * 2 * BLK:(h % 2 + 1) * 2 * BLK] = (
                jnp.where(masked, NEG, acc[h]))


def _mixer_attention(pb_ref, kt_ref, r0, table, sink_ref, omix_ref, kbd, vbd, k_prev_t, v_prev,
                     mb_scr):
    rows = slice(r0, r0 + BLK)
    lo1 = lax.broadcasted_iota(jnp.int32, (BLK, LANES), 1) < HEAD_DIM
    k_t = kt_ref[:, r0:r0 + BLK]
    for g in range(ATTN_KV_HEADS):
        hd = slice(g * HEAD_DIM, (g + 1) * HEAD_DIM)
        kbd[g, 0:HEAD_DIM, 0:BLK] = k_prev_t[hd]
        kbd[g, 0:HEAD_DIM, BLK:2 * BLK] = k_t[hd]
        kbd[g, HEAD_DIM:2 * HEAD_DIM, 2 * BLK:3 * BLK] = k_prev_t[hd]
        kbd[g, HEAD_DIM:2 * HEAD_DIM, 3 * BLK:4 * BLK] = k_t[hd]
    lo_bf = jnp.where(lo1, 1.0, 0.0).astype(BF16)
    hi_bf = jnp.where(lo1, 0.0, 1.0).astype(BF16)
    v_cur = pb_ref[rows,H_VA:H_VS]
    v_swap = pb_ref[rows,H_VS:H_VG]
    v_parts = (v_cur * lo_bf, v_swap * hi_bf, v_swap * lo_bf, v_cur * hi_bf)
    for n, part in enumerate(v_parts):
        g, half = divmod(n, 2)
        vbd[g, 2 * half * BLK:(2 * half + 1) * BLK, 0:KV_WIDTH] = v_prev[n]
        vbd[g, (2 * half + 1) * BLK:(2 * half + 2) * BLK, 0:KV_WIDTH] = part
    chunks_per_kv = ATTN_HEADS // ATTN_KV_HEADS // 2
    for g in range(ATTN_KV_HEADS):
        c_first = g * chunks_per_kv
        qs = jnp.concatenate(
            [pb_ref[rows,H_QA + (c_first + c) * LANES:H_QA + (c_first + c + 1) * LANES]
             for c in range(chunks_per_kv)], axis=0)
        mb = jnp.concatenate([mb_scr[table, c_first + c] for c in range(chunks_per_kv)], axis=0)
        s = jnp.dot(qs, kbd[g], preferred_element_type=F32) + mb
        prob_rows, sink_rows = [], []
        for c in range(chunks_per_kv):
            probs, sinks = [], []
            for e in range(2):
                se = s[c * BLK:(c + 1) * BLK, e * 2 * BLK:(e + 1) * 2 * BLK]
                sk = sink_ref[2 * (c_first + c) + e]
                m = jnp.maximum(jnp.max(se, axis=-1, keepdims=True), sk)
                probs.append(jnp.exp(se - m).astype(BF16))
                sinks.append(jnp.exp(sk - m))
            prob_rows.append(jnp.concatenate(probs, axis=1))
            sink_rows.append(jnp.where(lo1, sinks[0], sinks[1]))
        o_den = jnp.dot(jnp.concatenate(prob_rows, axis=0), vbd[g], preferred_element_type=F32)
        o = o_den[:, :KV_WIDTH] / (o_den[:, KV_WIDTH:] + jnp.concatenate(sink_rows, axis=0))
        for c in range(chunks_per_kv):
            omix_ref[rows,(c_first + c) * LANES:(c_first + c + 1) * LANES] = (
                o[c * BLK:(c + 1) * BLK].astype(omix_ref.dtype))
    return k_t, v_parts


def _mixer_gla(state, p_ref, pb_ref, r0, lev_ref, gn_ref, omix_ref):
    new_state = []
    pairs = range(len(state))
    rows = slice(r0, r0 + BLK)
    lo1 = lax.broadcasted_iota(jnp.int32, (BLK, LANES), 1) < HEAD_DIM
    lo_bf = jnp.where(lo1, 1.0, 0.0).astype(BF16)
    hi_bf = jnp.where(lo1, 0.0, 1.0).astype(BF16)
    row = lax.broadcasted_iota(jnp.int32, (BLK, LANES), 0)
    zero_blk = jnp.zeros((BLK, LANES), BF16)
    for c in pairs:
        c0 = c * LANES
        q_at = lambda a, z: p_ref[r0 + a:r0 + z, F_QG + c0:F_QG + c0 + LANES]
        k_at = lambda a, z: p_ref[r0 + a:r0 + z, F_KG + c0:F_KG + c0 + LANES]
        b_at = lambda a, z: p_ref[r0 + a:r0 + z, F_LA + c0:F_LA + c0 + LANES]

        def pair_scores(qtb, ktb):
            rhs = jnp.concatenate([ktb * lo_bf, ktb * hi_bf], axis=0)
            return _nt_dot(qtb, rhs)

        s0 = pair_scores(q_at(0, BLK).astype(BF16), k_at(0, BLK).astype(BF16))
        sc = [jnp.where(lev_ref[...] == 0, s0[:, e * BLK:(e + 1) * BLK], 0.0) for e in range(2)]
        for level in range(1, N_LEVELS + 1):
            m = 1 << (level - 1)
            if m >= SUBLANES:
                qs, ks = [], []
                zeros = jnp.zeros((m, LANES), BF16)
                for g in range(BLK // (2 * m)):
                    lo_a, up_a, up_z = g * 2 * m, g * 2 * m + m, (g + 1) * 2 * m
                    rb = jnp.broadcast_to(b_at(up_a - 1, up_a), (m, LANES))
                    qs += [zeros, (q_at(up_a, up_z) * jnp.exp(b_at(up_a, up_z) - rb)).astype(BF16)]
                    ks += [(k_at(lo_a, up_a) * jnp.exp(rb - b_at(lo_a, up_a))).astype(BF16), zeros]
                qtb = jnp.concatenate(qs, axis=0)
                ktb = jnp.concatenate(ks, axis=0)
            else:
                bc = b_at(0, BLK)
                d = bc - _boundary_rows(p_ref, r0, F_LA + c0, bc, row, level)
                upper = ((row >> (level - 1)) & 1) == 1
                qtb = (q_at(0, BLK) * jnp.exp(jnp.where(upper, d, NEG))).astype(BF16)
                ktb = (k_at(0, BLK) * jnp.exp(jnp.where(upper, NEG, -d))).astype(BF16)
            sl = pair_scores(qtb, ktb)
            sc = [jnp.where(lev_ref[...] == level, sl[:, e * BLK:(e + 1) * BLK], sc[e])
                  for e in range(2)]
        sc = jnp.concatenate(sc, axis=1)
        qc, kc, bc = q_at(0, BLK), k_at(0, BLK), b_at(0, BLK)

        b_last = bc[BLK - 1:BLK, :]
        v0 = pb_ref[rows,H_VG + 2 * c0:H_VG + 2 * c0 + LANES]
        v1 = pb_ref[rows,H_VG + 2 * c0 + LANES:H_VG + 2 * c0 + 2 * LANES]
        v_bd = jnp.concatenate([jnp.concatenate([v0, zero_blk], axis=1),
                                jnp.concatenate([zero_blk, v1], axis=1)], axis=0)
        st_c = state[c]
        stb = st_c.astype(BF16)
        st_rhs = jnp.concatenate([stb * lo_bf, stb * hi_bf], axis=0)
        o = (jnp.dot(sc.astype(BF16), v_bd, preferred_element_type=F32)
             + _nt_dot((qc * jnp.exp(bc)).astype(BF16), st_rhs))
        kd = (kc * jnp.exp(b_last - bc)).astype(BF16)
        upd = _tn_dot(jnp.concatenate([v0, v1], axis=1), kd)
        new_state.append(st_c * jnp.exp(b_last) + jnp.where(lo1, upd[:BLK], upd[BLK:]))
        for e in range(2):
            h = 2 * c + e
            oh = o[:, e * LANES:(e + 1) * LANES]
            og = oh * lax.rsqrt(jnp.mean(oh * oh, axis=-1, keepdims=True) + EPS) * gn_ref[...]
            rg = p_ref[rows,F_RG + h * GLA_DV:F_RG + (h + 1) * GLA_DV]
            gated = og * (rg * _sigmoid(rg))
            omix_ref[rows,ATTN_WIDTH + h * GLA_DV:ATTN_WIDTH + (h + 1) * GLA_DV] = (
                gated.astype(omix_ref.dtype))
    return new_state


def _prompt_mixer_kernel(relb_ref, sink_ref, p_ref, pb_ref, kt_ref, bucket_ref, lev_ref, gn_ref,
                         omix_ref, st_ref, kbd, vbd, kprev_t, vprev, st_scr, mb_scr):
    i = pl.program_id(0)

    @pl.when(i == 0)
    def _init():
        _mixer_init(relb_ref, bucket_ref, kbd, vbd, kprev_t, vprev, st_scr, mb_scr)

    k_prev_t = kprev_t[...]
    v_prev = [vprev[n] for n in range(2 * ATTN_KV_HEADS)]
    state = [st_scr[:, c * LANES:(c + 1) * LANES] for c in range(GLA_HEADS // 2)]
    n_blocks = p_ref.shape[0] // BLK
    for jb in range(n_blocks):
        table = jnp.where(i == 0, 1, 0) if jb == 0 else 0
        k_prev_t, v_prev = _mixer_attention(pb_ref, kt_ref, jb * BLK, table, sink_ref, omix_ref,
                                            kbd.at[jb], vbd.at[jb], k_prev_t, v_prev, mb_scr)
        state = _mixer_gla(state, p_ref, pb_ref, jb * BLK, lev_ref, gn_ref, omix_ref)
    kprev_t[...] = k_prev_t
    for n, part in enumerate(v_prev):
        vprev[n] = part
    for c, st_c in enumerate(state):
        st_scr[:, c * LANES:(c + 1) * LANES] = st_c
        st_ref[:, c * LANES:(c + 1) * LANES] = st_c


def _prompt_mixer(p, pb, kt, relb, sinks, gn):
    t = p.shape[0]
    smem = pl.BlockSpec(memory_space=pltpu.SMEM)
    return pl.pallas_call(
        _prompt_mixer_kernel,
        grid=(t // MIX_ROWS,),
        in_specs=[
            smem, smem,
            pl.BlockSpec((MIX_ROWS, F_WIDTH), lambda i: (i, 0)),
            pl.BlockSpec((MIX_ROWS, H_WIDTH), lambda i: (i, 0)),
            pl.BlockSpec((KV_WIDTH, MIX_ROWS), lambda i: (0, i)),
            _const_spec((2, BLK, 2 * BLK)),
            _const_spec((BLK, BLK)),
            _const_spec((1, GLA_DV)),
        ],
        out_specs=[
            pl.BlockSpec((MIX_ROWS, MIX_WIDTH), lambda i: (i, 0)),
            pl.BlockSpec((GLA_DV, GLA_QK_WIDTH), lambda i: (0, 0)),
        ],
        out_shape=[
            jax.ShapeDtypeStruct((t, MIX_WIDTH), BF16),
            jax.ShapeDtypeStruct((GLA_DV, GLA_QK_WIDTH), F32),
        ],
        scratch_shapes=[
            pltpu.VMEM((MIX_ROWS // BLK, ATTN_KV_HEADS, 2 * HEAD_DIM, 4 * BLK), BF16),
            pltpu.VMEM((MIX_ROWS // BLK, ATTN_KV_HEADS, 4 * BLK, 2 * KV_WIDTH), BF16),
            pltpu.VMEM((KV_WIDTH, BLK), BF16),
            pltpu.VMEM((2 * ATTN_KV_HEADS, BLK, KV_WIDTH), BF16),
            pltpu.VMEM((GLA_DV, GLA_QK_WIDTH), F32),
            pltpu.VMEM((2, ATTN_HEADS // 2, BLK, 4 * BLK), F32),
        ],
        compiler_params=pltpu.CompilerParams(
            dimension_semantics=("arbitrary",), vmem_limit_bytes=VMEM_LIMIT),
        name="prompt_mixer",
    )(relb, sinks, p, pb, kt, jnp.asarray(_BUCKET_PROMPT), jnp.asarray(_LEV), gn)


def _sample_mixer_kernel(ps_ref, ph_ref, kv_ref, pfull_ref, ck_ref, cv_ref, st_ref, relbt_ref, sink_ref,
                         bucket_ref,
                         gn_ref, omix_ref, kwin_ref, vwin_ref, stout_ref, lat_scr, kqt_scr,
                         bias_scr, s_scr, o_scr, og_scr):
    i = pl.program_id(0)
    nb = pfull_ref.shape[0]

    @pl.when(i == 0)
    def _init():
        lat_scr[...] = _split3_rows(pfull_ref[:, F_LA:F_WIDTH].T)
        kqt_scr[0:GLA_QK_WIDTH] = pfull_ref[:, F_KG:F_RG].T.astype(BF16)
        kqt_scr[GLA_QK_WIDTH:2 * GLA_QK_WIDTH] = pfull_ref[:, F_QG:F_KG].T.astype(BF16)
        bk = jnp.broadcast_to(bucket_ref[...], (ATTN_HEADS, WINDOW))
        acc = jnp.zeros((ATTN_HEADS, WINDOW), F32)
        for b in range(N_BUCKETS):
            acc = jnp.where(bk == b, relbt_ref[:, b:b + 1], acc)
        bias_scr[...] = acc

    lo = lax.broadcasted_iota(jnp.int32, (1, LANES), 1) < HEAD_DIM
    sub = lax.broadcasted_iota(jnp.int32, (ATTN_HEADS, LANES), 0)
    wrow = lax.broadcasted_iota(jnp.int32, (WINDOW, KV_WIDTH), 0)
    heads_per_kv = ATTN_HEADS // ATTN_KV_HEADS

    n_of_col = i * SAMPLE_BLK + lax.broadcasted_iota(jnp.int32, (nb, SAMPLE_BLK * LANES), 1) // LANES
    pick = jnp.where(lax.broadcasted_iota(jnp.int32, (nb, SAMPLE_BLK * LANES), 0) == n_of_col,
                     1.0, 0.0).astype(BF16)
    la_b = (jnp.dot(lat_scr[0:GLA_QK_WIDTH], pick, preferred_element_type=F32)
            + jnp.dot(lat_scr[GLA_QK_WIDTH:2 * GLA_QK_WIDTH], pick, preferred_element_type=F32)
            + jnp.dot(lat_scr[2 * GLA_QK_WIDTH:3 * GLA_QK_WIDTH], pick, preferred_element_type=F32))
    kq_b = jnp.dot(kqt_scr[...], pick, preferred_element_type=F32)

    for j in range(SAMPLE_BLK):
        k_new = kv_ref[j:j + 1, 0:KV_WIDTH]
        v_new = kv_ref[j:j + 1, KV_WIDTH:2 * KV_WIDTH]
        kwin_ref[j] = jnp.where(wrow == WINDOW - 1, k_new, pltpu.roll(ck_ref[j], WINDOW - 1, 0))
        vwin_ref[j] = jnp.where(wrow == WINDOW - 1, v_new, pltpu.roll(cv_ref[j], WINDOW - 1, 0))

    for j in range(SAMPLE_BLK):
        qexp = jnp.zeros((ATTN_HEADS, LANES), F32)
        for c in range(ATTN_HEADS // 2):
            chunk = ph_ref[j:j + 1, H_QA + c * LANES:H_QA + (c + 1) * LANES]
            swapped = pltpu.roll(chunk, HEAD_DIM, 1)
            if (2 * c) // heads_per_kv == 0:
                rows = (jnp.where(lo, chunk, 0.0), jnp.where(lo, swapped, 0.0))
            else:
                rows = (jnp.where(lo, 0.0, swapped), jnp.where(lo, 0.0, chunk))
            for e in range(2):
                qexp = jnp.where(sub == 2 * c + e, rows[e], qexp)
        s_scr[j * ATTN_HEADS:(j + 1) * ATTN_HEADS] = _nt_dot(qexp.astype(BF16), kwin_ref[j].astype(BF16))

    tile = lambda x: jnp.concatenate([x] * SAMPLE_BLK, axis=0)
    sink = tile(sink_ref[...])
    s = s_scr[...] + tile(bias_scr[...])
    m = jnp.maximum(jnp.max(s, axis=-1, keepdims=True), sink)
    pe = jnp.exp(s - m)
    inv_den = 1.0 / (jnp.sum(pe, axis=-1, keepdims=True) + jnp.exp(sink - m))
    peb = pe.astype(BF16)
    for j in range(SAMPLE_BLK):
        o_scr[j * ATTN_HEADS:(j + 1) * ATTN_HEADS] = jnp.dot(
            peb[j * ATTN_HEADS:(j + 1) * ATTN_HEADS], vwin_ref[j].astype(BF16),
            preferred_element_type=F32)
    o_all = o_scr[...] * inv_den
    o_swap = pltpu.roll(o_all, HEAD_DIM, 1)
    for j in range(SAMPLE_BLK):
        r = j * ATTN_HEADS
        for c in range(ATTN_HEADS // 2):
            if (2 * c) // heads_per_kv == 0:
                piece = jnp.where(lo, o_all[r + 2 * c:r + 2 * c + 1, :], o_swap[r + 2 * c + 1:r + 2 * c + 2, :])
            else:
                piece = jnp.where(lo, o_swap[r + 2 * c:r + 2 * c + 1, :], o_all[r + 2 * c + 1:r + 2 * c + 2, :])
            omix_ref[j:j + 1, c * LANES:(c + 1) * LANES] = piece

    for j in range(SAMPLE_BLK):
        cols = slice(j * LANES, (j + 1) * LANES)
        for h in range(GLA_HEADS):
            rs = slice(h * GLA_DK, (h + 1) * GLA_DK)
            qs = slice(GLA_QK_WIDTH + h * GLA_DK, GLA_QK_WIDTH + (h + 1) * GLA_DK)
            v_row = ph_ref[j:j + 1, H_VG + h * GLA_DV:H_VG + (h + 1) * GLA_DV]
            s_new = jnp.exp(la_b[rs, cols]) * st_ref[j, h] + kq_b[rs, cols] * v_row
            stout_ref[j, h] = s_new
            og_scr[j:j + 1, h * GLA_DV:(h + 1) * GLA_DV] = jnp.sum(
                kq_b[qs, cols] * s_new, axis=0, keepdims=True)
    for h in range(GLA_HEADS):
        hs = slice(h * GLA_DV, (h + 1) * GLA_DV)
        og = og_scr[:, hs]
        og = og * lax.rsqrt(jnp.mean(og * og, axis=-1, keepdims=True) + EPS) * gn_ref[...]
        rg = ps_ref[:, F_RG + h * GLA_DV:F_RG + (h + 1) * GLA_DV]
        omix_ref[:, ATTN_WIDTH + h * GLA_DV:ATTN_WIDTH + (h + 1) * GLA_DV] = og * (rg * _sigmoid(rg))


def _sample_mixer(ps, ph, kv, cache_k, cache_v, state, relbt, sinks_col, gn):
    nb = ps.shape[0]
    blk3 = lambda i: (i, 0, 0)
    blk4 = lambda i: (i, 0, 0, 0)
    return pl.pallas_call(
        _sample_mixer_kernel,
        grid=(nb // SAMPLE_BLK,),
        in_specs=[
            pl.BlockSpec((SAMPLE_BLK, F_WIDTH), lambda i: (i, 0)),
            pl.BlockSpec((SAMPLE_BLK, H_WIDTH), lambda i: (i, 0)),
            pl.BlockSpec((SAMPLE_BLK, 2 * KV_WIDTH), lambda i: (i, 0)),
            _const_spec((nb, F_WIDTH)),
            pl.BlockSpec((SAMPLE_BLK, WINDOW, KV_WIDTH), blk3),
            pl.BlockSpec((SAMPLE_BLK, WINDOW, KV_WIDTH), blk3),
            pl.BlockSpec((SAMPLE_BLK, GLA_HEADS, GLA_DK, GLA_DV), blk4),
            _const_spec((ATTN_HEADS, N_BUCKETS)),
            _const_spec((ATTN_HEADS, 1)),
            _const_spec((1, WINDOW)),
            _const_spec((1, GLA_DV)),
        ],
        out_specs=[
            pl.BlockSpec((SAMPLE_BLK, MIX_WIDTH), lambda i: (i, 0)),
            pl.BlockSpec((SAMPLE_BLK, WINDOW, KV_WIDTH), blk3),
            pl.BlockSpec((SAMPLE_BLK, WINDOW, KV_WIDTH), blk3),
            pl.BlockSpec((SAMPLE_BLK, GLA_HEADS, GLA_DK, GLA_DV), blk4),
        ],
        out_shape=[
            jax.ShapeDtypeStruct((nb, MIX_WIDTH), F32),
            jax.ShapeDtypeStruct((nb, WINDOW, KV_WIDTH), F32),
            jax.ShapeDtypeStruct((nb, WINDOW, KV_WIDTH), F32),
            jax.ShapeDtypeStruct((nb, GLA_HEADS, GLA_DK, GLA_DV), F32),
        ],
        scratch_shapes=[
            pltpu.VMEM((3 * GLA_QK_WIDTH, nb), BF16),
            pltpu.VMEM((2 * GLA_QK_WIDTH, nb), BF16),
            pltpu.VMEM((ATTN_HEADS, WINDOW), F32),
            pltpu.VMEM((SAMPLE_BLK * ATTN_HEADS, WINDOW), F32),
            pltpu.VMEM((SAMPLE_BLK * ATTN_HEADS, KV_WIDTH), F32),
            pltpu.VMEM((SAMPLE_BLK, GLA_WIDTH), F32),
        ],
        compiler_params=pltpu.CompilerParams(
            dimension_semantics=("arbitrary",), vmem_limit_bytes=VMEM_LIMIT),
        name="sample_mixer",
    )(ps, ph, kv, ps, cache_k, cache_v, state, relbt, sinks_col, jnp.asarray(_BUCKET_SAMPLE), gn)


def _finish_kernel(x_ref, mix_ref, xs_ref, mixs_ref, wo_ref, g_ref, wg_ref, wu_ref, wd_ref,
                   y_ref, ys_ref, *, ff_chunks):
    weights = (wo_ref, g_ref, wg_ref, wu_ref, wd_ref)
    _finish_rows(x_ref, mix_ref, *weights, y_ref, ff_chunks=ff_chunks)

    @pl.when(pl.program_id(0) == pl.num_programs(0) - 1)
    def _samples():
        _finish_rows(xs_ref, mixs_ref, *weights, ys_ref, ff_chunks=ff_chunks)


def _finish_rows(x_ref, mix_ref, wo_ref, g_ref, wg_ref, wu_ref, wd_ref, y_ref, *, ff_chunks):
    h = x_ref[...] + jnp.dot(mix_ref[...].astype(BF16), wo_ref[...], preferred_element_type=F32)
    r = lax.rsqrt(jnp.mean(h * h, axis=-1, keepdims=True) + EPS)
    z = (h * g_ref[...]).astype(BF16)
    n_tiles = wd_ref.shape[0] // MXU_TILE
    acc = h
    for c in range(ff_chunks):
        c0 = ((c * n_tiles) // ff_chunks) * MXU_TILE
        c1 = (((c + 1) * n_tiles) // ff_chunks) * MXU_TILE
        gate = jnp.dot(z, wg_ref[:, c0:c1], preferred_element_type=F32) * r
        up = jnp.dot(z, wu_ref[:, c0:c1], preferred_element_type=F32) * r
        act = ((gate * _sigmoid(gate)) * up).astype(BF16)
        acc = acc + jnp.dot(act, wd_ref[c0:c1, :], preferred_element_type=F32)
    y_ref[...] = acc


def _finish(x, mix, xs, mixs, wo, g_ffn, wg, wu, wd, rows):
    t = x.shape[0]
    ns = xs.shape[0]
    d_ff = wd.shape[0]
    assert d_ff % MXU_TILE == 0
    return pl.pallas_call(
        functools.partial(_finish_kernel, ff_chunks=FF_CHUNKS),
        grid=(t // rows,),
        in_specs=[
            pl.BlockSpec((rows, D_MODEL), lambda i: (i, 0)),
            pl.BlockSpec((rows, MIX_WIDTH), lambda i: (i, 0)),
            _const_spec((ns, D_MODEL)),
            _const_spec((ns, MIX_WIDTH)),
            _const_spec((MIX_WIDTH, D_MODEL)),
            _const_spec((1, D_MODEL)),
            _const_spec((D_MODEL, d_ff)),
            _const_spec((D_MODEL, d_ff)),
            _const_spec((d_ff, D_MODEL)),
        ],
        out_specs=[pl.BlockSpec((rows, D_MODEL), lambda i: (i, 0)),
                   pl.BlockSpec((ns, D_MODEL), lambda i: (0, 0))],
        out_shape=[jax.ShapeDtypeStruct((t, D_MODEL), F32),
                   jax.ShapeDtypeStruct((ns, D_MODEL), F32)],
        compiler_params=pltpu.CompilerParams(
            dimension_semantics=("arbitrary",), vmem_limit_bytes=VMEM_LIMIT),
        name="finish",
    )(x, mix, xs, mixs, wo, g_ffn, wg, wu, wd)


PROMPT_ROWS = 1024
PROJ_ROWS = 1024
MIX_ROWS = 512


def kernel(x_prompt, x_sample, cache_k, cache_v, state_gla, attn_norm_g, w_in, q_norm_g, k_norm_g,
           attn_sinks, rel_bias, w_gla_gate2, b_gla_gate, gla_norm_g, w_o, ffn_norm_g, w_gate, w_up,
           w_down):
    depth = w_in.shape[0]
    batch, seq, _ = x_prompt.shape
    dec_batch, dec_seq, _ = x_sample.shape
    wb = cache_k.shape[2]
    assert batch == 1 and dec_seq == 1 and wb == WINDOW
    assert seq % PROMPT_ROWS == 0 and seq % PROJ_ROWS == 0
    assert dec_batch % SAMPLE_BLK == 0 and dec_batch % LANES == 0
    assert rel_bias.shape == (N_BUCKETS, ATTN_HEADS)

    xp = x_prompt.reshape(seq, D_MODEL)
    xs = x_sample.reshape(dec_batch, D_MODEL)
    relb_flat = rel_bias.reshape(-1)
    relb_t = rel_bias.T
    outs = ([], [], [], [], [], [])
    for l in range(depth):
        kv_end = ATTN_WIDTH + 2 * KV_WIDTH
        w_in_p = jnp.concatenate(
            [w_in[l, :, :kv_end],
             jnp.pad(w_in[l, :, MAIN_WIDTH:], ((0, 0), (0, RANK_PAD - GLA_RANK))),
             w_in[l, :, kv_end:MAIN_WIDTH]], axis=1).astype(BF16)
        w2p = jnp.pad(w_gla_gate2[l], ((0, RANK_PAD - GLA_RANK), (0, 0))).astype(BF16)
        proj_w = (attn_norm_g[l][None, :], w_in_p, jnp.tile(q_norm_g[l], ATTN_HEADS)[None, :],
                  jnp.tile(k_norm_g[l], ATTN_KV_HEADS)[None, :], w2p, b_gla_gate[l][None, :])
        gn = gla_norm_g[l][None, :]

        pp, pb, kt_p, kv_win, ps, ph, kv_s, wo_b, wg_b, wu_b, wd_b = _project(
            xp, xs, *proj_w, rows=PROJ_ROWS, to_cast=(w_o[l], w_gate[l], w_up[l], w_down[l]))
        fin_w = (wo_b, ffn_norm_g[l][None, :], wg_b, wu_b, wd_b)
        mix_p, st_p = _prompt_mixer(pp, pb, kt_p, relb_flat, attn_sinks[l], gn)
        mix_s, kwin, vwin, st_s = _sample_mixer(
            ps, ph, kv_s, cache_k[l].reshape(dec_batch, wb, KV_WIDTH),
            cache_v[l].reshape(dec_batch, wb, KV_WIDTH),
            state_gla[l].astype(F32), relb_t, attn_sinks[l][:, None], gn)
        xp_in = xp
        xp, xs = _finish(xp_in, mix_p, xs, mix_s, *fin_w, rows=PROMPT_ROWS)
        outs[0].append(kv_win[:, :KV_WIDTH].reshape(batch, wb, ATTN_KV_HEADS, HEAD_DIM))
        outs[1].append(kv_win[:, KV_WIDTH:].reshape(batch, wb, ATTN_KV_HEADS, HEAD_DIM))
        outs[2].append(st_p.T.reshape(batch, GLA_HEADS, GLA_DK, GLA_DV).astype(state_gla.dtype))
        outs[3].append(kwin.reshape(dec_batch, wb, ATTN_KV_HEADS, HEAD_DIM))
        outs[4].append(vwin.reshape(dec_batch, wb, ATTN_KV_HEADS, HEAD_DIM))
        outs[5].append(st_s.astype(state_gla.dtype))

    y_prompt = xp.reshape(batch, seq, D_MODEL)
    y_sample = xs.reshape(dec_batch, dec_seq, D_MODEL)
    return (y_prompt, y_sample) + tuple(jnp.stack(o) for o in outs)
```

```python
import functools
import math

import numpy as np
import jax
import jax.numpy as jnp
from jax import lax
from jax.experimental import pallas as pl
from jax.experimental.pallas import tpu as pltpu

F32 = jnp.float32
BF16 = jnp.bfloat16

D_MODEL = 1024
HEAD_DIM = 64
ATTN_HEADS = 8
ATTN_KV_HEADS = 2
WINDOW = 128
N_BUCKETS = 32
MAX_DISTANCE = 128
GLA_HEADS = 4
GLA_DK = 64
GLA_DV = 128
GLA_RANK = 16
GLA_TAU = 16.0
EPS = 1e-6
ATTN_WIDTH = ATTN_HEADS * HEAD_DIM
KV_WIDTH = ATTN_KV_HEADS * HEAD_DIM
GLA_QK_WIDTH = GLA_HEADS * GLA_DK
GLA_WIDTH = GLA_HEADS * GLA_DV
MIX_WIDTH = ATTN_WIDTH + GLA_WIDTH
MAIN_WIDTH = ATTN_WIDTH + 2 * KV_WIDTH + 2 * GLA_QK_WIDTH + 2 * GLA_WIDTH
LANES = 128
SUBLANES = 8
MXU_TILE = 256
FF_CHUNKS = 4
RANK_PAD = LANES
IN_PAD_WIDTH = MAIN_WIDTH + RANK_PAD

F_QG = 0
F_KG = F_QG + GLA_QK_WIDTH
F_RG = F_KG + GLA_QK_WIDTH
F_LA = F_RG + GLA_WIDTH
F_WIDTH = F_LA + GLA_QK_WIDTH
H_QA = 0
H_VA = H_QA + ATTN_WIDTH
H_VS = H_VA + KV_WIDTH
H_VG = H_VS + KV_WIDTH
H_WIDTH = H_VG + GLA_WIDTH

W_QA = 0
W_KV = W_QA + ATTN_WIDTH
W_QG = W_KV + 2 * KV_WIDTH + RANK_PAD
W_VG = W_QG + 2 * GLA_QK_WIDTH
W_RG = W_VG + GLA_WIDTH

BLK = 128
N_LEVELS = 7
NEG = -1e30
ATTN_SCALE = HEAD_DIM ** -0.5
SAMPLE_BLK = 8
VMEM_LIMIT = 56 * 1024 * 1024


def _t5_bucket_np(dist):
    n = np.maximum(dist, 0)
    max_exact = N_BUCKETS // 2
    nf = np.maximum(n, 1).astype(np.float64)
    large = max_exact + (np.log(nf / max_exact) / math.log(MAX_DISTANCE / max_exact)
                         * (N_BUCKETS - max_exact)).astype(np.int32)
    large = np.minimum(large, N_BUCKETS - 1)
    return np.where(n < max_exact, n, large).astype(np.int32)


def _prompt_bucket_tables():
    i = np.arange(BLK)[:, None]
    j = np.arange(2 * BLK)[None, :]
    dist = BLK + i - j
    band = (dist >= 0) & (dist < WINDOW)
    bucket = _t5_bucket_np(dist)
    t0 = np.where(band, bucket, -1)
    t1 = np.where(band & (j >= BLK), bucket, -1)
    return np.stack([t0, t1]).astype(np.int32)


def _level_tables():
    t = np.arange(BLK)[:, None]
    s = np.arange(BLK)[None, :]
    x = t ^ s
    lev = np.where(x > 0, np.floor(np.log2(np.maximum(x, 1))).astype(np.int32) + 1, 0)
    lev = np.where(s > t, -1, lev).astype(np.int32)
    tri = (s <= t).astype(np.float32)
    return lev, np.concatenate([tri, tri, tri], axis=1)


_BUCKET_PROMPT = _prompt_bucket_tables()
_LEV, _TRI3 = _level_tables()
_BUCKET_SAMPLE = _t5_bucket_np((WINDOW - 1) - np.arange(WINDOW))[None, :].astype(np.int32)


def _nt_dot(a, b):
    return lax.dot_general(a, b, (((1,), (1,)), ((), ())), preferred_element_type=F32)


def _tn_dot(a, b):
    return lax.dot_general(a, b, (((0,), (0,)), ((), ())), preferred_element_type=F32)


def _head_mean_sq(x):
    lo = lax.broadcasted_iota(jnp.int32, (x.shape[0], LANES), 1) < HEAD_DIM
    outs = []
    for c in range(x.shape[1] // LANES):
        y = x[:, c * LANES:(c + 1) * LANES]
        y = y * y
        s_lo = jnp.sum(jnp.where(lo, y, 0.0), axis=-1, keepdims=True)
        s_hi = jnp.sum(jnp.where(lo, 0.0, y), axis=-1, keepdims=True)
        outs.append(jnp.where(lo, s_lo, s_hi) * (1.0 / HEAD_DIM))
    return outs[0] if len(outs) == 1 else jnp.concatenate(outs, axis=1)


def _sigmoid(x):
    return 1.0 / (1.0 + jnp.exp(-x))


def _split3_rows(x):
    hi = x.astype(BF16)
    r1 = x - hi.astype(F32)
    mid = r1.astype(BF16)
    lo = (r1 - mid.astype(F32)).astype(BF16)
    return jnp.concatenate([hi, mid, lo], axis=0)


def _proj_kernel(x_ref, xs_ref, g_ref, w_ref, qn_ref, kn_ref, w2_ref, b2_ref, tri_ref, *rest):
    n_out = 7
    n_cast = (len(rest) - n_out) // 2
    cast_in, cast_out = rest[:n_cast], rest[n_cast + n_out:]
    f32_ref, bf_ref, kt_ref, kvw_ref, f32s_ref, bfs_ref, kvs_ref = rest[n_cast:n_cast + n_out]
    for src, dst in zip(cast_in, cast_out):
        dst[...] = src[...].astype(BF16)
    weights = (g_ref, w_ref, qn_ref, kn_ref, w2_ref, b2_ref, tri_ref)
    _proj_rows(x_ref, *weights, f32_ref, bf_ref, kt_ref, kvw_ref, block_cumsum=True)

    @pl.when(pl.program_id(0) == pl.num_programs(0) - 1)
    def _samples():
        _proj_rows(xs_ref, *weights, f32s_ref, bfs_ref, None, kvs_ref, block_cumsum=False)


def _proj_rows(x_ref, g_ref, w_ref, qn_ref, kn_ref, w2_ref, b2_ref, tri_ref, f32_ref, bf_ref,
               kt_ref, kv_ref, *, block_cumsum):
    x = x_ref[...]
    r = lax.rsqrt(jnp.mean(x * x, axis=-1, keepdims=True) + EPS)
    n = (x * g_ref[...]).astype(BF16)

    def seg(c0, c1):
        return jnp.dot(n, w_ref[:, c0:c1], preferred_element_type=F32) * r

    kvl = seg(W_KV, W_QG)
    lr = kvl[:, 2 * KV_WIDTH:].astype(BF16)
    z = jnp.dot(lr, w2_ref[...], preferred_element_type=F32) + b2_ref[...]
    log_a = (jnp.minimum(z, 0.0) - jnp.log1p(jnp.exp(-jnp.abs(z)))) / GLA_TAU
    if block_cumsum:
        for blk in range(x.shape[0] // BLK):
            rows = slice(blk * BLK, (blk + 1) * BLK)
            f32_ref[rows, F_LA:F_WIDTH] = jnp.dot(
                tri_ref[...], _split3_rows(log_a[rows]), preferred_element_type=F32)
    else:
        f32_ref[:, F_LA:F_WIDTH] = log_a
    k = kvl[:, :KV_WIDTH]
    k = k * lax.rsqrt(_head_mean_sq(k) + EPS) * kn_ref[...]
    v = kvl[:, KV_WIDTH:2 * KV_WIDTH]
    n_kv = kv_ref.shape[0]
    kv_ref[:, 0:KV_WIDTH] = k[x.shape[0] - n_kv:]
    kv_ref[:, KV_WIDTH:2 * KV_WIDTH] = v[x.shape[0] - n_kv:]
    if kt_ref is not None:
        kt_ref[...] = k.T.astype(BF16)
    hdt = bf_ref.dtype
    bf_ref[:, H_VA:H_VS] = v.astype(hdt)
    bf_ref[:, H_VS:H_VG] = pltpu.roll(v, HEAD_DIM, 1).astype(hdt)
    q = seg(W_QA, W_KV)
    bf_ref[:, H_QA:H_VA] = (q * lax.rsqrt(_head_mean_sq(q) + EPS) * qn_ref[...] * ATTN_SCALE
                            ).astype(hdt)
    qk_g = seg(W_QG, W_VG)
    f32_ref[:, F_QG:F_KG] = qk_g[:, :GLA_QK_WIDTH] * (GLA_DK ** -0.5)
    f32_ref[:, F_KG:F_RG] = qk_g[:, GLA_QK_WIDTH:]
    bf_ref[:, H_VG:H_WIDTH] = seg(W_VG, W_RG).astype(hdt)
    f32_ref[:, F_RG:F_LA] = seg(W_RG, IN_PAD_WIDTH)


def _const_spec(shape):
    nd = len(shape)
    return pl.BlockSpec(shape, lambda i: (0,) * nd, pipeline_mode=pl.Buffered(1))


def _project(x, xs, g_attn, w_in_p, qn, kn, w2p, b2, rows, to_cast):
    t = x.shape[0]
    ns = xs.shape[0]
    steps = t // rows
    cast_specs = []
    for w in to_cast:
        slab = w.shape[0] // steps
        assert w.shape[0] % steps == 0 and slab % (2 * SUBLANES) == 0
        cast_specs.append(pl.BlockSpec((slab, w.shape[1]), lambda i: (i, 0)))
    return pl.pallas_call(
        _proj_kernel,
        grid=(steps,),
        in_specs=[
            pl.BlockSpec((rows, D_MODEL), lambda i: (i, 0)),
            _const_spec((ns, D_MODEL)),
            _const_spec((1, D_MODEL)),
            _const_spec((D_MODEL, IN_PAD_WIDTH)),
            _const_spec((1, ATTN_WIDTH)),
            _const_spec((1, KV_WIDTH)),
            _const_spec((RANK_PAD, GLA_QK_WIDTH)),
            _const_spec((1, GLA_QK_WIDTH)),
            _const_spec((BLK, 3 * BLK)),
        ] + cast_specs,
        out_specs=[pl.BlockSpec((rows, F_WIDTH), lambda i: (i, 0)),
                   pl.BlockSpec((rows, H_WIDTH), lambda i: (i, 0)),
                   pl.BlockSpec((KV_WIDTH, rows), lambda i: (0, i)),
                   pl.BlockSpec((WINDOW, 2 * KV_WIDTH), lambda i: (0, 0)),
                   pl.BlockSpec((ns, F_WIDTH), lambda i: (0, 0)),
                   pl.BlockSpec((ns, H_WIDTH), lambda i: (0, 0)),
                   pl.BlockSpec((ns, 2 * KV_WIDTH), lambda i: (0, 0))] + cast_specs,
        out_shape=[jax.ShapeDtypeStruct((t, F_WIDTH), F32),
                   jax.ShapeDtypeStruct((t, H_WIDTH), BF16),
                   jax.ShapeDtypeStruct((KV_WIDTH, t), BF16),
                   jax.ShapeDtypeStruct((WINDOW, 2 * KV_WIDTH), F32),
                   jax.ShapeDtypeStruct((ns, F_WIDTH), F32),
                   jax.ShapeDtypeStruct((ns, H_WIDTH), F32),
                   jax.ShapeDtypeStruct((ns, 2 * KV_WIDTH), F32)]
        + [jax.ShapeDtypeStruct(w.shape, BF16) for w in to_cast],
        compiler_params=pltpu.CompilerParams(
            dimension_semantics=("arbitrary",), vmem_limit_bytes=VMEM_LIMIT),
        name="proj",
    )(x, xs, g_attn, w_in_p, qn, kn, w2p, b2, jnp.asarray(_TRI3, BF16), *to_cast)


def _boundary_rows(b_ref, r0, c0, b, row, level):
    m = 1 << (level - 1)
    if 2 * m >= SUBLANES:
        pieces = [jnp.broadcast_to(b_ref[r0 + g * 2 * m + m - 1:r0 + g * 2 * m + m, c0:c0 + LANES],
                                   (2 * m, LANES))
                  for g in range(BLK // (2 * m))]
        return pieces[0] if len(pieces) == 1 else jnp.concatenate(pieces, axis=0)
    pos = row & (2 * m - 1)
    tiles = b.reshape(BLK // SUBLANES, SUBLANES, LANES)
    out = b
    for p in range(2 * m):
        shift = (m - 1) - p
        if shift != 0:
            rolled = pltpu.roll(tiles, (-shift) % SUBLANES, 1).reshape(BLK, LANES)
            out = jnp.where(pos == p, rolled, out)
    return out


def _mixer_init(relb_ref, bucket_ref, kbd, vbd, kprev_t, vprev, st_scr, mb_scr):
    kbd[...] = jnp.zeros_like(kbd)
    vbd[...] = jnp.zeros_like(vbd)
    for slot in range(vbd.shape[0]):
        for g in range(ATTN_KV_HEADS):
            for half in range(2):
                vbd[slot, g, 2 * half * BLK:(2 * half + 2) * BLK,
                    KV_WIDTH + half * HEAD_DIM:KV_WIDTH + (half + 1) * HEAD_DIM] = (
                        jnp.ones((2 * BLK, HEAD_DIM), BF16))
    kprev_t[...] = jnp.zeros_like(kprev_t)
    vprev[...] = jnp.zeros_like(vprev)
    st_scr[...] = jnp.zeros_like(st_scr)
    bk = bucket_ref[0]
    acc = [jnp.zeros(bk.shape, F32) for _ in range(ATTN_HEADS)]
    for b in range(N_BUCKETS):
        hit = bk == b
        for h in range(ATTN_HEADS):
            acc[h] = jnp.where(hit, relb_ref[b * ATTN_HEADS + h], acc[h])
    for tb in range(2):
        masked = bucket_ref[tb] < 0
        for h in range(ATTN_HEADS):
            mb_scr[tb, h // 2, :, (h % 2) * 2 * BLK:(h % 2 + 1) * 2 * BLK] = (
                jnp.where(masked, NEG, acc[h]))


def _mixer_attention(pb_ref, kt_ref, r0, table, sink_ref, omix_ref, kbd, vbd, k_prev_t, v_prev,
                     mb_scr):
    rows = slice(r0, r0 + BLK)
    lo1 = lax.broadcasted_iota(jnp.int32, (BLK, LANES), 1) < HEAD_DIM
    k_t = kt_ref[:, r0:r0 + BLK]
    for g in range(ATTN_KV_HEADS):
        hd = slice(g * HEAD_DIM, (g + 1) * HEAD_DIM)
        kbd[g, 0:HEAD_DIM, 0:BLK] = k_prev_t[hd]
        kbd[g, 0:HEAD_DIM, BLK:2 * BLK] = k_t[hd]
        kbd[g, HEAD_DIM:2 * HEAD_DIM, 2 * BLK:3 * BLK] = k_prev_t[hd]
        kbd[g, HEAD_DIM:2 * HEAD_DIM, 3 * BLK:4 * BLK] = k_t[hd]
    lo_bf = jnp.where(lo1, 1.0, 0.0).astype(BF16)
    hi_bf = jnp.where(lo1, 0.0, 1.0).astype(BF16)
    v_cur = pb_ref[rows,H_VA:H_VS]
    v_swap = pb_ref[rows,H_VS:H_VG]
    v_parts = (v_cur * lo_bf, v_swap * hi_bf, v_swap * lo_bf, v_cur * hi_bf)
    for n, part in enumerate(v_parts):
        g, half = divmod(n, 2)
        vbd[g, 2 * half * BLK:(2 * half + 1) * BLK, 0:KV_WIDTH] = v_prev[n]
        vbd[g, (2 * half + 1) * BLK:(2 * half + 2) * BLK, 0:KV_WIDTH] = part
    chunks_per_kv = ATTN_HEADS // ATTN_KV_HEADS // 2
    for g in range(ATTN_KV_HEADS):
        c_first = g * chunks_per_kv
        qs = jnp.concatenate(
            [pb_ref[rows,H_QA + (c_first + c) * LANES:H_QA + (c_first + c + 1) * LANES]
             for c in range(chunks_per_kv)], axis=0)
        mb = jnp.concatenate([mb_scr[table, c_first + c] for c in range(chunks_per_kv)], axis=0)
        s = jnp.dot(qs, kbd[g], preferred_element_type=F32) + mb
        prob_rows, sink_rows = [], []
        for c in range(chunks_per_kv):
            probs, sinks = [], []
            for e in range(2):
                se = s[c * BLK:(c + 1) * BLK, e * 2 * BLK:(e + 1) * 2 * BLK]
                sk = sink_ref[2 * (c_first + c) + e]
                m = jnp.maximum(jnp.max(se, axis=-1, keepdims=True), sk)
                probs.append(jnp.exp(se - m).astype(BF16))
                sinks.append(jnp.exp(sk - m))
            prob_rows.append(jnp.concatenate(probs, axis=1))
            sink_rows.append(jnp.where(lo1, sinks[0], sinks[1]))
        o_den = jnp.dot(jnp.concatenate(prob_rows, axis=0), vbd[g], preferred_element_type=F32)
        o = o_den[:, :KV_WIDTH] / (o_den[:, KV_WIDTH:] + jnp.concatenate(sink_rows, axis=0))
        for c in range(chunks_per_kv):
            omix_ref[rows,(c_first + c) * LANES:(c_first + c + 1) * LANES] = (
                o[c * BLK:(c + 1) * BLK].astype(omix_ref.dtype))
    return k_t, v_parts


def _mixer_gla(state, p_ref, pb_ref, r0, lev_ref, gn_ref, omix_ref):
    new_state = []
    pairs = range(len(state))
    rows = slice(r0, r0 + BLK)
    lo1 = lax.broadcasted_iota(jnp.int32, (BLK, LANES), 1) < HEAD_DIM
    lo_bf = jnp.where(lo1, 1.0, 0.0).astype(BF16)
    hi_bf = jnp.where(lo1, 0.0, 1.0).astype(BF16)
    row = lax.broadcasted_iota(jnp.int32, (BLK, LANES), 0)
    zero_blk = jnp.zeros((BLK, LANES), BF16)
    for c in pairs:
        c0 = c * LANES
        q_at = lambda a, z: p_ref[r0 + a:r0 + z, F_QG + c0:F_QG + c0 + LANES]
        k_at = lambda a, z: p_ref[r0 + a:r0 + z, F_KG + c0:F_KG + c0 + LANES]
        b_at = lambda a, z: p_ref[r0 + a:r0 + z, F_LA + c0:F_LA + c0 + LANES]

        def pair_scores(qtb, ktb):
            rhs = jnp.concatenate([ktb * lo_bf, ktb * hi_bf], axis=0)
            return _nt_dot(qtb, rhs)

        s0 = pair_scores(q_at(0, BLK).astype(BF16), k_at(0, BLK).astype(BF16))
        sc = [jnp.where(lev_ref[...] == 0, s0[:, e * BLK:(e + 1) * BLK], 0.0) for e in range(2)]
        for level in range(1, N_LEVELS + 1):
            m = 1 << (level - 1)
            if m >= SUBLANES:
                qs, ks = [], []
                zeros = jnp.zeros((m, LANES), BF16)
                for g in range(BLK // (2 * m)):
                    lo_a, up_a, up_z = g * 2 * m, g * 2 * m + m, (g + 1) * 2 * m
                    rb = jnp.broadcast_to(b_at(up_a - 1, up_a), (m, LANES))
                    qs += [zeros, (q_at(up_a, up_z) * jnp.exp(b_at(up_a, up_z) - rb)).astype(BF16)]
                    ks += [(k_at(lo_a, up_a) * jnp.exp(rb - b_at(lo_a, up_a))).astype(BF16), zeros]
                qtb = jnp.concatenate(qs, axis=0)
                ktb = jnp.concatenate(ks, axis=0)
            else:
                bc = b_at(0, BLK)
                d = bc - _boundary_rows(p_ref, r0, F_LA + c0, bc, row, level)
                upper = ((row >> (level - 1)) & 1) == 1
                qtb = (q_at(0, BLK) * jnp.exp(jnp.where(upper, d, NEG))).astype(BF16)
                ktb = (k_at(0, BLK) * jnp.exp(jnp.where(upper, NEG, -d))).astype(BF16)
            sl = pair_scores(qtb, ktb)
            sc = [jnp.where(lev_ref[...] == level, sl[:, e * BLK:(e + 1) * BLK], sc[e])
                  for e in range(2)]
        sc = jnp.concatenate(sc, axis=1)
        qc, kc, bc = q_at(0, BLK), k_at(0, BLK), b_at(0, BLK)

        b_last = bc[BLK - 1:BLK, :]
        v0 = pb_ref[rows,H_VG + 2 * c0:H_VG + 2 * c0 + LANES]
        v1 = pb_ref[rows,H_VG + 2 * c0 + LANES:H_VG + 2 * c0 + 2 * LANES]
        v_bd = jnp.concatenate([jnp.concatenate([v0, zero_blk], axis=1),
                                jnp.concatenate([zero_blk, v1], axis=1)], axis=0)
        st_c = state[c]
        stb = st_c.astype(BF16)
        st_rhs = jnp.concatenate([stb * lo_bf, stb * hi_bf], axis=0)
        o = (jnp.dot(sc.astype(BF16), v_bd, preferred_element_type=F32)
             + _nt_dot((qc * jnp.exp(bc)).astype(BF16), st_rhs))
        kd = (kc * jnp.exp(b_last - bc)).astype(BF16)
        upd = _tn_dot(jnp.concatenate([v0, v1], axis=1), kd)
        new_state.append(st_c * jnp.exp(b_last) + jnp.where(lo1, upd[:BLK], upd[BLK:]))
        for e in range(2):
            h = 2 * c + e
            oh = o[:, e * LANES:(e + 1) * LANES]
            og = oh * lax.rsqrt(jnp.mean(oh * oh, axis=-1, keepdims=True) + EPS) * gn_ref[...]
            rg = p_ref[rows,F_RG + h * GLA_DV:F_RG + (h + 1) * GLA_DV]
            gated = og * (rg * _sigmoid(rg))
            omix_ref[rows,ATTN_WIDTH + h * GLA_DV:ATTN_WIDTH + (h + 1) * GLA_DV] = (
                gated.astype(omix_ref.dtype))
    return new_state


def _prompt_mixer_kernel(relb_ref, sink_ref, p_ref, pb_ref, kt_ref, bucket_ref, lev_ref, gn_ref,
                         omix_ref, st_ref, kbd, vbd, kprev_t, vprev, st_scr, mb_scr):
    i = pl.program_id(0)

    @pl.when(i == 0)
    def _init():
        _mixer_init(relb_ref, bucket_ref, kbd, vbd, kprev_t, vprev, st_scr, mb_scr)

    k_prev_t = kprev_t[...]
    v_prev = [vprev[n] for n in range(2 * ATTN_KV_HEADS)]
    state = [st_scr[:, c * LANES:(c + 1) * LANES] for c in range(GLA_HEADS // 2)]
    n_blocks = p_ref.shape[0] // BLK
    for jb in range(n_blocks):
        table = jnp.where(i == 0, 1, 0) if jb == 0 else 0
        k_prev_t, v_prev = _mixer_attention(pb_ref, kt_ref, jb * BLK, table, sink_ref, omix_ref,
                                            kbd.at[jb], vbd.at[jb], k_prev_t, v_prev, mb_scr)
        state = _mixer_gla(state, p_ref, pb_ref, jb * BLK, lev_ref, gn_ref, omix_ref)
    kprev_t[...] = k_prev_t
    for n, part in enumerate(v_prev):
        vprev[n] = part
    for c, st_c in enumerate(state):
        st_scr[:, c * LANES:(c + 1) * LANES] = st_c
        st_ref[:, c * LANES:(c + 1) * LANES] = st_c


def _prompt_mixer(p, pb, kt, relb, sinks, gn):
    t = p.shape[0]
    smem = pl.BlockSpec(memory_space=pltpu.SMEM)
    return pl.pallas_call(
        _prompt_mixer_kernel,
        grid=(t // MIX_ROWS,),
        in_specs=[
            smem, smem,
            pl.BlockSpec((MIX_ROWS, F_WIDTH), lambda i: (i, 0)),
            pl.BlockSpec((MIX_ROWS, H_WIDTH), lambda i: (i, 0)),
            pl.BlockSpec((KV_WIDTH, MIX_ROWS), lambda i: (0, i)),
            _const_spec((2, BLK, 2 * BLK)),
            _const_spec((BLK, BLK)),
            _const_spec((1, GLA_DV)),
        ],
        out_specs=[
            pl.BlockSpec((MIX_ROWS, MIX_WIDTH), lambda i: (i, 0)),
            pl.BlockSpec((GLA_DV, GLA_QK_WIDTH), lambda i: (0, 0)),
        ],
        out_shape=[
            jax.ShapeDtypeStruct((t, MIX_WIDTH), BF16),
            jax.ShapeDtypeStruct((GLA_DV, GLA_QK_WIDTH), F32),
        ],
        scratch_shapes=[
            pltpu.VMEM((MIX_ROWS // BLK, ATTN_KV_HEADS, 2 * HEAD_DIM, 4 * BLK), BF16),
            pltpu.VMEM((MIX_ROWS // BLK, ATTN_KV_HEADS, 4 * BLK, 2 * KV_WIDTH), BF16),
            pltpu.VMEM((KV_WIDTH, BLK), BF16),
            pltpu.VMEM((2 * ATTN_KV_HEADS, BLK, KV_WIDTH), BF16),
            pltpu.VMEM((GLA_DV, GLA_QK_WIDTH), F32),
            pltpu.VMEM((2, ATTN_HEADS // 2, BLK, 4 * BLK), F32),
        ],
        compiler_params=pltpu.CompilerParams(
            dimension_semantics=("arbitrary",), vmem_limit_bytes=VMEM_LIMIT),
        name="prompt_mixer",
    )(relb, sinks, p, pb, kt, jnp.asarray(_BUCKET_PROMPT), jnp.asarray(_LEV), gn)


def _sample_mixer_kernel(ps_ref, ph_ref, kvfull_ref, pfull_ref, ck_ref, cv_ref, st_ref, relbt_ref,
                         sink_ref, bucket_ref,
                         gn_ref, omix_ref, kwin_ref, vwin_ref, stout_ref, lat_scr, kqt_scr, kvt_scr,
                         bias_scr, s_scr, o_scr, og_scr):
    i = pl.program_id(0)
    nb = pfull_ref.shape[0]

    @pl.when(i == 0)
    def _init():
        lat_scr[...] = _split3_rows(pfull_ref[:, F_LA:F_WIDTH].T)
        kqt_scr[0:GLA_QK_WIDTH] = pfull_ref[:, F_KG:F_RG].T.astype(BF16)
        kqt_scr[GLA_QK_WIDTH:2 * GLA_QK_WIDTH] = pfull_ref[:, F_QG:F_KG].T.astype(BF16)
        kvt = kvfull_ref[...].T
        for s in range(nb // SAMPLE_BLK):
            shift = nb - (s + 1) * SAMPLE_BLK
            kvt_scr[s] = pltpu.roll(kvt, shift, 1) if shift else kvt
        bk = jnp.broadcast_to(bucket_ref[...], (ATTN_HEADS, WINDOW))
        acc = jnp.zeros((ATTN_HEADS, WINDOW), F32)
        for b in range(N_BUCKETS):
            acc = jnp.where(bk == b, relbt_ref[:, b:b + 1], acc)
        bias_scr[...] = acc

    lo = lax.broadcasted_iota(jnp.int32, (1, LANES), 1) < HEAD_DIM
    sub = lax.broadcasted_iota(jnp.int32, (ATTN_HEADS, LANES), 0)
    newest = lax.broadcasted_iota(jnp.int32, (KV_WIDTH, WINDOW), 1) == WINDOW - 1
    heads_per_kv = ATTN_HEADS // ATTN_KV_HEADS

    n_of_col = i * SAMPLE_BLK + lax.broadcasted_iota(jnp.int32, (nb, SAMPLE_BLK * LANES), 1) // LANES
    pick = jnp.where(lax.broadcasted_iota(jnp.int32, (nb, SAMPLE_BLK * LANES), 0) == n_of_col,
                     1.0, 0.0).astype(BF16)
    la_b = (jnp.dot(lat_scr[0:GLA_QK_WIDTH], pick, preferred_element_type=F32)
            + jnp.dot(lat_scr[GLA_QK_WIDTH:2 * GLA_QK_WIDTH], pick, preferred_element_type=F32)
            + jnp.dot(lat_scr[2 * GLA_QK_WIDTH:3 * GLA_QK_WIDTH], pick, preferred_element_type=F32))
    kq_b = jnp.dot(kqt_scr[...], pick, preferred_element_type=F32)

    for j in range(SAMPLE_BLK):
        kv_new = kvt_scr[i]
        if j < SAMPLE_BLK - 1:
            kv_new = pltpu.roll(kv_new, SAMPLE_BLK - 1 - j, 1)
        kwin_ref[j] = jnp.where(newest, kv_new[0:KV_WIDTH], pltpu.roll(ck_ref[j], WINDOW - 1, 1))
        vwin_ref[j] = jnp.where(newest, kv_new[KV_WIDTH:2 * KV_WIDTH],
                                pltpu.roll(cv_ref[j], WINDOW - 1, 1))

    for j in range(SAMPLE_BLK):
        qexp = jnp.zeros((ATTN_HEADS, LANES), F32)
        for c in range(ATTN_HEADS // 2):
            chunk = ph_ref[j:j + 1, H_QA + c * LANES:H_QA + (c + 1) * LANES]
            swapped = pltpu.roll(chunk, HEAD_DIM, 1)
            if (2 * c) // heads_per_kv == 0:
                rows = (jnp.where(lo, chunk, 0.0), jnp.where(lo, swapped, 0.0))
            else:
                rows = (jnp.where(lo, 0.0, swapped), jnp.where(lo, 0.0, chunk))
            for e in range(2):
                qexp = jnp.where(sub == 2 * c + e, rows[e], qexp)
        s_scr[j * ATTN_HEADS:(j + 1) * ATTN_HEADS] = jnp.dot(
            qexp.astype(BF16), kwin_ref[j].astype(BF16), preferred_element_type=F32)

    tile = lambda x: jnp.concatenate([x] * SAMPLE_BLK, axis=0)
    sink = tile(sink_ref[...])
    s = s_scr[...] + tile(bias_scr[...])
    m = jnp.maximum(jnp.max(s, axis=-1, keepdims=True), sink)
    pe = jnp.exp(s - m)
    inv_den = 1.0 / (jnp.sum(pe, axis=-1, keepdims=True) + jnp.exp(sink - m))
    peb = pe.astype(BF16)
    for j in range(SAMPLE_BLK):
        o_scr[j * ATTN_HEADS:(j + 1) * ATTN_HEADS] = _nt_dot(
            peb[j * ATTN_HEADS:(j + 1) * ATTN_HEADS], vwin_ref[j].astype(BF16))
    o_all = o_scr[...] * inv_den
    o_swap = pltpu.roll(o_all, HEAD_DIM, 1)
    for j in range(SAMPLE_BLK):
        r = j * ATTN_HEADS
        for c in range(ATTN_HEADS // 2):
            if (2 * c) // heads_per_kv == 0:
                piece = jnp.where(lo, o_all[r + 2 * c:r + 2 * c + 1, :], o_swap[r + 2 * c + 1:r + 2 * c + 2, :])
            else:
                piece = jnp.where(lo, o_swap[r + 2 * c:r + 2 * c + 1, :], o_all[r + 2 * c + 1:r + 2 * c + 2, :])
            omix_ref[j:j + 1, c * LANES:(c + 1) * LANES] = piece

    for j in range(SAMPLE_BLK):
        cols = slice(j * LANES, (j + 1) * LANES)
        for h in range(GLA_HEADS):
            rs = slice(h * GLA_DK, (h + 1) * GLA_DK)
            qs = slice(GLA_QK_WIDTH + h * GLA_DK, GLA_QK_WIDTH + (h + 1) * GLA_DK)
            v_row = ph_ref[j:j + 1, H_VG + h * GLA_DV:H_VG + (h + 1) * GLA_DV]
            s_new = jnp.exp(la_b[rs, cols]) * st_ref[j, h] + kq_b[rs, cols] * v_row
            stout_ref[j, h] = s_new
            og_scr[j:j + 1, h * GLA_DV:(h + 1) * GLA_DV] = jnp.sum(
                kq_b[qs, cols] * s_new, axis=0, keepdims=True)
    for h in range(GLA_HEADS):
        hs = slice(h * GLA_DV, (h + 1) * GLA_DV)
        og = og_scr[:, hs]
        og = og * lax.rsqrt(jnp.mean(og * og, axis=-1, keepdims=True) + EPS) * gn_ref[...]
        rg = ps_ref[:, F_RG + h * GLA_DV:F_RG + (h + 1) * GLA_DV]
        omix_ref[:, ATTN_WIDTH + h * GLA_DV:ATTN_WIDTH + (h + 1) * GLA_DV] = og * (rg * _sigmoid(rg))


def _sample_mixer(ps, ph, kv, cache_k, cache_v, state, relbt, sinks_col, gn):
    nb = ps.shape[0]
    assert nb == LANES
    blk3 = lambda i: (i, 0, 0)
    blk4 = lambda i: (i, 0, 0, 0)
    return pl.pallas_call(
        _sample_mixer_kernel,
        grid=(nb // SAMPLE_BLK,),
        in_specs=[
            pl.BlockSpec((SAMPLE_BLK, F_WIDTH), lambda i: (i, 0)),
            pl.BlockSpec((SAMPLE_BLK, H_WIDTH), lambda i: (i, 0)),
            _const_spec((nb, 2 * KV_WIDTH)),
            _const_spec((nb, F_WIDTH)),
            pl.BlockSpec((SAMPLE_BLK, KV_WIDTH, WINDOW), blk3),
            pl.BlockSpec((SAMPLE_BLK, KV_WIDTH, WINDOW), blk3),
            pl.BlockSpec((SAMPLE_BLK, GLA_HEADS, GLA_DK, GLA_DV), blk4),
            _const_spec((ATTN_HEADS, N_BUCKETS)),
            _const_spec((ATTN_HEADS, 1)),
            _const_spec((1, WINDOW)),
            _const_spec((1, GLA_DV)),
        ],
        out_specs=[
            pl.BlockSpec((SAMPLE_BLK, MIX_WIDTH), lambda i: (i, 0)),
            pl.BlockSpec((SAMPLE_BLK, KV_WIDTH, WINDOW), blk3),
            pl.BlockSpec((SAMPLE_BLK, KV_WIDTH, WINDOW), blk3),
            pl.BlockSpec((SAMPLE_BLK, GLA_HEADS, GLA_DK, GLA_DV), blk4),
        ],
        out_shape=[
            jax.ShapeDtypeStruct((nb, MIX_WIDTH), F32),
            jax.ShapeDtypeStruct((nb, KV_WIDTH, WINDOW), F32),
            jax.ShapeDtypeStruct((nb, KV_WIDTH, WINDOW), F32),
            jax.ShapeDtypeStruct((nb, GLA_HEADS, GLA_DK, GLA_DV), F32),
        ],
        scratch_shapes=[
            pltpu.VMEM((3 * GLA_QK_WIDTH, nb), BF16),
            pltpu.VMEM((2 * GLA_QK_WIDTH, nb), BF16),
            pltpu.VMEM((nb // SAMPLE_BLK, 2 * KV_WIDTH, nb), F32),
            pltpu.VMEM((ATTN_HEADS, WINDOW), F32),
            pltpu.VMEM((SAMPLE_BLK * ATTN_HEADS, WINDOW), F32),
            pltpu.VMEM((SAMPLE_BLK * ATTN_HEADS, KV_WIDTH), F32),
            pltpu.VMEM((SAMPLE_BLK, GLA_WIDTH), F32),
        ],
        compiler_params=pltpu.CompilerParams(
            dimension_semantics=("arbitrary",), vmem_limit_bytes=VMEM_LIMIT),
        name="sample_mixer",
    )(ps, ph, kv, ps, cache_k, cache_v, state, relbt, sinks_col, jnp.asarray(_BUCKET_SAMPLE), gn)


def _finish_kernel(x_ref, mix_ref, xs_ref, mixs_ref, wo_ref, g_ref, wg_ref, wu_ref, wd_ref,
                   y_ref, ys_ref, *, ff_chunks):
    weights = (wo_ref, g_ref, wg_ref, wu_ref, wd_ref)
    _finish_rows(x_ref, mix_ref, *weights, y_ref, ff_chunks=ff_chunks)

    @pl.when(pl.program_id(0) == pl.num_programs(0) - 1)
    def _samples():
        _finish_rows(xs_ref, mixs_ref, *weights, ys_ref, ff_chunks=ff_chunks)


def _finish_rows(x_ref, mix_ref, wo_ref, g_ref, wg_ref, wu_ref, wd_ref, y_ref, *, ff_chunks):
    h = x_ref[...] + jnp.dot(mix_ref[...].astype(BF16), wo_ref[...], preferred_element_type=F32)
    r = lax.rsqrt(jnp.mean(h * h, axis=-1, keepdims=True) + EPS)
    z = (h * g_ref[...]).astype(BF16)
    n_tiles = wd_ref.shape[0] // MXU_TILE
    acc = h
    for c in range(ff_chunks):
        c0 = ((c * n_tiles) // ff_chunks) * MXU_TILE
        c1 = (((c + 1) * n_tiles) // ff_chunks) * MXU_TILE
        gate = jnp.dot(z, wg_ref[:, c0:c1], preferred_element_type=F32) * r
        up = jnp.dot(z, wu_ref[:, c0:c1], preferred_element_type=F32) * r
        act = ((gate * _sigmoid(gate)) * up).astype(BF16)
        acc = acc + jnp.dot(act, wd_ref[c0:c1, :], preferred_element_type=F32)
    y_ref[...] = acc


def _finish(x, mix, xs, mixs, wo, g_ffn, wg, wu, wd, rows):
    t = x.shape[0]
    ns = xs.shape[0]
    d_ff = wd.shape[0]
    assert d_ff % MXU_TILE == 0
    return pl.pallas_call(
        functools.partial(_finish_kernel, ff_chunks=FF_CHUNKS),
        grid=(t // rows,),
        in_specs=[
            pl.BlockSpec((rows, D_MODEL), lambda i: (i, 0)),
            pl.BlockSpec((rows, MIX_WIDTH), lambda i: (i, 0)),
            _const_spec((ns, D_MODEL)),
            _const_spec((ns, MIX_WIDTH)),
            _const_spec((MIX_WIDTH, D_MODEL)),
            _const_spec((1, D_MODEL)),
            _const_spec((D_MODEL, d_ff)),
            _const_spec((D_MODEL, d_ff)),
            _const_spec((d_ff, D_MODEL)),
        ],
        out_specs=[pl.BlockSpec((rows, D_MODEL), lambda i: (i, 0)),
                   pl.BlockSpec((ns, D_MODEL), lambda i: (0, 0))],
        out_shape=[jax.ShapeDtypeStruct((t, D_MODEL), F32),
                   jax.ShapeDtypeStruct((ns, D_MODEL), F32)],
        compiler_params=pltpu.CompilerParams(
            dimension_semantics=("arbitrary",), vmem_limit_bytes=VMEM_LIMIT),
        name="finish",
    )(x, mix, xs, mixs, wo, g_ffn, wg, wu, wd)


PROMPT_ROWS = 1024
PROJ_ROWS = 1024
MIX_ROWS = 512


def kernel(x_prompt, x_sample, cache_k, cache_v, state_gla, attn_norm_g, w_in, q_norm_g, k_norm_g,
           attn_sinks, rel_bias, w_gla_gate2, b_gla_gate, gla_norm_g, w_o, ffn_norm_g, w_gate, w_up,
           w_down):
    depth = w_in.shape[0]
    batch, seq, _ = x_prompt.shape
    dec_batch, dec_seq, _ = x_sample.shape
    wb = cache_k.shape[2]
    assert batch == 1 and dec_seq == 1 and wb == WINDOW
    assert seq % PROMPT_ROWS == 0 and seq % PROJ_ROWS == 0
    assert dec_batch % SAMPLE_BLK == 0 and dec_batch % LANES == 0
    assert rel_bias.shape == (N_BUCKETS, ATTN_HEADS)

    xp = x_prompt.reshape(seq, D_MODEL)
    xs = x_sample.reshape(dec_batch, D_MODEL)
    relb_flat = rel_bias.reshape(-1)
    relb_t = rel_bias.T
    outs = ([], [], [], [], [], [])
    for l in range(depth):
        kv_end = ATTN_WIDTH + 2 * KV_WIDTH
        w_in_p = jnp.concatenate(
            [w_in[l, :, :kv_end],
             jnp.pad(w_in[l, :, MAIN_WIDTH:], ((0, 0), (0, RANK_PAD - GLA_RANK))),
             w_in[l, :, kv_end:MAIN_WIDTH]], axis=1).astype(BF16)
        w2p = jnp.pad(w_gla_gate2[l], ((0, RANK_PAD - GLA_RANK), (0, 0))).astype(BF16)
        proj_w = (attn_norm_g[l][None, :], w_in_p, jnp.tile(q_norm_g[l], ATTN_HEADS)[None, :],
                  jnp.tile(k_norm_g[l], ATTN_KV_HEADS)[None, :], w2p, b_gla_gate[l][None, :])
        gn = gla_norm_g[l][None, :]

        pp, pb, kt_p, kv_win, ps, ph, kv_s, wo_b, wg_b, wu_b, wd_b = _project(
            xp, xs, *proj_w, rows=PROJ_ROWS, to_cast=(w_o[l], w_gate[l], w_up[l], w_down[l]))
        fin_w = (wo_b, ffn_norm_g[l][None, :], wg_b, wu_b, wd_b)
        mix_p, st_p = _prompt_mixer(pp, pb, kt_p, relb_flat, attn_sinks[l], gn)
        to_t = lambda c: jnp.transpose(c, (0, 2, 3, 1)).reshape(dec_batch, KV_WIDTH, wb)
        from_t = lambda c: jnp.transpose(c.reshape(dec_batch, ATTN_KV_HEADS, HEAD_DIM, wb), (0, 3, 1, 2))
        mix_s, kwin_t, vwin_t, st_s = _sample_mixer(
            ps, ph, kv_s, to_t(cache_k[l]), to_t(cache_v[l]),
            state_gla[l].astype(F32), relb_t, attn_sinks[l][:, None], gn)
        xp_in = xp
        xp, xs = _finish(xp_in, mix_p, xs, mix_s, *fin_w, rows=PROMPT_ROWS)
        outs[0].append(kv_win[:, :KV_WIDTH].reshape(batch, wb, ATTN_KV_HEADS, HEAD_DIM))
        outs[1].append(kv_win[:, KV_WIDTH:].reshape(batch, wb, ATTN_KV_HEADS, HEAD_DIM))
        outs[2].append(st_p.T.reshape(batch, GLA_HEADS, GLA_DK, GLA_DV).astype(state_gla.dtype))
        outs[3].append(from_t(kwin_t))
        outs[4].append(from_t(vwin_t))
        outs[5].append(st_s.astype(state_gla.dtype))

    y_prompt = xp.reshape(batch, seq, D_MODEL)
    y_sample = xs.reshape(dec_batch, dec_seq, D_MODEL)
    return (y_prompt, y_sample) + tuple(jnp.stack(o) for o in outs)
```

```python
import functools
import math

import numpy as np
import jax
import jax.numpy as jnp
from jax import lax
from jax.experimental import pallas as pl
from jax.experimental.pallas import tpu as pltpu

F32 = jnp.float32
BF16 = jnp.bfloat16

D_MODEL = 1024
HEAD_DIM = 64
ATTN_HEADS = 8
ATTN_KV_HEADS = 2
WINDOW = 128
N_BUCKETS = 32
MAX_DISTANCE = 128
GLA_HEADS = 4
GLA_DK = 64
GLA_DV = 128
GLA_RANK = 16
GLA_TAU = 16.0
EPS = 1e-6
ATTN_WIDTH = ATTN_HEADS * HEAD_DIM
KV_WIDTH = ATTN_KV_HEADS * HEAD_DIM
GLA_QK_WIDTH = GLA_HEADS * GLA_DK
GLA_WIDTH = GLA_HEADS * GLA_DV
MIX_WIDTH = ATTN_WIDTH + GLA_WIDTH
MAIN_WIDTH = ATTN_WIDTH + 2 * KV_WIDTH + 2 * GLA_QK_WIDTH + 2 * GLA_WIDTH
LANES = 128
SUBLANES = 8
MXU_TILE = 256
FF_CHUNKS = 4
RANK_PAD = LANES
IN_PAD_WIDTH = MAIN_WIDTH + RANK_PAD

F_QG = 0
F_KG = F_QG + GLA_QK_WIDTH
F_RG = F_KG + GLA_QK_WIDTH
F_LA = F_RG + GLA_WIDTH
F_WIDTH = F_LA + GLA_QK_WIDTH
H_QA = 0
H_VA = H_QA + ATTN_WIDTH
H_VS = H_VA + KV_WIDTH
H_VG = H_VS + KV_WIDTH
H_WIDTH = H_VG + GLA_WIDTH

W_QA = 0
W_KV = W_QA + ATTN_WIDTH
W_QG = W_KV + 2 * KV_WIDTH + RANK_PAD
W_VG = W_QG + 2 * GLA_QK_WIDTH
W_RG = W_VG + GLA_WIDTH

BLK = 128
N_LEVELS = 7
NEG = -1e30
ATTN_SCALE = HEAD_DIM ** -0.5
SAMPLE_BLK = 8
VMEM_LIMIT = 56 * 1024 * 1024


def _t5_bucket_np(dist):
    n = np.maximum(dist, 0)
    max_exact = N_BUCKETS // 2
    nf = np.maximum(n, 1).astype(np.float64)
    large = max_exact + (np.log(nf / max_exact) / math.log(MAX_DISTANCE / max_exact)
                         * (N_BUCKETS - max_exact)).astype(np.int32)
    large = np.minimum(large, N_BUCKETS - 1)
    return np.where(n < max_exact, n, large).astype(np.int32)


def _prompt_bucket_tables():
    assert WINDOW == BLK
    i = np.arange(BLK)[:, None]
    j = np.arange(BLK)[None, :]
    own = j <= i
    bucket = _t5_bucket_np(np.where(own, i - j, BLK + i - j))
    t1 = np.where(own, bucket, -1)
    return np.stack([bucket, t1]).astype(np.int32)


def _level_tables():
    t = np.arange(BLK)[:, None]
    s = np.arange(BLK)[None, :]
    x = t ^ s
    lev = np.where(x > 0, np.floor(np.log2(np.maximum(x, 1))).astype(np.int32) + 1, 0)
    lev = np.where(s > t, -1, lev).astype(np.int32)
    tri = (s <= t).astype(np.float32)
    return lev, np.concatenate([tri, tri, tri], axis=1)


_BUCKET_PROMPT = _prompt_bucket_tables()
_LEV, _TRI3 = _level_tables()
_BUCKET_SAMPLE = _t5_bucket_np((WINDOW - 1) - np.arange(WINDOW))[None, :].astype(np.int32)


def _nt_dot(a, b):
    return lax.dot_general(a, b, (((1,), (1,)), ((), ())), preferred_element_type=F32)


def _tn_dot(a, b):
    return lax.dot_general(a, b, (((0,), (0,)), ((), ())), preferred_element_type=F32)


def _head_mean_sq(x):
    lo = lax.broadcasted_iota(jnp.int32, (x.shape[0], LANES), 1) < HEAD_DIM
    outs = []
    for c in range(x.shape[1] // LANES):
        y = x[:, c * LANES:(c + 1) * LANES]
        y = y * y
        s_lo = jnp.sum(jnp.where(lo, y, 0.0), axis=-1, keepdims=True)
        s_hi = jnp.sum(jnp.where(lo, 0.0, y), axis=-1, keepdims=True)
        outs.append(jnp.where(lo, s_lo, s_hi) * (1.0 / HEAD_DIM))
    return outs[0] if len(outs) == 1 else jnp.concatenate(outs, axis=1)


def _sigmoid(x):
    return 1.0 / (1.0 + jnp.exp(-x))


def _split3_rows(x):
    hi = x.astype(BF16)
    r1 = x - hi.astype(F32)
    mid = r1.astype(BF16)
    lo = (r1 - mid.astype(F32)).astype(BF16)
    return jnp.concatenate([hi, mid, lo], axis=0)


def _proj_kernel(x_ref, xs_ref, g_ref, w_ref, qn_ref, kn_ref, w2_ref, b2_ref, tri_ref, *rest):
    n_out = 7
    n_cast = (len(rest) - n_out) // 2
    cast_in, cast_out = rest[:n_cast], rest[n_cast + n_out:]
    f32_ref, bf_ref, kt_ref, kvw_ref, f32s_ref, bfs_ref, kvs_ref = rest[n_cast:n_cast + n_out]
    for src, dst in zip(cast_in, cast_out):
        dst[...] = src[...].astype(BF16)
    weights = (g_ref, w_ref, qn_ref, kn_ref, w2_ref, b2_ref, tri_ref)
    _proj_rows(x_ref, *weights, f32_ref, bf_ref, kt_ref, kvw_ref, block_cumsum=True)

    @pl.when(pl.program_id(0) == pl.num_programs(0) - 1)
    def _samples():
        _proj_rows(xs_ref, *weights, f32s_ref, bfs_ref, None, kvs_ref, block_cumsum=False)


def _proj_rows(x_ref, g_ref, w_ref, qn_ref, kn_ref, w2_ref, b2_ref, tri_ref, f32_ref, bf_ref,
               kt_ref, kv_ref, *, block_cumsum):
    x = x_ref[...]
    r = lax.rsqrt(jnp.mean(x * x, axis=-1, keepdims=True) + EPS)
    n = (x * g_ref[...]).astype(BF16)

    def seg(c0, c1):
        return jnp.dot(n, w_ref[:, c0:c1], preferred_element_type=F32) * r

    kvl = seg(W_KV, W_QG)
    lr = kvl[:, 2 * KV_WIDTH:].astype(BF16)
    z = jnp.dot(lr, w2_ref[...], preferred_element_type=F32) + b2_ref[...]
    log_a = (jnp.minimum(z, 0.0) - jnp.log1p(jnp.exp(-jnp.abs(z)))) / GLA_TAU
    if block_cumsum:
        for blk in range(x.shape[0] // BLK):
            rows = slice(blk * BLK, (blk + 1) * BLK)
            f32_ref[rows, F_LA:F_WIDTH] = jnp.dot(
                tri_ref[...], _split3_rows(log_a[rows]), preferred_element_type=F32)
    else:
        f32_ref[:, F_LA:F_WIDTH] = log_a
    k = kvl[:, :KV_WIDTH]
    k = k * lax.rsqrt(_head_mean_sq(k) + EPS) * kn_ref[...]
    v = kvl[:, KV_WIDTH:2 * KV_WIDTH]
    n_kv = kv_ref.shape[0]
    kv_ref[:, 0:KV_WIDTH] = k[x.shape[0] - n_kv:]
    kv_ref[:, KV_WIDTH:2 * KV_WIDTH] = v[x.shape[0] - n_kv:]
    if kt_ref is not None:
        kt_ref[...] = k.T.astype(BF16)
    hdt = bf_ref.dtype
    bf_ref[:, H_VA:H_VS] = v.astype(hdt)
    bf_ref[:, H_VS:H_VG] = pltpu.roll(v, HEAD_DIM, 1).astype(hdt)
    q = seg(W_QA, W_KV)
    bf_ref[:, H_QA:H_VA] = (q * lax.rsqrt(_head_mean_sq(q) + EPS) * qn_ref[...] * ATTN_SCALE
                            ).astype(hdt)
    qk_g = seg(W_QG, W_VG)
    f32_ref[:, F_QG:F_KG] = qk_g[:, :GLA_QK_WIDTH] * (GLA_DK ** -0.5)
    f32_ref[:, F_KG:F_RG] = qk_g[:, GLA_QK_WIDTH:]
    bf_ref[:, H_VG:H_WIDTH] = seg(W_VG, W_RG).astype(hdt)
    f32_ref[:, F_RG:F_LA] = seg(W_RG, IN_PAD_WIDTH)


def _const_spec(shape):
    nd = len(shape)
    return pl.BlockSpec(shape, lambda i: (0,) * nd, pipeline_mode=pl.Buffered(1))


def _project(x, xs, g_attn, w_in_p, qn, kn, w2p, b2, rows, to_cast):
    t = x.shape[0]
    ns = xs.shape[0]
    steps = t // rows
    cast_specs = []
    for w in to_cast:
        slab = w.shape[0] // steps
        assert w.shape[0] % steps == 0 and slab % (2 * SUBLANES) == 0
        cast_specs.append(pl.BlockSpec((slab, w.shape[1]), lambda i: (i, 0)))
    return pl.pallas_call(
        _proj_kernel,
        grid=(steps,),
        in_specs=[
            pl.BlockSpec((rows, D_MODEL), lambda i: (i, 0)),
            _const_spec((ns, D_MODEL)),
            _const_spec((1, D_MODEL)),
            _const_spec((D_MODEL, IN_PAD_WIDTH)),
            _const_spec((1, ATTN_WIDTH)),
            _const_spec((1, KV_WIDTH)),
            _const_spec((RANK_PAD, GLA_QK_WIDTH)),
            _const_spec((1, GLA_QK_WIDTH)),
            _const_spec((BLK, 3 * BLK)),
        ] + cast_specs,
        out_specs=[pl.BlockSpec((rows, F_WIDTH), lambda i: (i, 0)),
                   pl.BlockSpec((rows, H_WIDTH), lambda i: (i, 0)),
                   pl.BlockSpec((KV_WIDTH, rows), lambda i: (0, i)),
                   pl.BlockSpec((WINDOW, 2 * KV_WIDTH), lambda i: (0, 0)),
                   pl.BlockSpec((ns, F_WIDTH), lambda i: (0, 0)),
                   pl.BlockSpec((ns, H_WIDTH), lambda i: (0, 0)),
                   pl.BlockSpec((ns, 2 * KV_WIDTH), lambda i: (0, 0))] + cast_specs,
        out_shape=[jax.ShapeDtypeStruct((t, F_WIDTH), F32),
                   jax.ShapeDtypeStruct((t, H_WIDTH), BF16),
                   jax.ShapeDtypeStruct((KV_WIDTH, t), BF16),
                   jax.ShapeDtypeStruct((WINDOW, 2 * KV_WIDTH), F32),
                   jax.ShapeDtypeStruct((ns, F_WIDTH), F32),
                   jax.ShapeDtypeStruct((ns, H_WIDTH), F32),
                   jax.ShapeDtypeStruct((ns, 2 * KV_WIDTH), F32)]
        + [jax.ShapeDtypeStruct(w.shape, BF16) for w in to_cast],
        compiler_params=pltpu.CompilerParams(
            dimension_semantics=("arbitrary",), vmem_limit_bytes=VMEM_LIMIT),
        name="proj",
    )(x, xs, g_attn, w_in_p, qn, kn, w2p, b2, jnp.asarray(_TRI3, BF16), *to_cast)


def _boundary_rows(b_ref, r0, c0, b, row, level):
    m = 1 << (level - 1)
    if 2 * m >= SUBLANES:
        pieces = [jnp.broadcast_to(b_ref[r0 + g * 2 * m + m - 1:r0 + g * 2 * m + m, c0:c0 + LANES],
                                   (2 * m, LANES))
                  for g in range(BLK // (2 * m))]
        return pieces[0] if len(pieces) == 1 else jnp.concatenate(pieces, axis=0)
    pos = row & (2 * m - 1)
    tiles = b.reshape(BLK // SUBLANES, SUBLANES, LANES)
    out = b
    for p in range(2 * m):
        shift = (m - 1) - p
        if shift != 0:
            rolled = pltpu.roll(tiles, (-shift) % SUBLANES, 1).reshape(BLK, LANES)
            out = jnp.where(pos == p, rolled, out)
    return out


def _mixer_init(relb_ref, bucket_ref, kbd, vbd, kprev_t, vprev, st_scr, mb_scr):
    kbd[...] = jnp.zeros_like(kbd)
    vbd[...] = jnp.zeros_like(vbd)
    for slot in range(vbd.shape[0]):
        for g in range(ATTN_KV_HEADS):
            for half in range(2):
                vbd[slot, g, 2 * half * BLK:(2 * half + 2) * BLK,
                    KV_WIDTH + half * HEAD_DIM:KV_WIDTH + (half + 1) * HEAD_DIM] = (
                        jnp.ones((2 * BLK, HEAD_DIM), BF16))
    kprev_t[...] = jnp.zeros_like(kprev_t)
    vprev[...] = jnp.zeros_like(vprev)
    st_scr[...] = jnp.zeros_like(st_scr)
    bk = bucket_ref[0]
    acc = [jnp.zeros(bk.shape, F32) for _ in range(ATTN_HEADS)]
    for b in range(N_BUCKETS):
        hit = bk == b
        for h in range(ATTN_HEADS):
            acc[h] = jnp.where(hit, relb_ref[b * ATTN_HEADS + h], acc[h])
    for tb in range(2):
        masked = bucket_ref[tb] < 0
        for h in range(ATTN_HEADS):
            mb_scr[tb, h // 2, :, (h % 2) * BLK:(h % 2 + 1) * BLK] = jnp.where(masked, NEG, acc[h])


def _mixer_attention(pb_ref, kt_ref, r0, table, sink_ref, omix_ref, kbd, vbd, k_prev_t, v_prev,
                     mb_scr):
    rows = slice(r0, r0 + BLK)
    lo1 = lax.broadcasted_iota(jnp.int32, (BLK, LANES), 1) < HEAD_DIM
    k_t = kt_ref[:, r0:r0 + BLK]
    for g in range(ATTN_KV_HEADS):
        hd = slice(g * HEAD_DIM, (g + 1) * HEAD_DIM)
        kbd[g, 0:HEAD_DIM, 0:BLK] = k_prev_t[hd]
        kbd[g, 0:HEAD_DIM, BLK:2 * BLK] = k_t[hd]
        kbd[g, HEAD_DIM:2 * HEAD_DIM, 2 * BLK:3 * BLK] = k_prev_t[hd]
        kbd[g, HEAD_DIM:2 * HEAD_DIM, 3 * BLK:4 * BLK] = k_t[hd]
    lo_bf = jnp.where(lo1, 1.0, 0.0).astype(BF16)
    hi_bf = jnp.where(lo1, 0.0, 1.0).astype(BF16)
    lo_row = lax.broadcasted_iota(jnp.int32, (1, LANES), 1) < HEAD_DIM
    own = (lax.broadcasted_iota(jnp.int32, (BLK, BLK), 1)
           <= lax.broadcasted_iota(jnp.int32, (BLK, BLK), 0))
    own_bf = jnp.where(own, 1.0, 0.0).astype(BF16)
    prev_bf = jnp.where(own, 0.0, 1.0).astype(BF16)
    v_cur = pb_ref[rows,H_VA:H_VS]
    v_swap = pb_ref[rows,H_VS:H_VG]
    v_parts = (v_cur * lo_bf, v_swap * hi_bf, v_swap * lo_bf, v_cur * hi_bf)
    for n, part in enumerate(v_parts):
        g, half = divmod(n, 2)
        vbd[g, 2 * half * BLK:(2 * half + 1) * BLK, 0:KV_WIDTH] = v_prev[n]
        vbd[g, (2 * half + 1) * BLK:(2 * half + 2) * BLK, 0:KV_WIDTH] = part
    chunks_per_kv = ATTN_HEADS // ATTN_KV_HEADS // 2
    for g in range(ATTN_KV_HEADS):
        c_first = g * chunks_per_kv
        qs = jnp.concatenate(
            [pb_ref[rows,H_QA + (c_first + c) * LANES:H_QA + (c_first + c + 1) * LANES]
             for c in range(chunks_per_kv)], axis=0)
        s = jnp.dot(qs, kbd[g], preferred_element_type=F32)
        prob_rows, sink_rows = [], []
        for c in range(chunks_per_kv):
            probs, maxes, sks = [], [], []
            for e in range(2):
                s_prev = s[c * BLK:(c + 1) * BLK, 2 * e * BLK:(2 * e + 1) * BLK]
                s_own = s[c * BLK:(c + 1) * BLK, (2 * e + 1) * BLK:(2 * e + 2) * BLK]
                se = (jnp.where(own, s_own, s_prev)
                      + mb_scr[table, c_first + c, :, e * BLK:(e + 1) * BLK])
                sk = sink_ref[2 * (c_first + c) + e]
                m = jnp.maximum(jnp.max(se, axis=-1, keepdims=True), sk)
                pe = jnp.exp(se - m).astype(BF16)
                probs += [pe * prev_bf, pe * own_bf]
                maxes.append(m)
                sks.append(sk)
            prob_rows.append(jnp.concatenate(probs, axis=1))
            sink_rows.append(jnp.exp(jnp.where(lo_row, sks[0], sks[1])
                                     - jnp.where(lo1, maxes[0], maxes[1])))
        o_den = jnp.dot(jnp.concatenate(prob_rows, axis=0), vbd[g], preferred_element_type=F32)
        o = o_den[:, :KV_WIDTH] / (o_den[:, KV_WIDTH:] + jnp.concatenate(sink_rows, axis=0))
        for c in range(chunks_per_kv):
            omix_ref[rows,(c_first + c) * LANES:(c_first + c + 1) * LANES] = (
                o[c * BLK:(c + 1) * BLK].astype(omix_ref.dtype))
    return k_t, v_parts


def _mixer_gla(state, p_ref, pb_ref, r0, lev_ref, gn_ref, omix_ref):
    new_state = []
    pairs = range(len(state))
    rows = slice(r0, r0 + BLK)
    lo1 = lax.broadcasted_iota(jnp.int32, (BLK, LANES), 1) < HEAD_DIM
    lo_bf = jnp.where(lo1, 1.0, 0.0).astype(BF16)
    hi_bf = jnp.where(lo1, 0.0, 1.0).astype(BF16)
    row = lax.broadcasted_iota(jnp.int32, (BLK, LANES), 0)
    zero_blk = jnp.zeros((BLK, LANES), BF16)
    for c in pairs:
        c0 = c * LANES
        q_at = lambda a, z: p_ref[r0 + a:r0 + z, F_QG + c0:F_QG + c0 + LANES]
        k_at = lambda a, z: p_ref[r0 + a:r0 + z, F_KG + c0:F_KG + c0 + LANES]
        b_at = lambda a, z: p_ref[r0 + a:r0 + z, F_LA + c0:F_LA + c0 + LANES]

        def pair_scores(qtb, ktb_lo, ktb_hi):
            return _nt_dot(qtb, jnp.concatenate([ktb_lo, ktb_hi], axis=0))

        kb = k_at(0, BLK).astype(BF16)
        s0 = pair_scores(q_at(0, BLK).astype(BF16), kb * lo_bf, kb * hi_bf)
        tile = lambda a, n: a[n * SUBLANES:(n + 1) * SUBLANES]
        n_tiles = BLK // SUBLANES
        lev_t = [lev_ref[n * SUBLANES:(n + 1) * SUBLANES, :] for n in range(n_tiles)]
        sc = [[jnp.where(lev_t[n] == 0, tile(s0[:, e * BLK:(e + 1) * BLK], n), 0.0)
               for n in range(n_tiles)] for e in range(2)]
        for level in range(1, N_LEVELS + 1):
            m = 1 << (level - 1)
            if m >= SUBLANES:
                qs, klos, khis, dest = [], [], [], []
                zeros = jnp.zeros((m, LANES), BF16)
                lane_m = lax.broadcasted_iota(jnp.int32, (m, LANES), 1) < HEAD_DIM
                lo_m = jnp.where(lane_m, 1.0, 0.0).astype(BF16)
                hi_m = jnp.where(lane_m, 0.0, 1.0).astype(BF16)
                for g in range(BLK // (2 * m)):
                    lo_a, up_a, up_z = g * 2 * m, g * 2 * m + m, (g + 1) * 2 * m
                    rb = jnp.broadcast_to(b_at(up_a - 1, up_a), (m, LANES))
                    qs.append((q_at(up_a, up_z) * jnp.exp(b_at(up_a, up_z) - rb)).astype(BF16))
                    kp = (k_at(lo_a, up_a) * jnp.exp(rb - b_at(lo_a, up_a))).astype(BF16)
                    klos += [kp * lo_m, zeros]
                    khis += [kp * hi_m, zeros]
                    dest += list(range(up_a // SUBLANES, up_z // SUBLANES))
                sl = pair_scores(jnp.concatenate(qs, axis=0), jnp.concatenate(klos, axis=0),
                                 jnp.concatenate(khis, axis=0))
            else:
                dest = list(range(n_tiles))
                bc = b_at(0, BLK)
                decay = jnp.exp(-jnp.abs(bc - _boundary_rows(p_ref, r0, F_LA + c0, bc, row, level)))
                upper = ((row >> (level - 1)) & 1) == 1
                kl = (k_at(0, BLK) * decay).astype(BF16)
                sl = pair_scores(
                    (q_at(0, BLK) * decay).astype(BF16) * jnp.where(upper, 1.0, 0.0).astype(BF16),
                    kl * jnp.where(upper | ~lo1, 0.0, 1.0).astype(BF16),
                    kl * jnp.where(upper | lo1, 0.0, 1.0).astype(BF16))
            for src, n in enumerate(dest):
                for e in range(2):
                    sc[e][n] = jnp.where(lev_t[n] == level, tile(sl[:, e * BLK:(e + 1) * BLK], src),
                                         sc[e][n])
        sc = jnp.concatenate([jnp.concatenate(sc[e], axis=0) for e in range(2)], axis=1)
        qc, kc, bc = q_at(0, BLK), k_at(0, BLK), b_at(0, BLK)

        b_last = bc[BLK - 1:BLK, :]
        v0 = pb_ref[rows,H_VG + 2 * c0:H_VG + 2 * c0 + LANES]
        v1 = pb_ref[rows,H_VG + 2 * c0 + LANES:H_VG + 2 * c0 + 2 * LANES]
        v_bd = jnp.concatenate([jnp.concatenate([v0, zero_blk], axis=1),
                                jnp.concatenate([zero_blk, v1], axis=1)], axis=0)
        st_c = state[c]
        stb = st_c.astype(BF16)
        st_rhs = jnp.concatenate([stb * lo_bf, stb * hi_bf], axis=0)
        o = (jnp.dot(sc.astype(BF16), v_bd, preferred_element_type=F32)
             + _nt_dot((qc * jnp.exp(bc)).astype(BF16), st_rhs))
        kd = (kc * jnp.exp(b_last - bc)).astype(BF16)
        upd = _tn_dot(jnp.concatenate([v0, v1], axis=1), kd)
        new_state.append(st_c * jnp.exp(b_last) + jnp.where(lo1, upd[:BLK], upd[BLK:]))
        for e in range(2):
            h = 2 * c + e
            oh = o[:, e * LANES:(e + 1) * LANES]
            og = oh * lax.rsqrt(jnp.mean(oh * oh, axis=-1, keepdims=True) + EPS) * gn_ref[...]
            rg = p_ref[rows,F_RG + h * GLA_DV:F_RG + (h + 1) * GLA_DV]
            gated = og * (rg * _sigmoid(rg))
            omix_ref[rows,ATTN_WIDTH + h * GLA_DV:ATTN_WIDTH + (h + 1) * GLA_DV] = (
                gated.astype(omix_ref.dtype))
    return new_state


def _prompt_mixer_kernel(relb_ref, sink_ref, p_ref, pb_ref, kt_ref, bucket_ref, lev_ref, gn_ref,
                         omix_ref, st_ref, kbd, vbd, kprev_t, vprev, st_scr, mb_scr):
    i = pl.program_id(0)

    @pl.when(i == 0)
    def _init():
        _mixer_init(relb_ref, bucket_ref, kbd, vbd, kprev_t, vprev, st_scr, mb_scr)

    k_prev_t = kprev_t[...]
    v_prev = [vprev[n] for n in range(2 * ATTN_KV_HEADS)]
    state = [st_scr[:, c * LANES:(c + 1) * LANES] for c in range(GLA_HEADS // 2)]
    n_blocks = p_ref.shape[0] // BLK
    for jb in range(n_blocks):
        table = jnp.where(i == 0, 1, 0) if jb == 0 else 0
        k_prev_t, v_prev = _mixer_attention(pb_ref, kt_ref, jb * BLK, table, sink_ref, omix_ref,
                                            kbd.at[jb], vbd.at[jb], k_prev_t, v_prev, mb_scr)
        state = _mixer_gla(state, p_ref, pb_ref, jb * BLK, lev_ref, gn_ref, omix_ref)
    kprev_t[...] = k_prev_t
    for n, part in enumerate(v_prev):
        vprev[n] = part
    for c, st_c in enumerate(state):
        st_scr[:, c * LANES:(c + 1) * LANES] = st_c
        st_ref[:, c * LANES:(c + 1) * LANES] = st_c


def _prompt_mixer(p, pb, kt, relb, sinks, gn):
    t = p.shape[0]
    smem = pl.BlockSpec(memory_space=pltpu.SMEM)
    return pl.pallas_call(
        _prompt_mixer_kernel,
        grid=(t // MIX_ROWS,),
        in_specs=[
            smem, smem,
            pl.BlockSpec((MIX_ROWS, F_WIDTH), lambda i: (i, 0)),
            pl.BlockSpec((MIX_ROWS, H_WIDTH), lambda i: (i, 0)),
            pl.BlockSpec((KV_WIDTH, MIX_ROWS), lambda i: (0, i)),
            _const_spec((2, BLK, BLK)),
            _const_spec((BLK, BLK)),
            _const_spec((1, GLA_DV)),
        ],
        out_specs=[
            pl.BlockSpec((MIX_ROWS, MIX_WIDTH), lambda i: (i, 0)),
            pl.BlockSpec((GLA_DV, GLA_QK_WIDTH), lambda i: (0, 0)),
        ],
        out_shape=[
            jax.ShapeDtypeStruct((t, MIX_WIDTH), BF16),
            jax.ShapeDtypeStruct((GLA_DV, GLA_QK_WIDTH), F32),
        ],
        scratch_shapes=[
            pltpu.VMEM((MIX_ROWS // BLK, ATTN_KV_HEADS, 2 * HEAD_DIM, 4 * BLK), BF16),
            pltpu.VMEM((MIX_ROWS // BLK, ATTN_KV_HEADS, 4 * BLK, 2 * KV_WIDTH), BF16),
            pltpu.VMEM((KV_WIDTH, BLK), BF16),
            pltpu.VMEM((2 * ATTN_KV_HEADS, BLK, KV_WIDTH), BF16),
            pltpu.VMEM((GLA_DV, GLA_QK_WIDTH), F32),
            pltpu.VMEM((2, ATTN_HEADS // 2, BLK, 2 * BLK), F32),
        ],
        compiler_params=pltpu.CompilerParams(
            dimension_semantics=("arbitrary",), vmem_limit_bytes=VMEM_LIMIT),
        name="prompt_mixer",
    )(relb, sinks, p, pb, kt, jnp.asarray(_BUCKET_PROMPT), jnp.asarray(_LEV), gn)


def _sample_mixer_kernel(ps_ref, ph_ref, kvfull_ref, pfull_ref, ck_ref, cv_ref, st_ref, relbt_ref,
                         sink_ref, bucket_ref,
                         gn_ref, omix_ref, kwin_ref, vwin_ref, stout_ref, lat_scr, kqt_scr, kvt_scr,
                         bias_scr, s_scr, o_scr, og_scr):
    i = pl.program_id(0)
    nb = pfull_ref.shape[0]

    @pl.when(i == 0)
    def _init():
        lat_scr[...] = _split3_rows(pfull_ref[:, F_LA:F_WIDTH].T)
        kqt_scr[0:GLA_QK_WIDTH] = pfull_ref[:, F_KG:F_RG].T.astype(BF16)
        kqt_scr[GLA_QK_WIDTH:2 * GLA_QK_WIDTH] = pfull_ref[:, F_QG:F_KG].T.astype(BF16)
        kvt = kvfull_ref[...].T
        for s in range(nb // SAMPLE_BLK):
            shift = nb - (s + 1) * SAMPLE_BLK
            kvt_scr[s] = pltpu.roll(kvt, shift, 1) if shift else kvt
        bk = jnp.broadcast_to(bucket_ref[...], (ATTN_HEADS, WINDOW))
        acc = jnp.zeros((ATTN_HEADS, WINDOW), F32)
        for b in range(N_BUCKETS):
            acc = jnp.where(bk == b, relbt_ref[:, b:b + 1], acc)
        bias_scr[...] = acc

    lo = lax.broadcasted_iota(jnp.int32, (1, LANES), 1) < HEAD_DIM
    sub = lax.broadcasted_iota(jnp.int32, (ATTN_HEADS, LANES), 0)
    newest = lax.broadcasted_iota(jnp.int32, (KV_WIDTH, WINDOW), 1) == WINDOW - 1
    heads_per_kv = ATTN_HEADS // ATTN_KV_HEADS

    n_of_col = i * SAMPLE_BLK + lax.broadcasted_iota(jnp.int32, (nb, SAMPLE_BLK * LANES), 1) // LANES
    pick = jnp.where(lax.broadcasted_iota(jnp.int32, (nb, SAMPLE_BLK * LANES), 0) == n_of_col,
                     1.0, 0.0).astype(BF16)
    la_b = (jnp.dot(lat_scr[0:GLA_QK_WIDTH], pick, preferred_element_type=F32)
            + jnp.dot(lat_scr[GLA_QK_WIDTH:2 * GLA_QK_WIDTH], pick, preferred_element_type=F32)
            + jnp.dot(lat_scr[2 * GLA_QK_WIDTH:3 * GLA_QK_WIDTH], pick, preferred_element_type=F32))
    kq_b = jnp.dot(kqt_scr[...], pick, preferred_element_type=F32)

    for j in range(SAMPLE_BLK):
        kv_new = kvt_scr[i]
        if j < SAMPLE_BLK - 1:
            kv_new = pltpu.roll(kv_new, SAMPLE_BLK - 1 - j, 1)
        kwin_ref[j] = jnp.where(newest, kv_new[0:KV_WIDTH], pltpu.roll(ck_ref[j], WINDOW - 1, 1))
        vwin_ref[j] = jnp.where(newest, kv_new[KV_WIDTH:2 * KV_WIDTH],
                                pltpu.roll(cv_ref[j], WINDOW - 1, 1))

    for j in range(SAMPLE_BLK):
        qexp = jnp.zeros((ATTN_HEADS, LANES), F32)
        for c in range(ATTN_HEADS // 2):
            chunk = ph_ref[j:j + 1, H_QA + c * LANES:H_QA + (c + 1) * LANES]
            swapped = pltpu.roll(chunk, HEAD_DIM, 1)
            if (2 * c) // heads_per_kv == 0:
                rows = (jnp.where(lo, chunk, 0.0), jnp.where(lo, swapped, 0.0))
            else:
                rows = (jnp.where(lo, 0.0, swapped), jnp.where(lo, 0.0, chunk))
            for e in range(2):
                qexp = jnp.where(sub == 2 * c + e, rows[e], qexp)
        s_scr[j * ATTN_HEADS:(j + 1) * ATTN_HEADS] = jnp.dot(
            qexp.astype(BF16), kwin_ref[j].astype(BF16), preferred_element_type=F32)

    tile = lambda x: jnp.concatenate([x] * SAMPLE_BLK, axis=0)
    sink = tile(sink_ref[...])
    s = s_scr[...] + tile(bias_scr[...])
    m = jnp.maximum(jnp.max(s, axis=-1, keepdims=True), sink)
    pe = jnp.exp(s - m)
    inv_den = 1.0 / (jnp.sum(pe, axis=-1, keepdims=True) + jnp.exp(sink - m))
    peb = pe.astype(BF16)
    for j in range(SAMPLE_BLK):
        o_scr[j * ATTN_HEADS:(j + 1) * ATTN_HEADS] = _nt_dot(
            peb[j * ATTN_HEADS:(j + 1) * ATTN_HEADS], vwin_ref[j].astype(BF16))
    o_all = o_scr[...] * inv_den
    o_swap = pltpu.roll(o_all, HEAD_DIM, 1)
    for j in range(SAMPLE_BLK):
        r = j * ATTN_HEADS
        for c in range(ATTN_HEADS // 2):
            if (2 * c) // heads_per_kv == 0:
                piece = jnp.where(lo, o_all[r + 2 * c:r + 2 * c + 1, :], o_swap[r + 2 * c + 1:r + 2 * c + 2, :])
            else:
                piece = jnp.where(lo, o_swap[r + 2 * c:r + 2 * c + 1, :], o_all[r + 2 * c + 1:r + 2 * c + 2, :])
            omix_ref[j:j + 1, c * LANES:(c + 1) * LANES] = piece

    for j in range(SAMPLE_BLK):
        cols = slice(j * LANES, (j + 1) * LANES)
        for h in range(GLA_HEADS):
            rs = slice(h * GLA_DK, (h + 1) * GLA_DK)
            qs = slice(GLA_QK_WIDTH + h * GLA_DK, GLA_QK_WIDTH + (h + 1) * GLA_DK)
            v_row = ph_ref[j:j + 1, H_VG + h * GLA_DV:H_VG + (h + 1) * GLA_DV]
            s_new = jnp.exp(la_b[rs, cols]) * st_ref[j, h] + kq_b[rs, cols] * v_row
            stout_ref[j, h] = s_new
            og_scr[j:j + 1, h * GLA_DV:(h + 1) * GLA_DV] = jnp.sum(
                kq_b[qs, cols] * s_new, axis=0, keepdims=True)
    for h in range(GLA_HEADS):
        hs = slice(h * GLA_DV, (h + 1) * GLA_DV)
        og = og_scr[:, hs]
        og = og * lax.rsqrt(jnp.mean(og * og, axis=-1, keepdims=True) + EPS) * gn_ref[...]
        rg = ps_ref[:, F_RG + h * GLA_DV:F_RG + (h + 1) * GLA_DV]
        omix_ref[:, ATTN_WIDTH + h * GLA_DV:ATTN_WIDTH + (h + 1) * GLA_DV] = og * (rg * _sigmoid(rg))


def _sample_mixer(ps, ph, kv, cache_k, cache_v, state, relbt, sinks_col, gn):
    nb = ps.shape[0]
    assert nb == LANES
    blk3 = lambda i: (i, 0, 0)
    blk4 = lambda i: (i, 0, 0, 0)
    return pl.pallas_call(
        _sample_mixer_kernel,
        grid=(nb // SAMPLE_BLK,),
        in_specs=[
            pl.BlockSpec((SAMPLE_BLK, F_WIDTH), lambda i: (i, 0)),
            pl.BlockSpec((SAMPLE_BLK, H_WIDTH), lambda i: (i, 0)),
            _const_spec((nb, 2 * KV_WIDTH)),
            _const_spec((nb, F_WIDTH)),
            pl.BlockSpec((SAMPLE_BLK, KV_WIDTH, WINDOW), blk3),
            pl.BlockSpec((SAMPLE_BLK, KV_WIDTH, WINDOW), blk3),
            pl.BlockSpec((SAMPLE_BLK, GLA_HEADS, GLA_DK, GLA_DV), blk4),
            _const_spec((ATTN_HEADS, N_BUCKETS)),
            _const_spec((ATTN_HEADS, 1)),
            _const_spec((1, WINDOW)),
            _const_spec((1, GLA_DV)),
        ],
        out_specs=[
            pl.BlockSpec((SAMPLE_BLK, MIX_WIDTH), lambda i: (i, 0)),
            pl.BlockSpec((SAMPLE_BLK, KV_WIDTH, WINDOW), blk3),
            pl.BlockSpec((SAMPLE_BLK, KV_WIDTH, WINDOW), blk3),
            pl.BlockSpec((SAMPLE_BLK, GLA_HEADS, GLA_DK, GLA_DV), blk4),
        ],
        out_shape=[
            jax.ShapeDtypeStruct((nb, MIX_WIDTH), F32),
            jax.ShapeDtypeStruct((nb, KV_WIDTH, WINDOW), F32),
            jax.ShapeDtypeStruct((nb, KV_WIDTH, WINDOW), F32),
            jax.ShapeDtypeStruct((nb, GLA_HEADS, GLA_DK, GLA_DV), F32),
        ],
        scratch_shapes=[
            pltpu.VMEM((3 * GLA_QK_WIDTH, nb), BF16),
            pltpu.VMEM((2 * GLA_QK_WIDTH, nb), BF16),
            pltpu.VMEM((nb // SAMPLE_BLK, 2 * KV_WIDTH, nb), F32),
            pltpu.VMEM((ATTN_HEADS, WINDOW), F32),
            pltpu.VMEM((SAMPLE_BLK * ATTN_HEADS, WINDOW), F32),
            pltpu.VMEM((SAMPLE_BLK * ATTN_HEADS, KV_WIDTH), F32),
            pltpu.VMEM((SAMPLE_BLK, GLA_WIDTH), F32),
        ],
        compiler_params=pltpu.CompilerParams(
            dimension_semantics=("arbitrary",), vmem_limit_bytes=VMEM_LIMIT),
        name="sample_mixer",
    )(ps, ph, kv, ps, cache_k, cache_v, state, relbt, sinks_col, jnp.asarray(_BUCKET_SAMPLE), gn)


def _finish_kernel(x_ref, mix_ref, xs_ref, mixs_ref, wo_ref, g_ref, wg_ref, wu_ref, wd_ref,
                   y_ref, ys_ref, *, ff_chunks):
    weights = (wo_ref, g_ref, wg_ref, wu_ref, wd_ref)
    _finish_rows(x_ref, mix_ref, *weights, y_ref, ff_chunks=ff_chunks)

    @pl.when(pl.program_id(0) == pl.num_programs(0) - 1)
    def _samples():
        _finish_rows(xs_ref, mixs_ref, *weights, ys_ref, ff_chunks=ff_chunks)


def _finish_rows(x_ref, mix_ref, wo_ref, g_ref, wg_ref, wu_ref, wd_ref, y_ref, *, ff_chunks):
    h = x_ref[...] + jnp.dot(mix_ref[...].astype(BF16), wo_ref[...], preferred_element_type=F32)
    r = lax.rsqrt(jnp.mean(h * h, axis=-1, keepdims=True) + EPS)
    z = (h * g_ref[...]).astype(BF16)
    n_tiles = wd_ref.shape[0] // MXU_TILE
    acc = h
    for c in range(ff_chunks):
        c0 = ((c * n_tiles) // ff_chunks) * MXU_TILE
        c1 = (((c + 1) * n_tiles) // ff_chunks) * MXU_TILE
        gate = jnp.dot(z, wg_ref[:, c0:c1], preferred_element_type=F32) * r
        up = jnp.dot(z, wu_ref[:, c0:c1], preferred_element_type=F32) * r
        act = ((gate * _sigmoid(gate)) * up).astype(BF16)
        acc = acc + jnp.dot(act, wd_ref[c0:c1, :], preferred_element_type=F32)
    y_ref[...] = acc


def _finish(x, mix, xs, mixs, wo, g_ffn, wg, wu, wd, rows):
    t = x.shape[0]
    ns = xs.shape[0]
    d_ff = wd.shape[0]
    assert d_ff % MXU_TILE == 0
    return pl.pallas_call(
        functools.partial(_finish_kernel, ff_chunks=FF_CHUNKS),
        grid=(t // rows,),
        in_specs=[
            pl.BlockSpec((rows, D_MODEL), lambda i: (i, 0)),
            pl.BlockSpec((rows, MIX_WIDTH), lambda i: (i, 0)),
            _const_spec((ns, D_MODEL)),
            _const_spec((ns, MIX_WIDTH)),
            _const_spec((MIX_WIDTH, D_MODEL)),
            _const_spec((1, D_MODEL)),
            _const_spec((D_MODEL, d_ff)),
            _const_spec((D_MODEL, d_ff)),
            _const_spec((d_ff, D_MODEL)),
        ],
        out_specs=[pl.BlockSpec((rows, D_MODEL), lambda i: (i, 0)),
                   pl.BlockSpec((ns, D_MODEL), lambda i: (0, 0))],
        out_shape=[jax.ShapeDtypeStruct((t, D_MODEL), F32),
                   jax.ShapeDtypeStruct((ns, D_MODEL), F32)],
        compiler_params=pltpu.CompilerParams(
            dimension_semantics=("arbitrary",), vmem_limit_bytes=VMEM_LIMIT),
        name="finish",
    )(x, mix, xs, mixs, wo, g_ffn, wg, wu, wd)


PROMPT_ROWS = 1024
PROJ_ROWS = 1024
MIX_ROWS = 1024


def kernel(x_prompt, x_sample, cache_k, cache_v, state_gla, attn_norm_g, w_in, q_norm_g, k_norm_g,
           attn_sinks, rel_bias, w_gla_gate2, b_gla_gate, gla_norm_g, w_o, ffn_norm_g, w_gate, w_up,
           w_down):
    depth = w_in.shape[0]
    batch, seq, _ = x_prompt.shape
    dec_batch, dec_seq, _ = x_sample.shape
    wb = cache_k.shape[2]
    assert batch == 1 and dec_seq == 1 and wb == WINDOW
    assert seq % PROMPT_ROWS == 0 and seq % PROJ_ROWS == 0
    assert dec_batch % SAMPLE_BLK == 0 and dec_batch % LANES == 0
    assert rel_bias.shape == (N_BUCKETS, ATTN_HEADS)

    xp = x_prompt.reshape(seq, D_MODEL)
    xs = x_sample.reshape(dec_batch, D_MODEL)
    relb_flat = rel_bias.reshape(-1)
    relb_t = rel_bias.T
    outs = ([], [], [], [], [], [])
    for l in range(depth):
        kv_end = ATTN_WIDTH + 2 * KV_WIDTH
        w_in_p = jnp.concatenate(
            [w_in[l, :, :kv_end],
             jnp.pad(w_in[l, :, MAIN_WIDTH:], ((0, 0), (0, RANK_PAD - GLA_RANK))),
             w_in[l, :, kv_end:MAIN_WIDTH]], axis=1).astype(BF16)
        w2p = jnp.pad(w_gla_gate2[l], ((0, RANK_PAD - GLA_RANK), (0, 0))).astype(BF16)
        proj_w = (attn_norm_g[l][None, :], w_in_p, jnp.tile(q_norm_g[l], ATTN_HEADS)[None, :],
                  jnp.tile(k_norm_g[l], ATTN_KV_HEADS)[None, :], w2p, b_gla_gate[l][None, :])
        gn = gla_norm_g[l][None, :]

        pp, pb, kt_p, kv_win, ps, ph, kv_s, wo_b, wg_b, wu_b, wd_b = _project(
            xp, xs, *proj_w, rows=PROJ_ROWS, to_cast=(w_o[l], w_gate[l], w_up[l], w_down[l]))
        fin_w = (wo_b, ffn_norm_g[l][None, :], wg_b, wu_b, wd_b)
        mix_p, st_p = _prompt_mixer(pp, pb, kt_p, relb_flat, attn_sinks[l], gn)
        to_t = lambda c: jnp.transpose(c, (0, 2, 3, 1)).reshape(dec_batch, KV_WIDTH, wb)
        from_t = lambda c: jnp.transpose(c.reshape(dec_batch, ATTN_KV_HEADS, HEAD_DIM, wb), (0, 3, 1, 2))
        mix_s, kwin_t, vwin_t, st_s = _sample_mixer(
            ps, ph, kv_s, to_t(cache_k[l]), to_t(cache_v[l]),
            state_gla[l].astype(F32), relb_t, attn_sinks[l][:, None], gn)
        xp_in = xp
        xp, xs = _finish(xp_in, mix_p, xs, mix_s, *fin_w, rows=PROMPT_ROWS)
        outs[0].append(kv_win[:, :KV_WIDTH].reshape(batch, wb, ATTN_KV_HEADS, HEAD_DIM))
        outs[1].append(kv_win[:, KV_WIDTH:].reshape(batch, wb, ATTN_KV_HEADS, HEAD_DIM))
        outs[2].append(st_p.T.reshape(batch, GLA_HEADS, GLA_DK, GLA_DV).astype(state_gla.dtype))
        outs[3].append(from_t(kwin_t))
        outs[4].append(from_t(vwin_t))
        outs[5].append(st_s.astype(state_gla.dtype))

    y_prompt = xp.reshape(batch, seq, D_MODEL)
    y_sample = xs.reshape(dec_batch, dec_seq, D_MODEL)
    return (y_prompt, y_sample) + tuple(jnp.stack(o) for o in outs)
```

```python
import functools
import math

import numpy as np
import jax
import jax.numpy as jnp
from jax import lax
from jax.experimental import pallas as pl
from jax.experimental.pallas import tpu as pltpu

F32 = jnp.float32
BF16 = jnp.bfloat16

D_MODEL = 1024
HEAD_DIM = 64
ATTN_HEADS = 8
ATTN_KV_HEADS = 2
WINDOW = 128
N_BUCKETS = 32
MAX_DISTANCE = 128
GLA_HEADS = 4
GLA_DK = 64
GLA_DV = 128
GLA_RANK = 16
GLA_TAU = 16.0
EPS = 1e-6
ATTN_WIDTH = ATTN_HEADS * HEAD_DIM
KV_WIDTH = ATTN_KV_HEADS * HEAD_DIM
GLA_QK_WIDTH = GLA_HEADS * GLA_DK
GLA_WIDTH = GLA_HEADS * GLA_DV
MIX_WIDTH = ATTN_WIDTH + GLA_WIDTH
MAIN_WIDTH = ATTN_WIDTH + 2 * KV_WIDTH + 2 * GLA_QK_WIDTH + 2 * GLA_WIDTH
LANES = 128
SUBLANES = 8
MXU_TILE = 256
FF_CHUNKS = 4
RANK_PAD = LANES
IN_PAD_WIDTH = MAIN_WIDTH + RANK_PAD

F_QG = 0
F_KG = F_QG + GLA_QK_WIDTH
F_RG = F_KG + GLA_QK_WIDTH
F_LA = F_RG + GLA_WIDTH
F_WIDTH = F_LA + GLA_QK_WIDTH
H_QA = 0
H_VA = H_QA + ATTN_WIDTH
H_VS = H_VA + KV_WIDTH
H_VG = H_VS + KV_WIDTH
H_WIDTH = H_VG + GLA_WIDTH

W_QA = 0
W_KV = W_QA + ATTN_WIDTH
W_QG = W_KV + 2 * KV_WIDTH + RANK_PAD
W_VG = W_QG + 2 * GLA_QK_WIDTH
W_RG = W_VG + GLA_WIDTH

BLK = 128
N_LEVELS = 7
NEG = -1e30
ATTN_SCALE = HEAD_DIM ** -0.5
SAMPLE_BLK = 16
VMEM_LIMIT = 56 * 1024 * 1024


def _t5_bucket_np(dist):
    n = np.maximum(dist, 0)
    max_exact = N_BUCKETS // 2
    nf = np.maximum(n, 1).astype(np.float64)
    large = max_exact + (np.log(nf / max_exact) / math.log(MAX_DISTANCE / max_exact)
                         * (N_BUCKETS - max_exact)).astype(np.int32)
    large = np.minimum(large, N_BUCKETS - 1)
    return np.where(n < max_exact, n, large).astype(np.int32)


def _prompt_bucket_tables():
    assert WINDOW == BLK
    i = np.arange(BLK)[:, None]
    j = np.arange(BLK)[None, :]
    own = j <= i
    bucket = _t5_bucket_np(np.where(own, i - j, BLK + i - j))
    t1 = np.where(own, bucket, -1)
    return np.stack([bucket, t1]).astype(np.int32)


def _level_tables():
    t = np.arange(BLK)[:, None]
    s = np.arange(BLK)[None, :]
    x = t ^ s
    lev = np.where(x > 0, np.floor(np.log2(np.maximum(x, 1))).astype(np.int32) + 1, 0)
    lev = np.where(s > t, -1, lev).astype(np.int32)
    tri = (s <= t).astype(np.float32)
    return lev, np.concatenate([tri, tri, tri], axis=1)


_BUCKET_PROMPT = _prompt_bucket_tables()
_LEV, _TRI3 = _level_tables()
_BUCKET_SAMPLE = _t5_bucket_np((WINDOW - 1) - np.arange(WINDOW))[None, :].astype(np.int32)


def _nt_dot(a, b):
    return lax.dot_general(a, b, (((1,), (1,)), ((), ())), preferred_element_type=F32)


def _tn_dot(a, b):
    return lax.dot_general(a, b, (((0,), (0,)), ((), ())), preferred_element_type=F32)


def _head_mean_sq(x):
    lo = lax.broadcasted_iota(jnp.int32, (x.shape[0], LANES), 1) < HEAD_DIM
    outs = []
    for c in range(x.shape[1] // LANES):
        y = x[:, c * LANES:(c + 1) * LANES]
        y = y * y
        s_lo = jnp.sum(jnp.where(lo, y, 0.0), axis=-1, keepdims=True)
        s_hi = jnp.sum(jnp.where(lo, 0.0, y), axis=-1, keepdims=True)
        outs.append(jnp.where(lo, s_lo, s_hi) * (1.0 / HEAD_DIM))
    return outs[0] if len(outs) == 1 else jnp.concatenate(outs, axis=1)


def _sigmoid(x):
    return 1.0 / (1.0 + jnp.exp(-x))


def _split3_rows(x):
    hi = x.astype(BF16)
    r1 = x - hi.astype(F32)
    mid = r1.astype(BF16)
    lo = (r1 - mid.astype(F32)).astype(BF16)
    return jnp.concatenate([hi, mid, lo], axis=0)


def _proj_kernel(x_ref, xs_ref, g_ref, w_ref, qn_ref, kn_ref, w2_ref, b2_ref, tri_ref, *rest):
    n_out = 7
    n_cast = (len(rest) - n_out) // 2
    cast_in, cast_out = rest[:n_cast], rest[n_cast + n_out:]
    f32_ref, bf_ref, kt_ref, kvw_ref, f32s_ref, bfs_ref, kvs_ref = rest[n_cast:n_cast + n_out]
    for src, dst in zip(cast_in, cast_out):
        dst[...] = src[...].astype(BF16)
    weights = (g_ref, w_ref, qn_ref, kn_ref, w2_ref, b2_ref, tri_ref)
    _proj_rows(x_ref, *weights, f32_ref, bf_ref, kt_ref, kvw_ref, block_cumsum=True)

    @pl.when(pl.program_id(0) == pl.num_programs(0) - 1)
    def _samples():
        _proj_rows(xs_ref, *weights, f32s_ref, bfs_ref, None, kvs_ref, block_cumsum=False)


def _proj_rows(x_ref, g_ref, w_ref, qn_ref, kn_ref, w2_ref, b2_ref, tri_ref, f32_ref, bf_ref,
               kt_ref, kv_ref, *, block_cumsum):
    x = x_ref[...]
    r = lax.rsqrt(jnp.mean(x * x, axis=-1, keepdims=True) + EPS)
    n = (x * g_ref[...]).astype(BF16)

    def seg(c0, c1):
        return jnp.dot(n, w_ref[:, c0:c1], preferred_element_type=F32) * r

    kvl = seg(W_KV, W_QG)
    lr = kvl[:, 2 * KV_WIDTH:].astype(BF16)
    z = jnp.dot(lr, w2_ref[...], preferred_element_type=F32) + b2_ref[...]
    log_a = (jnp.minimum(z, 0.0) - jnp.log1p(jnp.exp(-jnp.abs(z)))) / GLA_TAU
    if block_cumsum:
        for blk in range(x.shape[0] // BLK):
            rows = slice(blk * BLK, (blk + 1) * BLK)
            f32_ref[rows, F_LA:F_WIDTH] = jnp.dot(
                tri_ref[...], _split3_rows(log_a[rows]), preferred_element_type=F32)
    else:
        f32_ref[:, F_LA:F_WIDTH] = log_a
    k = kvl[:, :KV_WIDTH]
    k = k * lax.rsqrt(_head_mean_sq(k) + EPS) * kn_ref[...]
    v = kvl[:, KV_WIDTH:2 * KV_WIDTH]
    n_kv = kv_ref.shape[0]
    kv_ref[:, 0:KV_WIDTH] = k[x.shape[0] - n_kv:]
    kv_ref[:, KV_WIDTH:2 * KV_WIDTH] = v[x.shape[0] - n_kv:]
    if kt_ref is not None:
        kt_ref[...] = k.T.astype(BF16)
    hdt = bf_ref.dtype
    bf_ref[:, H_VA:H_VS] = v.astype(hdt)
    bf_ref[:, H_VS:H_VG] = pltpu.roll(v, HEAD_DIM, 1).astype(hdt)
    q = seg(W_QA, W_KV)
    bf_ref[:, H_QA:H_VA] = (q * lax.rsqrt(_head_mean_sq(q) + EPS) * qn_ref[...] * ATTN_SCALE
                            ).astype(hdt)
    qk_g = seg(W_QG, W_VG)
    f32_ref[:, F_QG:F_KG] = qk_g[:, :GLA_QK_WIDTH] * (GLA_DK ** -0.5)
    f32_ref[:, F_KG:F_RG] = qk_g[:, GLA_QK_WIDTH:]
    bf_ref[:, H_VG:H_WIDTH] = seg(W_VG, W_RG).astype(hdt)
    f32_ref[:, F_RG:F_LA] = seg(W_RG, IN_PAD_WIDTH)


def _const_spec(shape):
    nd = len(shape)
    return pl.BlockSpec(shape, lambda i: (0,) * nd, pipeline_mode=pl.Buffered(1))


def _weight_prep_kernel(w_ref, o_ref):
    o_ref[:, W_QA:W_KV + 2 * KV_WIDTH] = w_ref[:, 0:W_KV + 2 * KV_WIDTH].astype(BF16)
    o_ref[:, W_KV + 2 * KV_WIDTH:W_QG] = jnp.zeros((o_ref.shape[0], RANK_PAD), BF16)
    o_ref[:, W_KV + 2 * KV_WIDTH:W_KV + 2 * KV_WIDTH + GLA_RANK] = (
        w_ref[:, MAIN_WIDTH:MAIN_WIDTH + GLA_RANK].astype(BF16))
    o_ref[:, W_QG:IN_PAD_WIDTH] = w_ref[:, W_KV + 2 * KV_WIDTH:MAIN_WIDTH].astype(BF16)


def _weight_prep(w, rows):
    k, n = w.shape
    assert n == MAIN_WIDTH + GLA_RANK and k % rows == 0
    return pl.pallas_call(
        _weight_prep_kernel,
        grid=(k // rows,),
        in_specs=[pl.BlockSpec((rows, n), lambda i: (i, 0))],
        out_specs=pl.BlockSpec((rows, IN_PAD_WIDTH), lambda i: (i, 0)),
        out_shape=jax.ShapeDtypeStruct((k, IN_PAD_WIDTH), BF16),
        compiler_params=pltpu.CompilerParams(
            dimension_semantics=("arbitrary",), vmem_limit_bytes=VMEM_LIMIT),
        name="weight_prep",
    )(w)


def _project(x, xs, g_attn, w_in_p, qn, kn, w2p, b2, rows, to_cast):
    t = x.shape[0]
    ns = xs.shape[0]
    steps = t // rows
    cast_specs = []
    for w in to_cast:
        slab = w.shape[0] // steps
        assert w.shape[0] % steps == 0 and slab % (2 * SUBLANES) == 0
        cast_specs.append(pl.BlockSpec((slab, w.shape[1]), lambda i: (i, 0)))
    return pl.pallas_call(
        _proj_kernel,
        grid=(steps,),
        in_specs=[
            pl.BlockSpec((rows, D_MODEL), lambda i: (i, 0)),
            _const_spec((ns, D_MODEL)),
            _const_spec((1, D_MODEL)),
            _const_spec((D_MODEL, IN_PAD_WIDTH)),
            _const_spec((1, ATTN_WIDTH)),
            _const_spec((1, KV_WIDTH)),
            _const_spec((RANK_PAD, GLA_QK_WIDTH)),
            _const_spec((1, GLA_QK_WIDTH)),
            _const_spec((BLK, 3 * BLK)),
        ] + cast_specs,
        out_specs=[pl.BlockSpec((rows, F_WIDTH), lambda i: (i, 0)),
                   pl.BlockSpec((rows, H_WIDTH), lambda i: (i, 0)),
                   pl.BlockSpec((KV_WIDTH, rows), lambda i: (0, i)),
                   pl.BlockSpec((WINDOW, 2 * KV_WIDTH), lambda i: (0, 0)),
                   pl.BlockSpec((ns, F_WIDTH), lambda i: (0, 0)),
                   pl.BlockSpec((ns, H_WIDTH), lambda i: (0, 0)),
                   pl.BlockSpec((ns, 2 * KV_WIDTH), lambda i: (0, 0))] + cast_specs,
        out_shape=[jax.ShapeDtypeStruct((t, F_WIDTH), F32),
                   jax.ShapeDtypeStruct((t, H_WIDTH), BF16),
                   jax.ShapeDtypeStruct((KV_WIDTH, t), BF16),
                   jax.ShapeDtypeStruct((WINDOW, 2 * KV_WIDTH), F32),
                   jax.ShapeDtypeStruct((ns, F_WIDTH), F32),
                   jax.ShapeDtypeStruct((ns, H_WIDTH), F32),
                   jax.ShapeDtypeStruct((ns, 2 * KV_WIDTH), F32)]
        + [jax.ShapeDtypeStruct(w.shape, BF16) for w in to_cast],
        compiler_params=pltpu.CompilerParams(
            dimension_semantics=("arbitrary",), vmem_limit_bytes=VMEM_LIMIT),
        name="proj",
    )(x, xs, g_attn, w_in_p, qn, kn, w2p, b2, jnp.asarray(_TRI3, BF16), *to_cast)


def _boundary_rows(b_ref, r0, c0, b, row, level):
    m = 1 << (level - 1)
    if 2 * m >= SUBLANES:
        pieces = [jnp.broadcast_to(b_ref[r0 + g * 2 * m + m - 1:r0 + g * 2 * m + m, c0:c0 + LANES],
                                   (2 * m, LANES))
                  for g in range(BLK // (2 * m))]
        return pieces[0] if len(pieces) == 1 else jnp.concatenate(pieces, axis=0)
    pos = row & (2 * m - 1)
    tiles = b.reshape(BLK // SUBLANES, SUBLANES, LANES)
    out = b
    for p in range(2 * m):
        shift = (m - 1) - p
        if shift != 0:
            rolled = pltpu.roll(tiles, (-shift) % SUBLANES, 1).reshape(BLK, LANES)
            out = jnp.where(pos == p, rolled, out)
    return out


def _mixer_init(relb_ref, bucket_ref, kbd, vbd, kprev_t, vprev, st_scr, mb_scr):
    kbd[...] = jnp.zeros_like(kbd)
    vbd[...] = jnp.zeros_like(vbd)
    for slot in range(vbd.shape[0]):
        for g in range(ATTN_KV_HEADS):
            for half in range(2):
                vbd[slot, g, 2 * half * BLK:(2 * half + 2) * BLK,
                    KV_WIDTH + half * HEAD_DIM:KV_WIDTH + (half + 1) * HEAD_DIM] = (
                        jnp.ones((2 * BLK, HEAD_DIM), BF16))
    kprev_t[...] = jnp.zeros_like(kprev_t)
    vprev[...] = jnp.zeros_like(vprev)
    st_scr[...] = jnp.zeros_like(st_scr)
    bk = bucket_ref[0]
    acc = [jnp.zeros(bk.shape, F32) for _ in range(ATTN_HEADS)]
    for b in range(N_BUCKETS):
        hit = bk == b
        for h in range(ATTN_HEADS):
            acc[h] = jnp.where(hit, relb_ref[b * ATTN_HEADS + h], acc[h])
    for tb in range(2):
        masked = bucket_ref[tb] < 0
        for h in range(ATTN_HEADS):
            mb_scr[tb, h // 2, :, (h % 2) * BLK:(h % 2 + 1) * BLK] = jnp.where(masked, NEG, acc[h])


def _mixer_attention(pb_ref, kt_ref, r0, table, sink_ref, omix_ref, kbd, vbd, k_prev_t, v_prev,
                     mb_scr):
    rows = slice(r0, r0 + BLK)
    lo1 = lax.broadcasted_iota(jnp.int32, (BLK, LANES), 1) < HEAD_DIM
    k_t = kt_ref[:, r0:r0 + BLK]
    for g in range(ATTN_KV_HEADS):
        hd = slice(g * HEAD_DIM, (g + 1) * HEAD_DIM)
        kbd[g, 0:HEAD_DIM, 0:BLK] = k_prev_t[hd]
        kbd[g, 0:HEAD_DIM, BLK:2 * BLK] = k_t[hd]
        kbd[g, HEAD_DIM:2 * HEAD_DIM, 2 * BLK:3 * BLK] = k_prev_t[hd]
        kbd[g, HEAD_DIM:2 * HEAD_DIM, 3 * BLK:4 * BLK] = k_t[hd]
    lo_bf = jnp.where(lo1, 1.0, 0.0).astype(BF16)
    hi_bf = jnp.where(lo1, 0.0, 1.0).astype(BF16)
    lo_row = lax.broadcasted_iota(jnp.int32, (1, LANES), 1) < HEAD_DIM
    own = (lax.broadcasted_iota(jnp.int32, (BLK, BLK), 1)
           <= lax.broadcasted_iota(jnp.int32, (BLK, BLK), 0))
    own_bf = jnp.where(own, 1.0, 0.0).astype(BF16)
    prev_bf = jnp.where(own, 0.0, 1.0).astype(BF16)
    v_cur = pb_ref[rows,H_VA:H_VS]
    v_swap = pb_ref[rows,H_VS:H_VG]
    v_parts = (v_cur * lo_bf, v_swap * hi_bf, v_swap * lo_bf, v_cur * hi_bf)
    for n, part in enumerate(v_parts):
        g, half = divmod(n, 2)
        vbd[g, 2 * half * BLK:(2 * half + 1) * BLK, 0:KV_WIDTH] = v_prev[n]
        vbd[g, (2 * half + 1) * BLK:(2 * half + 2) * BLK, 0:KV_WIDTH] = part
    chunks_per_kv = ATTN_HEADS // ATTN_KV_HEADS // 2
    for g in range(ATTN_KV_HEADS):
        c_first = g * chunks_per_kv
        qs = jnp.concatenate(
            [pb_ref[rows,H_QA + (c_first + c) * LANES:H_QA + (c_first + c + 1) * LANES]
             for c in range(chunks_per_kv)], axis=0)
        s = jnp.dot(qs, kbd[g], preferred_element_type=F32)
        prob_rows, sink_rows = [], []
        for c in range(chunks_per_kv):
            probs, maxes, sks = [], [], []
            for e in range(2):
                s_prev = s[c * BLK:(c + 1) * BLK, 2 * e * BLK:(2 * e + 1) * BLK]
                s_own = s[c * BLK:(c + 1) * BLK, (2 * e + 1) * BLK:(2 * e + 2) * BLK]
                se = (jnp.where(own, s_own, s_prev)
                      + mb_scr[table, c_first + c, :, e * BLK:(e + 1) * BLK])
                sk = sink_ref[2 * (c_first + c) + e]
                m = jnp.maximum(jnp.max(se, axis=-1, keepdims=True), sk)
                pe = jnp.exp(se - m).astype(BF16)
                probs += [pe * prev_bf, pe * own_bf]
                maxes.append(m)
                sks.append(sk)
            prob_rows.append(jnp.concatenate(probs, axis=1))
            sink_rows.append(jnp.exp(jnp.where(lo_row, sks[0], sks[1])
                                     - jnp.where(lo1, maxes[0], maxes[1])))
        o_den = jnp.dot(jnp.concatenate(prob_rows, axis=0), vbd[g], preferred_element_type=F32)
        o = o_den[:, :KV_WIDTH] / (o_den[:, KV_WIDTH:] + jnp.concatenate(sink_rows, axis=0))
        for c in range(chunks_per_kv):
            omix_ref[rows,(c_first + c) * LANES:(c_first + c + 1) * LANES] = (
                o[c * BLK:(c + 1) * BLK].astype(omix_ref.dtype))
    return k_t, v_parts


def _mixer_gla(state, p_ref, pb_ref, r0, lev_ref, gn_ref, omix_ref):
    new_state = []
    pairs = range(len(state))
    rows = slice(r0, r0 + BLK)
    lo1 = lax.broadcasted_iota(jnp.int32, (BLK, LANES), 1) < HEAD_DIM
    lo_bf = jnp.where(lo1, 1.0, 0.0).astype(BF16)
    hi_bf = jnp.where(lo1, 0.0, 1.0).astype(BF16)
    row = lax.broadcasted_iota(jnp.int32, (BLK, LANES), 0)
    zero_blk = jnp.zeros((BLK, LANES), BF16)
    for c in pairs:
        c0 = c * LANES
        q_at = lambda a, z: p_ref[r0 + a:r0 + z, F_QG + c0:F_QG + c0 + LANES]
        k_at = lambda a, z: p_ref[r0 + a:r0 + z, F_KG + c0:F_KG + c0 + LANES]
        b_at = lambda a, z: p_ref[r0 + a:r0 + z, F_LA + c0:F_LA + c0 + LANES]

        def pair_scores(qtb, ktb_lo, ktb_hi):
            return _nt_dot(qtb, jnp.concatenate([ktb_lo, ktb_hi], axis=0))

        kb = k_at(0, BLK).astype(BF16)
        s0 = pair_scores(q_at(0, BLK).astype(BF16), kb * lo_bf, kb * hi_bf)
        tile = lambda a, n: a[n * SUBLANES:(n + 1) * SUBLANES]
        n_tiles = BLK // SUBLANES
        lev_t = [lev_ref[n * SUBLANES:(n + 1) * SUBLANES, :] for n in range(n_tiles)]
        sc = [[jnp.where(lev_t[n] == 0, tile(s0[:, e * BLK:(e + 1) * BLK], n), 0.0)
               for n in range(n_tiles)] for e in range(2)]
        for level in range(1, N_LEVELS + 1):
            m = 1 << (level - 1)
            if m >= SUBLANES:
                qs, klos, khis, dest = [], [], [], []
                zeros = jnp.zeros((m, LANES), BF16)
                lane_m = lax.broadcasted_iota(jnp.int32, (m, LANES), 1) < HEAD_DIM
                lo_m = jnp.where(lane_m, 1.0, 0.0).astype(BF16)
                hi_m = jnp.where(lane_m, 0.0, 1.0).astype(BF16)
                for g in range(BLK // (2 * m)):
                    lo_a, up_a, up_z = g * 2 * m, g * 2 * m + m, (g + 1) * 2 * m
                    rb = jnp.broadcast_to(b_at(up_a - 1, up_a), (m, LANES))
                    qs.append((q_at(up_a, up_z) * jnp.exp(b_at(up_a, up_z) - rb)).astype(BF16))
                    kp = (k_at(lo_a, up_a) * jnp.exp(rb - b_at(lo_a, up_a))).astype(BF16)
                    klos += [kp * lo_m, zeros]
                    khis += [kp * hi_m, zeros]
                    dest += list(range(up_a // SUBLANES, up_z // SUBLANES))
                sl = pair_scores(jnp.concatenate(qs, axis=0), jnp.concatenate(klos, axis=0),
                                 jnp.concatenate(khis, axis=0))
            else:
                dest = list(range(n_tiles))
                bc = b_at(0, BLK)
                decay = jnp.exp(-jnp.abs(bc - _boundary_rows(p_ref, r0, F_LA + c0, bc, row, level)))
                upper = ((row >> (level - 1)) & 1) == 1
                kl = (k_at(0, BLK) * decay).astype(BF16)
                sl = pair_scores(
                    (q_at(0, BLK) * decay).astype(BF16) * jnp.where(upper, 1.0, 0.0).astype(BF16),
                    kl * jnp.where(upper | ~lo1, 0.0, 1.0).astype(BF16),
                    kl * jnp.where(upper | lo1, 0.0, 1.0).astype(BF16))
            for src, n in enumerate(dest):
                for e in range(2):
                    sc[e][n] = jnp.where(lev_t[n] == level, tile(sl[:, e * BLK:(e + 1) * BLK], src),
                                         sc[e][n])
        sc = jnp.concatenate([jnp.concatenate(sc[e], axis=0) for e in range(2)], axis=1)
        qc, kc, bc = q_at(0, BLK), k_at(0, BLK), b_at(0, BLK)

        b_last = bc[BLK - 1:BLK, :]
        v0 = pb_ref[rows,H_VG + 2 * c0:H_VG + 2 * c0 + LANES]
        v1 = pb_ref[rows,H_VG + 2 * c0 + LANES:H_VG + 2 * c0 + 2 * LANES]
        v_bd = jnp.concatenate([jnp.concatenate([v0, zero_blk], axis=1),
                                jnp.concatenate([zero_blk, v1], axis=1)], axis=0)
        st_c = state[c]
        stb = st_c.astype(BF16)
        st_rhs = jnp.concatenate([stb * lo_bf, stb * hi_bf], axis=0)
        o = (jnp.dot(sc.astype(BF16), v_bd, preferred_element_type=F32)
             + _nt_dot((qc * jnp.exp(bc)).astype(BF16), st_rhs))
        kd = (kc * jnp.exp(b_last - bc)).astype(BF16)
        upd = _tn_dot(jnp.concatenate([v0, v1], axis=1), kd)
        new_state.append(st_c * jnp.exp(b_last) + jnp.where(lo1, upd[:BLK], upd[BLK:]))
        for e in range(2):
            h = 2 * c + e
            oh = o[:, e * LANES:(e + 1) * LANES]
            og = oh * lax.rsqrt(jnp.mean(oh * oh, axis=-1, keepdims=True) + EPS) * gn_ref[...]
            rg = p_ref[rows,F_RG + h * GLA_DV:F_RG + (h + 1) * GLA_DV]
            gated = og * (rg * _sigmoid(rg))
            omix_ref[rows,ATTN_WIDTH + h * GLA_DV:ATTN_WIDTH + (h + 1) * GLA_DV] = (
                gated.astype(omix_ref.dtype))
    return new_state


def _prompt_mixer_kernel(relb_ref, sink_ref, p_ref, pb_ref, kt_ref, bucket_ref, lev_ref, gn_ref,
                         omix_ref, st_ref, kbd, vbd, kprev_t, vprev, st_scr, mb_scr):
    i = pl.program_id(0)

    @pl.when(i == 0)
    def _init():
        _mixer_init(relb_ref, bucket_ref, kbd, vbd, kprev_t, vprev, st_scr, mb_scr)

    k_prev_t = kprev_t[...]
    v_prev = [vprev[n] for n in range(2 * ATTN_KV_HEADS)]
    state = [st_scr[:, c * LANES:(c + 1) * LANES] for c in range(GLA_HEADS // 2)]
    n_blocks = p_ref.shape[0] // BLK
    for jb in range(n_blocks):
        table = jnp.where(i == 0, 1, 0) if jb == 0 else 0
        k_prev_t, v_prev = _mixer_attention(pb_ref, kt_ref, jb * BLK, table, sink_ref, omix_ref,
                                            kbd.at[jb], vbd.at[jb], k_prev_t, v_prev, mb_scr)
        state = _mixer_gla(state, p_ref, pb_ref, jb * BLK, lev_ref, gn_ref, omix_ref)
    kprev_t[...] = k_prev_t
    for n, part in enumerate(v_prev):
        vprev[n] = part
    for c, st_c in enumerate(state):
        st_scr[:, c * LANES:(c + 1) * LANES] = st_c
        st_ref[:, c * LANES:(c + 1) * LANES] = st_c


def _prompt_mixer(p, pb, kt, relb, sinks, gn):
    t = p.shape[0]
    smem = pl.BlockSpec(memory_space=pltpu.SMEM)
    return pl.pallas_call(
        _prompt_mixer_kernel,
        grid=(t // MIX_ROWS,),
        in_specs=[
            smem, smem,
            pl.BlockSpec((MIX_ROWS, F_WIDTH), lambda i: (i, 0)),
            pl.BlockSpec((MIX_ROWS, H_WIDTH), lambda i: (i, 0)),
            pl.BlockSpec((KV_WIDTH, MIX_ROWS), lambda i: (0, i)),
            _const_spec((2, BLK, BLK)),
            _const_spec((BLK, BLK)),
            _const_spec((1, GLA_DV)),
        ],
        out_specs=[
            pl.BlockSpec((MIX_ROWS, MIX_WIDTH), lambda i: (i, 0)),
            pl.BlockSpec((GLA_DV, GLA_QK_WIDTH), lambda i: (0, 0)),
        ],
        out_shape=[
            jax.ShapeDtypeStruct((t, MIX_WIDTH), BF16),
            jax.ShapeDtypeStruct((GLA_DV, GLA_QK_WIDTH), F32),
        ],
        scratch_shapes=[
            pltpu.VMEM((MIX_ROWS // BLK, ATTN_KV_HEADS, 2 * HEAD_DIM, 4 * BLK), BF16),
            pltpu.VMEM((MIX_ROWS // BLK, ATTN_KV_HEADS, 4 * BLK, 2 * KV_WIDTH), BF16),
            pltpu.VMEM((KV_WIDTH, BLK), BF16),
            pltpu.VMEM((2 * ATTN_KV_HEADS, BLK, KV_WIDTH), BF16),
            pltpu.VMEM((GLA_DV, GLA_QK_WIDTH), F32),
            pltpu.VMEM((2, ATTN_HEADS // 2, BLK, 2 * BLK), F32),
        ],
        compiler_params=pltpu.CompilerParams(
            dimension_semantics=("arbitrary",), vmem_limit_bytes=VMEM_LIMIT),
        name="prompt_mixer",
    )(relb, sinks, p, pb, kt, jnp.asarray(_BUCKET_PROMPT), jnp.asarray(_LEV), gn)


def _sample_mixer_kernel(ps_ref, ph_ref, kvfull_ref, pfull_ref, ck_ref, cv_ref, st_ref, relbt_ref,
                         sink_ref, bucket_ref,
                         gn_ref, omix_ref, kwin_ref, vwin_ref, stout_ref, lat_scr, kqt_scr, kvt_scr,
                         bias_scr, s_scr, o_scr, og_scr):
    i = pl.program_id(0)
    nb = pfull_ref.shape[0]

    @pl.when(i == 0)
    def _init():
        lat_scr[...] = _split3_rows(pfull_ref[:, F_LA:F_WIDTH].T)
        kqt_scr[0:GLA_QK_WIDTH] = pfull_ref[:, F_KG:F_RG].T.astype(BF16)
        kqt_scr[GLA_QK_WIDTH:2 * GLA_QK_WIDTH] = pfull_ref[:, F_QG:F_KG].T.astype(BF16)
        kvt = kvfull_ref[...].T
        for s in range(nb // SAMPLE_BLK):
            shift = nb - (s + 1) * SAMPLE_BLK
            kvt_scr[s] = pltpu.roll(kvt, shift, 1) if shift else kvt
        bk = jnp.broadcast_to(bucket_ref[...], (ATTN_HEADS, WINDOW))
        acc = jnp.zeros((ATTN_HEADS, WINDOW), F32)
        for b in range(N_BUCKETS):
            acc = jnp.where(bk == b, relbt_ref[:, b:b + 1], acc)
        bias_scr[...] = acc

    lo = lax.broadcasted_iota(jnp.int32, (1, LANES), 1) < HEAD_DIM
    sub = lax.broadcasted_iota(jnp.int32, (ATTN_HEADS, LANES), 0)
    newest = lax.broadcasted_iota(jnp.int32, (KV_WIDTH, WINDOW), 1) == WINDOW - 1
    heads_per_kv = ATTN_HEADS // ATTN_KV_HEADS

    n_of_col = i * SAMPLE_BLK + lax.broadcasted_iota(jnp.int32, (nb, SAMPLE_BLK * LANES), 1) // LANES
    pick = jnp.where(lax.broadcasted_iota(jnp.int32, (nb, SAMPLE_BLK * LANES), 0) == n_of_col,
                     1.0, 0.0).astype(BF16)
    la_b = (jnp.dot(lat_scr[0:GLA_QK_WIDTH], pick, preferred_element_type=F32)
            + jnp.dot(lat_scr[GLA_QK_WIDTH:2 * GLA_QK_WIDTH], pick, preferred_element_type=F32)
            + jnp.dot(lat_scr[2 * GLA_QK_WIDTH:3 * GLA_QK_WIDTH], pick, preferred_element_type=F32))
    kq_b = jnp.dot(kqt_scr[...], pick, preferred_element_type=F32)

    for j in range(SAMPLE_BLK):
        kv_new = kvt_scr[i]
        if j < SAMPLE_BLK - 1:
            kv_new = pltpu.roll(kv_new, SAMPLE_BLK - 1 - j, 1)
        kwin_ref[j] = jnp.where(newest, kv_new[0:KV_WIDTH], pltpu.roll(ck_ref[j], WINDOW - 1, 1))
        vwin_ref[j] = jnp.where(newest, kv_new[KV_WIDTH:2 * KV_WIDTH],
                                pltpu.roll(cv_ref[j], WINDOW - 1, 1))

    for j in range(SAMPLE_BLK):
        qexp = jnp.zeros((ATTN_HEADS, LANES), F32)
        for c in range(ATTN_HEADS // 2):
            chunk = ph_ref[j:j + 1, H_QA + c * LANES:H_QA + (c + 1) * LANES]
            swapped = pltpu.roll(chunk, HEAD_DIM, 1)
            if (2 * c) // heads_per_kv == 0:
                rows = (jnp.where(lo, chunk, 0.0), jnp.where(lo, swapped, 0.0))
            else:
                rows = (jnp.where(lo, 0.0, swapped), jnp.where(lo, 0.0, chunk))
            for e in range(2):
                qexp = jnp.where(sub == 2 * c + e, rows[e], qexp)
        s_scr[j * ATTN_HEADS:(j + 1) * ATTN_HEADS] = jnp.dot(
            qexp.astype(BF16), kwin_ref[j].astype(BF16), preferred_element_type=F32)

    tile = lambda x: jnp.concatenate([x] * SAMPLE_BLK, axis=0)
    sink = tile(sink_ref[...])
    s = s_scr[...] + tile(bias_scr[...])
    m = jnp.maximum(jnp.max(s, axis=-1, keepdims=True), sink)
    pe = jnp.exp(s - m)
    inv_den = 1.0 / (jnp.sum(pe, axis=-1, keepdims=True) + jnp.exp(sink - m))
    peb = pe.astype(BF16)
    for j in range(SAMPLE_BLK):
        o_scr[j * ATTN_HEADS:(j + 1) * ATTN_HEADS] = _nt_dot(
            peb[j * ATTN_HEADS:(j + 1) * ATTN_HEADS], vwin_ref[j].astype(BF16))
    o_all = o_scr[...] * inv_den
    o_swap = pltpu.roll(o_all, HEAD_DIM, 1)
    for j in range(SAMPLE_BLK):
        r = j * ATTN_HEADS
        for c in range(ATTN_HEADS // 2):
            if (2 * c) // heads_per_kv == 0:
                piece = jnp.where(lo, o_all[r + 2 * c:r + 2 * c + 1, :], o_swap[r + 2 * c + 1:r + 2 * c + 2, :])
            else:
                piece = jnp.where(lo, o_swap[r + 2 * c:r + 2 * c + 1, :], o_all[r + 2 * c + 1:r + 2 * c + 2, :])
            omix_ref[j:j + 1, c * LANES:(c + 1) * LANES] = piece

    for j in range(SAMPLE_BLK):
        cols = slice(j * LANES, (j + 1) * LANES)
        for h in range(GLA_HEADS):
            rs = slice(h * GLA_DK, (h + 1) * GLA_DK)
            qs = slice(GLA_QK_WIDTH + h * GLA_DK, GLA_QK_WIDTH + (h + 1) * GLA_DK)
            v_row = ph_ref[j:j + 1, H_VG + h * GLA_DV:H_VG + (h + 1) * GLA_DV]
            s_new = jnp.exp(la_b[rs, cols]) * st_ref[j, h] + kq_b[rs, cols] * v_row
            stout_ref[j, h] = s_new
            og_scr[j:j + 1, h * GLA_DV:(h + 1) * GLA_DV] = jnp.sum(
                kq_b[qs, cols] * s_new, axis=0, keepdims=True)
    for h in range(GLA_HEADS):
        hs = slice(h * GLA_DV, (h + 1) * GLA_DV)
        og = og_scr[:, hs]
        og = og * lax.rsqrt(jnp.mean(og * og, axis=-1, keepdims=True) + EPS) * gn_ref[...]
        rg = ps_ref[:, F_RG + h * GLA_DV:F_RG + (h + 1) * GLA_DV]
        omix_ref[:, ATTN_WIDTH + h * GLA_DV:ATTN_WIDTH + (h + 1) * GLA_DV] = og * (rg * _sigmoid(rg))


def _sample_mixer(ps, ph, kv, cache_k, cache_v, state, relbt, sinks_col, gn):
    nb = ps.shape[0]
    assert nb == LANES
    blk3 = lambda i: (i, 0, 0)
    blk4 = lambda i: (i, 0, 0, 0)
    return pl.pallas_call(
        _sample_mixer_kernel,
        grid=(nb // SAMPLE_BLK,),
        in_specs=[
            pl.BlockSpec((SAMPLE_BLK, F_WIDTH), lambda i: (i, 0)),
            pl.BlockSpec((SAMPLE_BLK, H_WIDTH), lambda i: (i, 0)),
            _const_spec((nb, 2 * KV_WIDTH)),
            _const_spec((nb, F_WIDTH)),
            pl.BlockSpec((SAMPLE_BLK, KV_WIDTH, WINDOW), blk3),
            pl.BlockSpec((SAMPLE_BLK, KV_WIDTH, WINDOW), blk3),
            pl.BlockSpec((SAMPLE_BLK, GLA_HEADS, GLA_DK, GLA_DV), blk4),
            _const_spec((ATTN_HEADS, N_BUCKETS)),
            _const_spec((ATTN_HEADS, 1)),
            _const_spec((1, WINDOW)),
            _const_spec((1, GLA_DV)),
        ],
        out_specs=[
            pl.BlockSpec((SAMPLE_BLK, MIX_WIDTH), lambda i: (i, 0)),
            pl.BlockSpec((SAMPLE_BLK, KV_WIDTH, WINDOW), blk3),
            pl.BlockSpec((SAMPLE_BLK, KV_WIDTH, WINDOW), blk3),
            pl.BlockSpec((SAMPLE_BLK, GLA_HEADS, GLA_DK, GLA_DV), blk4),
        ],
        out_shape=[
            jax.ShapeDtypeStruct((nb, MIX_WIDTH), F32),
            jax.ShapeDtypeStruct((nb, KV_WIDTH, WINDOW), F32),
            jax.ShapeDtypeStruct((nb, KV_WIDTH, WINDOW), F32),
            jax.ShapeDtypeStruct((nb, GLA_HEADS, GLA_DK, GLA_DV), F32),
        ],
        scratch_shapes=[
            pltpu.VMEM((3 * GLA_QK_WIDTH, nb), BF16),
            pltpu.VMEM((2 * GLA_QK_WIDTH, nb), BF16),
            pltpu.VMEM((nb // SAMPLE_BLK, 2 * KV_WIDTH, nb), F32),
            pltpu.VMEM((ATTN_HEADS, WINDOW), F32),
            pltpu.VMEM((SAMPLE_BLK * ATTN_HEADS, WINDOW), F32),
            pltpu.VMEM((SAMPLE_BLK * ATTN_HEADS, KV_WIDTH), F32),
            pltpu.VMEM((SAMPLE_BLK, GLA_WIDTH), F32),
        ],
        compiler_params=pltpu.CompilerParams(
            dimension_semantics=("arbitrary",), vmem_limit_bytes=VMEM_LIMIT),
        name="sample_mixer",
    )(ps, ph, kv, ps, cache_k, cache_v, state, relbt, sinks_col, jnp.asarray(_BUCKET_SAMPLE), gn)


def _finish_kernel(x_ref, mix_ref, xs_ref, mixs_ref, wo_ref, g_ref, wg_ref, wu_ref, wd_ref,
                   y_ref, ys_ref, *, ff_chunks):
    weights = (wo_ref, g_ref, wg_ref, wu_ref, wd_ref)
    _finish_rows(x_ref, mix_ref, *weights, y_ref, ff_chunks=ff_chunks)

    @pl.when(pl.program_id(0) == pl.num_programs(0) - 1)
    def _samples():
        _finish_rows(xs_ref, mixs_ref, *weights, ys_ref, ff_chunks=ff_chunks)


def _finish_rows(x_ref, mix_ref, wo_ref, g_ref, wg_ref, wu_ref, wd_ref, y_ref, *, ff_chunks):
    h = x_ref[...] + jnp.dot(mix_ref[...].astype(BF16), wo_ref[...], preferred_element_type=F32)
    r = lax.rsqrt(jnp.mean(h * h, axis=-1, keepdims=True) + EPS)
    z = (h * g_ref[...]).astype(BF16)
    n_tiles = wd_ref.shape[0] // MXU_TILE
    acc = h
    for c in range(ff_chunks):
        c0 = ((c * n_tiles) // ff_chunks) * MXU_TILE
        c1 = (((c + 1) * n_tiles) // ff_chunks) * MXU_TILE
        gate = jnp.dot(z, wg_ref[:, c0:c1], preferred_element_type=F32) * r
        up = jnp.dot(z, wu_ref[:, c0:c1], preferred_element_type=F32) * r
        act = ((gate * _sigmoid(gate)) * up).astype(BF16)
        acc = acc + jnp.dot(act, wd_ref[c0:c1, :], preferred_element_type=F32)
    y_ref[...] = acc


def _finish(x, mix, xs, mixs, wo, g_ffn, wg, wu, wd, rows):
    t = x.shape[0]
    ns = xs.shape[0]
    d_ff = wd.shape[0]
    assert d_ff % MXU_TILE == 0
    return pl.pallas_call(
        functools.partial(_finish_kernel, ff_chunks=FF_CHUNKS),
        grid=(t // rows,),
        in_specs=[
            pl.BlockSpec((rows, D_MODEL), lambda i: (i, 0)),
            pl.BlockSpec((rows, MIX_WIDTH), lambda i: (i, 0)),
            _const_spec((ns, D_MODEL)),
            _const_spec((ns, MIX_WIDTH)),
            _const_spec((MIX_WIDTH, D_MODEL)),
            _const_spec((1, D_MODEL)),
            _const_spec((D_MODEL, d_ff)),
            _const_spec((D_MODEL, d_ff)),
            _const_spec((d_ff, D_MODEL)),
        ],
        out_specs=[pl.BlockSpec((rows, D_MODEL), lambda i: (i, 0)),
                   pl.BlockSpec((ns, D_MODEL), lambda i: (0, 0))],
        out_shape=[jax.ShapeDtypeStruct((t, D_MODEL), F32),
                   jax.ShapeDtypeStruct((ns, D_MODEL), F32)],
        compiler_params=pltpu.CompilerParams(
            dimension_semantics=("arbitrary",), vmem_limit_bytes=VMEM_LIMIT),
        name="finish",
    )(x, mix, xs, mixs, wo, g_ffn, wg, wu, wd)


PROMPT_ROWS = 1024
PROJ_ROWS = 1024
PREP_ROWS = 256
MIX_ROWS = 1024


def kernel(x_prompt, x_sample, cache_k, cache_v, state_gla, attn_norm_g, w_in, q_norm_g, k_norm_g,
           attn_sinks, rel_bias, w_gla_gate2, b_gla_gate, gla_norm_g, w_o, ffn_norm_g, w_gate, w_up,
           w_down):
    depth = w_in.shape[0]
    batch, seq, _ = x_prompt.shape
    dec_batch, dec_seq, _ = x_sample.shape
    wb = cache_k.shape[2]
    assert batch == 1 and dec_seq == 1 and wb == WINDOW
    assert seq % PROMPT_ROWS == 0 and seq % PROJ_ROWS == 0
    assert dec_batch % SAMPLE_BLK == 0 and dec_batch % LANES == 0
    assert rel_bias.shape == (N_BUCKETS, ATTN_HEADS)

    xp = x_prompt.reshape(seq, D_MODEL)
    xs = x_sample.reshape(dec_batch, D_MODEL)
    relb_flat = rel_bias.reshape(-1)
    relb_t = rel_bias.T
    outs = ([], [], [], [], [], [])
    for l in range(depth):
        w_in_p = _weight_prep(w_in[l], rows=PREP_ROWS)
        w2p = jnp.pad(w_gla_gate2[l], ((0, RANK_PAD - GLA_RANK), (0, 0))).astype(BF16)
        proj_w = (attn_norm_g[l][None, :], w_in_p, jnp.tile(q_norm_g[l], ATTN_HEADS)[None, :],
                  jnp.tile(k_norm_g[l], ATTN_KV_HEADS)[None, :], w2p, b_gla_gate[l][None, :])
        gn = gla_norm_g[l][None, :]

        pp, pb, kt_p, kv_win, ps, ph, kv_s, wo_b, wg_b, wu_b, wd_b = _project(
            xp, xs, *proj_w, rows=PROJ_ROWS, to_cast=(w_o[l], w_gate[l], w_up[l], w_down[l]))
        fin_w = (wo_b, ffn_norm_g[l][None, :], wg_b, wu_b, wd_b)
        mix_p, st_p = _prompt_mixer(pp, pb, kt_p, relb_flat, attn_sinks[l], gn)
        to_t = lambda c: jnp.transpose(c, (0, 2, 3, 1)).reshape(dec_batch, KV_WIDTH, wb)
        from_t = lambda c: jnp.transpose(c.reshape(dec_batch, ATTN_KV_HEADS, HEAD_DIM, wb), (0, 3, 1, 2))
        mix_s, kwin_t, vwin_t, st_s = _sample_mixer(
            ps, ph, kv_s, to_t(cache_k[l]), to_t(cache_v[l]),
            state_gla[l].astype(F32), relb_t, attn_sinks[l][:, None], gn)
        xp_in = xp
        xp, xs = _finish(xp_in, mix_p, xs, mix_s, *fin_w, rows=PROMPT_ROWS)
        outs[0].append(kv_win[:, :KV_WIDTH].reshape(batch, wb, ATTN_KV_HEADS, HEAD_DIM))
        outs[1].append(kv_win[:, KV_WIDTH:].reshape(batch, wb, ATTN_KV_HEADS, HEAD_DIM))
        outs[2].append(st_p.T.reshape(batch, GLA_HEADS, GLA_DK, GLA_DV).astype(state_gla.dtype))
        outs[3].append(from_t(kwin_t))
        outs[4].append(from_t(vwin_t))
        outs[5].append(st_s.astype(state_gla.dtype))

    y_prompt = xp.reshape(batch, seq, D_MODEL)
    y_sample = xs.reshape(dec_batch, dec_seq, D_MODEL)
    return (y_prompt, y_sample) + tuple(jnp.stack(o) for o in outs)
```

```python
import functools
import math

import numpy as np
import jax
import jax.numpy as jnp
from jax import lax
from jax.experimental import pallas as pl
from jax.experimental.pallas import tpu as pltpu

F32 = jnp.float32
BF16 = jnp.bfloat16

D_MODEL = 1024
HEAD_DIM = 64
ATTN_HEADS = 8
ATTN_KV_HEADS = 2
WINDOW = 128
N_BUCKETS = 32
MAX_DISTANCE = 128
GLA_HEADS = 4
GLA_DK = 64
GLA_DV = 128
GLA_RANK = 16
GLA_TAU = 16.0
EPS = 1e-6
ATTN_WIDTH = ATTN_HEADS * HEAD_DIM
KV_WIDTH = ATTN_KV_HEADS * HEAD_DIM
GLA_QK_WIDTH = GLA_HEADS * GLA_DK
GLA_WIDTH = GLA_HEADS * GLA_DV
MIX_WIDTH = ATTN_WIDTH + GLA_WIDTH
MAIN_WIDTH = ATTN_WIDTH + 2 * KV_WIDTH + 2 * GLA_QK_WIDTH + 2 * GLA_WIDTH
LANES = 128
SUBLANES = 8
MXU_TILE = 256
FF_CHUNKS = 4
RANK_PAD = LANES
IN_PAD_WIDTH = MAIN_WIDTH + RANK_PAD

F_QG = 0
F_KG = F_QG + GLA_QK_WIDTH
F_RG = F_KG + GLA_QK_WIDTH
F_LA = F_RG + GLA_WIDTH
F_WIDTH = F_LA + GLA_QK_WIDTH
H_QA = 0
H_VA = H_QA + ATTN_WIDTH
H_VS = H_VA + KV_WIDTH
H_VG = H_VS + KV_WIDTH
H_WIDTH = H_VG + GLA_WIDTH

W_QA = 0
W_KV = W_QA + ATTN_WIDTH
W_QG = W_KV + 2 * KV_WIDTH + RANK_PAD
W_VG = W_QG + 2 * GLA_QK_WIDTH
W_RG = W_VG + GLA_WIDTH

BLK = 128
N_LEVELS = 7
NEG = -1e30
ATTN_SCALE = HEAD_DIM ** -0.5
SAMPLE_BLK = 16
VMEM_LIMIT = 56 * 1024 * 1024


def _t5_bucket_np(dist):
    n = np.maximum(dist, 0)
    max_exact = N_BUCKETS // 2
    nf = np.maximum(n, 1).astype(np.float64)
    large = max_exact + (np.log(nf / max_exact) / math.log(MAX_DISTANCE / max_exact)
                         * (N_BUCKETS - max_exact)).astype(np.int32)
    large = np.minimum(large, N_BUCKETS - 1)
    return np.where(n < max_exact, n, large).astype(np.int32)


def _prompt_bucket_tables():
    assert WINDOW == BLK
    i = np.arange(BLK)[:, None]
    j = np.arange(BLK)[None, :]
    own = j <= i
    bucket = _t5_bucket_np(np.where(own, i - j, BLK + i - j))
    t1 = np.where(own, bucket, -1)
    return np.stack([bucket, t1]).astype(np.int32)


def _level_tables():
    t = np.arange(BLK)[:, None]
    s = np.arange(BLK)[None, :]
    x = t ^ s
    lev = np.where(x > 0, np.floor(np.log2(np.maximum(x, 1))).astype(np.int32) + 1, 0)
    lev = np.where(s > t, -1, lev).astype(np.int32)
    tri = (s <= t).astype(np.float32)
    return lev, np.concatenate([tri, tri, tri], axis=1)


_BUCKET_PROMPT = _prompt_bucket_tables()
_LEV, _TRI3 = _level_tables()
_BUCKET_SAMPLE = _t5_bucket_np((WINDOW - 1) - np.arange(WINDOW))[None, :].astype(np.int32)


def _nt_dot(a, b):
    return lax.dot_general(a, b, (((1,), (1,)), ((), ())), preferred_element_type=F32)


def _tn_dot(a, b):
    return lax.dot_general(a, b, (((0,), (0,)), ((), ())), preferred_element_type=F32)


def _head_mean_sq(x):
    lo = lax.broadcasted_iota(jnp.int32, (x.shape[0], LANES), 1) < HEAD_DIM
    outs = []
    for c in range(x.shape[1] // LANES):
        y = x[:, c * LANES:(c + 1) * LANES]
        y = y * y
        s_lo = jnp.sum(jnp.where(lo, y, 0.0), axis=-1, keepdims=True)
        s_hi = jnp.sum(jnp.where(lo, 0.0, y), axis=-1, keepdims=True)
        outs.append(jnp.where(lo, s_lo, s_hi) * (1.0 / HEAD_DIM))
    return outs[0] if len(outs) == 1 else jnp.concatenate(outs, axis=1)


def _sigmoid(x):
    return 1.0 / (1.0 + jnp.exp(-x))


def _split3_rows(x):
    hi = x.astype(BF16)
    r1 = x - hi.astype(F32)
    mid = r1.astype(BF16)
    lo = (r1 - mid.astype(F32)).astype(BF16)
    return jnp.concatenate([hi, mid, lo], axis=0)


def _proj_kernel(x_ref, xs_ref, g_ref, w_ref, qn_ref, kn_ref, w2_ref, b2_ref, tri_ref, *rest):
    n_out = 7
    n_cast = (len(rest) - n_out) // 2
    cast_in, cast_out = rest[:n_cast], rest[n_cast + n_out:]
    f32_ref, bf_ref, kt_ref, kvw_ref, f32s_ref, bfs_ref, kvs_ref = rest[n_cast:n_cast + n_out]
    for src, dst in zip(cast_in, cast_out):
        dst[...] = src[...].astype(BF16)
    weights = (g_ref, w_ref, qn_ref, kn_ref, w2_ref, b2_ref, tri_ref)
    _proj_rows(x_ref, *weights, f32_ref, bf_ref, kt_ref, kvw_ref, block_cumsum=True)

    @pl.when(pl.program_id(0) == pl.num_programs(0) - 1)
    def _samples():
        _proj_rows(xs_ref, *weights, f32s_ref, bfs_ref, None, kvs_ref, block_cumsum=False)


def _proj_rows(x_ref, g_ref, w_ref, qn_ref, kn_ref, w2_ref, b2_ref, tri_ref, f32_ref, bf_ref,
               kt_ref, kv_ref, *, block_cumsum):
    x = x_ref[...]
    r = lax.rsqrt(jnp.mean(x * x, axis=-1, keepdims=True) + EPS)
    n = (x * g_ref[...]).astype(BF16)

    def seg(c0, c1):
        return _nt_dot(n, w_ref[c0:c1, :]) * r

    kvl = seg(W_KV, W_QG)
    lr = kvl[:, 2 * KV_WIDTH:].astype(BF16)
    z = jnp.dot(lr, w2_ref[...], preferred_element_type=F32) + b2_ref[...]
    log_a = (jnp.minimum(z, 0.0) - jnp.log1p(jnp.exp(-jnp.abs(z)))) / GLA_TAU
    if block_cumsum:
        for blk in range(x.shape[0] // BLK):
            rows = slice(blk * BLK, (blk + 1) * BLK)
            f32_ref[rows, F_LA:F_WIDTH] = jnp.dot(
                tri_ref[...], _split3_rows(log_a[rows]), preferred_element_type=F32)
    else:
        f32_ref[:, F_LA:F_WIDTH] = log_a
    k = kvl[:, :KV_WIDTH]
    k = k * lax.rsqrt(_head_mean_sq(k) + EPS) * kn_ref[...]
    v = kvl[:, KV_WIDTH:2 * KV_WIDTH]
    n_kv = kv_ref.shape[0]
    kv_ref[:, 0:KV_WIDTH] = k[x.shape[0] - n_kv:]
    kv_ref[:, KV_WIDTH:2 * KV_WIDTH] = v[x.shape[0] - n_kv:]
    if kt_ref is not None:
        kt_ref[...] = k.T.astype(BF16)
    hdt = bf_ref.dtype
    bf_ref[:, H_VA:H_VS] = v.astype(hdt)
    bf_ref[:, H_VS:H_VG] = pltpu.roll(v, HEAD_DIM, 1).astype(hdt)
    q = seg(W_QA, W_KV)
    bf_ref[:, H_QA:H_VA] = (q * lax.rsqrt(_head_mean_sq(q) + EPS) * qn_ref[...] * ATTN_SCALE
                            ).astype(hdt)
    qk_g = seg(W_QG, W_VG)
    f32_ref[:, F_QG:F_KG] = qk_g[:, :GLA_QK_WIDTH] * (GLA_DK ** -0.5)
    f32_ref[:, F_KG:F_RG] = qk_g[:, GLA_QK_WIDTH:]
    bf_ref[:, H_VG:H_WIDTH] = seg(W_VG, W_RG).astype(hdt)
    f32_ref[:, F_RG:F_LA] = seg(W_RG, IN_PAD_WIDTH)


def _const_spec(shape):
    nd = len(shape)
    return pl.BlockSpec(shape, lambda i: (0,) * nd, pipeline_mode=pl.Buffered(1))


def _weight_prep_kernel(w_ref, o_ref):
    kv_end = W_KV + 2 * KV_WIDTH
    o_ref[W_QA:kv_end] = w_ref[0:kv_end].astype(BF16)
    o_ref[kv_end:W_QG] = jnp.zeros((RANK_PAD, o_ref.shape[1]), BF16)
    o_ref[kv_end:kv_end + GLA_RANK] = w_ref[MAIN_WIDTH:MAIN_WIDTH + GLA_RANK].astype(BF16)
    o_ref[W_QG:IN_PAD_WIDTH] = w_ref[kv_end:MAIN_WIDTH].astype(BF16)


def _weight_prep(w_t, cols):
    n, k = w_t.shape
    assert n == MAIN_WIDTH + GLA_RANK and k % cols == 0
    return pl.pallas_call(
        _weight_prep_kernel,
        grid=(k // cols,),
        in_specs=[pl.BlockSpec((n, cols), lambda i: (0, i))],
        out_specs=pl.BlockSpec((IN_PAD_WIDTH, cols), lambda i: (0, i)),
        out_shape=jax.ShapeDtypeStruct((IN_PAD_WIDTH, k), BF16),
        compiler_params=pltpu.CompilerParams(
            dimension_semantics=("arbitrary",), vmem_limit_bytes=VMEM_LIMIT),
        name="weight_prep",
    )(w_t)


def _project(x, xs, g_attn, w_in_p, qn, kn, w2p, b2, rows, to_cast):
    t = x.shape[0]
    ns = xs.shape[0]
    steps = t // rows
    cast_specs = []
    for w in to_cast:
        slab = w.shape[0] // steps
        assert w.shape[0] % steps == 0 and slab % (2 * SUBLANES) == 0
        cast_specs.append(pl.BlockSpec((slab, w.shape[1]), lambda i: (i, 0)))
    return pl.pallas_call(
        _proj_kernel,
        grid=(steps,),
        in_specs=[
            pl.BlockSpec((rows, D_MODEL), lambda i: (i, 0)),
            _const_spec((ns, D_MODEL)),
            _const_spec((1, D_MODEL)),
            _const_spec((IN_PAD_WIDTH, D_MODEL)),
            _const_spec((1, ATTN_WIDTH)),
            _const_spec((1, KV_WIDTH)),
            _const_spec((RANK_PAD, GLA_QK_WIDTH)),
            _const_spec((1, GLA_QK_WIDTH)),
            _const_spec((BLK, 3 * BLK)),
        ] + cast_specs,
        out_specs=[pl.BlockSpec((rows, F_WIDTH), lambda i: (i, 0)),
                   pl.BlockSpec((rows, H_WIDTH), lambda i: (i, 0)),
                   pl.BlockSpec((KV_WIDTH, rows), lambda i: (0, i)),
                   pl.BlockSpec((WINDOW, 2 * KV_WIDTH), lambda i: (0, 0)),
                   pl.BlockSpec((ns, F_WIDTH), lambda i: (0, 0)),
                   pl.BlockSpec((ns, H_WIDTH), lambda i: (0, 0)),
                   pl.BlockSpec((ns, 2 * KV_WIDTH), lambda i: (0, 0))] + cast_specs,
        out_shape=[jax.ShapeDtypeStruct((t, F_WIDTH), F32),
                   jax.ShapeDtypeStruct((t, H_WIDTH), BF16),
                   jax.ShapeDtypeStruct((KV_WIDTH, t), BF16),
                   jax.ShapeDtypeStruct((WINDOW, 2 * KV_WIDTH), F32),
                   jax.ShapeDtypeStruct((ns, F_WIDTH), F32),
                   jax.ShapeDtypeStruct((ns, H_WIDTH), F32),
                   jax.ShapeDtypeStruct((ns, 2 * KV_WIDTH), F32)]
        + [jax.ShapeDtypeStruct(w.shape, BF16) for w in to_cast],
        compiler_params=pltpu.CompilerParams(
            dimension_semantics=("arbitrary",), vmem_limit_bytes=VMEM_LIMIT),
        name="proj",
    )(x, xs, g_attn, w_in_p, qn, kn, w2p, b2, jnp.asarray(_TRI3, BF16), *to_cast)


def _boundary_rows(b_ref, r0, c0, b, row, level):
    m = 1 << (level - 1)
    if 2 * m >= SUBLANES:
        pieces = [jnp.broadcast_to(b_ref[r0 + g * 2 * m + m - 1:r0 + g * 2 * m + m, c0:c0 + LANES],
                                   (2 * m, LANES))
                  for g in range(BLK // (2 * m))]
        return pieces[0] if len(pieces) == 1 else jnp.concatenate(pieces, axis=0)
    pos = row & (2 * m - 1)
    tiles = b.reshape(BLK // SUBLANES, SUBLANES, LANES)
    out = b
    for p in range(2 * m):
        shift = (m - 1) - p
        if shift != 0:
            rolled = pltpu.roll(tiles, (-shift) % SUBLANES, 1).reshape(BLK, LANES)
            out = jnp.where(pos == p, rolled, out)
    return out


def _mixer_init(relb_ref, bucket_ref, kbd, vbd, kprev_t, vprev, st_scr, mb_scr):
    kbd[...] = jnp.zeros_like(kbd)
    vbd[...] = jnp.zeros_like(vbd)
    for slot in range(vbd.shape[0]):
        for g in range(ATTN_KV_HEADS):
            for half in range(2):
                vbd[slot, g, 2 * half * BLK:(2 * half + 2) * BLK,
                    KV_WIDTH + half * HEAD_DIM:KV_WIDTH + (half + 1) * HEAD_DIM] = (
                        jnp.ones((2 * BLK, HEAD_DIM), BF16))
    kprev_t[...] = jnp.zeros_like(kprev_t)
    vprev[...] = jnp.zeros_like(vprev)
    st_scr[...] = jnp.zeros_like(st_scr)
    bk = bucket_ref[0]
    acc = [jnp.zeros(bk.shape, F32) for _ in range(ATTN_HEADS)]
    for b in range(N_BUCKETS):
        hit = bk == b
        for h in range(ATTN_HEADS):
            acc[h] = jnp.where(hit, relb_ref[b * ATTN_HEADS + h], acc[h])
    for tb in range(2):
        masked = bucket_ref[tb] < 0
        for h in range(ATTN_HEADS):
            mb_scr[tb, h // 2, :, (h % 2) * BLK:(h % 2 + 1) * BLK] = jnp.where(masked, NEG, acc[h])


def _mixer_attention(pb_ref, kt_ref, r0, table, sink_ref, omix_ref, kbd, vbd, k_prev_t, v_prev,
                     mb_scr):
    rows = slice(r0, r0 + BLK)
    lo1 = lax.broadcasted_iota(jnp.int32, (BLK, LANES), 1) < HEAD_DIM
    k_t = kt_ref[:, r0:r0 + BLK]
    for g in range(ATTN_KV_HEADS):
        hd = slice(g * HEAD_DIM, (g + 1) * HEAD_DIM)
        kbd[g, 0:HEAD_DIM, 0:BLK] = k_prev_t[hd]
        kbd[g, 0:HEAD_DIM, BLK:2 * BLK] = k_t[hd]
        kbd[g, HEAD_DIM:2 * HEAD_DIM, 2 * BLK:3 * BLK] = k_prev_t[hd]
        kbd[g, HEAD_DIM:2 * HEAD_DIM, 3 * BLK:4 * BLK] = k_t[hd]
    lo_bf = jnp.where(lo1, 1.0, 0.0).astype(BF16)
    hi_bf = jnp.where(lo1, 0.0, 1.0).astype(BF16)
    lo_row = lax.broadcasted_iota(jnp.int32, (1, LANES), 1) < HEAD_DIM
    own = (lax.broadcasted_iota(jnp.int32, (BLK, BLK), 1)
           <= lax.broadcasted_iota(jnp.int32, (BLK, BLK), 0))
    own_bf = jnp.where(own, 1.0, 0.0).astype(BF16)
    prev_bf = jnp.where(own, 0.0, 1.0).astype(BF16)
    v_cur = pb_ref[rows,H_VA:H_VS]
    v_swap = pb_ref[rows,H_VS:H_VG]
    v_parts = (v_cur * lo_bf, v_swap * hi_bf, v_swap * lo_bf, v_cur * hi_bf)
    for n, part in enumerate(v_parts):
        g, half = divmod(n, 2)
        vbd[g, 2 * half * BLK:(2 * half + 1) * BLK, 0:KV_WIDTH] = v_prev[n]
        vbd[g, (2 * half + 1) * BLK:(2 * half + 2) * BLK, 0:KV_WIDTH] = part
    chunks_per_kv = ATTN_HEADS // ATTN_KV_HEADS // 2
    for g in range(ATTN_KV_HEADS):
        c_first = g * chunks_per_kv
        qs = jnp.concatenate(
            [pb_ref[rows,H_QA + (c_first + c) * LANES:H_QA + (c_first + c + 1) * LANES]
             for c in range(chunks_per_kv)], axis=0)
        s = jnp.dot(qs, kbd[g], preferred_element_type=F32)
        prob_rows, sink_rows = [], []
        for c in range(chunks_per_kv):
            probs, maxes, sks = [], [], []
            for e in range(2):
                s_prev = s[c * BLK:(c + 1) * BLK, 2 * e * BLK:(2 * e + 1) * BLK]
                s_own = s[c * BLK:(c + 1) * BLK, (2 * e + 1) * BLK:(2 * e + 2) * BLK]
                se = (jnp.where(own, s_own, s_prev)
                      + mb_scr[table, c_first + c, :, e * BLK:(e + 1) * BLK])
                sk = sink_ref[2 * (c_first + c) + e]
                m = jnp.maximum(jnp.max(se, axis=-1, keepdims=True), sk)
                pe = jnp.exp(se - m).astype(BF16)
                probs += [pe * prev_bf, pe * own_bf]
                maxes.append(m)
                sks.append(sk)
            prob_rows.append(jnp.concatenate(probs, axis=1))
            sink_rows.append(jnp.exp(jnp.where(lo_row, sks[0], sks[1])
                                     - jnp.where(lo1, maxes[0], maxes[1])))
        o_den = jnp.dot(jnp.concatenate(prob_rows, axis=0), vbd[g], preferred_element_type=F32)
        o = o_den[:, :KV_WIDTH] / (o_den[:, KV_WIDTH:] + jnp.concatenate(sink_rows, axis=0))
        for c in range(chunks_per_kv):
            omix_ref[rows,(c_first + c) * LANES:(c_first + c + 1) * LANES] = (
                o[c * BLK:(c + 1) * BLK].astype(omix_ref.dtype))
    return k_t, v_parts


def _mixer_gla(state, p_ref, pb_ref, r0, lev_ref, gn_ref, omix_ref):
    new_state = []
    pairs = range(len(state))
    rows = slice(r0, r0 + BLK)
    lo1 = lax.broadcasted_iota(jnp.int32, (BLK, LANES), 1) < HEAD_DIM
    lo_bf = jnp.where(lo1, 1.0, 0.0).astype(BF16)
    hi_bf = jnp.where(lo1, 0.0, 1.0).astype(BF16)
    row = lax.broadcasted_iota(jnp.int32, (BLK, LANES), 0)
    zero_blk = jnp.zeros((BLK, LANES), BF16)
    for c in pairs:
        c0 = c * LANES
        q_at = lambda a, z: p_ref[r0 + a:r0 + z, F_QG + c0:F_QG + c0 + LANES]
        k_at = lambda a, z: p_ref[r0 + a:r0 + z, F_KG + c0:F_KG + c0 + LANES]
        b_at = lambda a, z: p_ref[r0 + a:r0 + z, F_LA + c0:F_LA + c0 + LANES]

        def pair_scores(qtb, ktb_lo, ktb_hi):
            return _nt_dot(qtb, jnp.concatenate([ktb_lo, ktb_hi], axis=0))

        kb = k_at(0, BLK).astype(BF16)
        s0 = pair_scores(q_at(0, BLK).astype(BF16), kb * lo_bf, kb * hi_bf)
        tile = lambda a, n: a[n * SUBLANES:(n + 1) * SUBLANES]
        n_tiles = BLK // SUBLANES
        lev_t = [lev_ref[n * SUBLANES:(n + 1) * SUBLANES, :] for n in range(n_tiles)]
        sc = [[jnp.where(lev_t[n] == 0, tile(s0[:, e * BLK:(e + 1) * BLK], n), 0.0)
               for n in range(n_tiles)] for e in range(2)]
        for level in range(1, N_LEVELS + 1):
            m = 1 << (level - 1)
            if m >= SUBLANES:
                qs, klos, khis, dest = [], [], [], []
                zeros = jnp.zeros((m, LANES), BF16)
                lane_m = lax.broadcasted_iota(jnp.int32, (m, LANES), 1) < HEAD_DIM
                lo_m = jnp.where(lane_m, 1.0, 0.0).astype(BF16)
                hi_m = jnp.where(lane_m, 0.0, 1.0).astype(BF16)
                for g in range(BLK // (2 * m)):
                    lo_a, up_a, up_z = g * 2 * m, g * 2 * m + m, (g + 1) * 2 * m
                    rb = jnp.broadcast_to(b_at(up_a - 1, up_a), (m, LANES))
                    qs.append((q_at(up_a, up_z) * jnp.exp(b_at(up_a, up_z) - rb)).astype(BF16))
                    kp = (k_at(lo_a, up_a) * jnp.exp(rb - b_at(lo_a, up_a))).astype(BF16)
                    klos += [kp * lo_m, zeros]
                    khis += [kp * hi_m, zeros]
                    dest += list(range(up_a // SUBLANES, up_z // SUBLANES))
                sl = pair_scores(jnp.concatenate(qs, axis=0), jnp.concatenate(klos, axis=0),
                                 jnp.concatenate(khis, axis=0))
            else:
                dest = list(range(n_tiles))
                bc = b_at(0, BLK)
                decay = jnp.exp(-jnp.abs(bc - _boundary_rows(p_ref, r0, F_LA + c0, bc, row, level)))
                upper = ((row >> (level - 1)) & 1) == 1
                kl = (k_at(0, BLK) * decay).astype(BF16)
                sl = pair_scores(
                    (q_at(0, BLK) * decay).astype(BF16) * jnp.where(upper, 1.0, 0.0).astype(BF16),
                    kl * jnp.where(upper | ~lo1, 0.0, 1.0).astype(BF16),
                    kl * jnp.where(upper | lo1, 0.0, 1.0).astype(BF16))
            for src, n in enumerate(dest):
                for e in range(2):
                    sc[e][n] = jnp.where(lev_t[n] == level, tile(sl[:, e * BLK:(e + 1) * BLK], src),
                                         sc[e][n])
        sc = jnp.concatenate([jnp.concatenate(sc[e], axis=0) for e in range(2)], axis=1)
        qc, kc, bc = q_at(0, BLK), k_at(0, BLK), b_at(0, BLK)

        b_last = bc[BLK - 1:BLK, :]
        v0 = pb_ref[rows,H_VG + 2 * c0:H_VG + 2 * c0 + LANES]
        v1 = pb_ref[rows,H_VG + 2 * c0 + LANES:H_VG + 2 * c0 + 2 * LANES]
        v_bd = jnp.concatenate([jnp.concatenate([v0, zero_blk], axis=1),
                                jnp.concatenate([zero_blk, v1], axis=1)], axis=0)
        st_c = state[c]
        stb = st_c.astype(BF16)
        st_rhs = jnp.concatenate([stb * lo_bf, stb * hi_bf], axis=0)
        o = (jnp.dot(sc.astype(BF16), v_bd, preferred_element_type=F32)
             + _nt_dot((qc * jnp.exp(bc)).astype(BF16), st_rhs))
        kd = (kc * jnp.exp(b_last - bc)).astype(BF16)
        upd = _tn_dot(jnp.concatenate([v0, v1], axis=1), kd)
        new_state.append(st_c * jnp.exp(b_last) + jnp.where(lo1, upd[:BLK], upd[BLK:]))
        for e in range(2):
            h = 2 * c + e
            oh = o[:, e * LANES:(e + 1) * LANES]
            og = oh * lax.rsqrt(jnp.mean(oh * oh, axis=-1, keepdims=True) + EPS) * gn_ref[...]
            rg = p_ref[rows,F_RG + h * GLA_DV:F_RG + (h + 1) * GLA_DV]
            gated = og * (rg * _sigmoid(rg))
            omix_ref[rows,ATTN_WIDTH + h * GLA_DV:ATTN_WIDTH + (h + 1) * GLA_DV] = (
                gated.astype(omix_ref.dtype))
    return new_state


def _prompt_mixer_kernel(relb_ref, sink_ref, p_ref, pb_ref, kt_ref, bucket_ref, lev_ref, gn_ref,
                         omix_ref, st_ref, kbd, vbd, kprev_t, vprev, st_scr, mb_scr):
    i = pl.program_id(0)

    @pl.when(i == 0)
    def _init():
        _mixer_init(relb_ref, bucket_ref, kbd, vbd, kprev_t, vprev, st_scr, mb_scr)

    k_prev_t = kprev_t[...]
    v_prev = [vprev[n] for n in range(2 * ATTN_KV_HEADS)]
    state = [st_scr[:, c * LANES:(c + 1) * LANES] for c in range(GLA_HEADS // 2)]
    n_blocks = p_ref.shape[0] // BLK
    for jb in range(n_blocks):
        table = jnp.where(i == 0, 1, 0) if jb == 0 else 0
        k_prev_t, v_prev = _mixer_attention(pb_ref, kt_ref, jb * BLK, table, sink_ref, omix_ref,
                                            kbd.at[jb], vbd.at[jb], k_prev_t, v_prev, mb_scr)
        state = _mixer_gla(state, p_ref, pb_ref, jb * BLK, lev_ref, gn_ref, omix_ref)
    kprev_t[...] = k_prev_t
    for n, part in enumerate(v_prev):
        vprev[n] = part
    for c, st_c in enumerate(state):
        st_scr[:, c * LANES:(c + 1) * LANES] = st_c
        st_ref[:, c * LANES:(c + 1) * LANES] = st_c


def _prompt_mixer(p, pb, kt, relb, sinks, gn):
    t = p.shape[0]
    smem = pl.BlockSpec(memory_space=pltpu.SMEM)
    return pl.pallas_call(
        _prompt_mixer_kernel,
        grid=(t // MIX_ROWS,),
        in_specs=[
            smem, smem,
            pl.BlockSpec((MIX_ROWS, F_WIDTH), lambda i: (i, 0)),
            pl.BlockSpec((MIX_ROWS, H_WIDTH), lambda i: (i, 0)),
            pl.BlockSpec((KV_WIDTH, MIX_ROWS), lambda i: (0, i)),
            _const_spec((2, BLK, BLK)),
            _const_spec((BLK, BLK)),
            _const_spec((1, GLA_DV)),
        ],
        out_specs=[
            pl.BlockSpec((MIX_ROWS, MIX_WIDTH), lambda i: (i, 0)),
            pl.BlockSpec((GLA_DV, GLA_QK_WIDTH), lambda i: (0, 0)),
        ],
        out_shape=[
            jax.ShapeDtypeStruct((t, MIX_WIDTH), BF16),
            jax.ShapeDtypeStruct((GLA_DV, GLA_QK_WIDTH), F32),
        ],
        scratch_shapes=[
            pltpu.VMEM((MIX_ROWS // BLK, ATTN_KV_HEADS, 2 * HEAD_DIM, 4 * BLK), BF16),
            pltpu.VMEM((MIX_ROWS // BLK, ATTN_KV_HEADS, 4 * BLK, 2 * KV_WIDTH), BF16),
            pltpu.VMEM((KV_WIDTH, BLK), BF16),
            pltpu.VMEM((2 * ATTN_KV_HEADS, BLK, KV_WIDTH), BF16),
            pltpu.VMEM((GLA_DV, GLA_QK_WIDTH), F32),
            pltpu.VMEM((2, ATTN_HEADS // 2, BLK, 2 * BLK), F32),
        ],
        compiler_params=pltpu.CompilerParams(
            dimension_semantics=("arbitrary",), vmem_limit_bytes=VMEM_LIMIT),
        name="prompt_mixer",
    )(relb, sinks, p, pb, kt, jnp.asarray(_BUCKET_PROMPT), jnp.asarray(_LEV), gn)


def _sample_mixer_kernel(ps_ref, ph_ref, kvfull_ref, pfull_ref, ck_ref, cv_ref, st_ref, relbt_ref,
                         sink_ref, bucket_ref,
                         gn_ref, omix_ref, kwin_ref, vwin_ref, stout_ref, lat_scr, kqt_scr, kvt_scr,
                         bias_scr, s_scr, o_scr, og_scr):
    i = pl.program_id(0)
    nb = pfull_ref.shape[0]

    @pl.when(i == 0)
    def _init():
        lat_scr[...] = _split3_rows(pfull_ref[:, F_LA:F_WIDTH].T)
        kqt_scr[0:GLA_QK_WIDTH] = pfull_ref[:, F_KG:F_RG].T.astype(BF16)
        kqt_scr[GLA_QK_WIDTH:2 * GLA_QK_WIDTH] = pfull_ref[:, F_QG:F_KG].T.astype(BF16)
        kvt = kvfull_ref[...].T
        for s in range(nb // SAMPLE_BLK):
            shift = nb - (s + 1) * SAMPLE_BLK
            kvt_scr[s] = pltpu.roll(kvt, shift, 1) if shift else kvt
        bk = jnp.broadcast_to(bucket_ref[...], (ATTN_HEADS, WINDOW))
        acc = jnp.zeros((ATTN_HEADS, WINDOW), F32)
        for b in range(N_BUCKETS):
            acc = jnp.where(bk == b, relbt_ref[:, b:b + 1], acc)
        bias_scr[...] = acc

    lo = lax.broadcasted_iota(jnp.int32, (1, LANES), 1) < HEAD_DIM
    sub = lax.broadcasted_iota(jnp.int32, (ATTN_HEADS, LANES), 0)
    newest = lax.broadcasted_iota(jnp.int32, (KV_WIDTH, WINDOW), 1) == WINDOW - 1
    heads_per_kv = ATTN_HEADS // ATTN_KV_HEADS

    n_of_col = i * SAMPLE_BLK + lax.broadcasted_iota(jnp.int32, (nb, SAMPLE_BLK * LANES), 1) // LANES
    pick = jnp.where(lax.broadcasted_iota(jnp.int32, (nb, SAMPLE_BLK * LANES), 0) == n_of_col,
                     1.0, 0.0).astype(BF16)
    la_b = (jnp.dot(lat_scr[0:GLA_QK_WIDTH], pick, preferred_element_type=F32)
            + jnp.dot(lat_scr[GLA_QK_WIDTH:2 * GLA_QK_WIDTH], pick, preferred_element_type=F32)
            + jnp.dot(lat_scr[2 * GLA_QK_WIDTH:3 * GLA_QK_WIDTH], pick, preferred_element_type=F32))
    kq_b = jnp.dot(kqt_scr[...], pick, preferred_element_type=F32)

    for j in range(SAMPLE_BLK):
        kv_new = kvt_scr[i]
        if j < SAMPLE_BLK - 1:
            kv_new = pltpu.roll(kv_new, SAMPLE_BLK - 1 - j, 1)
        kwin_ref[j] = jnp.where(newest, kv_new[0:KV_WIDTH], pltpu.roll(ck_ref[j], WINDOW - 1, 1))
        vwin_ref[j] = jnp.where(newest, kv_new[KV_WIDTH:2 * KV_WIDTH],
                                pltpu.roll(cv_ref[j], WINDOW - 1, 1))

    for j in range(SAMPLE_BLK):
        qexp = jnp.zeros((ATTN_HEADS, LANES), F32)
        for c in range(ATTN_HEADS // 2):
            chunk = ph_ref[j:j + 1, H_QA + c * LANES:H_QA + (c + 1) * LANES]
            swapped = pltpu.roll(chunk, HEAD_DIM, 1)
            if (2 * c) // heads_per_kv == 0:
                rows = (jnp.where(lo, chunk, 0.0), jnp.where(lo, swapped, 0.0))
            else:
                rows = (jnp.where(lo, 0.0, swapped), jnp.where(lo, 0.0, chunk))
            for e in range(2):
                qexp = jnp.where(sub == 2 * c + e, rows[e], qexp)
        s_scr[j * ATTN_HEADS:(j + 1) * ATTN_HEADS] = jnp.dot(
            qexp.astype(BF16), kwin_ref[j].astype(BF16), preferred_element_type=F32)

    tile = lambda x: jnp.concatenate([x] * SAMPLE_BLK, axis=0)
    sink = tile(sink_ref[...])
    s = s_scr[...] + tile(bias_scr[...])
    m = jnp.maximum(jnp.max(s, axis=-1, keepdims=True), sink)
    pe = jnp.exp(s - m)
    inv_den = 1.0 / (jnp.sum(pe, axis=-1, keepdims=True) + jnp.exp(sink - m))
    peb = pe.astype(BF16)
    for j in range(SAMPLE_BLK):
        o_scr[j * ATTN_HEADS:(j + 1) * ATTN_HEADS] = _nt_dot(
            peb[j * ATTN_HEADS:(j + 1) * ATTN_HEADS], vwin_ref[j].astype(BF16))
    o_all = o_scr[...] * inv_den
    o_swap = pltpu.roll(o_all, HEAD_DIM, 1)
    for j in range(SAMPLE_BLK):
        r = j * ATTN_HEADS
        for c in range(ATTN_HEADS // 2):
            if (2 * c) // heads_per_kv == 0:
                piece = jnp.where(lo, o_all[r + 2 * c:r + 2 * c + 1, :], o_swap[r + 2 * c + 1:r + 2 * c + 2, :])
            else:
                piece = jnp.where(lo, o_swap[r + 2 * c:r + 2 * c + 1, :], o_all[r + 2 * c + 1:r + 2 * c + 2, :])
            omix_ref[j:j + 1, c * LANES:(c + 1) * LANES] = piece

    for j in range(SAMPLE_BLK):
        cols = slice(j * LANES, (j + 1) * LANES)
        for h in range(GLA_HEADS):
            rs = slice(h * GLA_DK, (h + 1) * GLA_DK)
            qs = slice(GLA_QK_WIDTH + h * GLA_DK, GLA_QK_WIDTH + (h + 1) * GLA_DK)
            v_row = ph_ref[j:j + 1, H_VG + h * GLA_DV:H_VG + (h + 1) * GLA_DV]
            s_new = jnp.exp(la_b[rs, cols]) * st_ref[j, h] + kq_b[rs, cols] * v_row
            stout_ref[j, h] = s_new
            og_scr[j:j + 1, h * GLA_DV:(h + 1) * GLA_DV] = jnp.sum(
                kq_b[qs, cols] * s_new, axis=0, keepdims=True)
    for h in range(GLA_HEADS):
        hs = slice(h * GLA_DV, (h + 1) * GLA_DV)
        og = og_scr[:, hs]
        og = og * lax.rsqrt(jnp.mean(og * og, axis=-1, keepdims=True) + EPS) * gn_ref[...]
        rg = ps_ref[:, F_RG + h * GLA_DV:F_RG + (h + 1) * GLA_DV]
        omix_ref[:, ATTN_WIDTH + h * GLA_DV:ATTN_WIDTH + (h + 1) * GLA_DV] = og * (rg * _sigmoid(rg))


def _sample_mixer(ps, ph, kv, cache_k, cache_v, state, relbt, sinks_col, gn):
    nb = ps.shape[0]
    assert nb == LANES
    blk3 = lambda i: (i, 0, 0)
    blk4 = lambda i: (i, 0, 0, 0)
    return pl.pallas_call(
        _sample_mixer_kernel,
        grid=(nb // SAMPLE_BLK,),
        in_specs=[
            pl.BlockSpec((SAMPLE_BLK, F_WIDTH), lambda i: (i, 0)),
            pl.BlockSpec((SAMPLE_BLK, H_WIDTH), lambda i: (i, 0)),
            _const_spec((nb, 2 * KV_WIDTH)),
            _const_spec((nb, F_WIDTH)),
            pl.BlockSpec((SAMPLE_BLK, KV_WIDTH, WINDOW), blk3),
            pl.BlockSpec((SAMPLE_BLK, KV_WIDTH, WINDOW), blk3),
            pl.BlockSpec((SAMPLE_BLK, GLA_HEADS, GLA_DK, GLA_DV), blk4),
            _const_spec((ATTN_HEADS, N_BUCKETS)),
            _const_spec((ATTN_HEADS, 1)),
            _const_spec((1, WINDOW)),
            _const_spec((1, GLA_DV)),
        ],
        out_specs=[
            pl.BlockSpec((SAMPLE_BLK, MIX_WIDTH), lambda i: (i, 0)),
            pl.BlockSpec((SAMPLE_BLK, KV_WIDTH, WINDOW), blk3),
            pl.BlockSpec((SAMPLE_BLK, KV_WIDTH, WINDOW), blk3),
            pl.BlockSpec((SAMPLE_BLK, GLA_HEADS, GLA_DK, GLA_DV), blk4),
        ],
        out_shape=[
            jax.ShapeDtypeStruct((nb, MIX_WIDTH), F32),
            jax.ShapeDtypeStruct((nb, KV_WIDTH, WINDOW), F32),
            jax.ShapeDtypeStruct((nb, KV_WIDTH, WINDOW), F32),
            jax.ShapeDtypeStruct((nb, GLA_HEADS, GLA_DK, GLA_DV), F32),
        ],
        scratch_shapes=[
            pltpu.VMEM((3 * GLA_QK_WIDTH, nb), BF16),
            pltpu.VMEM((2 * GLA_QK_WIDTH, nb), BF16),
            pltpu.VMEM((nb // SAMPLE_BLK, 2 * KV_WIDTH, nb), F32),
            pltpu.VMEM((ATTN_HEADS, WINDOW), F32),
            pltpu.VMEM((SAMPLE_BLK * ATTN_HEADS, WINDOW), F32),
            pltpu.VMEM((SAMPLE_BLK * ATTN_HEADS, KV_WIDTH), F32),
            pltpu.VMEM((SAMPLE_BLK, GLA_WIDTH), F32),
        ],
        compiler_params=pltpu.CompilerParams(
            dimension_semantics=("arbitrary",), vmem_limit_bytes=VMEM_LIMIT),
        name="sample_mixer",
    )(ps, ph, kv, ps, cache_k, cache_v, state, relbt, sinks_col, jnp.asarray(_BUCKET_SAMPLE), gn)


def _finish_kernel(x_ref, mix_ref, xs_ref, mixs_ref, wo_ref, g_ref, wg_ref, wu_ref, wd_ref,
                   y_ref, ys_ref, *, ff_chunks):
    weights = (wo_ref, g_ref, wg_ref, wu_ref, wd_ref)
    _finish_rows(x_ref, mix_ref, *weights, y_ref, ff_chunks=ff_chunks)

    @pl.when(pl.program_id(0) == pl.num_programs(0) - 1)
    def _samples():
        _finish_rows(xs_ref, mixs_ref, *weights, ys_ref, ff_chunks=ff_chunks)


def _finish_rows(x_ref, mix_ref, wo_ref, g_ref, wg_ref, wu_ref, wd_ref, y_ref, *, ff_chunks):
    h = x_ref[...] + jnp.dot(mix_ref[...].astype(BF16), wo_ref[...], preferred_element_type=F32)
    r = lax.rsqrt(jnp.mean(h * h, axis=-1, keepdims=True) + EPS)
    z = (h * g_ref[...]).astype(BF16)
    n_tiles = wd_ref.shape[0] // MXU_TILE
    acc = h
    for c in range(ff_chunks):
        c0 = ((c * n_tiles) // ff_chunks) * MXU_TILE
        c1 = (((c + 1) * n_tiles) // ff_chunks) * MXU_TILE
        gate = jnp.dot(z, wg_ref[:, c0:c1], preferred_element_type=F32) * r
        up = jnp.dot(z, wu_ref[:, c0:c1], preferred_element_type=F32) * r
        act = ((gate * _sigmoid(gate)) * up).astype(BF16)
        acc = acc + jnp.dot(act, wd_ref[c0:c1, :], preferred_element_type=F32)
    y_ref[...] = acc


def _finish(x, mix, xs, mixs, wo, g_ffn, wg, wu, wd, rows):
    t = x.shape[0]
    ns = xs.shape[0]
    d_ff = wd.shape[0]
    assert d_ff % MXU_TILE == 0
    return pl.pallas_call(
        functools.partial(_finish_kernel, ff_chunks=FF_CHUNKS),
        grid=(t // rows,),
        in_specs=[
            pl.BlockSpec((rows, D_MODEL), lambda i: (i, 0)),
            pl.BlockSpec((rows, MIX_WIDTH), lambda i: (i, 0)),
            _const_spec((ns, D_MODEL)),
            _const_spec((ns, MIX_WIDTH)),
            _const_spec((MIX_WIDTH, D_MODEL)),
            _const_spec((1, D_MODEL)),
            _const_spec((D_MODEL, d_ff)),
            _const_spec((D_MODEL, d_ff)),
            _const_spec((d_ff, D_MODEL)),
        ],
        out_specs=[pl.BlockSpec((rows, D_MODEL), lambda i: (i, 0)),
                   pl.BlockSpec((ns, D_MODEL), lambda i: (0, 0))],
        out_shape=[jax.ShapeDtypeStruct((t, D_MODEL), F32),
                   jax.ShapeDtypeStruct((ns, D_MODEL), F32)],
        compiler_params=pltpu.CompilerParams(
            dimension_semantics=("arbitrary",), vmem_limit_bytes=VMEM_LIMIT),
        name="finish",
    )(x, mix, xs, mixs, wo, g_ffn, wg, wu, wd)


PROMPT_ROWS = 1024
PROJ_ROWS = 1024
PREP_COLS = 256
MIX_ROWS = 1024


def kernel(x_prompt, x_sample, cache_k, cache_v, state_gla, attn_norm_g, w_in, q_norm_g, k_norm_g,
           attn_sinks, rel_bias, w_gla_gate2, b_gla_gate, gla_norm_g, w_o, ffn_norm_g, w_gate, w_up,
           w_down):
    depth = w_in.shape[0]
    batch, seq, _ = x_prompt.shape
    dec_batch, dec_seq, _ = x_sample.shape
    wb = cache_k.shape[2]
    assert batch == 1 and dec_seq == 1 and wb == WINDOW
    assert seq % PROMPT_ROWS == 0 and seq % PROJ_ROWS == 0
    assert dec_batch % SAMPLE_BLK == 0 and dec_batch % LANES == 0
    assert rel_bias.shape == (N_BUCKETS, ATTN_HEADS)

    xp = x_prompt.reshape(seq, D_MODEL)
    xs = x_sample.reshape(dec_batch, D_MODEL)
    relb_flat = rel_bias.reshape(-1)
    relb_t = rel_bias.T
    outs = ([], [], [], [], [], [])
    for l in range(depth):
        w_in_p = _weight_prep(w_in[l].T, cols=PREP_COLS)
        w2p = jnp.pad(w_gla_gate2[l], ((0, RANK_PAD - GLA_RANK), (0, 0))).astype(BF16)
        proj_w = (attn_norm_g[l][None, :], w_in_p, jnp.tile(q_norm_g[l], ATTN_HEADS)[None, :],
                  jnp.tile(k_norm_g[l], ATTN_KV_HEADS)[None, :], w2p, b_gla_gate[l][None, :])
        gn = gla_norm_g[l][None, :]

        pp, pb, kt_p, kv_win, ps, ph, kv_s, wo_b, wg_b, wu_b, wd_b = _project(
            xp, xs, *proj_w, rows=PROJ_ROWS, to_cast=(w_o[l], w_gate[l], w_up[l], w_down[l]))
        fin_w = (wo_b, ffn_norm_g[l][None, :], wg_b, wu_b, wd_b)
        mix_p, st_p = _prompt_mixer(pp, pb, kt_p, relb_flat, attn_sinks[l], gn)
        to_t = lambda c: jnp.transpose(c, (0, 2, 3, 1)).reshape(dec_batch, KV_WIDTH, wb)
        from_t = lambda c: jnp.transpose(c.reshape(dec_batch, ATTN_KV_HEADS, HEAD_DIM, wb), (0, 3, 1, 2))
        mix_s, kwin_t, vwin_t, st_s = _sample_mixer(
            ps, ph, kv_s, to_t(cache_k[l]), to_t(cache_v[l]),
            state_gla[l].astype(F32), relb_t, attn_sinks[l][:, None], gn)
        xp_in = xp
        xp, xs = _finish(xp_in, mix_p, xs, mix_s, *fin_w, rows=PROMPT_ROWS)
        outs[0].append(kv_win[:, :KV_WIDTH].reshape(batch, wb, ATTN_KV_HEADS, HEAD_DIM))
        outs[1].append(kv_win[:, KV_WIDTH:].reshape(batch, wb, ATTN_KV_HEADS, HEAD_DIM))
        outs[2].append(st_p.T.reshape(batch, GLA_HEADS, GLA_DK, GLA_DV).astype(state_gla.dtype))
        outs[3].append(from_t(kwin_t))
        outs[4].append(from_t(vwin_t))
        outs[5].append(st_s.astype(state_gla.dtype))

    y_prompt = xp.reshape(batch, seq, D_MODEL)
    y_sample = xs.reshape(dec_batch, dec_seq, D_MODEL)
    return (y_prompt, y_sample) + tuple(jnp.stack(o) for o in outs)
```

```python
import functools
import math

import numpy as np
import jax
import jax.numpy as jnp
from jax import lax
from jax.experimental import pallas as pl
from jax.experimental.pallas import tpu as pltpu

F32 = jnp.float32
BF16 = jnp.bfloat16

D_MODEL = 1024
HEAD_DIM = 64
ATTN_HEADS = 8
ATTN_KV_HEADS = 2
WINDOW = 128
N_BUCKETS = 32
MAX_DISTANCE = 128
GLA_HEADS = 4
GLA_DK = 64
GLA_DV = 128
GLA_RANK = 16
GLA_TAU = 16.0
EPS = 1e-6
ATTN_WIDTH = ATTN_HEADS * HEAD_DIM
KV_WIDTH = ATTN_KV_HEADS * HEAD_DIM
GLA_QK_WIDTH = GLA_HEADS * GLA_DK
GLA_WIDTH = GLA_HEADS * GLA_DV
MIX_WIDTH = ATTN_WIDTH + GLA_WIDTH
MAIN_WIDTH = ATTN_WIDTH + 2 * KV_WIDTH + 2 * GLA_QK_WIDTH + 2 * GLA_WIDTH
LANES = 128
SUBLANES = 8
MXU_TILE = 256
FF_CHUNKS = 4
RANK_PAD = LANES
IN_PAD_WIDTH = MAIN_WIDTH + RANK_PAD

F_QG = 0
F_KG = F_QG + GLA_QK_WIDTH
F_RG = F_KG + GLA_QK_WIDTH
F_LA = F_RG + GLA_WIDTH
F_WIDTH = F_LA + GLA_QK_WIDTH
H_QA = 0
H_VA = H_QA + ATTN_WIDTH
H_VS = H_VA + KV_WIDTH
H_VG = H_VS + KV_WIDTH
H_WIDTH = H_VG + GLA_WIDTH

W_QA = 0
W_KV = W_QA + ATTN_WIDTH
W_QG = W_KV + 2 * KV_WIDTH + RANK_PAD
W_VG = W_QG + 2 * GLA_QK_WIDTH
W_RG = W_VG + GLA_WIDTH

BLK = 128
N_LEVELS = 7
NEG = -1e30
ATTN_SCALE = HEAD_DIM ** -0.5
SAMPLE_BLK = 16
VMEM_LIMIT = 56 * 1024 * 1024


def _t5_bucket_np(dist):
    n = np.maximum(dist, 0)
    max_exact = N_BUCKETS // 2
    nf = np.maximum(n, 1).astype(np.float64)
    large = max_exact + (np.log(nf / max_exact) / math.log(MAX_DISTANCE / max_exact)
                         * (N_BUCKETS - max_exact)).astype(np.int32)
    large = np.minimum(large, N_BUCKETS - 1)
    return np.where(n < max_exact, n, large).astype(np.int32)


def _prompt_bucket_tables():
    assert WINDOW == BLK
    i = np.arange(BLK)[:, None]
    j = np.arange(BLK)[None, :]
    own = j <= i
    bucket = _t5_bucket_np(np.where(own, i - j, BLK + i - j))
    t1 = np.where(own, bucket, -1)
    return np.stack([bucket, t1]).astype(np.int32)


def _level_tables():
    t = np.arange(BLK)[:, None]
    s = np.arange(BLK)[None, :]
    x = t ^ s
    lev = np.where(x > 0, np.floor(np.log2(np.maximum(x, 1))).astype(np.int32) + 1, 0)
    lev = np.where(s > t, -1, lev).astype(np.int32)
    tri = (s <= t).astype(np.float32)
    return lev, np.concatenate([tri, tri, tri], axis=1)


_BUCKET_PROMPT = _prompt_bucket_tables()
_LEV, _TRI3 = _level_tables()
_BUCKET_SAMPLE = _t5_bucket_np((WINDOW - 1) - np.arange(WINDOW))[None, :].astype(np.int32)


def _nt_dot(a, b):
    return lax.dot_general(a, b, (((1,), (1,)), ((), ())), preferred_element_type=F32)


def _tn_dot(a, b):
    return lax.dot_general(a, b, (((0,), (0,)), ((), ())), preferred_element_type=F32)


def _head_mean_sq(x):
    lo = lax.broadcasted_iota(jnp.int32, (x.shape[0], LANES), 1) < HEAD_DIM
    outs = []
    for c in range(x.shape[1] // LANES):
        y = x[:, c * LANES:(c + 1) * LANES]
        y = y * y
        s_lo = jnp.sum(jnp.where(lo, y, 0.0), axis=-1, keepdims=True)
        s_hi = jnp.sum(jnp.where(lo, 0.0, y), axis=-1, keepdims=True)
        outs.append(jnp.where(lo, s_lo, s_hi) * (1.0 / HEAD_DIM))
    return outs[0] if len(outs) == 1 else jnp.concatenate(outs, axis=1)


def _sigmoid(x):
    return 1.0 / (1.0 + jnp.exp(-x))


def _split3_rows(x):
    hi = x.astype(BF16)
    r1 = x - hi.astype(F32)
    mid = r1.astype(BF16)
    lo = (r1 - mid.astype(F32)).astype(BF16)
    return jnp.concatenate([hi, mid, lo], axis=0)


def _proj_kernel(x_ref, xs_ref, g_ref, w_ref, qn_ref, kn_ref, w2_ref, b2_ref, tri_ref, *rest):
    n_out = 7
    n_cast = (len(rest) - n_out) // 2
    cast_in, cast_out = rest[:n_cast], rest[n_cast + n_out:]
    f32_ref, bf_ref, kt_ref, kvw_ref, f32s_ref, bfs_ref, kvs_ref = rest[n_cast:n_cast + n_out]
    for src, dst in zip(cast_in, cast_out):
        dst[...] = src[...].astype(BF16)
    weights = (g_ref, w_ref, qn_ref, kn_ref, w2_ref, b2_ref, tri_ref)
    _proj_rows(x_ref, *weights, f32_ref, bf_ref, kt_ref, kvw_ref, block_cumsum=True)

    @pl.when(pl.program_id(0) == pl.num_programs(0) - 1)
    def _samples():
        _proj_rows(xs_ref, *weights, f32s_ref, bfs_ref, None, kvs_ref, block_cumsum=False)


def _proj_rows(x_ref, g_ref, w_ref, qn_ref, kn_ref, w2_ref, b2_ref, tri_ref, f32_ref, bf_ref,
               kt_ref, kv_ref, *, block_cumsum):
    x = x_ref[...]
    r = lax.rsqrt(jnp.mean(x * x, axis=-1, keepdims=True) + EPS)
    n = (x * g_ref[...]).astype(BF16)

    def seg(c0, c1):
        return _nt_dot(n, w_ref[c0:c1, :]) * r

    kvl = seg(W_KV, W_QG)
    lr = kvl[:, 2 * KV_WIDTH:].astype(BF16)
    z = jnp.dot(lr, w2_ref[...], preferred_element_type=F32) + b2_ref[...]
    log_a = (jnp.minimum(z, 0.0) - jnp.log(1.0 + jnp.exp(-jnp.abs(z)))) / GLA_TAU
    if block_cumsum:
        for blk in range(x.shape[0] // BLK):
            rows = slice(blk * BLK, (blk + 1) * BLK)
            f32_ref[rows, F_LA:F_WIDTH] = jnp.dot(
                tri_ref[...], _split3_rows(log_a[rows]), preferred_element_type=F32)
    else:
        f32_ref[:, F_LA:F_WIDTH] = log_a
    k = kvl[:, :KV_WIDTH]
    k = k * lax.rsqrt(_head_mean_sq(k) + EPS) * kn_ref[...]
    v = kvl[:, KV_WIDTH:2 * KV_WIDTH]
    n_kv = kv_ref.shape[0]
    kv_ref[:, 0:KV_WIDTH] = k[x.shape[0] - n_kv:]
    kv_ref[:, KV_WIDTH:2 * KV_WIDTH] = v[x.shape[0] - n_kv:]
    if kt_ref is not None:
        kt_ref[...] = k.T.astype(BF16)
    hdt = bf_ref.dtype
    bf_ref[:, H_VA:H_VS] = v.astype(hdt)
    bf_ref[:, H_VS:H_VG] = pltpu.roll(v, HEAD_DIM, 1).astype(hdt)
    q = seg(W_QA, W_KV)
    bf_ref[:, H_QA:H_VA] = (q * lax.rsqrt(_head_mean_sq(q) + EPS) * qn_ref[...]).astype(hdt)
    qk_g = seg(W_QG, W_VG)
    f32_ref[:, F_QG:F_KG] = qk_g[:, :GLA_QK_WIDTH] * (GLA_DK ** -0.5)
    f32_ref[:, F_KG:F_RG] = qk_g[:, GLA_QK_WIDTH:]
    bf_ref[:, H_VG:H_WIDTH] = seg(W_VG, W_RG).astype(hdt)
    f32_ref[:, F_RG:F_LA] = seg(W_RG, IN_PAD_WIDTH)


def _const_spec(shape):
    nd = len(shape)
    return pl.BlockSpec(shape, lambda i: (0,) * nd, pipeline_mode=pl.Buffered(1))


def _sample_rows_spec(ns, **kwargs):
    return pl.BlockSpec((ns, None, D_MODEL), lambda i: (0, 0, 0), **kwargs)


def _weight_prep_kernel(w_ref, o_ref):
    kv_end = W_KV + 2 * KV_WIDTH
    o_ref[W_QA:kv_end] = w_ref[0:kv_end].astype(BF16)
    o_ref[kv_end:W_QG] = jnp.zeros((RANK_PAD, o_ref.shape[1]), BF16)
    o_ref[kv_end:kv_end + GLA_RANK] = w_ref[MAIN_WIDTH:MAIN_WIDTH + GLA_RANK].astype(BF16)
    o_ref[W_QG:IN_PAD_WIDTH] = w_ref[kv_end:MAIN_WIDTH].astype(BF16)


def _weight_prep(w_t, cols):
    n, k = w_t.shape
    assert n == MAIN_WIDTH + GLA_RANK and k % cols == 0
    return pl.pallas_call(
        _weight_prep_kernel,
        grid=(k // cols,),
        in_specs=[pl.BlockSpec((n, cols), lambda i: (0, i))],
        out_specs=pl.BlockSpec((IN_PAD_WIDTH, cols), lambda i: (0, i)),
        out_shape=jax.ShapeDtypeStruct((IN_PAD_WIDTH, k), BF16),
        compiler_params=pltpu.CompilerParams(
            dimension_semantics=("arbitrary",), vmem_limit_bytes=VMEM_LIMIT),
        name="weight_prep",
    )(w_t)


def _project(x, xs, g_attn, w_in_p, qn, kn, w2p, b2, rows, to_cast):
    t = x.shape[0]
    ns = xs.shape[0]
    steps = t // rows
    cast_specs = []
    for w in to_cast:
        slab = w.shape[0] // steps
        assert w.shape[0] % steps == 0 and slab % (2 * SUBLANES) == 0
        cast_specs.append(pl.BlockSpec((slab, w.shape[1]), lambda i: (i, 0)))
    return pl.pallas_call(
        _proj_kernel,
        grid=(steps,),
        in_specs=[
            pl.BlockSpec((rows, D_MODEL), lambda i: (i, 0)),
            _sample_rows_spec(ns, pipeline_mode=pl.Buffered(1)),
            _const_spec((1, D_MODEL)),
            _const_spec((IN_PAD_WIDTH, D_MODEL)),
            _const_spec((1, ATTN_WIDTH)),
            _const_spec((1, KV_WIDTH)),
            _const_spec((RANK_PAD, GLA_QK_WIDTH)),
            _const_spec((1, GLA_QK_WIDTH)),
            _const_spec((BLK, 3 * BLK)),
        ] + cast_specs,
        out_specs=[pl.BlockSpec((rows, F_WIDTH), lambda i: (i, 0)),
                   pl.BlockSpec((rows, H_WIDTH), lambda i: (i, 0)),
                   pl.BlockSpec((KV_WIDTH, rows), lambda i: (0, i)),
                   pl.BlockSpec((WINDOW, 2 * KV_WIDTH), lambda i: (0, 0)),
                   pl.BlockSpec((ns, F_WIDTH), lambda i: (0, 0)),
                   pl.BlockSpec((ns, H_WIDTH), lambda i: (0, 0)),
                   pl.BlockSpec((ns, 2 * KV_WIDTH), lambda i: (0, 0))] + cast_specs,
        out_shape=[jax.ShapeDtypeStruct((t, F_WIDTH), F32),
                   jax.ShapeDtypeStruct((t, H_WIDTH), BF16),
                   jax.ShapeDtypeStruct((KV_WIDTH, t), BF16),
                   jax.ShapeDtypeStruct((WINDOW, 2 * KV_WIDTH), F32),
                   jax.ShapeDtypeStruct((ns, F_WIDTH), F32),
                   jax.ShapeDtypeStruct((ns, H_WIDTH), F32),
                   jax.ShapeDtypeStruct((ns, 2 * KV_WIDTH), F32)]
        + [jax.ShapeDtypeStruct(w.shape, BF16) for w in to_cast],
        compiler_params=pltpu.CompilerParams(
            dimension_semantics=("arbitrary",), vmem_limit_bytes=VMEM_LIMIT),
        name="proj",
    )(x, xs, g_attn, w_in_p, qn, kn, w2p, b2, jnp.asarray(_TRI3, BF16), *to_cast)


def _boundary_rows(b_ref, r0, c0, b, row, level):
    m = 1 << (level - 1)
    if 2 * m >= SUBLANES:
        pieces = [jnp.broadcast_to(b_ref[r0 + g * 2 * m + m - 1:r0 + g * 2 * m + m, c0:c0 + LANES],
                                   (2 * m, LANES))
                  for g in range(BLK // (2 * m))]
        return pieces[0] if len(pieces) == 1 else jnp.concatenate(pieces, axis=0)
    pos = row & (2 * m - 1)
    tiles = b.reshape(BLK // SUBLANES, SUBLANES, LANES)
    out = b
    for p in range(2 * m):
        shift = (m - 1) - p
        if shift != 0:
            rolled = pltpu.roll(tiles, (-shift) % SUBLANES, 1).reshape(BLK, LANES)
            out = jnp.where(pos == p, rolled, out)
    return out


def _mixer_init(relb_ref, bucket_ref, kbd, vbd, kprev_t, vprev, st_scr, mb_scr):
    kbd[...] = jnp.zeros_like(kbd)
    vbd[...] = jnp.zeros_like(vbd)
    for slot in range(vbd.shape[0]):
        for g in range(ATTN_KV_HEADS):
            for half in range(2):
                vbd[slot, g, 2 * half * BLK:(2 * half + 2) * BLK,
                    KV_WIDTH + half * HEAD_DIM:KV_WIDTH + (half + 1) * HEAD_DIM] = (
                        jnp.ones((2 * BLK, HEAD_DIM), BF16))
    kprev_t[...] = jnp.zeros_like(kprev_t)
    vprev[...] = jnp.zeros_like(vprev)
    st_scr[...] = jnp.zeros_like(st_scr)
    bk = bucket_ref[0]
    acc = [jnp.zeros(bk.shape, F32) for _ in range(ATTN_HEADS)]
    for b in range(N_BUCKETS):
        hit = bk == b
        for h in range(ATTN_HEADS):
            acc[h] = jnp.where(hit, relb_ref[b * ATTN_HEADS + h], acc[h])
    for tb in range(2):
        masked = bucket_ref[tb] < 0
        for h in range(ATTN_HEADS):
            mb_scr[tb, h // 2, :, (h % 2) * BLK:(h % 2 + 1) * BLK] = jnp.where(masked, NEG, acc[h])


def _mixer_attention(pb_ref, kt_ref, r0, table, sink_ref, omix_ref, kbd, vbd, k_prev_t, v_prev,
                     mb_scr):
    rows = slice(r0, r0 + BLK)
    lo1 = lax.broadcasted_iota(jnp.int32, (BLK, LANES), 1) < HEAD_DIM
    k_t = kt_ref[:, r0:r0 + BLK]
    for g in range(ATTN_KV_HEADS):
        hd = slice(g * HEAD_DIM, (g + 1) * HEAD_DIM)
        kbd[g, 0:HEAD_DIM, 0:BLK] = k_prev_t[hd]
        kbd[g, 0:HEAD_DIM, BLK:2 * BLK] = k_t[hd]
        kbd[g, HEAD_DIM:2 * HEAD_DIM, 2 * BLK:3 * BLK] = k_prev_t[hd]
        kbd[g, HEAD_DIM:2 * HEAD_DIM, 3 * BLK:4 * BLK] = k_t[hd]
    lo_bf = jnp.where(lo1, 1.0, 0.0).astype(BF16)
    hi_bf = jnp.where(lo1, 0.0, 1.0).astype(BF16)
    lo_row = lax.broadcasted_iota(jnp.int32, (1, LANES), 1) < HEAD_DIM
    own = (lax.broadcasted_iota(jnp.int32, (BLK, BLK), 1)
           <= lax.broadcasted_iota(jnp.int32, (BLK, BLK), 0))
    own_bf = jnp.where(own, 1.0, 0.0).astype(BF16)
    prev_bf = jnp.where(own, 0.0, 1.0).astype(BF16)
    v_cur = pb_ref[rows,H_VA:H_VS]
    v_swap = pb_ref[rows,H_VS:H_VG]
    v_parts = (v_cur * lo_bf, v_swap * hi_bf, v_swap * lo_bf, v_cur * hi_bf)
    for n, part in enumerate(v_parts):
        g, half = divmod(n, 2)
        vbd[g, 2 * half * BLK:(2 * half + 1) * BLK, 0:KV_WIDTH] = v_prev[n]
        vbd[g, (2 * half + 1) * BLK:(2 * half + 2) * BLK, 0:KV_WIDTH] = part
    chunks_per_kv = ATTN_HEADS // ATTN_KV_HEADS // 2
    for g in range(ATTN_KV_HEADS):
        c_first = g * chunks_per_kv
        qs = jnp.concatenate(
            [pb_ref[rows,H_QA + (c_first + c) * LANES:H_QA + (c_first + c + 1) * LANES]
             for c in range(chunks_per_kv)], axis=0)
        s = jnp.dot(qs, kbd[g], preferred_element_type=F32)
        prob_rows, sink_rows = [], []
        for c in range(chunks_per_kv):
            probs, maxes, sks = [], [], []
            for e in range(2):
                s_prev = s[c * BLK:(c + 1) * BLK, 2 * e * BLK:(2 * e + 1) * BLK]
                s_own = s[c * BLK:(c + 1) * BLK, (2 * e + 1) * BLK:(2 * e + 2) * BLK]
                se = (jnp.where(own, s_own, s_prev)
                      + mb_scr[table, c_first + c, :, e * BLK:(e + 1) * BLK])
                sk = sink_ref[2 * (c_first + c) + e]
                m = jnp.maximum(jnp.max(se, axis=-1, keepdims=True), sk)
                pe = jnp.exp(se - m).astype(BF16)
                probs += [pe * prev_bf, pe * own_bf]
                maxes.append(m)
                sks.append(sk)
            prob_rows.append(jnp.concatenate(probs, axis=1))
            sink_rows.append(jnp.exp(jnp.where(lo_row, sks[0], sks[1])
                                     - jnp.where(lo1, maxes[0], maxes[1])))
        o_den = jnp.dot(jnp.concatenate(prob_rows, axis=0), vbd[g], preferred_element_type=F32)
        o = o_den[:, :KV_WIDTH] / (o_den[:, KV_WIDTH:] + jnp.concatenate(sink_rows, axis=0))
        for c in range(chunks_per_kv):
            omix_ref[rows,(c_first + c) * LANES:(c_first + c + 1) * LANES] = (
                o[c * BLK:(c + 1) * BLK].astype(omix_ref.dtype))
    return k_t, v_parts


def _mixer_gla(state, p_ref, pb_ref, r0, lev_ref, gn_ref, omix_ref):
    new_state = []
    pairs = range(len(state))
    rows = slice(r0, r0 + BLK)
    lo1 = lax.broadcasted_iota(jnp.int32, (BLK, LANES), 1) < HEAD_DIM
    lo_bf = jnp.where(lo1, 1.0, 0.0).astype(BF16)
    hi_bf = jnp.where(lo1, 0.0, 1.0).astype(BF16)
    row = lax.broadcasted_iota(jnp.int32, (BLK, LANES), 0)
    zero_blk = jnp.zeros((BLK, LANES), BF16)
    for c in pairs:
        c0 = c * LANES
        q_at = lambda a, z: p_ref[r0 + a:r0 + z, F_QG + c0:F_QG + c0 + LANES]
        k_at = lambda a, z: p_ref[r0 + a:r0 + z, F_KG + c0:F_KG + c0 + LANES]
        b_at = lambda a, z: p_ref[r0 + a:r0 + z, F_LA + c0:F_LA + c0 + LANES]

        def pair_scores(qtb, ktb_lo, ktb_hi):
            return _nt_dot(qtb, jnp.concatenate([ktb_lo, ktb_hi], axis=0))

        kb = k_at(0, BLK).astype(BF16)
        s0 = pair_scores(q_at(0, BLK).astype(BF16), kb * lo_bf, kb * hi_bf)
        tile = lambda a, n: a[n * SUBLANES:(n + 1) * SUBLANES]
        n_tiles = BLK // SUBLANES
        lev_t = [lev_ref[n * SUBLANES:(n + 1) * SUBLANES, :] for n in range(n_tiles)]
        sc = [[jnp.where(lev_t[n] == 0, tile(s0[:, e * BLK:(e + 1) * BLK], n), 0.0)
               for n in range(n_tiles)] for e in range(2)]
        for level in range(1, N_LEVELS + 1):
            m = 1 << (level - 1)
            if m >= SUBLANES:
                qs, klos, khis, dest = [], [], [], []
                zeros = jnp.zeros((m, LANES), BF16)
                lane_m = lax.broadcasted_iota(jnp.int32, (m, LANES), 1) < HEAD_DIM
                lo_m = jnp.where(lane_m, 1.0, 0.0).astype(BF16)
                hi_m = jnp.where(lane_m, 0.0, 1.0).astype(BF16)
                for g in range(BLK // (2 * m)):
                    lo_a, up_a, up_z = g * 2 * m, g * 2 * m + m, (g + 1) * 2 * m
                    rb = jnp.broadcast_to(b_at(up_a - 1, up_a), (m, LANES))
                    qs.append((q_at(up_a, up_z) * jnp.exp(b_at(up_a, up_z) - rb)).astype(BF16))
                    kp = (k_at(lo_a, up_a) * jnp.exp(rb - b_at(lo_a, up_a))).astype(BF16)
                    klos += [kp * lo_m, zeros]
                    khis += [kp * hi_m, zeros]
                    dest += list(range(up_a // SUBLANES, up_z // SUBLANES))
                sl = pair_scores(jnp.concatenate(qs, axis=0), jnp.concatenate(klos, axis=0),
                                 jnp.concatenate(khis, axis=0))
            else:
                dest = list(range(n_tiles))
                bc = b_at(0, BLK)
                decay = jnp.exp(-jnp.abs(bc - _boundary_rows(p_ref, r0, F_LA + c0, bc, row, level)))
                upper = ((row >> (level - 1)) & 1) == 1
                kl = (k_at(0, BLK) * decay).astype(BF16)
                sl = pair_scores(
                    (q_at(0, BLK) * decay).astype(BF16) * jnp.where(upper, 1.0, 0.0).astype(BF16),
                    kl * jnp.where(upper | ~lo1, 0.0, 1.0).astype(BF16),
                    kl * jnp.where(upper | lo1, 0.0, 1.0).astype(BF16))
            for src, n in enumerate(dest):
                for e in range(2):
                    sc[e][n] = jnp.where(lev_t[n] == level, tile(sl[:, e * BLK:(e + 1) * BLK], src),
                                         sc[e][n])
        sc = jnp.concatenate([jnp.concatenate(sc[e], axis=0) for e in range(2)], axis=1)
        qc, kc, bc = q_at(0, BLK), k_at(0, BLK), b_at(0, BLK)

        b_last = bc[BLK - 1:BLK, :]
        v0 = pb_ref[rows,H_VG + 2 * c0:H_VG + 2 * c0 + LANES]
        v1 = pb_ref[rows,H_VG + 2 * c0 + LANES:H_VG + 2 * c0 + 2 * LANES]
        v_bd = jnp.concatenate([jnp.concatenate([v0, zero_blk], axis=1),
                                jnp.concatenate([zero_blk, v1], axis=1)], axis=0)
        st_c = state[c]
        stb = st_c.astype(BF16)
        st_rhs = jnp.concatenate([stb * lo_bf, stb * hi_bf], axis=0)
        o = (jnp.dot(sc.astype(BF16), v_bd, preferred_element_type=F32)
             + _nt_dot((qc * jnp.exp(bc)).astype(BF16), st_rhs))
        kd = (kc * jnp.exp(b_last - bc)).astype(BF16)
        upd = _tn_dot(jnp.concatenate([v0, v1], axis=1), kd)
        new_state.append(st_c * jnp.exp(b_last) + jnp.where(lo1, upd[:BLK], upd[BLK:]))
        for e in range(2):
            h = 2 * c + e
            oh = o[:, e * LANES:(e + 1) * LANES]
            og = oh * lax.rsqrt(jnp.mean(oh * oh, axis=-1, keepdims=True) + EPS) * gn_ref[...]
            rg = p_ref[rows,F_RG + h * GLA_DV:F_RG + (h + 1) * GLA_DV]
            gated = og * (rg * _sigmoid(rg))
            omix_ref[rows,ATTN_WIDTH + h * GLA_DV:ATTN_WIDTH + (h + 1) * GLA_DV] = (
                gated.astype(omix_ref.dtype))
    return new_state


def _prompt_mixer_kernel(relb_ref, sink_ref, p_ref, pb_ref, kt_ref, bucket_ref, lev_ref, gn_ref,
                         omix_ref, st_ref, kbd, vbd, kprev_t, vprev, st_scr, mb_scr):
    i = pl.program_id(0)

    @pl.when(i == 0)
    def _init():
        _mixer_init(relb_ref, bucket_ref, kbd, vbd, kprev_t, vprev, st_scr, mb_scr)

    k_prev_t = kprev_t[...]
    v_prev = [vprev[n] for n in range(2 * ATTN_KV_HEADS)]
    state = [st_scr[:, c * LANES:(c + 1) * LANES] for c in range(GLA_HEADS // 2)]
    n_blocks = p_ref.shape[0] // BLK
    for jb in range(n_blocks):
        table = jnp.where(i == 0, 1, 0) if jb == 0 else 0
        k_prev_t, v_prev = _mixer_attention(pb_ref, kt_ref, jb * BLK, table, sink_ref, omix_ref,
                                            kbd.at[jb], vbd.at[jb], k_prev_t, v_prev, mb_scr)
        state = _mixer_gla(state, p_ref, pb_ref, jb * BLK, lev_ref, gn_ref, omix_ref)
    kprev_t[...] = k_prev_t
    for n, part in enumerate(v_prev):
        vprev[n] = part
    for c, st_c in enumerate(state):
        st_scr[:, c * LANES:(c + 1) * LANES] = st_c
        st_ref[:, c * LANES:(c + 1) * LANES] = st_c


def _prompt_mixer(p, pb, kt, relb, sinks, gn):
    t = p.shape[0]
    smem = pl.BlockSpec(memory_space=pltpu.SMEM)
    return pl.pallas_call(
        _prompt_mixer_kernel,
        grid=(t // MIX_ROWS,),
        in_specs=[
            smem, smem,
            pl.BlockSpec((MIX_ROWS, F_WIDTH), lambda i: (i, 0)),
            pl.BlockSpec((MIX_ROWS, H_WIDTH), lambda i: (i, 0)),
            pl.BlockSpec((KV_WIDTH, MIX_ROWS), lambda i: (0, i)),
            _const_spec((2, BLK, BLK)),
            _const_spec((BLK, BLK)),
            _const_spec((1, GLA_DV)),
        ],
        out_specs=[
            pl.BlockSpec((MIX_ROWS, MIX_WIDTH), lambda i: (i, 0)),
            pl.BlockSpec((GLA_DV, GLA_QK_WIDTH), lambda i: (0, 0)),
        ],
        out_shape=[
            jax.ShapeDtypeStruct((t, MIX_WIDTH), BF16),
            jax.ShapeDtypeStruct((GLA_DV, GLA_QK_WIDTH), F32),
        ],
        scratch_shapes=[
            pltpu.VMEM((MIX_ROWS // BLK, ATTN_KV_HEADS, 2 * HEAD_DIM, 4 * BLK), BF16),
            pltpu.VMEM((MIX_ROWS // BLK, ATTN_KV_HEADS, 4 * BLK, 2 * KV_WIDTH), BF16),
            pltpu.VMEM((KV_WIDTH, BLK), BF16),
            pltpu.VMEM((2 * ATTN_KV_HEADS, BLK, KV_WIDTH), BF16),
            pltpu.VMEM((GLA_DV, GLA_QK_WIDTH), F32),
            pltpu.VMEM((2, ATTN_HEADS // 2, BLK, 2 * BLK), F32),
        ],
        compiler_params=pltpu.CompilerParams(
            dimension_semantics=("arbitrary",), vmem_limit_bytes=VMEM_LIMIT),
        name="prompt_mixer",
    )(relb, sinks, p, pb, kt, jnp.asarray(_BUCKET_PROMPT), jnp.asarray(_LEV), gn)


def _sample_mixer_kernel(ps_ref, ph_ref, kvfull_ref, pfull_ref, ck_ref, cv_ref, st_ref, relbt_ref,
                         sink_ref, bucket_ref,
                         gn_ref, omix_ref, kwin_ref, vwin_ref, stout_ref, lat_scr, kqt_scr, kvt_scr,
                         bias_scr, s_scr, o_scr, og_scr):
    i = pl.program_id(0)
    nb = pfull_ref.shape[0]

    @pl.when(i == 0)
    def _init():
        lat_scr[...] = _split3_rows(pfull_ref[:, F_LA:F_WIDTH].T)
        kqt_scr[0:GLA_QK_WIDTH] = pfull_ref[:, F_KG:F_RG].T.astype(BF16)
        kqt_scr[GLA_QK_WIDTH:2 * GLA_QK_WIDTH] = pfull_ref[:, F_QG:F_KG].T.astype(BF16)
        kvt = kvfull_ref[...].T
        for s in range(nb // SAMPLE_BLK):
            shift = nb - (s + 1) * SAMPLE_BLK
            kvt_scr[s] = pltpu.roll(kvt, shift, 1) if shift else kvt
        bk = jnp.broadcast_to(bucket_ref[...], (ATTN_HEADS, WINDOW))
        acc = jnp.zeros((ATTN_HEADS, WINDOW), F32)
        for b in range(N_BUCKETS):
            acc = jnp.where(bk == b, relbt_ref[:, b:b + 1], acc)
        bias_scr[...] = acc

    lo = lax.broadcasted_iota(jnp.int32, (1, LANES), 1) < HEAD_DIM
    sub = lax.broadcasted_iota(jnp.int32, (ATTN_HEADS, LANES), 0)
    newest = lax.broadcasted_iota(jnp.int32, (KV_WIDTH, WINDOW), 1) == WINDOW - 1
    heads_per_kv = ATTN_HEADS // ATTN_KV_HEADS

    n_of_col = i * SAMPLE_BLK + lax.broadcasted_iota(jnp.int32, (nb, SAMPLE_BLK * LANES), 1) // LANES
    pick = jnp.where(lax.broadcasted_iota(jnp.int32, (nb, SAMPLE_BLK * LANES), 0) == n_of_col,
                     1.0, 0.0).astype(BF16)
    la_b = (jnp.dot(lat_scr[0:GLA_QK_WIDTH], pick, preferred_element_type=F32)
            + jnp.dot(lat_scr[GLA_QK_WIDTH:2 * GLA_QK_WIDTH], pick, preferred_element_type=F32)
            + jnp.dot(lat_scr[2 * GLA_QK_WIDTH:3 * GLA_QK_WIDTH], pick, preferred_element_type=F32))
    kq_b = jnp.dot(kqt_scr[...], pick, preferred_element_type=F32)

    for j in range(SAMPLE_BLK):
        kv_new = kvt_scr[i]
        if j < SAMPLE_BLK - 1:
            kv_new = pltpu.roll(kv_new, SAMPLE_BLK - 1 - j, 1)
        kwin_ref[j] = jnp.where(newest, kv_new[0:KV_WIDTH], pltpu.roll(ck_ref[j], WINDOW - 1, 1))
        vwin_ref[j] = jnp.where(newest, kv_new[KV_WIDTH:2 * KV_WIDTH],
                                pltpu.roll(cv_ref[j], WINDOW - 1, 1))

    for j in range(SAMPLE_BLK):
        qexp = jnp.zeros((ATTN_HEADS, LANES), F32)
        for c in range(ATTN_HEADS // 2):
            chunk = ph_ref[j:j + 1, H_QA + c * LANES:H_QA + (c + 1) * LANES]
            swapped = pltpu.roll(chunk, HEAD_DIM, 1)
            if (2 * c) // heads_per_kv == 0:
                rows = (jnp.where(lo, chunk, 0.0), jnp.where(lo, swapped, 0.0))
            else:
                rows = (jnp.where(lo, 0.0, swapped), jnp.where(lo, 0.0, chunk))
            for e in range(2):
                qexp = jnp.where(sub == 2 * c + e, rows[e], qexp)
        s_scr[j * ATTN_HEADS:(j + 1) * ATTN_HEADS] = jnp.dot(
            qexp.astype(BF16), kwin_ref[j].astype(BF16), preferred_element_type=F32)

    tile = lambda x: jnp.concatenate([x] * SAMPLE_BLK, axis=0)
    sink = tile(sink_ref[...])
    s = s_scr[...] + tile(bias_scr[...])
    m = jnp.maximum(jnp.max(s, axis=-1, keepdims=True), sink)
    pe = jnp.exp(s - m)
    inv_den = 1.0 / (jnp.sum(pe, axis=-1, keepdims=True) + jnp.exp(sink - m))
    peb = pe.astype(BF16)
    for j in range(SAMPLE_BLK):
        o_scr[j * ATTN_HEADS:(j + 1) * ATTN_HEADS] = _nt_dot(
            peb[j * ATTN_HEADS:(j + 1) * ATTN_HEADS], vwin_ref[j].astype(BF16))
    o_all = o_scr[...] * inv_den
    o_swap = pltpu.roll(o_all, HEAD_DIM, 1)
    for j in range(SAMPLE_BLK):
        r = j * ATTN_HEADS
        for c in range(ATTN_HEADS // 2):
            if (2 * c) // heads_per_kv == 0:
                piece = jnp.where(lo, o_all[r + 2 * c:r + 2 * c + 1, :], o_swap[r + 2 * c + 1:r + 2 * c + 2, :])
            else:
                piece = jnp.where(lo, o_swap[r + 2 * c:r + 2 * c + 1, :], o_all[r + 2 * c + 1:r + 2 * c + 2, :])
            omix_ref[j:j + 1, c * LANES:(c + 1) * LANES] = piece

    for j in range(SAMPLE_BLK):
        cols = slice(j * LANES, (j + 1) * LANES)
        for h in range(GLA_HEADS):
            rs = slice(h * GLA_DK, (h + 1) * GLA_DK)
            qs = slice(GLA_QK_WIDTH + h * GLA_DK, GLA_QK_WIDTH + (h + 1) * GLA_DK)
            v_row = ph_ref[j:j + 1, H_VG + h * GLA_DV:H_VG + (h + 1) * GLA_DV]
            s_new = jnp.exp(la_b[rs, cols]) * st_ref[j, h] + kq_b[rs, cols] * v_row
            stout_ref[j, h] = s_new
            og_scr[j:j + 1, h * GLA_DV:(h + 1) * GLA_DV] = jnp.sum(
                kq_b[qs, cols] * s_new, axis=0, keepdims=True)
    for h in range(GLA_HEADS):
        hs = slice(h * GLA_DV, (h + 1) * GLA_DV)
        og = og_scr[:, hs]
        og = og * lax.rsqrt(jnp.mean(og * og, axis=-1, keepdims=True) + EPS) * gn_ref[...]
        rg = ps_ref[:, F_RG + h * GLA_DV:F_RG + (h + 1) * GLA_DV]
        omix_ref[:, ATTN_WIDTH + h * GLA_DV:ATTN_WIDTH + (h + 1) * GLA_DV] = og * (rg * _sigmoid(rg))


def _sample_mixer(ps, ph, kv, cache_k, cache_v, state, relbt, sinks_col, gn):
    nb = ps.shape[0]
    assert nb == LANES
    blk3 = lambda i: (i, 0, 0)
    blk4 = lambda i: (i, 0, 0, 0)
    return pl.pallas_call(
        _sample_mixer_kernel,
        grid=(nb // SAMPLE_BLK,),
        in_specs=[
            pl.BlockSpec((SAMPLE_BLK, F_WIDTH), lambda i: (i, 0)),
            pl.BlockSpec((SAMPLE_BLK, H_WIDTH), lambda i: (i, 0)),
            _const_spec((nb, 2 * KV_WIDTH)),
            _const_spec((nb, F_WIDTH)),
            pl.BlockSpec((SAMPLE_BLK, KV_WIDTH, WINDOW), blk3),
            pl.BlockSpec((SAMPLE_BLK, KV_WIDTH, WINDOW), blk3),
            pl.BlockSpec((SAMPLE_BLK, GLA_HEADS, GLA_DK, GLA_DV), blk4),
            _const_spec((ATTN_HEADS, N_BUCKETS)),
            _const_spec((ATTN_HEADS, 1)),
            _const_spec((1, WINDOW)),
            _const_spec((1, GLA_DV)),
        ],
        out_specs=[
            pl.BlockSpec((SAMPLE_BLK, MIX_WIDTH), lambda i: (i, 0)),
            pl.BlockSpec((SAMPLE_BLK, KV_WIDTH, WINDOW), blk3),
            pl.BlockSpec((SAMPLE_BLK, KV_WIDTH, WINDOW), blk3),
            pl.BlockSpec((SAMPLE_BLK, GLA_HEADS, GLA_DK, GLA_DV), blk4),
        ],
        out_shape=[
            jax.ShapeDtypeStruct((nb, MIX_WIDTH), F32),
            jax.ShapeDtypeStruct((nb, KV_WIDTH, WINDOW), F32),
            jax.ShapeDtypeStruct((nb, KV_WIDTH, WINDOW), F32),
            jax.ShapeDtypeStruct((nb, GLA_HEADS, GLA_DK, GLA_DV), F32),
        ],
        scratch_shapes=[
            pltpu.VMEM((3 * GLA_QK_WIDTH, nb), BF16),
            pltpu.VMEM((2 * GLA_QK_WIDTH, nb), BF16),
            pltpu.VMEM((nb // SAMPLE_BLK, 2 * KV_WIDTH, nb), F32),
            pltpu.VMEM((ATTN_HEADS, WINDOW), F32),
            pltpu.VMEM((SAMPLE_BLK * ATTN_HEADS, WINDOW), F32),
            pltpu.VMEM((SAMPLE_BLK * ATTN_HEADS, KV_WIDTH), F32),
            pltpu.VMEM((SAMPLE_BLK, GLA_WIDTH), F32),
        ],
        compiler_params=pltpu.CompilerParams(
            dimension_semantics=("arbitrary",), vmem_limit_bytes=VMEM_LIMIT),
        name="sample_mixer",
    )(ps, ph, kv, ps, cache_k, cache_v, state, relbt, sinks_col, jnp.asarray(_BUCKET_SAMPLE), gn)


def _finish_kernel(x_ref, mix_ref, xs_ref, mixs_ref, wo_ref, g_ref, wg_ref, wu_ref, wd_ref,
                   y_ref, ys_ref, *, ff_chunks):
    weights = (wo_ref, g_ref, wg_ref, wu_ref, wd_ref)
    _finish_rows(x_ref, mix_ref, *weights, y_ref, ff_chunks=ff_chunks)

    @pl.when(pl.program_id(0) == pl.num_programs(0) - 1)
    def _samples():
        _finish_rows(xs_ref, mixs_ref, *weights, ys_ref, ff_chunks=ff_chunks)


def _finish_rows(x_ref, mix_ref, wo_ref, g_ref, wg_ref, wu_ref, wd_ref, y_ref, *, ff_chunks):
    h = x_ref[...] + jnp.dot(mix_ref[...].astype(BF16), wo_ref[...], preferred_element_type=F32)
    r = lax.rsqrt(jnp.mean(h * h, axis=-1, keepdims=True) + EPS)
    z = (h * g_ref[...]).astype(BF16)
    n_tiles = wd_ref.shape[0] // MXU_TILE
    acc = h
    for c in range(ff_chunks):
        c0 = ((c * n_tiles) // ff_chunks) * MXU_TILE
        c1 = (((c + 1) * n_tiles) // ff_chunks) * MXU_TILE
        gate = jnp.dot(z, wg_ref[:, c0:c1], preferred_element_type=F32) * r
        up = jnp.dot(z, wu_ref[:, c0:c1], preferred_element_type=F32) * r
        act = ((gate * _sigmoid(gate)) * up).astype(BF16)
        acc = acc + jnp.dot(act, wd_ref[c0:c1, :], preferred_element_type=F32)
    y_ref[...] = acc


def _finish(x, mix, xs, mixs, wo, g_ffn, wg, wu, wd, rows):
    t = x.shape[0]
    ns = xs.shape[0]
    d_ff = wd.shape[0]
    assert d_ff % MXU_TILE == 0
    return pl.pallas_call(
        functools.partial(_finish_kernel, ff_chunks=FF_CHUNKS),
        grid=(t // rows,),
        in_specs=[
            pl.BlockSpec((rows, D_MODEL), lambda i: (i, 0)),
            pl.BlockSpec((rows, MIX_WIDTH), lambda i: (i, 0)),
            _sample_rows_spec(ns, pipeline_mode=pl.Buffered(1)),
            _const_spec((ns, MIX_WIDTH)),
            _const_spec((MIX_WIDTH, D_MODEL)),
            _const_spec((1, D_MODEL)),
            _const_spec((D_MODEL, d_ff)),
            _const_spec((D_MODEL, d_ff)),
            _const_spec((d_ff, D_MODEL)),
        ],
        out_specs=[pl.BlockSpec((rows, D_MODEL), lambda i: (i, 0)),
                   _sample_rows_spec(ns)],
        out_shape=[jax.ShapeDtypeStruct((t, D_MODEL), F32),
                   jax.ShapeDtypeStruct((ns, 1, D_MODEL), F32)],
        compiler_params=pltpu.CompilerParams(
            dimension_semantics=("arbitrary",), vmem_limit_bytes=VMEM_LIMIT),
        name="finish",
    )(x, mix, xs, mixs, wo, g_ffn, wg, wu, wd)


PROMPT_ROWS = 1024
PROJ_ROWS = 1024
PREP_COLS = 256
MIX_ROWS = 1024


def kernel(x_prompt, x_sample, cache_k, cache_v, state_gla, attn_norm_g, w_in, q_norm_g, k_norm_g,
           attn_sinks, rel_bias, w_gla_gate2, b_gla_gate, gla_norm_g, w_o, ffn_norm_g, w_gate, w_up,
           w_down):
    depth = w_in.shape[0]
    batch, seq, _ = x_prompt.shape
    dec_batch, dec_seq, _ = x_sample.shape
    wb = cache_k.shape[2]
    assert batch == 1 and dec_seq == 1 and wb == WINDOW
    assert seq % PROMPT_ROWS == 0 and seq % PROJ_ROWS == 0
    assert dec_batch % SAMPLE_BLK == 0 and dec_batch % LANES == 0
    assert rel_bias.shape == (N_BUCKETS, ATTN_HEADS)

    xp = x_prompt.reshape(seq, D_MODEL)
    xs = x_sample
    relb_flat = rel_bias.reshape(-1)
    relb_t = rel_bias.T
    outs = ([], [], [], [], [], [])
    for l in range(depth):
        w_in_p = _weight_prep(w_in[l].T, cols=PREP_COLS)
        w2p = jnp.pad(w_gla_gate2[l], ((0, RANK_PAD - GLA_RANK), (0, 0))).astype(BF16)
        proj_w = (attn_norm_g[l][None, :], w_in_p,
                  jnp.tile(q_norm_g[l], ATTN_HEADS)[None, :] * ATTN_SCALE,
                  jnp.tile(k_norm_g[l], ATTN_KV_HEADS)[None, :], w2p, b_gla_gate[l][None, :])
        gn = gla_norm_g[l][None, :]

        pp, pb, kt_p, kv_win, ps, ph, kv_s, wo_b, wg_b, wu_b, wd_b = _project(
            xp, xs, *proj_w, rows=PROJ_ROWS, to_cast=(w_o[l], w_gate[l], w_up[l], w_down[l]))
        fin_w = (wo_b, ffn_norm_g[l][None, :], wg_b, wu_b, wd_b)
        mix_p, st_p = _prompt_mixer(pp, pb, kt_p, relb_flat, attn_sinks[l], gn)
        to_t = lambda c: jnp.transpose(c, (0, 2, 3, 1)).reshape(dec_batch, KV_WIDTH, wb)
        from_t = lambda c: jnp.transpose(c.reshape(dec_batch, ATTN_KV_HEADS, HEAD_DIM, wb), (0, 3, 1, 2))
        mix_s, kwin_t, vwin_t, st_s = _sample_mixer(
            ps, ph, kv_s, to_t(cache_k[l]), to_t(cache_v[l]),
            state_gla[l].astype(F32), relb_t, attn_sinks[l][:, None], gn)
        xp_in = xp
        xp, xs = _finish(xp_in, mix_p, xs, mix_s, *fin_w, rows=PROMPT_ROWS)
        outs[0].append(kv_win[:, :KV_WIDTH].reshape(batch, wb, ATTN_KV_HEADS, HEAD_DIM))
        outs[1].append(kv_win[:, KV_WIDTH:].reshape(batch, wb, ATTN_KV_HEADS, HEAD_DIM))
        outs[2].append(st_p.T.reshape(batch, GLA_HEADS, GLA_DK, GLA_DV).astype(state_gla.dtype))
        outs[3].append(from_t(kwin_t))
        outs[4].append(from_t(vwin_t))
        outs[5].append(st_s.astype(state_gla.dtype))

    y_prompt = xp.reshape(batch, seq, D_MODEL)
    y_sample = xs
    return (y_prompt, y_sample) + tuple(jnp.stack(o) for o in outs)
```

```python
import functools
import math

import numpy as np
import jax
import jax.numpy as jnp
from jax import lax
from jax.experimental import pallas as pl
from jax.experimental.pallas import tpu as pltpu

F32 = jnp.float32
BF16 = jnp.bfloat16

D_MODEL = 1024
HEAD_DIM = 64
ATTN_HEADS = 8
ATTN_KV_HEADS = 2
WINDOW = 128
N_BUCKETS = 32
MAX_DISTANCE = 128
GLA_HEADS = 4
GLA_DK = 64
GLA_DV = 128
GLA_RANK = 16
GLA_TAU = 16.0
EPS = 1e-6
ATTN_WIDTH = ATTN_HEADS * HEAD_DIM
KV_WIDTH = ATTN_KV_HEADS * HEAD_DIM
GLA_QK_WIDTH = GLA_HEADS * GLA_DK
GLA_WIDTH = GLA_HEADS * GLA_DV
MIX_WIDTH = ATTN_WIDTH + GLA_WIDTH
MAIN_WIDTH = ATTN_WIDTH + 2 * KV_WIDTH + 2 * GLA_QK_WIDTH + 2 * GLA_WIDTH
LANES = 128
SUBLANES = 8
MXU_TILE = 256
FF_CHUNKS = 4
RANK_PAD = LANES
IN_PAD_WIDTH = MAIN_WIDTH + RANK_PAD

F_QG = 0
F_KG = F_QG + GLA_QK_WIDTH
F_RG = F_KG + GLA_QK_WIDTH
F_LA = F_RG + GLA_WIDTH
F_WIDTH = F_LA + GLA_QK_WIDTH
H_QA = 0
H_VA = H_QA + ATTN_WIDTH
H_VS = H_VA + KV_WIDTH
H_VG = H_VS + KV_WIDTH
H_WIDTH = H_VG + GLA_WIDTH

W_QA = 0
W_KV = W_QA + ATTN_WIDTH
W_QG = W_KV + 2 * KV_WIDTH + RANK_PAD
W_VG = W_QG + 2 * GLA_QK_WIDTH
W_RG = W_VG + GLA_WIDTH

BLK = 128
N_LEVELS = 7
NEG = -1e30
ATTN_SCALE = HEAD_DIM ** -0.5
SAMPLE_BLK = 16
VMEM_LIMIT = 56 * 1024 * 1024


def _t5_bucket_np(dist):
    n = np.maximum(dist, 0)
    max_exact = N_BUCKETS // 2
    nf = np.maximum(n, 1).astype(np.float64)
    large = max_exact + (np.log(nf / max_exact) / math.log(MAX_DISTANCE / max_exact)
                         * (N_BUCKETS - max_exact)).astype(np.int32)
    large = np.minimum(large, N_BUCKETS - 1)
    return np.where(n < max_exact, n, large).astype(np.int32)


def _prompt_bucket_tables():
    assert WINDOW == BLK
    i = np.arange(BLK)[:, None]
    j = np.arange(BLK)[None, :]
    own = j <= i
    bucket = _t5_bucket_np(np.where(own, i - j, BLK + i - j))
    t1 = np.where(own, bucket, -1)
    return np.stack([bucket, t1]).astype(np.int32)


def _level_tables():
    t = np.arange(BLK)[:, None]
    s = np.arange(BLK)[None, :]
    x = t ^ s
    lev = np.where(x > 0, np.floor(np.log2(np.maximum(x, 1))).astype(np.int32) + 1, 0)
    lev = np.where(s > t, -1, lev).astype(np.int32)
    tri = (s <= t).astype(np.float32)
    return lev, np.concatenate([tri, tri, tri], axis=1)


_BUCKET_PROMPT = _prompt_bucket_tables()
_LEV, _TRI3 = _level_tables()
_BUCKET_SAMPLE = _t5_bucket_np((WINDOW - 1) - np.arange(WINDOW))[None, :].astype(np.int32)


def _nt_dot(a, b):
    return lax.dot_general(a, b, (((1,), (1,)), ((), ())), preferred_element_type=F32)


def _tn_dot(a, b):
    return lax.dot_general(a, b, (((0,), (0,)), ((), ())), preferred_element_type=F32)


def _head_mean_sq(x):
    lo = lax.broadcasted_iota(jnp.int32, (x.shape[0], LANES), 1) < HEAD_DIM
    outs = []
    for c in range(x.shape[1] // LANES):
        y = x[:, c * LANES:(c + 1) * LANES]
        y = y * y
        s_lo = jnp.sum(jnp.where(lo, y, 0.0), axis=-1, keepdims=True)
        s_hi = jnp.sum(jnp.where(lo, 0.0, y), axis=-1, keepdims=True)
        outs.append(jnp.where(lo, s_lo, s_hi) * (1.0 / HEAD_DIM))
    return outs[0] if len(outs) == 1 else jnp.concatenate(outs, axis=1)


def _sigmoid(x):
    return 1.0 / (1.0 + jnp.exp(-x))


def _split3_rows(x):
    hi = x.astype(BF16)
    r1 = x - hi.astype(F32)
    mid = r1.astype(BF16)
    lo = (r1 - mid.astype(F32)).astype(BF16)
    return jnp.concatenate([hi, mid, lo], axis=0)


def _proj_kernel(x_ref, xs_ref, g_ref, w_ref, qn_ref, kn_ref, w2_ref, b2_ref, tri_ref, *rest):
    n_out = 7
    n_cast = (len(rest) - n_out) // 2
    cast_in, cast_out = rest[:n_cast], rest[n_cast + n_out:]
    f32_ref, bf_ref, kt_ref, kvw_ref, f32s_ref, bfs_ref, kvs_ref = rest[n_cast:n_cast + n_out]
    for src, dst in zip(cast_in, cast_out):
        dst[...] = src[...].astype(BF16)
    weights = (g_ref, w_ref, qn_ref, kn_ref, w2_ref, b2_ref, tri_ref)
    _proj_rows(x_ref, *weights, f32_ref, bf_ref, kt_ref, kvw_ref, block_cumsum=True)

    @pl.when(pl.program_id(0) == pl.num_programs(0) - 1)
    def _samples():
        _proj_rows(xs_ref, *weights, f32s_ref, bfs_ref, None, kvs_ref, block_cumsum=False)


def _proj_rows(x_ref, g_ref, w_ref, qn_ref, kn_ref, w2_ref, b2_ref, tri_ref, f32_ref, bf_ref,
               kt_ref, kv_ref, *, block_cumsum):
    x = x_ref[...]
    r = lax.rsqrt(jnp.mean(x * x, axis=-1, keepdims=True) + EPS)
    n = (x * g_ref[...]).astype(BF16)

    def seg(c0, c1):
        return _nt_dot(n, w_ref[c0:c1, :]) * r

    kvl = seg(W_KV, W_QG)
    lr = kvl[:, 2 * KV_WIDTH:].astype(BF16)
    z = jnp.dot(lr, w2_ref[...], preferred_element_type=F32) + b2_ref[...]
    log_a = (jnp.minimum(z, 0.0) - jnp.log(1.0 + jnp.exp(-jnp.abs(z)))) / GLA_TAU
    if block_cumsum:
        for blk in range(x.shape[0] // BLK):
            rows = slice(blk * BLK, (blk + 1) * BLK)
            f32_ref[rows, F_LA:F_WIDTH] = jnp.dot(
                tri_ref[...], _split3_rows(log_a[rows]), preferred_element_type=F32)
    else:
        f32_ref[:, F_LA:F_WIDTH] = log_a
    k = kvl[:, :KV_WIDTH]
    k = k * lax.rsqrt(_head_mean_sq(k) + EPS) * kn_ref[...]
    v = kvl[:, KV_WIDTH:2 * KV_WIDTH]
    if kt_ref is None:
        kv_ref[:, 0:KV_WIDTH] = k
        kv_ref[:, KV_WIDTH:2 * KV_WIDTH] = v
    else:
        @pl.when(pl.program_id(0) == pl.num_programs(0) - 1)
        def _window():
            first = x.shape[0] - kv_ref.shape[1]
            kv_ref[0:KV_WIDTH, :] = k[first:].T
            kv_ref[KV_WIDTH:2 * KV_WIDTH, :] = v[first:].T
    if kt_ref is not None:
        kt_ref[...] = k.T.astype(BF16)
    hdt = bf_ref.dtype
    bf_ref[:, H_VA:H_VS] = v.astype(hdt)
    bf_ref[:, H_VS:H_VG] = pltpu.roll(v, HEAD_DIM, 1).astype(hdt)
    q = seg(W_QA, W_KV)
    bf_ref[:, H_QA:H_VA] = (q * lax.rsqrt(_head_mean_sq(q) + EPS) * qn_ref[...]).astype(hdt)
    qk_g = seg(W_QG, W_VG)
    f32_ref[:, F_QG:F_KG] = qk_g[:, :GLA_QK_WIDTH] * (GLA_DK ** -0.5)
    f32_ref[:, F_KG:F_RG] = qk_g[:, GLA_QK_WIDTH:]
    bf_ref[:, H_VG:H_WIDTH] = seg(W_VG, W_RG).astype(hdt)
    f32_ref[:, F_RG:F_LA] = seg(W_RG, IN_PAD_WIDTH)


def _const_spec(shape):
    nd = len(shape)
    return pl.BlockSpec(shape, lambda i: (0,) * nd, pipeline_mode=pl.Buffered(1))


def _sample_rows_spec(ns, **kwargs):
    return pl.BlockSpec((ns, None, D_MODEL), lambda i: (0, 0, 0), **kwargs)


def _weight_prep_kernel(w_ref, w2_ref, o_ref, o2_ref):
    o2_ref[...] = jnp.zeros(o2_ref.shape, BF16)
    o2_ref[0:GLA_RANK] = w2_ref[...].astype(BF16)
    kv_end = W_KV + 2 * KV_WIDTH
    o_ref[W_QA:kv_end] = w_ref[0:kv_end].astype(BF16)
    o_ref[kv_end:W_QG] = jnp.zeros((RANK_PAD, o_ref.shape[1]), BF16)
    o_ref[kv_end:kv_end + GLA_RANK] = w_ref[MAIN_WIDTH:MAIN_WIDTH + GLA_RANK].astype(BF16)
    o_ref[W_QG:IN_PAD_WIDTH] = w_ref[kv_end:MAIN_WIDTH].astype(BF16)


def _weight_prep(w_t, w2, cols):
    n, k = w_t.shape
    assert n == MAIN_WIDTH + GLA_RANK and k % cols == 0 and w2.shape == (GLA_RANK, GLA_QK_WIDTH)
    return pl.pallas_call(
        _weight_prep_kernel,
        grid=(k // cols,),
        in_specs=[pl.BlockSpec((n, cols), lambda i: (0, i)),
                  _const_spec((GLA_RANK, GLA_QK_WIDTH))],
        out_specs=[pl.BlockSpec((IN_PAD_WIDTH, cols), lambda i: (0, i)),
                   pl.BlockSpec((RANK_PAD, GLA_QK_WIDTH), lambda i: (0, 0))],
        out_shape=[jax.ShapeDtypeStruct((IN_PAD_WIDTH, k), BF16),
                   jax.ShapeDtypeStruct((RANK_PAD, GLA_QK_WIDTH), BF16)],
        compiler_params=pltpu.CompilerParams(
            dimension_semantics=("arbitrary",), vmem_limit_bytes=VMEM_LIMIT),
        name="weight_prep",
    )(w_t, w2)


def _project(x, xs, g_attn, w_in_p, qn, kn, w2p, b2, rows, to_cast):
    t = x.shape[0]
    ns = xs.shape[0]
    steps = t // rows
    cast_specs = []
    for w in to_cast:
        slab = w.shape[0] // steps
        assert w.shape[0] % steps == 0 and slab % (2 * SUBLANES) == 0
        cast_specs.append(pl.BlockSpec((slab, w.shape[1]), lambda i: (i, 0)))
    return pl.pallas_call(
        _proj_kernel,
        grid=(steps,),
        in_specs=[
            pl.BlockSpec((rows, D_MODEL), lambda i: (i, 0)),
            _sample_rows_spec(ns, pipeline_mode=pl.Buffered(1)),
            _const_spec((1, D_MODEL)),
            _const_spec((IN_PAD_WIDTH, D_MODEL)),
            _const_spec((1, ATTN_WIDTH)),
            _const_spec((1, KV_WIDTH)),
            _const_spec((RANK_PAD, GLA_QK_WIDTH)),
            _const_spec((1, GLA_QK_WIDTH)),
            _const_spec((BLK, 3 * BLK)),
        ] + cast_specs,
        out_specs=[pl.BlockSpec((rows, F_WIDTH), lambda i: (i, 0)),
                   pl.BlockSpec((rows, H_WIDTH), lambda i: (i, 0)),
                   pl.BlockSpec((KV_WIDTH, rows), lambda i: (0, i)),
                   pl.BlockSpec((2 * KV_WIDTH, WINDOW), lambda i: (0, 0)),
                   pl.BlockSpec((ns, F_WIDTH), lambda i: (0, 0)),
                   pl.BlockSpec((ns, H_WIDTH), lambda i: (0, 0)),
                   pl.BlockSpec((ns, 2 * KV_WIDTH), lambda i: (0, 0))] + cast_specs,
        out_shape=[jax.ShapeDtypeStruct((t, F_WIDTH), F32),
                   jax.ShapeDtypeStruct((t, H_WIDTH), BF16),
                   jax.ShapeDtypeStruct((KV_WIDTH, t), BF16),
                   jax.ShapeDtypeStruct((2 * KV_WIDTH, WINDOW), F32),
                   jax.ShapeDtypeStruct((ns, F_WIDTH), F32),
                   jax.ShapeDtypeStruct((ns, H_WIDTH), F32),
                   jax.ShapeDtypeStruct((ns, 2 * KV_WIDTH), F32)]
        + [jax.ShapeDtypeStruct(w.shape, BF16) for w in to_cast],
        compiler_params=pltpu.CompilerParams(
            dimension_semantics=("arbitrary",), vmem_limit_bytes=VMEM_LIMIT),
        name="proj",
    )(x, xs, g_attn, w_in_p, qn, kn, w2p, b2, jnp.asarray(_TRI3, BF16), *to_cast)


def _boundary_rows(b_ref, r0, c0, b, row, level):
    m = 1 << (level - 1)
    if 2 * m >= SUBLANES:
        pieces = [jnp.broadcast_to(b_ref[r0 + g * 2 * m + m - 1:r0 + g * 2 * m + m, c0:c0 + LANES],
                                   (2 * m, LANES))
                  for g in range(BLK // (2 * m))]
        return pieces[0] if len(pieces) == 1 else jnp.concatenate(pieces, axis=0)
    pos = row & (2 * m - 1)
    tiles = b.reshape(BLK // SUBLANES, SUBLANES, LANES)
    out = b
    for p in range(2 * m):
        shift = (m - 1) - p
        if shift != 0:
            rolled = pltpu.roll(tiles, (-shift) % SUBLANES, 1).reshape(BLK, LANES)
            out = jnp.where(pos == p, rolled, out)
    return out


def _mixer_init(relb_ref, bucket_ref, kbd, vbd, kprev_t, vprev, st_scr, mb_scr):
    kbd[...] = jnp.zeros_like(kbd)
    vbd[...] = jnp.zeros_like(vbd)
    for slot in range(vbd.shape[0]):
        for g in range(ATTN_KV_HEADS):
            for half in range(2):
                vbd[slot, g, 2 * half * BLK:(2 * half + 2) * BLK,
                    KV_WIDTH + half * HEAD_DIM:KV_WIDTH + (half + 1) * HEAD_DIM] = (
                        jnp.ones((2 * BLK, HEAD_DIM), BF16))
    kprev_t[...] = jnp.zeros_like(kprev_t)
    vprev[...] = jnp.zeros_like(vprev)
    st_scr[...] = jnp.zeros_like(st_scr)
    bk = bucket_ref[0]
    acc = [jnp.zeros(bk.shape, F32) for _ in range(ATTN_HEADS)]
    for b in range(N_BUCKETS):
        hit = bk == b
        for h in range(ATTN_HEADS):
            acc[h] = jnp.where(hit, relb_ref[h, b], acc[h])
    for tb in range(2):
        masked = bucket_ref[tb] < 0
        for h in range(ATTN_HEADS):
            mb_scr[tb, h // 2, :, (h % 2) * BLK:(h % 2 + 1) * BLK] = jnp.where(masked, NEG, acc[h])


def _mixer_attention(pb_ref, kt_ref, r0, table, sink_ref, omix_ref, kbd, vbd, k_prev_t, v_prev,
                     mb_scr):
    rows = slice(r0, r0 + BLK)
    lo1 = lax.broadcasted_iota(jnp.int32, (BLK, LANES), 1) < HEAD_DIM
    k_t = kt_ref[:, r0:r0 + BLK]
    for g in range(ATTN_KV_HEADS):
        hd = slice(g * HEAD_DIM, (g + 1) * HEAD_DIM)
        kbd[g, 0:HEAD_DIM, 0:BLK] = k_prev_t[hd]
        kbd[g, 0:HEAD_DIM, BLK:2 * BLK] = k_t[hd]
        kbd[g, HEAD_DIM:2 * HEAD_DIM, 2 * BLK:3 * BLK] = k_prev_t[hd]
        kbd[g, HEAD_DIM:2 * HEAD_DIM, 3 * BLK:4 * BLK] = k_t[hd]
    lo_bf = jnp.where(lo1, 1.0, 0.0).astype(BF16)
    hi_bf = jnp.where(lo1, 0.0, 1.0).astype(BF16)
    lo_row = lax.broadcasted_iota(jnp.int32, (1, LANES), 1) < HEAD_DIM
    own = (lax.broadcasted_iota(jnp.int32, (BLK, BLK), 1)
           <= lax.broadcasted_iota(jnp.int32, (BLK, BLK), 0))
    own_bf = jnp.where(own, 1.0, 0.0).astype(BF16)
    prev_bf = jnp.where(own, 0.0, 1.0).astype(BF16)
    v_cur = pb_ref[rows,H_VA:H_VS]
    v_swap = pb_ref[rows,H_VS:H_VG]
    v_parts = (v_cur * lo_bf, v_swap * hi_bf, v_swap * lo_bf, v_cur * hi_bf)
    for n, part in enumerate(v_parts):
        g, half = divmod(n, 2)
        vbd[g, 2 * half * BLK:(2 * half + 1) * BLK, 0:KV_WIDTH] = v_prev[n]
        vbd[g, (2 * half + 1) * BLK:(2 * half + 2) * BLK, 0:KV_WIDTH] = part
    chunks_per_kv = ATTN_HEADS // ATTN_KV_HEADS // 2
    for g in range(ATTN_KV_HEADS):
        c_first = g * chunks_per_kv
        qs = jnp.concatenate(
            [pb_ref[rows,H_QA + (c_first + c) * LANES:H_QA + (c_first + c + 1) * LANES]
             for c in range(chunks_per_kv)], axis=0)
        s = jnp.dot(qs, kbd[g], preferred_element_type=F32)
        prob_rows, sink_rows = [], []
        for c in range(chunks_per_kv):
            probs, maxes, sks = [], [], []
            for e in range(2):
                s_prev = s[c * BLK:(c + 1) * BLK, 2 * e * BLK:(2 * e + 1) * BLK]
                s_own = s[c * BLK:(c + 1) * BLK, (2 * e + 1) * BLK:(2 * e + 2) * BLK]
                se = (jnp.where(own, s_own, s_prev)
                      + mb_scr[table, c_first + c, :, e * BLK:(e + 1) * BLK])
                sk = sink_ref[2 * (c_first + c) + e]
                m = jnp.maximum(jnp.max(se, axis=-1, keepdims=True), sk)
                pe = jnp.exp(se - m).astype(BF16)
                probs += [pe * prev_bf, pe * own_bf]
                maxes.append(m)
                sks.append(sk)
            prob_rows.append(jnp.concatenate(probs, axis=1))
            sink_rows.append(jnp.exp(jnp.where(lo_row, sks[0], sks[1])
                                     - jnp.where(lo1, maxes[0], maxes[1])))
        o_den = jnp.dot(jnp.concatenate(prob_rows, axis=0), vbd[g], preferred_element_type=F32)
        o = o_den[:, :KV_WIDTH] / (o_den[:, KV_WIDTH:] + jnp.concatenate(sink_rows, axis=0))
        for c in range(chunks_per_kv):
            omix_ref[rows,(c_first + c) * LANES:(c_first + c + 1) * LANES] = (
                o[c * BLK:(c + 1) * BLK].astype(omix_ref.dtype))
    return k_t, v_parts


def _mixer_gla(state, p_ref, pb_ref, r0, lev_ref, gn_ref, omix_ref):
    new_state = []
    pairs = range(len(state))
    rows = slice(r0, r0 + BLK)
    lo1 = lax.broadcasted_iota(jnp.int32, (BLK, LANES), 1) < HEAD_DIM
    lo_bf = jnp.where(lo1, 1.0, 0.0).astype(BF16)
    hi_bf = jnp.where(lo1, 0.0, 1.0).astype(BF16)
    row = lax.broadcasted_iota(jnp.int32, (BLK, LANES), 0)
    zero_blk = jnp.zeros((BLK, LANES), BF16)
    for c in pairs:
        c0 = c * LANES
        q_at = lambda a, z: p_ref[r0 + a:r0 + z, F_QG + c0:F_QG + c0 + LANES]
        k_at = lambda a, z: p_ref[r0 + a:r0 + z, F_KG + c0:F_KG + c0 + LANES]
        b_at = lambda a, z: p_ref[r0 + a:r0 + z, F_LA + c0:F_LA + c0 + LANES]

        def pair_scores(qtb, ktb_lo, ktb_hi):
            return _nt_dot(qtb, jnp.concatenate([ktb_lo, ktb_hi], axis=0))

        kb = k_at(0, BLK).astype(BF16)
        s0 = pair_scores(q_at(0, BLK).astype(BF16), kb * lo_bf, kb * hi_bf)
        tile = lambda a, n: a[n * SUBLANES:(n + 1) * SUBLANES]
        n_tiles = BLK // SUBLANES
        lev_t = [lev_ref[n * SUBLANES:(n + 1) * SUBLANES, :] for n in range(n_tiles)]
        sc = [[jnp.where(lev_t[n] == 0, tile(s0[:, e * BLK:(e + 1) * BLK], n), 0.0)
               for n in range(n_tiles)] for e in range(2)]
        for level in range(1, N_LEVELS + 1):
            m = 1 << (level - 1)
            if m >= SUBLANES:
                qs, klos, khis, dest = [], [], [], []
                zeros = jnp.zeros((m, LANES), BF16)
                lane_m = lax.broadcasted_iota(jnp.int32, (m, LANES), 1) < HEAD_DIM
                lo_m = jnp.where(lane_m, 1.0, 0.0).astype(BF16)
                hi_m = jnp.where(lane_m, 0.0, 1.0).astype(BF16)
                for g in range(BLK // (2 * m)):
                    lo_a, up_a, up_z = g * 2 * m, g * 2 * m + m, (g + 1) * 2 * m
                    rb = jnp.broadcast_to(b_at(up_a - 1, up_a), (m, LANES))
                    qs.append((q_at(up_a, up_z) * jnp.exp(b_at(up_a, up_z) - rb)).astype(BF16))
                    kp = (k_at(lo_a, up_a) * jnp.exp(rb - b_at(lo_a, up_a))).astype(BF16)
                    klos += [kp * lo_m, zeros]
                    khis += [kp * hi_m, zeros]
                    dest += list(range(up_a // SUBLANES, up_z // SUBLANES))
                sl = pair_scores(jnp.concatenate(qs, axis=0), jnp.concatenate(klos, axis=0),
                                 jnp.concatenate(khis, axis=0))
            else:
                dest = list(range(n_tiles))
                bc = b_at(0, BLK)
                decay = jnp.exp(-jnp.abs(bc - _boundary_rows(p_ref, r0, F_LA + c0, bc, row, level)))
                upper = ((row >> (level - 1)) & 1) == 1
                kl = (k_at(0, BLK) * decay).astype(BF16)
                sl = pair_scores(
                    (q_at(0, BLK) * decay).astype(BF16) * jnp.where(upper, 1.0, 0.0).astype(BF16),
                    kl * jnp.where(upper | ~lo1, 0.0, 1.0).astype(BF16),
                    kl * jnp.where(upper | lo1, 0.0, 1.0).astype(BF16))
            for src, n in enumerate(dest):
                for e in range(2):
                    sc[e][n] = jnp.where(lev_t[n] == level, tile(sl[:, e * BLK:(e + 1) * BLK], src),
                                         sc[e][n])
        sc = jnp.concatenate([jnp.concatenate(sc[e], axis=0) for e in range(2)], axis=1)
        qc, kc, bc = q_at(0, BLK), k_at(0, BLK), b_at(0, BLK)

        b_last = bc[BLK - 1:BLK, :]
        v0 = pb_ref[rows,H_VG + 2 * c0:H_VG + 2 * c0 + LANES]
        v1 = pb_ref[rows,H_VG + 2 * c0 + LANES:H_VG + 2 * c0 + 2 * LANES]
        v_bd = jnp.concatenate([jnp.concatenate([v0, zero_blk], axis=1),
                                jnp.concatenate([zero_blk, v1], axis=1)], axis=0)
        st_c = state[c]
        stb = st_c.astype(BF16)
        st_rhs = jnp.concatenate([stb * lo_bf, stb * hi_bf], axis=0)
        o = (jnp.dot(sc.astype(BF16), v_bd, preferred_element_type=F32)
             + _nt_dot((qc * jnp.exp(bc)).astype(BF16), st_rhs))
        kd = (kc * jnp.exp(b_last - bc)).astype(BF16)
        upd = _tn_dot(jnp.concatenate([v0, v1], axis=1), kd)
        new_state.append(st_c * jnp.exp(b_last) + jnp.where(lo1, upd[:BLK], upd[BLK:]))
        for e in range(2):
            h = 2 * c + e
            oh = o[:, e * LANES:(e + 1) * LANES]
            og = oh * lax.rsqrt(jnp.mean(oh * oh, axis=-1, keepdims=True) + EPS) * gn_ref[...]
            rg = p_ref[rows,F_RG + h * GLA_DV:F_RG + (h + 1) * GLA_DV]
            gated = og * (rg * _sigmoid(rg))
            omix_ref[rows,ATTN_WIDTH + h * GLA_DV:ATTN_WIDTH + (h + 1) * GLA_DV] = (
                gated.astype(omix_ref.dtype))
    return new_state


def _prompt_mixer_kernel(relb_ref, sink_ref, p_ref, pb_ref, kt_ref, bucket_ref, lev_ref, gn_ref,
                         omix_ref, st_ref, kbd, vbd, kprev_t, vprev, st_scr, mb_scr):
    i = pl.program_id(0)

    @pl.when(i == 0)
    def _init():
        _mixer_init(relb_ref, bucket_ref, kbd, vbd, kprev_t, vprev, st_scr, mb_scr)

    k_prev_t = kprev_t[...]
    v_prev = [vprev[n] for n in range(2 * ATTN_KV_HEADS)]
    state = [st_scr[:, c * LANES:(c + 1) * LANES] for c in range(GLA_HEADS // 2)]
    n_blocks = p_ref.shape[0] // BLK
    for jb in range(n_blocks):
        table = jnp.where(i == 0, 1, 0) if jb == 0 else 0
        k_prev_t, v_prev = _mixer_attention(pb_ref, kt_ref, jb * BLK, table, sink_ref, omix_ref,
                                            kbd.at[jb], vbd.at[jb], k_prev_t, v_prev, mb_scr)
        state = _mixer_gla(state, p_ref, pb_ref, jb * BLK, lev_ref, gn_ref, omix_ref)
    kprev_t[...] = k_prev_t
    for n, part in enumerate(v_prev):
        vprev[n] = part
    for c, st_c in enumerate(state):
        st_scr[:, c * LANES:(c + 1) * LANES] = st_c

    @pl.when(i == pl.num_programs(0) - 1)
    def _final_state():
        for c, st_c in enumerate(state):
            st_ref[c * LANES:(c + 1) * LANES, :] = st_c.T


def _prompt_mixer(p, pb, kt, relb, sinks, gn):
    t = p.shape[0]
    smem = pl.BlockSpec(memory_space=pltpu.SMEM)
    return pl.pallas_call(
        _prompt_mixer_kernel,
        grid=(t // MIX_ROWS,),
        in_specs=[
            smem, smem,
            pl.BlockSpec((MIX_ROWS, F_WIDTH), lambda i: (i, 0)),
            pl.BlockSpec((MIX_ROWS, H_WIDTH), lambda i: (i, 0)),
            pl.BlockSpec((KV_WIDTH, MIX_ROWS), lambda i: (0, i)),
            _const_spec((2, BLK, BLK)),
            _const_spec((BLK, BLK)),
            _const_spec((1, GLA_DV)),
        ],
        out_specs=[
            pl.BlockSpec((MIX_ROWS, MIX_WIDTH), lambda i: (i, 0)),
            pl.BlockSpec((GLA_QK_WIDTH, GLA_DV), lambda i: (0, 0)),
        ],
        out_shape=[
            jax.ShapeDtypeStruct((t, MIX_WIDTH), BF16),
            jax.ShapeDtypeStruct((GLA_QK_WIDTH, GLA_DV), F32),
        ],
        scratch_shapes=[
            pltpu.VMEM((MIX_ROWS // BLK, ATTN_KV_HEADS, 2 * HEAD_DIM, 4 * BLK), BF16),
            pltpu.VMEM((MIX_ROWS // BLK, ATTN_KV_HEADS, 4 * BLK, 2 * KV_WIDTH), BF16),
            pltpu.VMEM((KV_WIDTH, BLK), BF16),
            pltpu.VMEM((2 * ATTN_KV_HEADS, BLK, KV_WIDTH), BF16),
            pltpu.VMEM((GLA_DV, GLA_QK_WIDTH), F32),
            pltpu.VMEM((2, ATTN_HEADS // 2, BLK, 2 * BLK), F32),
        ],
        compiler_params=pltpu.CompilerParams(
            dimension_semantics=("arbitrary",), vmem_limit_bytes=VMEM_LIMIT),
        name="prompt_mixer",
    )(relb, sinks, p, pb, kt, jnp.asarray(_BUCKET_PROMPT), jnp.asarray(_LEV), gn)


def _sample_mixer_kernel(ps_ref, ph_ref, kvfull_ref, pfull_ref, ck_ref, cv_ref, st_ref, relbt_ref,
                         sink_ref, bucket_ref,
                         gn_ref, omix_ref, kwin_ref, vwin_ref, stout_ref, lat_scr, kqt_scr, kvt_scr,
                         bias_scr, s_scr, o_scr, og_scr):
    i = pl.program_id(0)
    nb = pfull_ref.shape[0]

    @pl.when(i == 0)
    def _init():
        lat_scr[...] = _split3_rows(pfull_ref[:, F_LA:F_WIDTH].T)
        kqt_scr[0:GLA_QK_WIDTH] = pfull_ref[:, F_KG:F_RG].T.astype(BF16)
        kqt_scr[GLA_QK_WIDTH:2 * GLA_QK_WIDTH] = pfull_ref[:, F_QG:F_KG].T.astype(BF16)
        kvt = kvfull_ref[...].T
        for s in range(nb // SAMPLE_BLK):
            shift = nb - (s + 1) * SAMPLE_BLK
            kvt_scr[s] = pltpu.roll(kvt, shift, 1) if shift else kvt
        bk = jnp.broadcast_to(bucket_ref[...], (ATTN_HEADS, WINDOW))
        acc = jnp.zeros((ATTN_HEADS, WINDOW), F32)
        for b in range(N_BUCKETS):
            acc = jnp.where(bk == b, relbt_ref[:, b:b + 1], acc)
        bias_scr[...] = acc

    lo = lax.broadcasted_iota(jnp.int32, (1, LANES), 1) < HEAD_DIM
    sub = lax.broadcasted_iota(jnp.int32, (ATTN_HEADS, LANES), 0)
    newest = lax.broadcasted_iota(jnp.int32, (KV_WIDTH, WINDOW), 1) == WINDOW - 1
    heads_per_kv = ATTN_HEADS // ATTN_KV_HEADS

    n_of_col = i * SAMPLE_BLK + lax.broadcasted_iota(jnp.int32, (nb, SAMPLE_BLK * LANES), 1) // LANES
    pick = jnp.where(lax.broadcasted_iota(jnp.int32, (nb, SAMPLE_BLK * LANES), 0) == n_of_col,
                     1.0, 0.0).astype(BF16)
    la_b = (jnp.dot(lat_scr[0:GLA_QK_WIDTH], pick, preferred_element_type=F32)
            + jnp.dot(lat_scr[GLA_QK_WIDTH:2 * GLA_QK_WIDTH], pick, preferred_element_type=F32)
            + jnp.dot(lat_scr[2 * GLA_QK_WIDTH:3 * GLA_QK_WIDTH], pick, preferred_element_type=F32))
    kq_b = jnp.dot(kqt_scr[...], pick, preferred_element_type=F32)

    for j in range(SAMPLE_BLK):
        kv_new = kvt_scr[i]
        if j < SAMPLE_BLK - 1:
            kv_new = pltpu.roll(kv_new, SAMPLE_BLK - 1 - j, 1)
        kwin_ref[j] = jnp.where(newest, kv_new[0:KV_WIDTH], pltpu.roll(ck_ref[j], WINDOW - 1, 1))
        vwin_ref[j] = jnp.where(newest, kv_new[KV_WIDTH:2 * KV_WIDTH],
                                pltpu.roll(cv_ref[j], WINDOW - 1, 1))

    for j in range(SAMPLE_BLK):
        qexp = jnp.zeros((ATTN_HEADS, LANES), F32)
        for c in range(ATTN_HEADS // 2):
            chunk = ph_ref[j:j + 1, H_QA + c * LANES:H_QA + (c + 1) * LANES]
            swapped = pltpu.roll(chunk, HEAD_DIM, 1)
            if (2 * c) // heads_per_kv == 0:
                rows = (jnp.where(lo, chunk, 0.0), jnp.where(lo, swapped, 0.0))
            else:
                rows = (jnp.where(lo, 0.0, swapped), jnp.where(lo, 0.0, chunk))
            for e in range(2):
                qexp = jnp.where(sub == 2 * c + e, rows[e], qexp)
        s_scr[j * ATTN_HEADS:(j + 1) * ATTN_HEADS] = jnp.dot(
            qexp.astype(BF16), kwin_ref[j].astype(BF16), preferred_element_type=F32)

    tile = lambda x: jnp.concatenate([x] * SAMPLE_BLK, axis=0)
    head_of_row = lax.broadcasted_iota(jnp.int32, (ATTN_HEADS, 1), 0)
    sink8 = jnp.zeros((ATTN_HEADS, 1), F32)
    for h in range(ATTN_HEADS):
        sink8 = jnp.where(head_of_row == h, sink_ref[h], sink8)
    sink = tile(sink8)
    s = s_scr[...] + tile(bias_scr[...])
    m = jnp.maximum(jnp.max(s, axis=-1, keepdims=True), sink)
    pe = jnp.exp(s - m)
    inv_den = 1.0 / (jnp.sum(pe, axis=-1, keepdims=True) + jnp.exp(sink - m))
    peb = pe.astype(BF16)
    for j in range(SAMPLE_BLK):
        o_scr[j * ATTN_HEADS:(j + 1) * ATTN_HEADS] = _nt_dot(
            peb[j * ATTN_HEADS:(j + 1) * ATTN_HEADS], vwin_ref[j].astype(BF16))
    o_all = o_scr[...] * inv_den
    o_swap = pltpu.roll(o_all, HEAD_DIM, 1)
    for j in range(SAMPLE_BLK):
        r = j * ATTN_HEADS
        for c in range(ATTN_HEADS // 2):
            if (2 * c) // heads_per_kv == 0:
                piece = jnp.where(lo, o_all[r + 2 * c:r + 2 * c + 1, :], o_swap[r + 2 * c + 1:r + 2 * c + 2, :])
            else:
                piece = jnp.where(lo, o_swap[r + 2 * c:r + 2 * c + 1, :], o_all[r + 2 * c + 1:r + 2 * c + 2, :])
            omix_ref[j:j + 1, c * LANES:(c + 1) * LANES] = piece

    for j in range(SAMPLE_BLK):
        cols = slice(j * LANES, (j + 1) * LANES)
        for h in range(GLA_HEADS):
            rs = slice(h * GLA_DK, (h + 1) * GLA_DK)
            qs = slice(GLA_QK_WIDTH + h * GLA_DK, GLA_QK_WIDTH + (h + 1) * GLA_DK)
            v_row = ph_ref[j:j + 1, H_VG + h * GLA_DV:H_VG + (h + 1) * GLA_DV]
            s_new = jnp.exp(la_b[rs, cols]) * st_ref[j, h] + kq_b[rs, cols] * v_row
            stout_ref[j, h] = s_new
            og_scr[j:j + 1, h * GLA_DV:(h + 1) * GLA_DV] = jnp.sum(
                kq_b[qs, cols] * s_new, axis=0, keepdims=True)
    for h in range(GLA_HEADS):
        hs = slice(h * GLA_DV, (h + 1) * GLA_DV)
        og = og_scr[:, hs]
        og = og * lax.rsqrt(jnp.mean(og * og, axis=-1, keepdims=True) + EPS) * gn_ref[...]
        rg = ps_ref[:, F_RG + h * GLA_DV:F_RG + (h + 1) * GLA_DV]
        omix_ref[:, ATTN_WIDTH + h * GLA_DV:ATTN_WIDTH + (h + 1) * GLA_DV] = og * (rg * _sigmoid(rg))


def _sample_mixer(ps, ph, kv, cache_k, cache_v, state, relbt, sinks, gn):
    nb = ps.shape[0]
    assert nb == LANES
    blk3 = lambda i: (i, 0, 0)
    blk4 = lambda i: (i, 0, 0, 0)
    return pl.pallas_call(
        _sample_mixer_kernel,
        grid=(nb // SAMPLE_BLK,),
        in_specs=[
            pl.BlockSpec((SAMPLE_BLK, F_WIDTH), lambda i: (i, 0)),
            pl.BlockSpec((SAMPLE_BLK, H_WIDTH), lambda i: (i, 0)),
            _const_spec((nb, 2 * KV_WIDTH)),
            _const_spec((nb, F_WIDTH)),
            pl.BlockSpec((SAMPLE_BLK, KV_WIDTH, WINDOW), blk3),
            pl.BlockSpec((SAMPLE_BLK, KV_WIDTH, WINDOW), blk3),
            pl.BlockSpec((SAMPLE_BLK, GLA_HEADS, GLA_DK, GLA_DV), blk4),
            _const_spec((ATTN_HEADS, N_BUCKETS)),
            pl.BlockSpec(memory_space=pltpu.SMEM),
            _const_spec((1, WINDOW)),
            _const_spec((1, GLA_DV)),
        ],
        out_specs=[
            pl.BlockSpec((SAMPLE_BLK, MIX_WIDTH), lambda i: (i, 0)),
            pl.BlockSpec((SAMPLE_BLK, KV_WIDTH, WINDOW), blk3),
            pl.BlockSpec((SAMPLE_BLK, KV_WIDTH, WINDOW), blk3),
            pl.BlockSpec((SAMPLE_BLK, GLA_HEADS, GLA_DK, GLA_DV), blk4),
        ],
        out_shape=[
            jax.ShapeDtypeStruct((nb, MIX_WIDTH), F32),
            jax.ShapeDtypeStruct((nb, KV_WIDTH, WINDOW), F32),
            jax.ShapeDtypeStruct((nb, KV_WIDTH, WINDOW), F32),
            jax.ShapeDtypeStruct((nb, GLA_HEADS, GLA_DK, GLA_DV), F32),
        ],
        scratch_shapes=[
            pltpu.VMEM((3 * GLA_QK_WIDTH, nb), BF16),
            pltpu.VMEM((2 * GLA_QK_WIDTH, nb), BF16),
            pltpu.VMEM((nb // SAMPLE_BLK, 2 * KV_WIDTH, nb), F32),
            pltpu.VMEM((ATTN_HEADS, WINDOW), F32),
            pltpu.VMEM((SAMPLE_BLK * ATTN_HEADS, WINDOW), F32),
            pltpu.VMEM((SAMPLE_BLK * ATTN_HEADS, KV_WIDTH), F32),
            pltpu.VMEM((SAMPLE_BLK, GLA_WIDTH), F32),
        ],
        compiler_params=pltpu.CompilerParams(
            dimension_semantics=("arbitrary",), vmem_limit_bytes=VMEM_LIMIT),
        name="sample_mixer",
    )(ps, ph, kv, ps, cache_k, cache_v, state, relbt, sinks, jnp.asarray(_BUCKET_SAMPLE), gn)


def _finish_kernel(x_ref, mix_ref, xs_ref, mixs_ref, wo_ref, g_ref, wg_ref, wu_ref, wd_ref,
                   y_ref, ys_ref, *, ff_chunks):
    weights = (wo_ref, g_ref, wg_ref, wu_ref, wd_ref)
    _finish_rows(x_ref, mix_ref, *weights, y_ref, ff_chunks=ff_chunks)

    @pl.when(pl.program_id(0) == pl.num_programs(0) - 1)
    def _samples():
        _finish_rows(xs_ref, mixs_ref, *weights, ys_ref, ff_chunks=ff_chunks)


def _finish_rows(x_ref, mix_ref, wo_ref, g_ref, wg_ref, wu_ref, wd_ref, y_ref, *, ff_chunks):
    h = x_ref[...] + jnp.dot(mix_ref[...].astype(BF16), wo_ref[...], preferred_element_type=F32)
    r = lax.rsqrt(jnp.mean(h * h, axis=-1, keepdims=True) + EPS)
    z = (h * g_ref[...]).astype(BF16)
    n_tiles = wd_ref.shape[0] // MXU_TILE
    acc = h
    for c in range(ff_chunks):
        c0 = ((c * n_tiles) // ff_chunks) * MXU_TILE
        c1 = (((c + 1) * n_tiles) // ff_chunks) * MXU_TILE
        gate = jnp.dot(z, wg_ref[:, c0:c1], preferred_element_type=F32) * r
        up = jnp.dot(z, wu_ref[:, c0:c1], preferred_element_type=F32) * r
        act = ((gate * _sigmoid(gate)) * up).astype(BF16)
        acc = acc + jnp.dot(act, wd_ref[c0:c1, :], preferred_element_type=F32)
    y_ref[...] = acc


def _finish(x, mix, xs, mixs, wo, g_ffn, wg, wu, wd, rows):
    t = x.shape[0]
    ns = xs.shape[0]
    d_ff = wd.shape[0]
    assert d_ff % MXU_TILE == 0
    return pl.pallas_call(
        functools.partial(_finish_kernel, ff_chunks=FF_CHUNKS),
        grid=(t // rows,),
        in_specs=[
            pl.BlockSpec((rows, D_MODEL), lambda i: (i, 0)),
            pl.BlockSpec((rows, MIX_WIDTH), lambda i: (i, 0)),
            _sample_rows_spec(ns, pipeline_mode=pl.Buffered(1)),
            _const_spec((ns, MIX_WIDTH)),
            _const_spec((MIX_WIDTH, D_MODEL)),
            _const_spec((1, D_MODEL)),
            _const_spec((D_MODEL, d_ff)),
            _const_spec((D_MODEL, d_ff)),
            _const_spec((d_ff, D_MODEL)),
        ],
        out_specs=[pl.BlockSpec((rows, D_MODEL), lambda i: (i, 0)),
                   _sample_rows_spec(ns)],
        out_shape=[jax.ShapeDtypeStruct((t, D_MODEL), F32),
                   jax.ShapeDtypeStruct((ns, 1, D_MODEL), F32)],
        compiler_params=pltpu.CompilerParams(
            dimension_semantics=("arbitrary",), vmem_limit_bytes=VMEM_LIMIT),
        name="finish",
    )(x, mix, xs, mixs, wo, g_ffn, wg, wu, wd)


PROMPT_ROWS = 1024
PROJ_ROWS = 1024
PREP_COLS = 256
MIX_ROWS = 1024


def kernel(x_prompt, x_sample, cache_k, cache_v, state_gla, attn_norm_g, w_in, q_norm_g, k_norm_g,
           attn_sinks, rel_bias, w_gla_gate2, b_gla_gate, gla_norm_g, w_o, ffn_norm_g, w_gate, w_up,
           w_down):
    depth = w_in.shape[0]
    batch, seq, _ = x_prompt.shape
    dec_batch, dec_seq, _ = x_sample.shape
    wb = cache_k.shape[2]
    assert batch == 1 and dec_seq == 1 and wb == WINDOW
    assert seq % PROMPT_ROWS == 0 and seq % PROJ_ROWS == 0
    assert dec_batch % SAMPLE_BLK == 0 and dec_batch % LANES == 0
    assert rel_bias.shape == (N_BUCKETS, ATTN_HEADS)

    xp = x_prompt.reshape(seq, D_MODEL)
    xs = x_sample
    relb_t = rel_bias.T
    outs = ([], [], [], [], [], [])
    for l in range(depth):
        w_in_p, w2p = _weight_prep(w_in[l].T, w_gla_gate2[l], cols=PREP_COLS)
        proj_w = (attn_norm_g[l][None, :], w_in_p,
                  jnp.tile(q_norm_g[l], ATTN_HEADS)[None, :] * ATTN_SCALE,
                  jnp.tile(k_norm_g[l], ATTN_KV_HEADS)[None, :], w2p, b_gla_gate[l][None, :])
        gn = gla_norm_g[l][None, :]

        pp, pb, kt_p, kv_win, ps, ph, kv_s, wo_b, wg_b, wu_b, wd_b = _project(
            xp, xs, *proj_w, rows=PROJ_ROWS, to_cast=(w_o[l], w_gate[l], w_up[l], w_down[l]))
        fin_w = (wo_b, ffn_norm_g[l][None, :], wg_b, wu_b, wd_b)
        mix_p, st_p = _prompt_mixer(pp, pb, kt_p, relb_t, attn_sinks[l], gn)
        to_t = lambda c: jnp.transpose(c, (0, 2, 3, 1)).reshape(dec_batch, KV_WIDTH, wb)
        from_t = lambda c: jnp.transpose(c.reshape(dec_batch, ATTN_KV_HEADS, HEAD_DIM, wb), (0, 3, 1, 2))
        mix_s, kwin_t, vwin_t, st_s = _sample_mixer(
            ps, ph, kv_s, to_t(cache_k[l]), to_t(cache_v[l]),
            state_gla[l].astype(F32), relb_t, attn_sinks[l], gn)
        xp_in = xp
        xp, xs = _finish(xp_in, mix_p, xs, mix_s, *fin_w, rows=PROMPT_ROWS)
        win = lambda c: jnp.transpose(c.reshape(batch, ATTN_KV_HEADS, HEAD_DIM, wb), (0, 3, 1, 2))
        outs[0].append(win(kv_win[:KV_WIDTH]))
        outs[1].append(win(kv_win[KV_WIDTH:]))
        outs[2].append(st_p.reshape(batch, GLA_HEADS, GLA_DK, GLA_DV).astype(state_gla.dtype))
        outs[3].append(from_t(kwin_t))
        outs[4].append(from_t(vwin_t))
        outs[5].append(st_s.astype(state_gla.dtype))

    y_prompt = xp.reshape(batch, seq, D_MODEL)
    y_sample = xs
    return (y_prompt, y_sample) + tuple(jnp.stack(o) for o in outs)
```

```python
import functools
import math

import numpy as np
import jax
import jax.numpy as jnp
from jax import lax
from jax.experimental import pallas as pl
from jax.experimental.pallas import tpu as pltpu

F32 = jnp.float32
BF16 = jnp.bfloat16

D_MODEL = 1024
HEAD_DIM = 64
ATTN_HEADS = 8
ATTN_KV_HEADS = 2
WINDOW = 128
N_BUCKETS = 32
MAX_DISTANCE = 128
GLA_HEADS = 4
GLA_DK = 64
GLA_DV = 128
GLA_RANK = 16
GLA_TAU = 16.0
EPS = 1e-6
ATTN_WIDTH = ATTN_HEADS * HEAD_DIM
KV_WIDTH = ATTN_KV_HEADS * HEAD_DIM
GLA_QK_WIDTH = GLA_HEADS * GLA_DK
GLA_WIDTH = GLA_HEADS * GLA_DV
MIX_WIDTH = ATTN_WIDTH + GLA_WIDTH
MAIN_WIDTH = ATTN_WIDTH + 2 * KV_WIDTH + 2 * GLA_QK_WIDTH + 2 * GLA_WIDTH
LANES = 128
SUBLANES = 8
MXU_TILE = 256
FF_CHUNKS = 4
RANK_PAD = LANES
IN_PAD_WIDTH = MAIN_WIDTH + RANK_PAD

F_QG = 0
F_KG = F_QG + GLA_QK_WIDTH
F_RG = F_KG + GLA_QK_WIDTH
F_LA = F_RG + GLA_WIDTH
F_WIDTH = F_LA + GLA_QK_WIDTH
H_QA = 0
H_VA = H_QA + ATTN_WIDTH
H_VS = H_VA + KV_WIDTH
H_VG = H_VS + KV_WIDTH
H_WIDTH = H_VG + GLA_WIDTH

W_QA = 0
W_KV = W_QA + ATTN_WIDTH
W_QG = W_KV + 2 * KV_WIDTH + RANK_PAD
W_VG = W_QG + 2 * GLA_QK_WIDTH
W_RG = W_VG + GLA_WIDTH

BLK = 128
N_LEVELS = 7
NEG = -1e30
ATTN_SCALE = HEAD_DIM ** -0.5
SAMPLE_BLK = 16
VMEM_LIMIT = 56 * 1024 * 1024


def _t5_bucket_np(dist):
    n = np.maximum(dist, 0)
    max_exact = N_BUCKETS // 2
    nf = np.maximum(n, 1).astype(np.float64)
    large = max_exact + (np.log(nf / max_exact) / math.log(MAX_DISTANCE / max_exact)
                         * (N_BUCKETS - max_exact)).astype(np.int32)
    large = np.minimum(large, N_BUCKETS - 1)
    return np.where(n < max_exact, n, large).astype(np.int32)


def _prompt_bucket_tables():
    assert WINDOW == BLK
    i = np.arange(BLK)[:, None]
    j = np.arange(BLK)[None, :]
    own = j <= i
    bucket = _t5_bucket_np(np.where(own, i - j, BLK + i - j))
    t1 = np.where(own, bucket, -1)
    return np.stack([bucket, t1]).astype(np.int32)


def _level_tables():
    t = np.arange(BLK)[:, None]
    s = np.arange(BLK)[None, :]
    x = t ^ s
    lev = np.where(x > 0, np.floor(np.log2(np.maximum(x, 1))).astype(np.int32) + 1, 0)
    lev = np.where(s > t, -1, lev).astype(np.int32)
    tri = (s <= t).astype(np.float32)
    return lev, np.concatenate([tri, tri, tri], axis=1)


_BUCKET_PROMPT = _prompt_bucket_tables()
_LEV, _TRI3 = _level_tables()
_BUCKET_SAMPLE = _t5_bucket_np((WINDOW - 1) - np.arange(WINDOW))[None, :].astype(np.int32)


def _nt_dot(a, b):
    return lax.dot_general(a, b, (((1,), (1,)), ((), ())), preferred_element_type=F32)


def _tn_dot(a, b):
    return lax.dot_general(a, b, (((0,), (0,)), ((), ())), preferred_element_type=F32)


def _head_mean_sq(x):
    lo = lax.broadcasted_iota(jnp.int32, (x.shape[0], LANES), 1) < HEAD_DIM
    outs = []
    for c in range(x.shape[1] // LANES):
        y = x[:, c * LANES:(c + 1) * LANES]
        y = y * y
        s_lo = jnp.sum(jnp.where(lo, y, 0.0), axis=-1, keepdims=True)
        s_hi = jnp.sum(jnp.where(lo, 0.0, y), axis=-1, keepdims=True)
        outs.append(jnp.where(lo, s_lo, s_hi) * (1.0 / HEAD_DIM))
    return outs[0] if len(outs) == 1 else jnp.concatenate(outs, axis=1)


def _sigmoid(x):
    return 1.0 / (1.0 + jnp.exp(-x))


def _split3_rows(x):
    hi = x.astype(BF16)
    r1 = x - hi.astype(F32)
    mid = r1.astype(BF16)
    lo = (r1 - mid.astype(F32)).astype(BF16)
    return jnp.concatenate([hi, mid, lo], axis=0)


def _proj_kernel(x_ref, xs_ref, g_ref, w_ref, qn_ref, kn_ref, w2_ref, b2_ref, tri_ref, *rest):
    n_out = 7
    n_cast = (len(rest) - n_out) // 2
    cast_in, cast_out = rest[:n_cast], rest[n_cast + n_out:]
    f32_ref, bf_ref, kt_ref, kvw_ref, f32s_ref, bfs_ref, kvs_ref = rest[n_cast:n_cast + n_out]
    for src, dst in zip(cast_in, cast_out):
        dst[...] = src[...].astype(BF16)
    weights = (g_ref, w_ref, qn_ref, kn_ref, w2_ref, b2_ref, tri_ref)
    _proj_rows(x_ref, *weights, f32_ref, bf_ref, kt_ref, kvw_ref, block_cumsum=True)

    @pl.when(pl.program_id(0) == pl.num_programs(0) - 1)
    def _samples():
        _proj_rows(xs_ref, *weights, f32s_ref, bfs_ref, None, kvs_ref, block_cumsum=False)


def _proj_rows(x_ref, g_ref, w_ref, qn_ref, kn_ref, w2_ref, b2_ref, tri_ref, f32_ref, bf_ref,
               kt_ref, kv_ref, *, block_cumsum):
    x = x_ref[...]
    r = lax.rsqrt(jnp.mean(x * x, axis=-1, keepdims=True) + EPS)
    n = (x * g_ref[...]).astype(BF16)

    def seg(c0, c1):
        return _nt_dot(n, w_ref[c0:c1, :]) * r

    kvl = seg(W_KV, W_QG)
    lr = kvl[:, 2 * KV_WIDTH:].astype(BF16)
    z = jnp.dot(lr, w2_ref[...], preferred_element_type=F32) + b2_ref[...]
    log_a = (jnp.minimum(z, 0.0) - jnp.log(1.0 + jnp.exp(-jnp.abs(z)))) / GLA_TAU
    if block_cumsum:
        for blk in range(x.shape[0] // BLK):
            rows = slice(blk * BLK, (blk + 1) * BLK)
            f32_ref[rows, F_LA:F_WIDTH] = jnp.dot(
                tri_ref[...], _split3_rows(log_a[rows]), preferred_element_type=F32)
    else:
        f32_ref[:, F_LA:F_WIDTH] = log_a
    k = kvl[:, :KV_WIDTH]
    k = k * lax.rsqrt(_head_mean_sq(k) + EPS) * kn_ref[...]
    v = kvl[:, KV_WIDTH:2 * KV_WIDTH]
    n_kv = kv_ref.shape[0]
    kv_ref[:, 0:KV_WIDTH] = k[x.shape[0] - n_kv:]
    kv_ref[:, KV_WIDTH:2 * KV_WIDTH] = v[x.shape[0] - n_kv:]
    if kt_ref is not None:
        kt_ref[...] = k.T.astype(BF16)
    hdt = bf_ref.dtype
    bf_ref[:, H_VA:H_VS] = v.astype(hdt)
    bf_ref[:, H_VS:H_VG] = pltpu.roll(v, HEAD_DIM, 1).astype(hdt)
    q = seg(W_QA, W_KV)
    bf_ref[:, H_QA:H_VA] = (q * lax.rsqrt(_head_mean_sq(q) + EPS) * qn_ref[...]).astype(hdt)
    qk_g = seg(W_QG, W_VG)
    f32_ref[:, F_QG:F_KG] = qk_g[:, :GLA_QK_WIDTH] * (GLA_DK ** -0.5)
    f32_ref[:, F_KG:F_RG] = qk_g[:, GLA_QK_WIDTH:]
    bf_ref[:, H_VG:H_WIDTH] = seg(W_VG, W_RG).astype(hdt)
    f32_ref[:, F_RG:F_LA] = seg(W_RG, IN_PAD_WIDTH)


def _const_spec(shape):
    nd = len(shape)
    return pl.BlockSpec(shape, lambda i: (0,) * nd, pipeline_mode=pl.Buffered(1))


def _sample_rows_spec(ns, **kwargs):
    return pl.BlockSpec((ns, None, D_MODEL), lambda i: (0, 0, 0), **kwargs)


def _weight_prep_kernel(w_ref, w2_ref, o_ref, o2_ref):
    o2_ref[...] = jnp.zeros(o2_ref.shape, BF16)
    o2_ref[0:GLA_RANK] = w2_ref[...].astype(BF16)
    kv_end = W_KV + 2 * KV_WIDTH
    o_ref[W_QA:kv_end] = w_ref[0:kv_end].astype(BF16)
    o_ref[kv_end:W_QG] = jnp.zeros((RANK_PAD, o_ref.shape[1]), BF16)
    o_ref[kv_end:kv_end + GLA_RANK] = w_ref[MAIN_WIDTH:MAIN_WIDTH + GLA_RANK].astype(BF16)
    o_ref[W_QG:IN_PAD_WIDTH] = w_ref[kv_end:MAIN_WIDTH].astype(BF16)


def _weight_prep(w_t, w2, cols):
    n, k = w_t.shape
    assert n == MAIN_WIDTH + GLA_RANK and k % cols == 0 and w2.shape == (GLA_RANK, GLA_QK_WIDTH)
    return pl.pallas_call(
        _weight_prep_kernel,
        grid=(k // cols,),
        in_specs=[pl.BlockSpec((n, cols), lambda i: (0, i)),
                  _const_spec((GLA_RANK, GLA_QK_WIDTH))],
        out_specs=[pl.BlockSpec((IN_PAD_WIDTH, cols), lambda i: (0, i)),
                   pl.BlockSpec((RANK_PAD, GLA_QK_WIDTH), lambda i: (0, 0))],
        out_shape=[jax.ShapeDtypeStruct((IN_PAD_WIDTH, k), BF16),
                   jax.ShapeDtypeStruct((RANK_PAD, GLA_QK_WIDTH), BF16)],
        compiler_params=pltpu.CompilerParams(
            dimension_semantics=("arbitrary",), vmem_limit_bytes=VMEM_LIMIT),
        name="weight_prep",
    )(w_t, w2)


def _project(x, xs, g_attn, w_in_p, qn, kn, w2p, b2, rows, to_cast):
    t = x.shape[0]
    ns = xs.shape[0]
    steps = t // rows
    cast_specs = []
    for w in to_cast:
        slab = w.shape[0] // steps
        assert w.shape[0] % steps == 0 and slab % (2 * SUBLANES) == 0
        cast_specs.append(pl.BlockSpec((slab, w.shape[1]), lambda i: (i, 0)))
    return pl.pallas_call(
        _proj_kernel,
        grid=(steps,),
        in_specs=[
            pl.BlockSpec((rows, D_MODEL), lambda i: (i, 0)),
            _sample_rows_spec(ns, pipeline_mode=pl.Buffered(1)),
            _const_spec((1, D_MODEL)),
            _const_spec((IN_PAD_WIDTH, D_MODEL)),
            _const_spec((1, ATTN_WIDTH)),
            _const_spec((1, KV_WIDTH)),
            _const_spec((RANK_PAD, GLA_QK_WIDTH)),
            _const_spec((1, GLA_QK_WIDTH)),
            _const_spec((BLK, 3 * BLK)),
        ] + cast_specs,
        out_specs=[pl.BlockSpec((rows, F_WIDTH), lambda i: (i, 0)),
                   pl.BlockSpec((rows, H_WIDTH), lambda i: (i, 0)),
                   pl.BlockSpec((KV_WIDTH, rows), lambda i: (0, i)),
                   pl.BlockSpec((WINDOW, 2 * KV_WIDTH), lambda i: (0, 0)),
                   pl.BlockSpec((ns, F_WIDTH), lambda i: (0, 0)),
                   pl.BlockSpec((ns, H_WIDTH), lambda i: (0, 0)),
                   pl.BlockSpec((ns, 2 * KV_WIDTH), lambda i: (0, 0))] + cast_specs,
        out_shape=[jax.ShapeDtypeStruct((t, F_WIDTH), F32),
                   jax.ShapeDtypeStruct((t, H_WIDTH), BF16),
                   jax.ShapeDtypeStruct((KV_WIDTH, t), BF16),
                   jax.ShapeDtypeStruct((WINDOW, 2 * KV_WIDTH), F32),
                   jax.ShapeDtypeStruct((ns, F_WIDTH), F32),
                   jax.ShapeDtypeStruct((ns, H_WIDTH), F32),
                   jax.ShapeDtypeStruct((ns, 2 * KV_WIDTH), F32)]
        + [jax.ShapeDtypeStruct(w.shape, BF16) for w in to_cast],
        compiler_params=pltpu.CompilerParams(
            dimension_semantics=("arbitrary",), vmem_limit_bytes=VMEM_LIMIT),
        name="proj",
    )(x, xs, g_attn, w_in_p, qn, kn, w2p, b2, jnp.asarray(_TRI3, BF16), *to_cast)


def _boundary_rows(b_ref, r0, c0, b, row, level):
    m = 1 << (level - 1)
    if 2 * m >= SUBLANES:
        pieces = [jnp.broadcast_to(b_ref[r0 + g * 2 * m + m - 1:r0 + g * 2 * m + m, c0:c0 + LANES],
                                   (2 * m, LANES))
                  for g in range(BLK // (2 * m))]
        return pieces[0] if len(pieces) == 1 else jnp.concatenate(pieces, axis=0)
    pos = row & (2 * m - 1)
    tiles = b.reshape(BLK // SUBLANES, SUBLANES, LANES)
    out = b
    for p in range(2 * m):
        shift = (m - 1) - p
        if shift != 0:
            rolled = pltpu.roll(tiles, (-shift) % SUBLANES, 1).reshape(BLK, LANES)
            out = jnp.where(pos == p, rolled, out)
    return out


def _mixer_init(relb_ref, bucket_ref, kbd, vbd, kprev_t, vprev, st_scr, mb_scr):
    kbd[...] = jnp.zeros_like(kbd)
    vbd[...] = jnp.zeros_like(vbd)
    for slot in range(vbd.shape[0]):
        for g in range(ATTN_KV_HEADS):
            for half in range(2):
                vbd[slot, g, 2 * half * BLK:(2 * half + 2) * BLK,
                    KV_WIDTH + half * HEAD_DIM:KV_WIDTH + (half + 1) * HEAD_DIM] = (
                        jnp.ones((2 * BLK, HEAD_DIM), BF16))
    kprev_t[...] = jnp.zeros_like(kprev_t)
    vprev[...] = jnp.zeros_like(vprev)
    st_scr[...] = jnp.zeros_like(st_scr)
    bk = bucket_ref[0]
    acc = [jnp.zeros(bk.shape, F32) for _ in range(ATTN_HEADS)]
    for b in range(N_BUCKETS):
        hit = bk == b
        for h in range(ATTN_HEADS):
            acc[h] = jnp.where(hit, relb_ref[h, b], acc[h])
    for tb in range(2):
        masked = bucket_ref[tb] < 0
        for h in range(ATTN_HEADS):
            mb_scr[tb, h // 2, :, (h % 2) * BLK:(h % 2 + 1) * BLK] = jnp.where(masked, NEG, acc[h])


def _mixer_attention(pb_ref, kt_ref, r0, table, sink_ref, omix_ref, kbd, vbd, k_prev_t, v_prev,
                     mb_scr):
    rows = slice(r0, r0 + BLK)
    lo1 = lax.broadcasted_iota(jnp.int32, (BLK, LANES), 1) < HEAD_DIM
    k_t = kt_ref[:, r0:r0 + BLK]
    for g in range(ATTN_KV_HEADS):
        hd = slice(g * HEAD_DIM, (g + 1) * HEAD_DIM)
        kbd[g, 0:HEAD_DIM, 0:BLK] = k_prev_t[hd]
        kbd[g, 0:HEAD_DIM, BLK:2 * BLK] = k_t[hd]
        kbd[g, HEAD_DIM:2 * HEAD_DIM, 2 * BLK:3 * BLK] = k_prev_t[hd]
        kbd[g, HEAD_DIM:2 * HEAD_DIM, 3 * BLK:4 * BLK] = k_t[hd]
    lo_bf = jnp.where(lo1, 1.0, 0.0).astype(BF16)
    hi_bf = jnp.where(lo1, 0.0, 1.0).astype(BF16)
    lo_row = lax.broadcasted_iota(jnp.int32, (1, LANES), 1) < HEAD_DIM
    own = (lax.broadcasted_iota(jnp.int32, (BLK, BLK), 1)
           <= lax.broadcasted_iota(jnp.int32, (BLK, BLK), 0))
    own_bf = jnp.where(own, 1.0, 0.0).astype(BF16)
    prev_bf = jnp.where(own, 0.0, 1.0).astype(BF16)
    v_cur = pb_ref[rows,H_VA:H_VS]
    v_swap = pb_ref[rows,H_VS:H_VG]
    v_parts = (v_cur * lo_bf, v_swap * hi_bf, v_swap * lo_bf, v_cur * hi_bf)
    for n, part in enumerate(v_parts):
        g, half = divmod(n, 2)
        vbd[g, 2 * half * BLK:(2 * half + 1) * BLK, 0:KV_WIDTH] = v_prev[n]
        vbd[g, (2 * half + 1) * BLK:(2 * half + 2) * BLK, 0:KV_WIDTH] = part
    chunks_per_kv = ATTN_HEADS // ATTN_KV_HEADS // 2
    for g in range(ATTN_KV_HEADS):
        c_first = g * chunks_per_kv
        qs = jnp.concatenate(
            [pb_ref[rows,H_QA + (c_first + c) * LANES:H_QA + (c_first + c + 1) * LANES]
             for c in range(chunks_per_kv)], axis=0)
        s = jnp.dot(qs, kbd[g], preferred_element_type=F32)
        prob_rows, sink_rows = [], []
        for c in range(chunks_per_kv):
            probs, maxes, sks = [], [], []
            for e in range(2):
                s_prev = s[c * BLK:(c + 1) * BLK, 2 * e * BLK:(2 * e + 1) * BLK]
                s_own = s[c * BLK:(c + 1) * BLK, (2 * e + 1) * BLK:(2 * e + 2) * BLK]
                se = (jnp.where(own, s_own, s_prev)
                      + mb_scr[table, c_first + c, :, e * BLK:(e + 1) * BLK])
                sk = sink_ref[2 * (c_first + c) + e]
                m = jnp.maximum(jnp.max(se, axis=-1, keepdims=True), sk)
                pe = jnp.exp(se - m).astype(BF16)
                probs += [pe * prev_bf, pe * own_bf]
                maxes.append(m)
                sks.append(sk)
            prob_rows.append(jnp.concatenate(probs, axis=1))
            sink_rows.append(jnp.exp(jnp.where(lo_row, sks[0], sks[1])
                                     - jnp.where(lo1, maxes[0], maxes[1])))
        o_den = jnp.dot(jnp.concatenate(prob_rows, axis=0), vbd[g], preferred_element_type=F32)
        o = o_den[:, :KV_WIDTH] / (o_den[:, KV_WIDTH:] + jnp.concatenate(sink_rows, axis=0))
        for c in range(chunks_per_kv):
            omix_ref[rows,(c_first + c) * LANES:(c_first + c + 1) * LANES] = (
                o[c * BLK:(c + 1) * BLK].astype(omix_ref.dtype))
    return k_t, v_parts


def _mixer_gla(state, p_ref, pb_ref, r0, lev_ref, gn_ref, omix_ref):
    new_state = []
    pairs = range(len(state))
    rows = slice(r0, r0 + BLK)
    lo1 = lax.broadcasted_iota(jnp.int32, (BLK, LANES), 1) < HEAD_DIM
    lo_bf = jnp.where(lo1, 1.0, 0.0).astype(BF16)
    hi_bf = jnp.where(lo1, 0.0, 1.0).astype(BF16)
    row = lax.broadcasted_iota(jnp.int32, (BLK, LANES), 0)
    zero_blk = jnp.zeros((BLK, LANES), BF16)
    for c in pairs:
        c0 = c * LANES
        q_at = lambda a, z: p_ref[r0 + a:r0 + z, F_QG + c0:F_QG + c0 + LANES]
        k_at = lambda a, z: p_ref[r0 + a:r0 + z, F_KG + c0:F_KG + c0 + LANES]
        b_at = lambda a, z: p_ref[r0 + a:r0 + z, F_LA + c0:F_LA + c0 + LANES]

        def pair_scores(qtb, ktb_lo, ktb_hi):
            return _nt_dot(qtb, jnp.concatenate([ktb_lo, ktb_hi], axis=0))

        kb = k_at(0, BLK).astype(BF16)
        s0 = pair_scores(q_at(0, BLK).astype(BF16), kb * lo_bf, kb * hi_bf)
        tile = lambda a, n: a[n * SUBLANES:(n + 1) * SUBLANES]
        n_tiles = BLK // SUBLANES
        lev_t = [lev_ref[n * SUBLANES:(n + 1) * SUBLANES, :] for n in range(n_tiles)]
        sc = [[jnp.where(lev_t[n] == 0, tile(s0[:, e * BLK:(e + 1) * BLK], n), 0.0)
               for n in range(n_tiles)] for e in range(2)]
        for level in range(1, N_LEVELS + 1):
            m = 1 << (level - 1)
            if m >= SUBLANES:
                qs, klos, khis, dest = [], [], [], []
                zeros = jnp.zeros((m, LANES), BF16)
                lane_m = lax.broadcasted_iota(jnp.int32, (m, LANES), 1) < HEAD_DIM
                lo_m = jnp.where(lane_m, 1.0, 0.0).astype(BF16)
                hi_m = jnp.where(lane_m, 0.0, 1.0).astype(BF16)
                for g in range(BLK // (2 * m)):
                    lo_a, up_a, up_z = g * 2 * m, g * 2 * m + m, (g + 1) * 2 * m
                    rb = jnp.broadcast_to(b_at(up_a - 1, up_a), (m, LANES))
                    qs.append((q_at(up_a, up_z) * jnp.exp(b_at(up_a, up_z) - rb)).astype(BF16))
                    kp = (k_at(lo_a, up_a) * jnp.exp(rb - b_at(lo_a, up_a))).astype(BF16)
                    klos += [kp * lo_m, zeros]
                    khis += [kp * hi_m, zeros]
                    dest += list(range(up_a // SUBLANES, up_z // SUBLANES))
                sl = pair_scores(jnp.concatenate(qs, axis=0), jnp.concatenate(klos, axis=0),
                                 jnp.concatenate(khis, axis=0))
            else:
                dest = list(range(n_tiles))
                bc = b_at(0, BLK)
                decay = jnp.exp(-jnp.abs(bc - _boundary_rows(p_ref, r0, F_LA + c0, bc, row, level)))
                upper = ((row >> (level - 1)) & 1) == 1
                kl = (k_at(0, BLK) * decay).astype(BF16)
                sl = pair_scores(
                    (q_at(0, BLK) * decay).astype(BF16) * jnp.where(upper, 1.0, 0.0).astype(BF16),
                    kl * jnp.where(upper | ~lo1, 0.0, 1.0).astype(BF16),
                    kl * jnp.where(upper | lo1, 0.0, 1.0).astype(BF16))
            for src, n in enumerate(dest):
                for e in range(2):
                    sc[e][n] = jnp.where(lev_t[n] == level, tile(sl[:, e * BLK:(e + 1) * BLK], src),
                                         sc[e][n])
        sc = jnp.concatenate([jnp.concatenate(sc[e], axis=0) for e in range(2)], axis=1)
        qc, kc, bc = q_at(0, BLK), k_at(0, BLK), b_at(0, BLK)

        b_last = bc[BLK - 1:BLK, :]
        v0 = pb_ref[rows,H_VG + 2 * c0:H_VG + 2 * c0 + LANES]
        v1 = pb_ref[rows,H_VG + 2 * c0 + LANES:H_VG + 2 * c0 + 2 * LANES]
        v_bd = jnp.concatenate([jnp.concatenate([v0, zero_blk], axis=1),
                                jnp.concatenate([zero_blk, v1], axis=1)], axis=0)
        st_c = state[c]
        stb = st_c.astype(BF16)
        st_rhs = jnp.concatenate([stb * lo_bf, stb * hi_bf], axis=0)
        o = (jnp.dot(sc.astype(BF16), v_bd, preferred_element_type=F32)
             + _nt_dot((qc * jnp.exp(bc)).astype(BF16), st_rhs))
        kd = (kc * jnp.exp(b_last - bc)).astype(BF16)
        upd = _tn_dot(jnp.concatenate([v0, v1], axis=1), kd)
        new_state.append(st_c * jnp.exp(b_last) + jnp.where(lo1, upd[:BLK], upd[BLK:]))
        for e in range(2):
            h = 2 * c + e
            oh = o[:, e * LANES:(e + 1) * LANES]
            og = oh * lax.rsqrt(jnp.mean(oh * oh, axis=-1, keepdims=True) + EPS) * gn_ref[...]
            rg = p_ref[rows,F_RG + h * GLA_DV:F_RG + (h + 1) * GLA_DV]
            gated = og * (rg * _sigmoid(rg))
            omix_ref[rows,ATTN_WIDTH + h * GLA_DV:ATTN_WIDTH + (h + 1) * GLA_DV] = (
                gated.astype(omix_ref.dtype))
    return new_state


def _prompt_mixer_kernel(relb_ref, sink_ref, p_ref, pb_ref, kt_ref, bucket_ref, lev_ref, gn_ref,
                         omix_ref, st_ref, kbd, vbd, kprev_t, vprev, st_scr, mb_scr):
    i = pl.program_id(0)

    @pl.when(i == 0)
    def _init():
        _mixer_init(relb_ref, bucket_ref, kbd, vbd, kprev_t, vprev, st_scr, mb_scr)

    k_prev_t = kprev_t[...]
    v_prev = [vprev[n] for n in range(2 * ATTN_KV_HEADS)]
    state = [st_scr[:, c * LANES:(c + 1) * LANES] for c in range(GLA_HEADS // 2)]
    n_blocks = p_ref.shape[0] // BLK
    for jb in range(n_blocks):
        table = jnp.where(i == 0, 1, 0) if jb == 0 else 0
        k_prev_t, v_prev = _mixer_attention(pb_ref, kt_ref, jb * BLK, table, sink_ref, omix_ref,
                                            kbd.at[jb], vbd.at[jb], k_prev_t, v_prev, mb_scr)
        state = _mixer_gla(state, p_ref, pb_ref, jb * BLK, lev_ref, gn_ref, omix_ref)
    kprev_t[...] = k_prev_t
    for n, part in enumerate(v_prev):
        vprev[n] = part
    for c, st_c in enumerate(state):
        st_scr[:, c * LANES:(c + 1) * LANES] = st_c

    @pl.when(i == pl.num_programs(0) - 1)
    def _final_state():
        for c, st_c in enumerate(state):
            st_ref[c * LANES:(c + 1) * LANES, :] = st_c.T


def _prompt_mixer(p, pb, kt, relb, sinks, gn):
    t = p.shape[0]
    smem = pl.BlockSpec(memory_space=pltpu.SMEM)
    return pl.pallas_call(
        _prompt_mixer_kernel,
        grid=(t // MIX_ROWS,),
        in_specs=[
            smem, smem,
            pl.BlockSpec((MIX_ROWS, F_WIDTH), lambda i: (i, 0)),
            pl.BlockSpec((MIX_ROWS, H_WIDTH), lambda i: (i, 0)),
            pl.BlockSpec((KV_WIDTH, MIX_ROWS), lambda i: (0, i)),
            _const_spec((2, BLK, BLK)),
            _const_spec((BLK, BLK)),
            _const_spec((1, GLA_DV)),
        ],
        out_specs=[
            pl.BlockSpec((MIX_ROWS, MIX_WIDTH), lambda i: (i, 0)),
            pl.BlockSpec((GLA_QK_WIDTH, GLA_DV), lambda i: (0, 0)),
        ],
        out_shape=[
            jax.ShapeDtypeStruct((t, MIX_WIDTH), BF16),
            jax.ShapeDtypeStruct((GLA_QK_WIDTH, GLA_DV), F32),
        ],
        scratch_shapes=[
            pltpu.VMEM((MIX_ROWS // BLK, ATTN_KV_HEADS, 2 * HEAD_DIM, 4 * BLK), BF16),
            pltpu.VMEM((MIX_ROWS // BLK, ATTN_KV_HEADS, 4 * BLK, 2 * KV_WIDTH), BF16),
            pltpu.VMEM((KV_WIDTH, BLK), BF16),
            pltpu.VMEM((2 * ATTN_KV_HEADS, BLK, KV_WIDTH), BF16),
            pltpu.VMEM((GLA_DV, GLA_QK_WIDTH), F32),
            pltpu.VMEM((2, ATTN_HEADS // 2, BLK, 2 * BLK), F32),
        ],
        compiler_params=pltpu.CompilerParams(
            dimension_semantics=("arbitrary",), vmem_limit_bytes=VMEM_LIMIT),
        name="prompt_mixer",
    )(relb, sinks, p, pb, kt, jnp.asarray(_BUCKET_PROMPT), jnp.asarray(_LEV), gn)


def _sample_mixer_kernel(ps_ref, ph_ref, kvfull_ref, pfull_ref, ck_ref, cv_ref, st_ref, relbt_ref,
                         sink_ref, bucket_ref,
                         gn_ref, omix_ref, kwin_ref, vwin_ref, stout_ref, lat_scr, kqt_scr, kvt_scr,
                         bias_scr, s_scr, o_scr, og_scr):
    i = pl.program_id(0)
    nb = pfull_ref.shape[0]

    @pl.when(i == 0)
    def _init():
        lat_scr[...] = _split3_rows(pfull_ref[:, F_LA:F_WIDTH].T)
        kqt_scr[0:GLA_QK_WIDTH] = pfull_ref[:, F_KG:F_RG].T.astype(BF16)
        kqt_scr[GLA_QK_WIDTH:2 * GLA_QK_WIDTH] = pfull_ref[:, F_QG:F_KG].T.astype(BF16)
        kvt = kvfull_ref[...].T
        for s in range(nb // SAMPLE_BLK):
            shift = nb - (s + 1) * SAMPLE_BLK
            kvt_scr[s] = pltpu.roll(kvt, shift, 1) if shift else kvt
        bk = jnp.broadcast_to(bucket_ref[...], (ATTN_HEADS, WINDOW))
        acc = jnp.zeros((ATTN_HEADS, WINDOW), F32)
        for b in range(N_BUCKETS):
            acc = jnp.where(bk == b, relbt_ref[:, b:b + 1], acc)
        bias_scr[...] = acc

    lo = lax.broadcasted_iota(jnp.int32, (1, LANES), 1) < HEAD_DIM
    sub = lax.broadcasted_iota(jnp.int32, (ATTN_HEADS, LANES), 0)
    newest = lax.broadcasted_iota(jnp.int32, (KV_WIDTH, WINDOW), 1) == WINDOW - 1
    heads_per_kv = ATTN_HEADS // ATTN_KV_HEADS

    n_of_col = i * SAMPLE_BLK + lax.broadcasted_iota(jnp.int32, (nb, SAMPLE_BLK * LANES), 1) // LANES
    pick = jnp.where(lax.broadcasted_iota(jnp.int32, (nb, SAMPLE_BLK * LANES), 0) == n_of_col,
                     1.0, 0.0).astype(BF16)
    la_b = (jnp.dot(lat_scr[0:GLA_QK_WIDTH], pick, preferred_element_type=F32)
            + jnp.dot(lat_scr[GLA_QK_WIDTH:2 * GLA_QK_WIDTH], pick, preferred_element_type=F32)
            + jnp.dot(lat_scr[2 * GLA_QK_WIDTH:3 * GLA_QK_WIDTH], pick, preferred_element_type=F32))
    kq_b = jnp.dot(kqt_scr[...], pick, preferred_element_type=F32)

    for j in range(SAMPLE_BLK):
        kv_new = kvt_scr[i]
        if j < SAMPLE_BLK - 1:
            kv_new = pltpu.roll(kv_new, SAMPLE_BLK - 1 - j, 1)
        kwin_ref[j] = jnp.where(newest, kv_new[0:KV_WIDTH], pltpu.roll(ck_ref[j], WINDOW - 1, 1))
        vwin_ref[j] = jnp.where(newest, kv_new[KV_WIDTH:2 * KV_WIDTH],
                                pltpu.roll(cv_ref[j], WINDOW - 1, 1))

    for j in range(SAMPLE_BLK):
        qexp = jnp.zeros((ATTN_HEADS, LANES), F32)
        for c in range(ATTN_HEADS // 2):
            chunk = ph_ref[j:j + 1, H_QA + c * LANES:H_QA + (c + 1) * LANES]
            swapped = pltpu.roll(chunk, HEAD_DIM, 1)
            if (2 * c) // heads_per_kv == 0:
                rows = (jnp.where(lo, chunk, 0.0), jnp.where(lo, swapped, 0.0))
            else:
                rows = (jnp.where(lo, 0.0, swapped), jnp.where(lo, 0.0, chunk))
            for e in range(2):
                qexp = jnp.where(sub == 2 * c + e, rows[e], qexp)
        s_scr[j * ATTN_HEADS:(j + 1) * ATTN_HEADS] = jnp.dot(
            qexp.astype(BF16), kwin_ref[j].astype(BF16), preferred_element_type=F32)

    tile = lambda x: jnp.concatenate([x] * SAMPLE_BLK, axis=0)
    head_of_row = lax.broadcasted_iota(jnp.int32, (ATTN_HEADS, 1), 0)
    sink8 = jnp.zeros((ATTN_HEADS, 1), F32)
    for h in range(ATTN_HEADS):
        sink8 = jnp.where(head_of_row == h, sink_ref[h], sink8)
    sink = tile(sink8)
    s = s_scr[...] + tile(bias_scr[...])
    m = jnp.maximum(jnp.max(s, axis=-1, keepdims=True), sink)
    pe = jnp.exp(s - m)
    inv_den = 1.0 / (jnp.sum(pe, axis=-1, keepdims=True) + jnp.exp(sink - m))
    peb = pe.astype(BF16)
    for j in range(SAMPLE_BLK):
        o_scr[j * ATTN_HEADS:(j + 1) * ATTN_HEADS] = _nt_dot(
            peb[j * ATTN_HEADS:(j + 1) * ATTN_HEADS], vwin_ref[j].astype(BF16))
    o_all = o_scr[...] * inv_den
    o_swap = pltpu.roll(o_all, HEAD_DIM, 1)
    for j in range(SAMPLE_BLK):
        r = j * ATTN_HEADS
        for c in range(ATTN_HEADS // 2):
            if (2 * c) // heads_per_kv == 0:
                piece = jnp.where(lo, o_all[r + 2 * c:r + 2 * c + 1, :], o_swap[r + 2 * c + 1:r + 2 * c + 2, :])
            else:
                piece = jnp.where(lo, o_swap[r + 2 * c:r + 2 * c + 1, :], o_all[r + 2 * c + 1:r + 2 * c + 2, :])
            omix_ref[j:j + 1, c * LANES:(c + 1) * LANES] = piece

    for j in range(SAMPLE_BLK):
        cols = slice(j * LANES, (j + 1) * LANES)
        for h in range(GLA_HEADS):
            rs = slice(h * GLA_DK, (h + 1) * GLA_DK)
            qs = slice(GLA_QK_WIDTH + h * GLA_DK, GLA_QK_WIDTH + (h + 1) * GLA_DK)
            v_row = ph_ref[j:j + 1, H_VG + h * GLA_DV:H_VG + (h + 1) * GLA_DV]
            s_new = jnp.exp(la_b[rs, cols]) * st_ref[j, h] + kq_b[rs, cols] * v_row
            stout_ref[j, h] = s_new
            og_scr[j:j + 1, h * GLA_DV:(h + 1) * GLA_DV] = jnp.sum(
                kq_b[qs, cols] * s_new, axis=0, keepdims=True)
    for h in range(GLA_HEADS):
        hs = slice(h * GLA_DV, (h + 1) * GLA_DV)
        og = og_scr[:, hs]
        og = og * lax.rsqrt(jnp.mean(og * og, axis=-1, keepdims=True) + EPS) * gn_ref[...]
        rg = ps_ref[:, F_RG + h * GLA_DV:F_RG + (h + 1) * GLA_DV]
        omix_ref[:, ATTN_WIDTH + h * GLA_DV:ATTN_WIDTH + (h + 1) * GLA_DV] = og * (rg * _sigmoid(rg))


def _sample_mixer(ps, ph, kv, cache_k, cache_v, state, relbt, sinks, gn):
    nb = ps.shape[0]
    assert nb == LANES
    blk3 = lambda i: (i, 0, 0)
    blk4 = lambda i: (i, 0, 0, 0)
    return pl.pallas_call(
        _sample_mixer_kernel,
        grid=(nb // SAMPLE_BLK,),
        in_specs=[
            pl.BlockSpec((SAMPLE_BLK, F_WIDTH), lambda i: (i, 0)),
            pl.BlockSpec((SAMPLE_BLK, H_WIDTH), lambda i: (i, 0)),
            _const_spec((nb, 2 * KV_WIDTH)),
            _const_spec((nb, F_WIDTH)),
            pl.BlockSpec((SAMPLE_BLK, KV_WIDTH, WINDOW), blk3),
            pl.BlockSpec((SAMPLE_BLK, KV_WIDTH, WINDOW), blk3),
            pl.BlockSpec((SAMPLE_BLK, GLA_HEADS, GLA_DK, GLA_DV), blk4),
            _const_spec((ATTN_HEADS, N_BUCKETS)),
            pl.BlockSpec(memory_space=pltpu.SMEM),
            _const_spec((1, WINDOW)),
            _const_spec((1, GLA_DV)),
        ],
        out_specs=[
            pl.BlockSpec((SAMPLE_BLK, MIX_WIDTH), lambda i: (i, 0)),
            pl.BlockSpec((SAMPLE_BLK, KV_WIDTH, WINDOW), blk3),
            pl.BlockSpec((SAMPLE_BLK, KV_WIDTH, WINDOW), blk3),
            pl.BlockSpec((SAMPLE_BLK, GLA_HEADS, GLA_DK, GLA_DV), blk4),
        ],
        out_shape=[
            jax.ShapeDtypeStruct((nb, MIX_WIDTH), F32),
            jax.ShapeDtypeStruct((nb, KV_WIDTH, WINDOW), F32),
            jax.ShapeDtypeStruct((nb, KV_WIDTH, WINDOW), F32),
            jax.ShapeDtypeStruct((nb, GLA_HEADS, GLA_DK, GLA_DV), F32),
        ],
        scratch_shapes=[
            pltpu.VMEM((3 * GLA_QK_WIDTH, nb), BF16),
            pltpu.VMEM((2 * GLA_QK_WIDTH, nb), BF16),
            pltpu.VMEM((nb // SAMPLE_BLK, 2 * KV_WIDTH, nb), F32),
            pltpu.VMEM((ATTN_HEADS, WINDOW), F32),
            pltpu.VMEM((SAMPLE_BLK * ATTN_HEADS, WINDOW), F32),
            pltpu.VMEM((SAMPLE_BLK * ATTN_HEADS, KV_WIDTH), F32),
            pltpu.VMEM((SAMPLE_BLK, GLA_WIDTH), F32),
        ],
        compiler_params=pltpu.CompilerParams(
            dimension_semantics=("arbitrary",), vmem_limit_bytes=VMEM_LIMIT),
        name="sample_mixer",
    )(ps, ph, kv, ps, cache_k, cache_v, state, relbt, sinks, jnp.asarray(_BUCKET_SAMPLE), gn)


def _finish_kernel(x_ref, mix_ref, xs_ref, mixs_ref, wo_ref, g_ref, wg_ref, wu_ref, wd_ref,
                   y_ref, ys_ref, *, ff_chunks):
    weights = (wo_ref, g_ref, wg_ref, wu_ref, wd_ref)
    _finish_rows(x_ref, mix_ref, *weights, y_ref, ff_chunks=ff_chunks)

    @pl.when(pl.program_id(0) == pl.num_programs(0) - 1)
    def _samples():
        _finish_rows(xs_ref, mixs_ref, *weights, ys_ref, ff_chunks=ff_chunks)


def _finish_rows(x_ref, mix_ref, wo_ref, g_ref, wg_ref, wu_ref, wd_ref, y_ref, *, ff_chunks):
    h = x_ref[...] + jnp.dot(mix_ref[...].astype(BF16), wo_ref[...], preferred_element_type=F32)
    r = lax.rsqrt(jnp.mean(h * h, axis=-1, keepdims=True) + EPS)
    z = (h * g_ref[...]).astype(BF16)
    n_tiles = wd_ref.shape[0] // MXU_TILE
    acc = h
    for c in range(ff_chunks):
        c0 = ((c * n_tiles) // ff_chunks) * MXU_TILE
        c1 = (((c + 1) * n_tiles) // ff_chunks) * MXU_TILE
        gate = jnp.dot(z, wg_ref[:, c0:c1], preferred_element_type=F32) * r
        up = jnp.dot(z, wu_ref[:, c0:c1], preferred_element_type=F32) * r
        act = ((gate * _sigmoid(gate)) * up).astype(BF16)
        acc = acc + jnp.dot(act, wd_ref[c0:c1, :], preferred_element_type=F32)
    y_ref[...] = acc


def _finish(x, mix, xs, mixs, wo, g_ffn, wg, wu, wd, rows):
    t = x.shape[0]
    ns = xs.shape[0]
    d_ff = wd.shape[0]
    assert d_ff % MXU_TILE == 0
    return pl.pallas_call(
        functools.partial(_finish_kernel, ff_chunks=FF_CHUNKS),
        grid=(t // rows,),
        in_specs=[
            pl.BlockSpec((rows, D_MODEL), lambda i: (i, 0)),
            pl.BlockSpec((rows, MIX_WIDTH), lambda i: (i, 0)),
            _sample_rows_spec(ns, pipeline_mode=pl.Buffered(1)),
            _const_spec((ns, MIX_WIDTH)),
            _const_spec((MIX_WIDTH, D_MODEL)),
            _const_spec((1, D_MODEL)),
            _const_spec((D_MODEL, d_ff)),
            _const_spec((D_MODEL, d_ff)),
            _const_spec((d_ff, D_MODEL)),
        ],
        out_specs=[pl.BlockSpec((rows, D_MODEL), lambda i: (i, 0)),
                   _sample_rows_spec(ns)],
        out_shape=[jax.ShapeDtypeStruct((t, D_MODEL), F32),
                   jax.ShapeDtypeStruct((ns, 1, D_MODEL), F32)],
        compiler_params=pltpu.CompilerParams(
            dimension_semantics=("arbitrary",), vmem_limit_bytes=VMEM_LIMIT),
        name="finish",
    )(x, mix, xs, mixs, wo, g_ffn, wg, wu, wd)


PROMPT_ROWS = 1024
PROJ_ROWS = 1024
PREP_COLS = 256
MIX_ROWS = 1024


def kernel(x_prompt, x_sample, cache_k, cache_v, state_gla, attn_norm_g, w_in, q_norm_g, k_norm_g,
           attn_sinks, rel_bias, w_gla_gate2, b_gla_gate, gla_norm_g, w_o, ffn_norm_g, w_gate, w_up,
           w_down):
    depth = w_in.shape[0]
    batch, seq, _ = x_prompt.shape
    dec_batch, dec_seq, _ = x_sample.shape
    wb = cache_k.shape[2]
    assert batch == 1 and dec_seq == 1 and wb == WINDOW
    assert seq % PROMPT_ROWS == 0 and seq % PROJ_ROWS == 0
    assert dec_batch % SAMPLE_BLK == 0 and dec_batch % LANES == 0
    assert rel_bias.shape == (N_BUCKETS, ATTN_HEADS)

    xp = x_prompt.reshape(seq, D_MODEL)
    xs = x_sample
    relb_t = rel_bias.T
    outs = ([], [], [], [], [], [])
    for l in range(depth):
        w_in_p, w2p = _weight_prep(w_in[l].T, w_gla_gate2[l], cols=PREP_COLS)
        proj_w = (attn_norm_g[l][None, :], w_in_p,
                  jnp.tile(q_norm_g[l], ATTN_HEADS)[None, :] * ATTN_SCALE,
                  jnp.tile(k_norm_g[l], ATTN_KV_HEADS)[None, :], w2p, b_gla_gate[l][None, :])
        gn = gla_norm_g[l][None, :]

        pp, pb, kt_p, kv_win, ps, ph, kv_s, wo_b, wg_b, wu_b, wd_b = _project(
            xp, xs, *proj_w, rows=PROJ_ROWS, to_cast=(w_o[l], w_gate[l], w_up[l], w_down[l]))
        fin_w = (wo_b, ffn_norm_g[l][None, :], wg_b, wu_b, wd_b)
        mix_p, st_p = _prompt_mixer(pp, pb, kt_p, relb_t, attn_sinks[l], gn)
        to_t = lambda c: jnp.transpose(c, (0, 2, 3, 1)).reshape(dec_batch, KV_WIDTH, wb)
        from_t = lambda c: jnp.transpose(c.reshape(dec_batch, ATTN_KV_HEADS, HEAD_DIM, wb), (0, 3, 1, 2))
        mix_s, kwin_t, vwin_t, st_s = _sample_mixer(
            ps, ph, kv_s, to_t(cache_k[l]), to_t(cache_v[l]),
            state_gla[l].astype(F32), relb_t, attn_sinks[l], gn)
        xp_in = xp
        xp, xs = _finish(xp_in, mix_p, xs, mix_s, *fin_w, rows=PROMPT_ROWS)
        outs[0].append(kv_win[:, :KV_WIDTH].reshape(batch, wb, ATTN_KV_HEADS, HEAD_DIM))
        outs[1].append(kv_win[:, KV_WIDTH:].reshape(batch, wb, ATTN_KV_HEADS, HEAD_DIM))
        outs[2].append(st_p.reshape(batch, GLA_HEADS, GLA_DK, GLA_DV).astype(state_gla.dtype))
        outs[3].append(from_t(kwin_t))
        outs[4].append(from_t(vwin_t))
        outs[5].append(st_s.astype(state_gla.dtype))

    y_prompt = xp.reshape(batch, seq, D_MODEL)
    y_sample = xs
    return (y_prompt, y_sample) + tuple(jnp.stack(o) for o in outs)
```

```python
import functools
import math

import numpy as np
import jax
import jax.numpy as jnp
from jax import lax
from jax.experimental import pallas as pl
from jax.experimental.pallas import tpu as pltpu

F32 = jnp.float32
BF16 = jnp.bfloat16

D_MODEL = 1024
HEAD_DIM = 64
ATTN_HEADS = 8
ATTN_KV_HEADS = 2
WINDOW = 128
N_BUCKETS = 32
MAX_DISTANCE = 128
GLA_HEADS = 4
GLA_DK = 64
GLA_DV = 128
GLA_RANK = 16
GLA_TAU = 16.0
EPS = 1e-6
ATTN_WIDTH = ATTN_HEADS * HEAD_DIM
KV_WIDTH = ATTN_KV_HEADS * HEAD_DIM
GLA_QK_WIDTH = GLA_HEADS * GLA_DK
GLA_WIDTH = GLA_HEADS * GLA_DV
MIX_WIDTH = ATTN_WIDTH + GLA_WIDTH
MAIN_WIDTH = ATTN_WIDTH + 2 * KV_WIDTH + 2 * GLA_QK_WIDTH + 2 * GLA_WIDTH
LANES = 128
SUBLANES = 8
MXU_TILE = 256
FF_CHUNKS = 4
RANK_PAD = LANES
IN_PAD_WIDTH = MAIN_WIDTH + RANK_PAD

F_QG = 0
F_KG = F_QG + GLA_QK_WIDTH
F_RG = F_KG + GLA_QK_WIDTH
F_LA = F_RG + GLA_WIDTH
F_WIDTH = F_LA + GLA_QK_WIDTH
H_QA = 0
H_VA = H_QA + ATTN_WIDTH
H_VS = H_VA + KV_WIDTH
H_VG = H_VS + KV_WIDTH
H_WIDTH = H_VG + GLA_WIDTH

W_QA = 0
W_KV = W_QA + ATTN_WIDTH
W_QG = W_KV + 2 * KV_WIDTH + RANK_PAD
W_VG = W_QG + 2 * GLA_QK_WIDTH
W_RG = W_VG + GLA_WIDTH

BLK = 128
N_LEVELS = 7
NEG = -1e30
ATTN_SCALE = HEAD_DIM ** -0.5
SAMPLE_BLK = 32
VMEM_LIMIT = 56 * 1024 * 1024


def _t5_bucket_np(dist):
    n = np.maximum(dist, 0)
    max_exact = N_BUCKETS // 2
    nf = np.maximum(n, 1).astype(np.float64)
    large = max_exact + (np.log(nf / max_exact) / math.log(MAX_DISTANCE / max_exact)
                         * (N_BUCKETS - max_exact)).astype(np.int32)
    large = np.minimum(large, N_BUCKETS - 1)
    return np.where(n < max_exact, n, large).astype(np.int32)


def _prompt_bucket_tables():
    assert WINDOW == BLK
    i = np.arange(BLK)[:, None]
    j = np.arange(BLK)[None, :]
    own = j <= i
    bucket = _t5_bucket_np(np.where(own, i - j, BLK + i - j))
    t1 = np.where(own, bucket, -1)
    return np.stack([bucket, t1]).astype(np.int32)


def _level_tables():
    t = np.arange(BLK)[:, None]
    s = np.arange(BLK)[None, :]
    x = t ^ s
    lev = np.where(x > 0, np.floor(np.log2(np.maximum(x, 1))).astype(np.int32) + 1, 0)
    lev = np.where(s > t, -1, lev).astype(np.int32)
    tri = (s <= t).astype(np.float32)
    return lev, np.concatenate([tri, tri, tri], axis=1)


_BUCKET_PROMPT = _prompt_bucket_tables()
_LEV, _TRI3 = _level_tables()
_BUCKET_SAMPLE = _t5_bucket_np((WINDOW - 1) - np.arange(WINDOW))[None, :].astype(np.int32)


def _nt_dot(a, b):
    return lax.dot_general(a, b, (((1,), (1,)), ((), ())), preferred_element_type=F32)


def _tn_dot(a, b):
    return lax.dot_general(a, b, (((0,), (0,)), ((), ())), preferred_element_type=F32)


def _head_mean_sq(x):
    lo = lax.broadcasted_iota(jnp.int32, (x.shape[0], LANES), 1) < HEAD_DIM
    outs = []
    for c in range(x.shape[1] // LANES):
        y = x[:, c * LANES:(c + 1) * LANES]
        y = y * y
        s_lo = jnp.sum(jnp.where(lo, y, 0.0), axis=-1, keepdims=True)
        s_hi = jnp.sum(jnp.where(lo, 0.0, y), axis=-1, keepdims=True)
        outs.append(jnp.where(lo, s_lo, s_hi) * (1.0 / HEAD_DIM))
    return outs[0] if len(outs) == 1 else jnp.concatenate(outs, axis=1)


def _sigmoid(x):
    return 1.0 / (1.0 + jnp.exp(-x))


def _split3_rows(x):
    hi = x.astype(BF16)
    r1 = x - hi.astype(F32)
    mid = r1.astype(BF16)
    lo = (r1 - mid.astype(F32)).astype(BF16)
    return jnp.concatenate([hi, mid, lo], axis=0)


def _proj_kernel(x_ref, xs_ref, g_ref, w_ref, qn_ref, kn_ref, w2_ref, b2_ref, tri_ref, *rest):
    n_out = 7
    n_cast = (len(rest) - n_out) // 2
    cast_in, cast_out = rest[:n_cast], rest[n_cast + n_out:]
    f32_ref, bf_ref, kt_ref, kvw_ref, f32s_ref, bfs_ref, kvs_ref = rest[n_cast:n_cast + n_out]
    for src, dst in zip(cast_in, cast_out):
        dst[...] = src[...].astype(BF16)
    weights = (g_ref, w_ref, qn_ref, kn_ref, w2_ref, b2_ref, tri_ref)
    _proj_rows(x_ref, *weights, f32_ref, bf_ref, kt_ref, kvw_ref, block_cumsum=True)

    @pl.when(pl.program_id(0) == pl.num_programs(0) - 1)
    def _samples():
        _proj_rows(xs_ref, *weights, f32s_ref, bfs_ref, None, kvs_ref, block_cumsum=False)


def _proj_rows(x_ref, g_ref, w_ref, qn_ref, kn_ref, w2_ref, b2_ref, tri_ref, f32_ref, bf_ref,
               kt_ref, kv_ref, *, block_cumsum):
    x = x_ref[...]
    r = lax.rsqrt(jnp.mean(x * x, axis=-1, keepdims=True) + EPS)
    n = (x * g_ref[...]).astype(BF16)

    def seg(c0, c1):
        return _nt_dot(n, w_ref[c0:c1, :]) * r

    kvl = seg(W_KV, W_QG)
    lr = kvl[:, 2 * KV_WIDTH:].astype(BF16)
    z = jnp.dot(lr, w2_ref[...], preferred_element_type=F32) + b2_ref[...]
    log_a = (jnp.minimum(z, 0.0) - jnp.log(1.0 + jnp.exp(-jnp.abs(z)))) / GLA_TAU
    if block_cumsum:
        for blk in range(x.shape[0] // BLK):
            rows = slice(blk * BLK, (blk + 1) * BLK)
            f32_ref[rows, F_LA:F_WIDTH] = jnp.dot(
                tri_ref[...], _split3_rows(log_a[rows]), preferred_element_type=F32)
    else:
        f32_ref[:, F_LA:F_WIDTH] = log_a
    k = kvl[:, :KV_WIDTH]
    k = k * lax.rsqrt(_head_mean_sq(k) + EPS) * kn_ref[...]
    v = kvl[:, KV_WIDTH:2 * KV_WIDTH]
    n_kv = kv_ref.shape[0]
    kv_ref[:, 0:KV_WIDTH] = k[x.shape[0] - n_kv:]
    kv_ref[:, KV_WIDTH:2 * KV_WIDTH] = v[x.shape[0] - n_kv:]
    if kt_ref is not None:
        kt_ref[...] = k.T.astype(BF16)
    hdt = bf_ref.dtype
    bf_ref[:, H_VA:H_VS] = v.astype(hdt)
    bf_ref[:, H_VS:H_VG] = pltpu.roll(v, HEAD_DIM, 1).astype(hdt)
    q = seg(W_QA, W_KV)
    bf_ref[:, H_QA:H_VA] = (q * lax.rsqrt(_head_mean_sq(q) + EPS) * qn_ref[...]).astype(hdt)
    qk_g = seg(W_QG, W_VG)
    f32_ref[:, F_QG:F_KG] = qk_g[:, :GLA_QK_WIDTH] * (GLA_DK ** -0.5)
    f32_ref[:, F_KG:F_RG] = qk_g[:, GLA_QK_WIDTH:]
    bf_ref[:, H_VG:H_WIDTH] = seg(W_VG, W_RG).astype(hdt)
    f32_ref[:, F_RG:F_LA] = seg(W_RG, IN_PAD_WIDTH)


def _const_spec(shape):
    nd = len(shape)
    return pl.BlockSpec(shape, lambda i: (0,) * nd, pipeline_mode=pl.Buffered(1))


def _sample_rows_spec(ns, **kwargs):
    return pl.BlockSpec((ns, None, D_MODEL), lambda i: (0, 0, 0), **kwargs)


def _weight_prep_kernel(w_ref, w2_ref, o_ref, o2_ref):
    o2_ref[...] = jnp.zeros(o2_ref.shape, BF16)
    o2_ref[0:GLA_RANK] = w2_ref[...].astype(BF16)
    kv_end = W_KV + 2 * KV_WIDTH
    o_ref[W_QA:kv_end] = w_ref[0:kv_end].astype(BF16)
    o_ref[kv_end:W_QG] = jnp.zeros((RANK_PAD, o_ref.shape[1]), BF16)
    o_ref[kv_end:kv_end + GLA_RANK] = w_ref[MAIN_WIDTH:MAIN_WIDTH + GLA_RANK].astype(BF16)
    o_ref[W_QG:IN_PAD_WIDTH] = w_ref[kv_end:MAIN_WIDTH].astype(BF16)


def _weight_prep(w_t, w2, cols):
    n, k = w_t.shape
    assert n == MAIN_WIDTH + GLA_RANK and k % cols == 0 and w2.shape == (GLA_RANK, GLA_QK_WIDTH)
    return pl.pallas_call(
        _weight_prep_kernel,
        grid=(k // cols,),
        in_specs=[pl.BlockSpec((n, cols), lambda i: (0, i)),
                  _const_spec((GLA_RANK, GLA_QK_WIDTH))],
        out_specs=[pl.BlockSpec((IN_PAD_WIDTH, cols), lambda i: (0, i)),
                   pl.BlockSpec((RANK_PAD, GLA_QK_WIDTH), lambda i: (0, 0))],
        out_shape=[jax.ShapeDtypeStruct((IN_PAD_WIDTH, k), BF16),
                   jax.ShapeDtypeStruct((RANK_PAD, GLA_QK_WIDTH), BF16)],
        compiler_params=pltpu.CompilerParams(
            dimension_semantics=("arbitrary",), vmem_limit_bytes=VMEM_LIMIT),
        name="weight_prep",
    )(w_t, w2)


def _project(x, xs, g_attn, w_in_p, qn, kn, w2p, b2, rows, to_cast):
    t = x.shape[0]
    ns = xs.shape[0]
    steps = t // rows
    cast_specs = []
    for w in to_cast:
        slab = w.shape[0] // steps
        assert w.shape[0] % steps == 0 and slab % (2 * SUBLANES) == 0
        cast_specs.append(pl.BlockSpec((slab, w.shape[1]), lambda i: (i, 0)))
    return pl.pallas_call(
        _proj_kernel,
        grid=(steps,),
        in_specs=[
            pl.BlockSpec((rows, D_MODEL), lambda i: (i, 0)),
            _sample_rows_spec(ns, pipeline_mode=pl.Buffered(1)),
            _const_spec((1, D_MODEL)),
            _const_spec((IN_PAD_WIDTH, D_MODEL)),
            _const_spec((1, ATTN_WIDTH)),
            _const_spec((1, KV_WIDTH)),
            _const_spec((RANK_PAD, GLA_QK_WIDTH)),
            _const_spec((1, GLA_QK_WIDTH)),
            _const_spec((BLK, 3 * BLK)),
        ] + cast_specs,
        out_specs=[pl.BlockSpec((rows, F_WIDTH), lambda i: (i, 0)),
                   pl.BlockSpec((rows, H_WIDTH), lambda i: (i, 0)),
                   pl.BlockSpec((KV_WIDTH, rows), lambda i: (0, i)),
                   pl.BlockSpec((WINDOW, 2 * KV_WIDTH), lambda i: (0, 0)),
                   pl.BlockSpec((ns, F_WIDTH), lambda i: (0, 0)),
                   pl.BlockSpec((ns, H_WIDTH), lambda i: (0, 0)),
                   pl.BlockSpec((ns, 2 * KV_WIDTH), lambda i: (0, 0))] + cast_specs,
        out_shape=[jax.ShapeDtypeStruct((t, F_WIDTH), F32),
                   jax.ShapeDtypeStruct((t, H_WIDTH), BF16),
                   jax.ShapeDtypeStruct((KV_WIDTH, t), BF16),
                   jax.ShapeDtypeStruct((WINDOW, 2 * KV_WIDTH), F32),
                   jax.ShapeDtypeStruct((ns, F_WIDTH), F32),
                   jax.ShapeDtypeStruct((ns, H_WIDTH), F32),
                   jax.ShapeDtypeStruct((ns, 2 * KV_WIDTH), F32)]
        + [jax.ShapeDtypeStruct(w.shape, BF16) for w in to_cast],
        compiler_params=pltpu.CompilerParams(
            dimension_semantics=("arbitrary",), vmem_limit_bytes=VMEM_LIMIT),
        name="proj",
    )(x, xs, g_attn, w_in_p, qn, kn, w2p, b2, jnp.asarray(_TRI3, BF16), *to_cast)


def _boundary_rows(b_ref, r0, c0, b, row, level):
    m = 1 << (level - 1)
    if 2 * m >= SUBLANES:
        pieces = [jnp.broadcast_to(b_ref[r0 + g * 2 * m + m - 1:r0 + g * 2 * m + m, c0:c0 + LANES],
                                   (2 * m, LANES))
                  for g in range(BLK // (2 * m))]
        return pieces[0] if len(pieces) == 1 else jnp.concatenate(pieces, axis=0)
    pos = row & (2 * m - 1)
    tiles = b.reshape(BLK // SUBLANES, SUBLANES, LANES)
    out = b
    for p in range(2 * m):
        shift = (m - 1) - p
        if shift != 0:
            rolled = pltpu.roll(tiles, (-shift) % SUBLANES, 1).reshape(BLK, LANES)
            out = jnp.where(pos == p, rolled, out)
    return out


def _mixer_init(relb_ref, bucket_ref, kbd, vbd, kprev_t, vprev, st_scr, mb_scr):
    kbd[...] = jnp.zeros_like(kbd)
    vbd[...] = jnp.zeros_like(vbd)
    for slot in range(vbd.shape[0]):
        for g in range(ATTN_KV_HEADS):
            for half in range(2):
                vbd[slot, g, 2 * half * BLK:(2 * half + 2) * BLK,
                    KV_WIDTH + half * HEAD_DIM:KV_WIDTH + (half + 1) * HEAD_DIM] = (
                        jnp.ones((2 * BLK, HEAD_DIM), BF16))
    kprev_t[...] = jnp.zeros_like(kprev_t)
    vprev[...] = jnp.zeros_like(vprev)
    st_scr[...] = jnp.zeros_like(st_scr)
    bk = bucket_ref[0]
    acc = [jnp.zeros(bk.shape, F32) for _ in range(ATTN_HEADS)]
    for b in range(N_BUCKETS):
        hit = bk == b
        for h in range(ATTN_HEADS):
            acc[h] = jnp.where(hit, relb_ref[h, b], acc[h])
    for tb in range(2):
        masked = bucket_ref[tb] < 0
        for h in range(ATTN_HEADS):
            mb_scr[tb, h // 2, :, (h % 2) * BLK:(h % 2 + 1) * BLK] = jnp.where(masked, NEG, acc[h])


def _mixer_attention(pb_ref, kt_ref, r0, table, sink_ref, omix_ref, kbd, vbd, k_prev_t, v_prev,
                     mb_scr):
    rows = slice(r0, r0 + BLK)
    lo1 = lax.broadcasted_iota(jnp.int32, (BLK, LANES), 1) < HEAD_DIM
    k_t = kt_ref[:, r0:r0 + BLK]
    for g in range(ATTN_KV_HEADS):
        hd = slice(g * HEAD_DIM, (g + 1) * HEAD_DIM)
        kbd[g, 0:HEAD_DIM, 0:BLK] = k_prev_t[hd]
        kbd[g, 0:HEAD_DIM, BLK:2 * BLK] = k_t[hd]
        kbd[g, HEAD_DIM:2 * HEAD_DIM, 2 * BLK:3 * BLK] = k_prev_t[hd]
        kbd[g, HEAD_DIM:2 * HEAD_DIM, 3 * BLK:4 * BLK] = k_t[hd]
    lo_bf = jnp.where(lo1, 1.0, 0.0).astype(BF16)
    hi_bf = jnp.where(lo1, 0.0, 1.0).astype(BF16)
    lo_row = lax.broadcasted_iota(jnp.int32, (1, LANES), 1) < HEAD_DIM
    own = (lax.broadcasted_iota(jnp.int32, (BLK, BLK), 1)
           <= lax.broadcasted_iota(jnp.int32, (BLK, BLK), 0))
    own_bf = jnp.where(own, 1.0, 0.0).astype(BF16)
    prev_bf = jnp.where(own, 0.0, 1.0).astype(BF16)
    v_cur = pb_ref[rows,H_VA:H_VS]
    v_swap = pb_ref[rows,H_VS:H_VG]
    v_parts = (v_cur * lo_bf, v_swap * hi_bf, v_swap * lo_bf, v_cur * hi_bf)
    for n, part in enumerate(v_parts):
        g, half = divmod(n, 2)
        vbd[g, 2 * half * BLK:(2 * half + 1) * BLK, 0:KV_WIDTH] = v_prev[n]
        vbd[g, (2 * half + 1) * BLK:(2 * half + 2) * BLK, 0:KV_WIDTH] = part
    chunks_per_kv = ATTN_HEADS // ATTN_KV_HEADS // 2
    for g in range(ATTN_KV_HEADS):
        c_first = g * chunks_per_kv
        qs = jnp.concatenate(
            [pb_ref[rows,H_QA + (c_first + c) * LANES:H_QA + (c_first + c + 1) * LANES]
             for c in range(chunks_per_kv)], axis=0)
        s = jnp.dot(qs, kbd[g], preferred_element_type=F32)
        prob_rows, sink_rows = [], []
        for c in range(chunks_per_kv):
            probs, maxes, sks = [], [], []
            for e in range(2):
                s_prev = s[c * BLK:(c + 1) * BLK, 2 * e * BLK:(2 * e + 1) * BLK]
                s_own = s[c * BLK:(c + 1) * BLK, (2 * e + 1) * BLK:(2 * e + 2) * BLK]
                se = (jnp.where(own, s_own, s_prev)
                      + mb_scr[table, c_first + c, :, e * BLK:(e + 1) * BLK])
                sk = sink_ref[2 * (c_first + c) + e]
                m = jnp.maximum(jnp.max(se, axis=-1, keepdims=True), sk)
                pe = jnp.exp(se - m).astype(BF16)
                probs += [pe * prev_bf, pe * own_bf]
                maxes.append(m)
                sks.append(sk)
            prob_rows.append(jnp.concatenate(probs, axis=1))
            sink_rows.append(jnp.exp(jnp.where(lo_row, sks[0], sks[1])
                                     - jnp.where(lo1, maxes[0], maxes[1])))
        o_den = jnp.dot(jnp.concatenate(prob_rows, axis=0), vbd[g], preferred_element_type=F32)
        o = o_den[:, :KV_WIDTH] / (o_den[:, KV_WIDTH:] + jnp.concatenate(sink_rows, axis=0))
        for c in range(chunks_per_kv):
            omix_ref[rows,(c_first + c) * LANES:(c_first + c + 1) * LANES] = (
                o[c * BLK:(c + 1) * BLK].astype(omix_ref.dtype))
    return k_t, v_parts


def _mixer_gla(state, p_ref, pb_ref, r0, lev_ref, gn_ref, omix_ref):
    new_state = []
    pairs = range(len(state))
    rows = slice(r0, r0 + BLK)
    lo1 = lax.broadcasted_iota(jnp.int32, (BLK, LANES), 1) < HEAD_DIM
    lo_bf = jnp.where(lo1, 1.0, 0.0).astype(BF16)
    hi_bf = jnp.where(lo1, 0.0, 1.0).astype(BF16)
    row = lax.broadcasted_iota(jnp.int32, (BLK, LANES), 0)
    zero_blk = jnp.zeros((BLK, LANES), BF16)
    for c in pairs:
        c0 = c * LANES
        q_at = lambda a, z: p_ref[r0 + a:r0 + z, F_QG + c0:F_QG + c0 + LANES]
        k_at = lambda a, z: p_ref[r0 + a:r0 + z, F_KG + c0:F_KG + c0 + LANES]
        b_at = lambda a, z: p_ref[r0 + a:r0 + z, F_LA + c0:F_LA + c0 + LANES]

        def pair_scores(qtb, ktb_lo, ktb_hi):
            return _nt_dot(qtb, jnp.concatenate([ktb_lo, ktb_hi], axis=0))

        kb = k_at(0, BLK).astype(BF16)
        s0 = pair_scores(q_at(0, BLK).astype(BF16), kb * lo_bf, kb * hi_bf)
        tile = lambda a, n: a[n * SUBLANES:(n + 1) * SUBLANES]
        n_tiles = BLK // SUBLANES
        lev_t = [lev_ref[n * SUBLANES:(n + 1) * SUBLANES, :] for n in range(n_tiles)]
        sc = [[jnp.where(lev_t[n] == 0, tile(s0[:, e * BLK:(e + 1) * BLK], n), 0.0)
               for n in range(n_tiles)] for e in range(2)]
        for level in range(1, N_LEVELS + 1):
            m = 1 << (level - 1)
            if m >= SUBLANES:
                qs, klos, khis, dest = [], [], [], []
                zeros = jnp.zeros((m, LANES), BF16)
                lane_m = lax.broadcasted_iota(jnp.int32, (m, LANES), 1) < HEAD_DIM
                lo_m = jnp.where(lane_m, 1.0, 0.0).astype(BF16)
                hi_m = jnp.where(lane_m, 0.0, 1.0).astype(BF16)
                for g in range(BLK // (2 * m)):
                    lo_a, up_a, up_z = g * 2 * m, g * 2 * m + m, (g + 1) * 2 * m
                    rb = jnp.broadcast_to(b_at(up_a - 1, up_a), (m, LANES))
                    qs.append((q_at(up_a, up_z) * jnp.exp(b_at(up_a, up_z) - rb)).astype(BF16))
                    kp = (k_at(lo_a, up_a) * jnp.exp(rb - b_at(lo_a, up_a))).astype(BF16)
                    klos += [kp * lo_m, zeros]
                    khis += [kp * hi_m, zeros]
                    dest += list(range(up_a // SUBLANES, up_z // SUBLANES))
                sl = pair_scores(jnp.concatenate(qs, axis=0), jnp.concatenate(klos, axis=0),
                                 jnp.concatenate(khis, axis=0))
            else:
                dest = list(range(n_tiles))
                bc = b_at(0, BLK)
                decay = jnp.exp(-jnp.abs(bc - _boundary_rows(p_ref, r0, F_LA + c0, bc, row, level)))
                upper = ((row >> (level - 1)) & 1) == 1
                kl = (k_at(0, BLK) * decay).astype(BF16)
                sl = pair_scores(
                    (q_at(0, BLK) * decay).astype(BF16) * jnp.where(upper, 1.0, 0.0).astype(BF16),
                    kl * jnp.where(upper | ~lo1, 0.0, 1.0).astype(BF16),
                    kl * jnp.where(upper | lo1, 0.0, 1.0).astype(BF16))
            for src, n in enumerate(dest):
                for e in range(2):
                    sc[e][n] = jnp.where(lev_t[n] == level, tile(sl[:, e * BLK:(e + 1) * BLK], src),
                                         sc[e][n])
        sc = jnp.concatenate([jnp.concatenate(sc[e], axis=0) for e in range(2)], axis=1)
        qc, kc, bc = q_at(0, BLK), k_at(0, BLK), b_at(0, BLK)

        b_last = bc[BLK - 1:BLK, :]
        v0 = pb_ref[rows,H_VG + 2 * c0:H_VG + 2 * c0 + LANES]
        v1 = pb_ref[rows,H_VG + 2 * c0 + LANES:H_VG + 2 * c0 + 2 * LANES]
        v_bd = jnp.concatenate([jnp.concatenate([v0, zero_blk], axis=1),
                                jnp.concatenate([zero_blk, v1], axis=1)], axis=0)
        st_c = state[c]
        stb = st_c.astype(BF16)
        st_rhs = jnp.concatenate([stb * lo_bf, stb * hi_bf], axis=0)
        o = (jnp.dot(sc.astype(BF16), v_bd, preferred_element_type=F32)
             + _nt_dot((qc * jnp.exp(bc)).astype(BF16), st_rhs))
        kd = (kc * jnp.exp(b_last - bc)).astype(BF16)
        upd = _tn_dot(jnp.concatenate([v0, v1], axis=1), kd)
        new_state.append(st_c * jnp.exp(b_last) + jnp.where(lo1, upd[:BLK], upd[BLK:]))
        for e in range(2):
            h = 2 * c + e
            oh = o[:, e * LANES:(e + 1) * LANES]
            og = oh * lax.rsqrt(jnp.mean(oh * oh, axis=-1, keepdims=True) + EPS) * gn_ref[...]
            rg = p_ref[rows,F_RG + h * GLA_DV:F_RG + (h + 1) * GLA_DV]
            gated = og * (rg * _sigmoid(rg))
            omix_ref[rows,ATTN_WIDTH + h * GLA_DV:ATTN_WIDTH + (h + 1) * GLA_DV] = (
                gated.astype(omix_ref.dtype))
    return new_state


def _prompt_mixer_kernel(relb_ref, sink_ref, p_ref, pb_ref, kt_ref, bucket_ref, lev_ref, gn_ref,
                         omix_ref, st_ref, kbd, vbd, kprev_t, vprev, st_scr, mb_scr):
    i = pl.program_id(0)

    @pl.when(i == 0)
    def _init():
        _mixer_init(relb_ref, bucket_ref, kbd, vbd, kprev_t, vprev, st_scr, mb_scr)

    k_prev_t = kprev_t[...]
    v_prev = [vprev[n] for n in range(2 * ATTN_KV_HEADS)]
    state = [st_scr[:, c * LANES:(c + 1) * LANES] for c in range(GLA_HEADS // 2)]
    n_blocks = p_ref.shape[0] // BLK
    for jb in range(n_blocks):
        table = jnp.where(i == 0, 1, 0) if jb == 0 else 0
        k_prev_t, v_prev = _mixer_attention(pb_ref, kt_ref, jb * BLK, table, sink_ref, omix_ref,
                                            kbd.at[jb], vbd.at[jb], k_prev_t, v_prev, mb_scr)
        state = _mixer_gla(state, p_ref, pb_ref, jb * BLK, lev_ref, gn_ref, omix_ref)
    kprev_t[...] = k_prev_t
    for n, part in enumerate(v_prev):
        vprev[n] = part
    for c, st_c in enumerate(state):
        st_scr[:, c * LANES:(c + 1) * LANES] = st_c

    @pl.when(i == pl.num_programs(0) - 1)
    def _final_state():
        for c, st_c in enumerate(state):
            st_ref[c * LANES:(c + 1) * LANES, :] = st_c.T


def _prompt_mixer(p, pb, kt, relb, sinks, gn):
    t = p.shape[0]
    smem = pl.BlockSpec(memory_space=pltpu.SMEM)
    return pl.pallas_call(
        _prompt_mixer_kernel,
        grid=(t // MIX_ROWS,),
        in_specs=[
            smem, smem,
            pl.BlockSpec((MIX_ROWS, F_WIDTH), lambda i: (i, 0)),
            pl.BlockSpec((MIX_ROWS, H_WIDTH), lambda i: (i, 0)),
            pl.BlockSpec((KV_WIDTH, MIX_ROWS), lambda i: (0, i)),
            _const_spec((2, BLK, BLK)),
            _const_spec((BLK, BLK)),
            _const_spec((1, GLA_DV)),
        ],
        out_specs=[
            pl.BlockSpec((MIX_ROWS, MIX_WIDTH), lambda i: (i, 0)),
            pl.BlockSpec((GLA_QK_WIDTH, GLA_DV), lambda i: (0, 0)),
        ],
        out_shape=[
            jax.ShapeDtypeStruct((t, MIX_WIDTH), BF16),
            jax.ShapeDtypeStruct((GLA_QK_WIDTH, GLA_DV), F32),
        ],
        scratch_shapes=[
            pltpu.VMEM((MIX_ROWS // BLK, ATTN_KV_HEADS, 2 * HEAD_DIM, 4 * BLK), BF16),
            pltpu.VMEM((MIX_ROWS // BLK, ATTN_KV_HEADS, 4 * BLK, 2 * KV_WIDTH), BF16),
            pltpu.VMEM((KV_WIDTH, BLK), BF16),
            pltpu.VMEM((2 * ATTN_KV_HEADS, BLK, KV_WIDTH), BF16),
            pltpu.VMEM((GLA_DV, GLA_QK_WIDTH), F32),
            pltpu.VMEM((2, ATTN_HEADS // 2, BLK, 2 * BLK), F32),
        ],
        compiler_params=pltpu.CompilerParams(
            dimension_semantics=("arbitrary",), vmem_limit_bytes=VMEM_LIMIT),
        name="prompt_mixer",
    )(relb, sinks, p, pb, kt, jnp.asarray(_BUCKET_PROMPT), jnp.asarray(_LEV), gn)


def _sample_mixer_kernel(ps_ref, ph_ref, kvfull_ref, pfull_ref, ck_ref, cv_ref, st_ref, relbt_ref,
                         sink_ref, bucket_ref,
                         gn_ref, omix_ref, kwin_ref, vwin_ref, stout_ref, lat_scr, kqt_scr, kvt_scr,
                         bias_scr, s_scr, o_scr, og_scr):
    i = pl.program_id(0)
    nb = pfull_ref.shape[0]

    @pl.when(i == 0)
    def _init():
        lat_scr[...] = _split3_rows(pfull_ref[:, F_LA:F_WIDTH].T)
        kqt_scr[0:GLA_QK_WIDTH] = pfull_ref[:, F_KG:F_RG].T.astype(BF16)
        kqt_scr[GLA_QK_WIDTH:2 * GLA_QK_WIDTH] = pfull_ref[:, F_QG:F_KG].T.astype(BF16)
        kvt = kvfull_ref[...].T
        for s in range(nb // SAMPLE_BLK):
            shift = nb - (s + 1) * SAMPLE_BLK
            kvt_scr[s] = pltpu.roll(kvt, shift, 1) if shift else kvt
        bk = jnp.broadcast_to(bucket_ref[...], (ATTN_HEADS, WINDOW))
        acc = jnp.zeros((ATTN_HEADS, WINDOW), F32)
        for b in range(N_BUCKETS):
            acc = jnp.where(bk == b, relbt_ref[:, b:b + 1], acc)
        bias_scr[...] = acc

    lo = lax.broadcasted_iota(jnp.int32, (1, LANES), 1) < HEAD_DIM
    sub = lax.broadcasted_iota(jnp.int32, (ATTN_HEADS, LANES), 0)
    newest = lax.broadcasted_iota(jnp.int32, (KV_WIDTH, WINDOW), 1) == WINDOW - 1
    heads_per_kv = ATTN_HEADS // ATTN_KV_HEADS

    n_of_col = i * SAMPLE_BLK + lax.broadcasted_iota(jnp.int32, (nb, SAMPLE_BLK * LANES), 1) // LANES
    pick = jnp.where(lax.broadcasted_iota(jnp.int32, (nb, SAMPLE_BLK * LANES), 0) == n_of_col,
                     1.0, 0.0).astype(BF16)
    la_b = (jnp.dot(lat_scr[0:GLA_QK_WIDTH], pick, preferred_element_type=F32)
            + jnp.dot(lat_scr[GLA_QK_WIDTH:2 * GLA_QK_WIDTH], pick, preferred_element_type=F32)
            + jnp.dot(lat_scr[2 * GLA_QK_WIDTH:3 * GLA_QK_WIDTH], pick, preferred_element_type=F32))
    kq_b = jnp.dot(kqt_scr[...], pick, preferred_element_type=F32)

    for j in range(SAMPLE_BLK):
        kv_new = kvt_scr[i]
        if j < SAMPLE_BLK - 1:
            kv_new = pltpu.roll(kv_new, SAMPLE_BLK - 1 - j, 1)
        kwin_ref[j] = jnp.where(newest, kv_new[0:KV_WIDTH], pltpu.roll(ck_ref[j], WINDOW - 1, 1))
        vwin_ref[j] = jnp.where(newest, kv_new[KV_WIDTH:2 * KV_WIDTH],
                                pltpu.roll(cv_ref[j], WINDOW - 1, 1))

    for j in range(SAMPLE_BLK):
        qexp = jnp.zeros((ATTN_HEADS, LANES), F32)
        for c in range(ATTN_HEADS // 2):
            chunk = ph_ref[j:j + 1, H_QA + c * LANES:H_QA + (c + 1) * LANES]
            swapped = pltpu.roll(chunk, HEAD_DIM, 1)
            if (2 * c) // heads_per_kv == 0:
                rows = (jnp.where(lo, chunk, 0.0), jnp.where(lo, swapped, 0.0))
            else:
                rows = (jnp.where(lo, 0.0, swapped), jnp.where(lo, 0.0, chunk))
            for e in range(2):
                qexp = jnp.where(sub == 2 * c + e, rows[e], qexp)
        s_scr[j * ATTN_HEADS:(j + 1) * ATTN_HEADS] = jnp.dot(
            qexp.astype(BF16), kwin_ref[j].astype(BF16), preferred_element_type=F32)

    tile = lambda x: jnp.concatenate([x] * SAMPLE_BLK, axis=0)
    head_of_row = lax.broadcasted_iota(jnp.int32, (ATTN_HEADS, 1), 0)
    sink8 = jnp.zeros((ATTN_HEADS, 1), F32)
    for h in range(ATTN_HEADS):
        sink8 = jnp.where(head_of_row == h, sink_ref[h], sink8)
    sink = tile(sink8)
    s = s_scr[...] + tile(bias_scr[...])
    m = jnp.maximum(jnp.max(s, axis=-1, keepdims=True), sink)
    pe = jnp.exp(s - m)
    inv_den = 1.0 / (jnp.sum(pe, axis=-1, keepdims=True) + jnp.exp(sink - m))
    peb = pe.astype(BF16)
    for j in range(SAMPLE_BLK):
        o_scr[j * ATTN_HEADS:(j + 1) * ATTN_HEADS] = _nt_dot(
            peb[j * ATTN_HEADS:(j + 1) * ATTN_HEADS], vwin_ref[j].astype(BF16))
    o_all = o_scr[...] * inv_den
    o_swap = pltpu.roll(o_all, HEAD_DIM, 1)
    for j in range(SAMPLE_BLK):
        r = j * ATTN_HEADS
        for c in range(ATTN_HEADS // 2):
            if (2 * c) // heads_per_kv == 0:
                piece = jnp.where(lo, o_all[r + 2 * c:r + 2 * c + 1, :], o_swap[r + 2 * c + 1:r + 2 * c + 2, :])
            else:
                piece = jnp.where(lo, o_swap[r + 2 * c:r + 2 * c + 1, :], o_all[r + 2 * c + 1:r + 2 * c + 2, :])
            omix_ref[j:j + 1, c * LANES:(c + 1) * LANES] = piece

    for j in range(SAMPLE_BLK):
        cols = slice(j * LANES, (j + 1) * LANES)
        for h in range(GLA_HEADS):
            rs = slice(h * GLA_DK, (h + 1) * GLA_DK)
            qs = slice(GLA_QK_WIDTH + h * GLA_DK, GLA_QK_WIDTH + (h + 1) * GLA_DK)
            v_row = ph_ref[j:j + 1, H_VG + h * GLA_DV:H_VG + (h + 1) * GLA_DV]
            s_new = jnp.exp(la_b[rs, cols]) * st_ref[j, h] + kq_b[rs, cols] * v_row
            stout_ref[j, h] = s_new
            og_scr[j:j + 1, h * GLA_DV:(h + 1) * GLA_DV] = jnp.sum(
                kq_b[qs, cols] * s_new, axis=0, keepdims=True)
    for h in range(GLA_HEADS):
        hs = slice(h * GLA_DV, (h + 1) * GLA_DV)
        og = og_scr[:, hs]
        og = og * lax.rsqrt(jnp.mean(og * og, axis=-1, keepdims=True) + EPS) * gn_ref[...]
        rg = ps_ref[:, F_RG + h * GLA_DV:F_RG + (h + 1) * GLA_DV]
        omix_ref[:, ATTN_WIDTH + h * GLA_DV:ATTN_WIDTH + (h + 1) * GLA_DV] = og * (rg * _sigmoid(rg))


def _sample_mixer(ps, ph, kv, cache_k, cache_v, state, relbt, sinks, gn):
    nb = ps.shape[0]
    assert nb == LANES
    blk3 = lambda i: (i, 0, 0)
    blk4 = lambda i: (i, 0, 0, 0)
    return pl.pallas_call(
        _sample_mixer_kernel,
        grid=(nb // SAMPLE_BLK,),
        in_specs=[
            pl.BlockSpec((SAMPLE_BLK, F_WIDTH), lambda i: (i, 0)),
            pl.BlockSpec((SAMPLE_BLK, H_WIDTH), lambda i: (i, 0)),
            _const_spec((nb, 2 * KV_WIDTH)),
            _const_spec((nb, F_WIDTH)),
            pl.BlockSpec((SAMPLE_BLK, KV_WIDTH, WINDOW), blk3),
            pl.BlockSpec((SAMPLE_BLK, KV_WIDTH, WINDOW), blk3),
            pl.BlockSpec((SAMPLE_BLK, GLA_HEADS, GLA_DK, GLA_DV), blk4),
            _const_spec((ATTN_HEADS, N_BUCKETS)),
            pl.BlockSpec(memory_space=pltpu.SMEM),
            _const_spec((1, WINDOW)),
            _const_spec((1, GLA_DV)),
        ],
        out_specs=[
            pl.BlockSpec((SAMPLE_BLK, MIX_WIDTH), lambda i: (i, 0)),
            pl.BlockSpec((SAMPLE_BLK, KV_WIDTH, WINDOW), blk3),
            pl.BlockSpec((SAMPLE_BLK, KV_WIDTH, WINDOW), blk3),
            pl.BlockSpec((SAMPLE_BLK, GLA_HEADS, GLA_DK, GLA_DV), blk4),
        ],
        out_shape=[
            jax.ShapeDtypeStruct((nb, MIX_WIDTH), F32),
            jax.ShapeDtypeStruct((nb, KV_WIDTH, WINDOW), F32),
            jax.ShapeDtypeStruct((nb, KV_WIDTH, WINDOW), F32),
            jax.ShapeDtypeStruct((nb, GLA_HEADS, GLA_DK, GLA_DV), F32),
        ],
        scratch_shapes=[
            pltpu.VMEM((3 * GLA_QK_WIDTH, nb), BF16),
            pltpu.VMEM((2 * GLA_QK_WIDTH, nb), BF16),
            pltpu.VMEM((nb // SAMPLE_BLK, 2 * KV_WIDTH, nb), F32),
            pltpu.VMEM((ATTN_HEADS, WINDOW), F32),
            pltpu.VMEM((SAMPLE_BLK * ATTN_HEADS, WINDOW), F32),
            pltpu.VMEM((SAMPLE_BLK * ATTN_HEADS, KV_WIDTH), F32),
            pltpu.VMEM((SAMPLE_BLK, GLA_WIDTH), F32),
        ],
        compiler_params=pltpu.CompilerParams(
            dimension_semantics=("arbitrary",), vmem_limit_bytes=VMEM_LIMIT),
        name="sample_mixer",
    )(ps, ph, kv, ps, cache_k, cache_v, state, relbt, sinks, jnp.asarray(_BUCKET_SAMPLE), gn)


def _finish_kernel(x_ref, mix_ref, xs_ref, mixs_ref, wo_ref, g_ref, wg_ref, wu_ref, wd_ref,
                   y_ref, ys_ref, *, ff_chunks):
    weights = (wo_ref, g_ref, wg_ref, wu_ref, wd_ref)
    _finish_rows(x_ref, mix_ref, *weights, y_ref, ff_chunks=ff_chunks)

    @pl.when(pl.program_id(0) == pl.num_programs(0) - 1)
    def _samples():
        _finish_rows(xs_ref, mixs_ref, *weights, ys_ref, ff_chunks=ff_chunks)


def _finish_rows(x_ref, mix_ref, wo_ref, g_ref, wg_ref, wu_ref, wd_ref, y_ref, *, ff_chunks):
    h = x_ref[...] + jnp.dot(mix_ref[...].astype(BF16), wo_ref[...], preferred_element_type=F32)
    r = lax.rsqrt(jnp.mean(h * h, axis=-1, keepdims=True) + EPS)
    z = (h * g_ref[...]).astype(BF16)
    n_tiles = wd_ref.shape[0] // MXU_TILE
    acc = h
    for c in range(ff_chunks):
        c0 = ((c * n_tiles) // ff_chunks) * MXU_TILE
        c1 = (((c + 1) * n_tiles) // ff_chunks) * MXU_TILE
        gate = jnp.dot(z, wg_ref[:, c0:c1], preferred_element_type=F32) * r
        up = jnp.dot(z, wu_ref[:, c0:c1], preferred_element_type=F32) * r
        act = ((gate * _sigmoid(gate)) * up).astype(BF16)
        acc = acc + jnp.dot(act, wd_ref[c0:c1, :], preferred_element_type=F32)
    y_ref[...] = acc


def _finish(x, mix, xs, mixs, wo, g_ffn, wg, wu, wd, rows):
    t = x.shape[0]
    ns = xs.shape[0]
    d_ff = wd.shape[0]
    assert d_ff % MXU_TILE == 0
    return pl.pallas_call(
        functools.partial(_finish_kernel, ff_chunks=FF_CHUNKS),
        grid=(t // rows,),
        in_specs=[
            pl.BlockSpec((rows, D_MODEL), lambda i: (i, 0)),
            pl.BlockSpec((rows, MIX_WIDTH), lambda i: (i, 0)),
            _sample_rows_spec(ns, pipeline_mode=pl.Buffered(1)),
            _const_spec((ns, MIX_WIDTH)),
            _const_spec((MIX_WIDTH, D_MODEL)),
            _const_spec((1, D_MODEL)),
            _const_spec((D_MODEL, d_ff)),
            _const_spec((D_MODEL, d_ff)),
            _const_spec((d_ff, D_MODEL)),
        ],
        out_specs=[pl.BlockSpec((rows, D_MODEL), lambda i: (i, 0)),
                   _sample_rows_spec(ns)],
        out_shape=[jax.ShapeDtypeStruct((t, D_MODEL), F32),
                   jax.ShapeDtypeStruct((ns, 1, D_MODEL), F32)],
        compiler_params=pltpu.CompilerParams(
            dimension_semantics=("arbitrary",), vmem_limit_bytes=VMEM_LIMIT),
        name="finish",
    )(x, mix, xs, mixs, wo, g_ffn, wg, wu, wd)


PROMPT_ROWS = 1024
PROJ_ROWS = 1024
PREP_COLS = 512
MIX_ROWS = 1024


def kernel(x_prompt, x_sample, cache_k, cache_v, state_gla, attn_norm_g, w_in, q_norm_g, k_norm_g,
           attn_sinks, rel_bias, w_gla_gate2, b_gla_gate, gla_norm_g, w_o, ffn_norm_g, w_gate, w_up,
           w_down):
    depth = w_in.shape[0]
    batch, seq, _ = x_prompt.shape
    dec_batch, dec_seq, _ = x_sample.shape
    wb = cache_k.shape[2]
    assert batch == 1 and dec_seq == 1 and wb == WINDOW
    assert seq % PROMPT_ROWS == 0 and seq % PROJ_ROWS == 0
    assert dec_batch % SAMPLE_BLK == 0 and dec_batch % LANES == 0
    assert rel_bias.shape == (N_BUCKETS, ATTN_HEADS)

    xp = x_prompt.reshape(seq, D_MODEL)
    xs = x_sample
    relb_t = rel_bias.T
    outs = ([], [], [], [], [], [])
    for l in range(depth):
        w_in_p, w2p = _weight_prep(w_in[l].T, w_gla_gate2[l], cols=PREP_COLS)
        proj_w = (attn_norm_g[l][None, :], w_in_p,
                  jnp.tile(q_norm_g[l], ATTN_HEADS)[None, :] * ATTN_SCALE,
                  jnp.tile(k_norm_g[l], ATTN_KV_HEADS)[None, :], w2p, b_gla_gate[l][None, :])
        gn = gla_norm_g[l][None, :]

        pp, pb, kt_p, kv_win, ps, ph, kv_s, wo_b, wg_b, wu_b, wd_b = _project(
            xp, xs, *proj_w, rows=PROJ_ROWS, to_cast=(w_o[l], w_gate[l], w_up[l], w_down[l]))
        fin_w = (wo_b, ffn_norm_g[l][None, :], wg_b, wu_b, wd_b)
        mix_p, st_p = _prompt_mixer(pp, pb, kt_p, relb_t, attn_sinks[l], gn)
        to_t = lambda c: jnp.transpose(c, (0, 2, 3, 1)).reshape(dec_batch, KV_WIDTH, wb)
        from_t = lambda c: jnp.transpose(c.reshape(dec_batch, ATTN_KV_HEADS, HEAD_DIM, wb), (0, 3, 1, 2))
        mix_s, kwin_t, vwin_t, st_s = _sample_mixer(
            ps, ph, kv_s, to_t(cache_k[l]), to_t(cache_v[l]),
            state_gla[l].astype(F32), relb_t, attn_sinks[l], gn)
        xp_in = xp
        xp, xs = _finish(xp_in, mix_p, xs, mix_s, *fin_w, rows=PROMPT_ROWS)
        outs[0].append(kv_win[:, :KV_WIDTH].reshape(batch, wb, ATTN_KV_HEADS, HEAD_DIM))
        outs[1].append(kv_win[:, KV_WIDTH:].reshape(batch, wb, ATTN_KV_HEADS, HEAD_DIM))
        outs[2].append(st_p.reshape(batch, GLA_HEADS, GLA_DK, GLA_DV).astype(state_gla.dtype))
        outs[3].append(from_t(kwin_t))
        outs[4].append(from_t(vwin_t))
        outs[5].append(st_s.astype(state_gla.dtype))

    y_prompt = xp.reshape(batch, seq, D_MODEL)
    y_sample = xs
    return (y_prompt, y_sample) + tuple(jnp.stack(o) for o in outs)
```

```python
import functools
import math

import numpy as np
import jax
import jax.numpy as jnp
from jax import lax
from jax.experimental import pallas as pl
from jax.experimental.pallas import tpu as pltpu

F32 = jnp.float32
BF16 = jnp.bfloat16

D_MODEL = 1024
HEAD_DIM = 64
ATTN_HEADS = 8
ATTN_KV_HEADS = 2
WINDOW = 128
N_BUCKETS = 32
MAX_DISTANCE = 128
GLA_HEADS = 4
GLA_DK = 64
GLA_DV = 128
GLA_RANK = 16
GLA_TAU = 16.0
EPS = 1e-6
ATTN_WIDTH = ATTN_HEADS * HEAD_DIM
KV_WIDTH = ATTN_KV_HEADS * HEAD_DIM
GLA_QK_WIDTH = GLA_HEADS * GLA_DK
GLA_WIDTH = GLA_HEADS * GLA_DV
MIX_WIDTH = ATTN_WIDTH + GLA_WIDTH
MAIN_WIDTH = ATTN_WIDTH + 2 * KV_WIDTH + 2 * GLA_QK_WIDTH + 2 * GLA_WIDTH
LANES = 128
SUBLANES = 8
MXU_TILE = 256
FF_CHUNKS = 4
RANK_PAD = LANES
IN_PAD_WIDTH = MAIN_WIDTH + RANK_PAD

F_QG = 0
F_KG = F_QG + GLA_QK_WIDTH
F_RG = F_KG + GLA_QK_WIDTH
F_LA = F_RG + GLA_WIDTH
F_WIDTH = F_LA + GLA_QK_WIDTH
H_QA = 0
H_VA = H_QA + ATTN_WIDTH
H_VS = H_VA + KV_WIDTH
H_VG = H_VS + KV_WIDTH
H_WIDTH = H_VG + GLA_WIDTH

W_QA = 0
W_KV = W_QA + ATTN_WIDTH
W_QG = W_KV + 2 * KV_WIDTH + RANK_PAD
W_VG = W_QG + 2 * GLA_QK_WIDTH
W_RG = W_VG + GLA_WIDTH

BLK = 128
N_LEVELS = 7
NEG = -1e30
ATTN_SCALE = HEAD_DIM ** -0.5
SAMPLE_BLK = 32
VMEM_LIMIT = 56 * 1024 * 1024


def _t5_bucket_np(dist):
    n = np.maximum(dist, 0)
    max_exact = N_BUCKETS // 2
    nf = np.maximum(n, 1).astype(np.float64)
    large = max_exact + (np.log(nf / max_exact) / math.log(MAX_DISTANCE / max_exact)
                         * (N_BUCKETS - max_exact)).astype(np.int32)
    large = np.minimum(large, N_BUCKETS - 1)
    return np.where(n < max_exact, n, large).astype(np.int32)


def _prompt_bucket_tables():
    assert WINDOW == BLK
    i = np.arange(BLK)[:, None]
    j = np.arange(BLK)[None, :]
    own = j <= i
    bucket = _t5_bucket_np(np.where(own, i - j, BLK + i - j))
    t1 = np.where(own, bucket, -1)
    return np.stack([bucket, t1]).astype(np.int32)


def _level_tables():
    t = np.arange(BLK)[:, None]
    s = np.arange(BLK)[None, :]
    x = t ^ s
    lev = np.where(x > 0, np.floor(np.log2(np.maximum(x, 1))).astype(np.int32) + 1, 0)
    lev = np.where(s > t, -1, lev).astype(np.int32)
    tri = (s <= t).astype(np.float32)
    return lev, np.concatenate([tri, tri, tri], axis=1)


_BUCKET_PROMPT = _prompt_bucket_tables()
_LEV, _TRI3 = _level_tables()
_BUCKET_SAMPLE = _t5_bucket_np((WINDOW - 1) - np.arange(WINDOW))[None, :].astype(np.int32)


def _nt_dot(a, b):
    return lax.dot_general(a, b, (((1,), (1,)), ((), ())), preferred_element_type=F32)


def _tn_dot(a, b):
    return lax.dot_general(a, b, (((0,), (0,)), ((), ())), preferred_element_type=F32)


def _head_mean_sq(x):
    lo = lax.broadcasted_iota(jnp.int32, (x.shape[0], LANES), 1) < HEAD_DIM
    outs = []
    for c in range(x.shape[1] // LANES):
        y = x[:, c * LANES:(c + 1) * LANES]
        y = y * y
        s_lo = jnp.sum(jnp.where(lo, y, 0.0), axis=-1, keepdims=True)
        s_hi = jnp.sum(jnp.where(lo, 0.0, y), axis=-1, keepdims=True)
        outs.append(jnp.where(lo, s_lo, s_hi) * (1.0 / HEAD_DIM))
    return outs[0] if len(outs) == 1 else jnp.concatenate(outs, axis=1)


def _sigmoid(x):
    return 1.0 / (1.0 + jnp.exp(-x))


def _split3_rows(x):
    hi = x.astype(BF16)
    r1 = x - hi.astype(F32)
    mid = r1.astype(BF16)
    lo = (r1 - mid.astype(F32)).astype(BF16)
    return jnp.concatenate([hi, mid, lo], axis=0)


def _proj_kernel(x_ref, xs_ref, g_ref, w_ref, qn_ref, kn_ref, w2_ref, b2_ref, tri_ref, *rest):
    n_out = 7
    n_cast = (len(rest) - n_out) // 2
    cast_in, cast_out = rest[:n_cast], rest[n_cast + n_out:]
    f32_ref, bf_ref, kt_ref, kvw_ref, f32s_ref, bfs_ref, kvs_ref = rest[n_cast:n_cast + n_out]
    for src, dst in zip(cast_in, cast_out):
        dst[...] = src[...].astype(BF16)
    weights = (g_ref, w_ref, qn_ref, kn_ref, w2_ref, b2_ref, tri_ref)
    _proj_rows(x_ref, *weights, f32_ref, bf_ref, kt_ref, kvw_ref, block_cumsum=True)

    @pl.when(pl.program_id(0) == pl.num_programs(0) - 1)
    def _samples():
        _proj_rows(xs_ref, *weights, f32s_ref, bfs_ref, None, kvs_ref, block_cumsum=False)


def _proj_rows(x_ref, g_ref, w_ref, qn_ref, kn_ref, w2_ref, b2_ref, tri_ref, f32_ref, bf_ref,
               kt_ref, kv_ref, *, block_cumsum):
    x = x_ref[...]
    r = lax.rsqrt(jnp.mean(x * x, axis=-1, keepdims=True) + EPS)
    n = (x * g_ref[...]).astype(BF16)

    def seg(c0, c1):
        return _nt_dot(n, w_ref[c0:c1, :]) * r

    kvl = seg(W_KV, W_QG)
    lr = kvl[:, 2 * KV_WIDTH:].astype(BF16)
    z = jnp.dot(lr, w2_ref[...], preferred_element_type=F32) + b2_ref[...]
    log_a = (jnp.minimum(z, 0.0) - jnp.log(1.0 + jnp.exp(-jnp.abs(z)))) / GLA_TAU
    if block_cumsum:
        for blk in range(x.shape[0] // BLK):
            rows = slice(blk * BLK, (blk + 1) * BLK)
            f32_ref[rows, F_LA:F_WIDTH] = jnp.dot(
                tri_ref[...], _split3_rows(log_a[rows]), preferred_element_type=F32)
    else:
        f32_ref[:, F_LA:F_WIDTH] = log_a
    k = kvl[:, :KV_WIDTH]
    per_head = lambda g_ref, width: jnp.concatenate([g_ref[...]] * (width // HEAD_DIM), axis=1)
    k = k * lax.rsqrt(_head_mean_sq(k) + EPS) * per_head(kn_ref, KV_WIDTH)
    v = kvl[:, KV_WIDTH:2 * KV_WIDTH]
    n_kv = kv_ref.shape[0]
    kv_ref[:, 0:KV_WIDTH] = k[x.shape[0] - n_kv:]
    kv_ref[:, KV_WIDTH:2 * KV_WIDTH] = v[x.shape[0] - n_kv:]
    if kt_ref is not None:
        kt_ref[...] = k.T.astype(BF16)
    hdt = bf_ref.dtype
    bf_ref[:, H_VA:H_VS] = v.astype(hdt)
    bf_ref[:, H_VS:H_VG] = pltpu.roll(v, HEAD_DIM, 1).astype(hdt)
    q = seg(W_QA, W_KV)
    q_gain = per_head(qn_ref, ATTN_WIDTH) * ATTN_SCALE
    bf_ref[:, H_QA:H_VA] = (q * lax.rsqrt(_head_mean_sq(q) + EPS) * q_gain).astype(hdt)
    qk_g = seg(W_QG, W_VG)
    f32_ref[:, F_QG:F_KG] = qk_g[:, :GLA_QK_WIDTH] * (GLA_DK ** -0.5)
    f32_ref[:, F_KG:F_RG] = qk_g[:, GLA_QK_WIDTH:]
    bf_ref[:, H_VG:H_WIDTH] = seg(W_VG, W_RG).astype(hdt)
    f32_ref[:, F_RG:F_LA] = seg(W_RG, IN_PAD_WIDTH)


def _const_spec(shape):
    nd = len(shape)
    return pl.BlockSpec(shape, lambda i: (0,) * nd, pipeline_mode=pl.Buffered(1))


def _sample_rows_spec(ns, **kwargs):
    return pl.BlockSpec((ns, None, D_MODEL), lambda i: (0, 0, 0), **kwargs)


def _weight_prep_kernel(w_ref, w2_ref, o_ref, o2_ref):
    o2_ref[...] = jnp.zeros(o2_ref.shape, BF16)
    o2_ref[0:GLA_RANK] = w2_ref[...].astype(BF16)
    kv_end = W_KV + 2 * KV_WIDTH
    o_ref[W_QA:kv_end] = w_ref[0:kv_end].astype(BF16)
    o_ref[kv_end:W_QG] = jnp.zeros((RANK_PAD, o_ref.shape[1]), BF16)
    o_ref[kv_end:kv_end + GLA_RANK] = w_ref[MAIN_WIDTH:MAIN_WIDTH + GLA_RANK].astype(BF16)
    o_ref[W_QG:IN_PAD_WIDTH] = w_ref[kv_end:MAIN_WIDTH].astype(BF16)


def _weight_prep(w_t, w2, cols):
    n, k = w_t.shape
    assert n == MAIN_WIDTH + GLA_RANK and k % cols == 0 and w2.shape == (GLA_RANK, GLA_QK_WIDTH)
    return pl.pallas_call(
        _weight_prep_kernel,
        grid=(k // cols,),
        in_specs=[pl.BlockSpec((n, cols), lambda i: (0, i)),
                  _const_spec((GLA_RANK, GLA_QK_WIDTH))],
        out_specs=[pl.BlockSpec((IN_PAD_WIDTH, cols), lambda i: (0, i)),
                   pl.BlockSpec((RANK_PAD, GLA_QK_WIDTH), lambda i: (0, 0))],
        out_shape=[jax.ShapeDtypeStruct((IN_PAD_WIDTH, k), BF16),
                   jax.ShapeDtypeStruct((RANK_PAD, GLA_QK_WIDTH), BF16)],
        compiler_params=pltpu.CompilerParams(
            dimension_semantics=("arbitrary",), vmem_limit_bytes=VMEM_LIMIT),
        name="weight_prep",
    )(w_t, w2)


def _project(x, xs, g_attn, w_in_p, qn, kn, w2p, b2, rows, to_cast):
    t = x.shape[0]
    ns = xs.shape[0]
    steps = t // rows
    cast_specs = []
    for w in to_cast:
        slab = w.shape[0] // steps
        assert w.shape[0] % steps == 0 and slab % (2 * SUBLANES) == 0
        cast_specs.append(pl.BlockSpec((slab, w.shape[1]), lambda i: (i, 0)))
    return pl.pallas_call(
        _proj_kernel,
        grid=(steps,),
        in_specs=[
            pl.BlockSpec((rows, D_MODEL), lambda i: (i, 0)),
            _sample_rows_spec(ns, pipeline_mode=pl.Buffered(1)),
            _const_spec((1, D_MODEL)),
            _const_spec((IN_PAD_WIDTH, D_MODEL)),
            _const_spec((1, HEAD_DIM)),
            _const_spec((1, HEAD_DIM)),
            _const_spec((RANK_PAD, GLA_QK_WIDTH)),
            _const_spec((1, GLA_QK_WIDTH)),
            _const_spec((BLK, 3 * BLK)),
        ] + cast_specs,
        out_specs=[pl.BlockSpec((rows, F_WIDTH), lambda i: (i, 0)),
                   pl.BlockSpec((rows, H_WIDTH), lambda i: (i, 0)),
                   pl.BlockSpec((KV_WIDTH, rows), lambda i: (0, i)),
                   pl.BlockSpec((WINDOW, 2 * KV_WIDTH), lambda i: (0, 0)),
                   pl.BlockSpec((ns, F_WIDTH), lambda i: (0, 0)),
                   pl.BlockSpec((ns, H_WIDTH), lambda i: (0, 0)),
                   pl.BlockSpec((ns, 2 * KV_WIDTH), lambda i: (0, 0))] + cast_specs,
        out_shape=[jax.ShapeDtypeStruct((t, F_WIDTH), F32),
                   jax.ShapeDtypeStruct((t, H_WIDTH), BF16),
                   jax.ShapeDtypeStruct((KV_WIDTH, t), BF16),
                   jax.ShapeDtypeStruct((WINDOW, 2 * KV_WIDTH), F32),
                   jax.ShapeDtypeStruct((ns, F_WIDTH), F32),
                   jax.ShapeDtypeStruct((ns, H_WIDTH), F32),
                   jax.ShapeDtypeStruct((ns, 2 * KV_WIDTH), F32)]
        + [jax.ShapeDtypeStruct(w.shape, BF16) for w in to_cast],
        compiler_params=pltpu.CompilerParams(
            dimension_semantics=("arbitrary",), vmem_limit_bytes=VMEM_LIMIT),
        name="proj",
    )(x, xs, g_attn, w_in_p, qn, kn, w2p, b2, jnp.asarray(_TRI3, BF16), *to_cast)


def _boundary_rows(b_ref, r0, c0, b, row, level):
    m = 1 << (level - 1)
    if 2 * m >= SUBLANES:
        pieces = [jnp.broadcast_to(b_ref[r0 + g * 2 * m + m - 1:r0 + g * 2 * m + m, c0:c0 + LANES],
                                   (2 * m, LANES))
                  for g in range(BLK // (2 * m))]
        return pieces[0] if len(pieces) == 1 else jnp.concatenate(pieces, axis=0)
    pos = row & (2 * m - 1)
    tiles = b.reshape(BLK // SUBLANES, SUBLANES, LANES)
    out = b
    for p in range(2 * m):
        shift = (m - 1) - p
        if shift != 0:
            rolled = pltpu.roll(tiles, (-shift) % SUBLANES, 1).reshape(BLK, LANES)
            out = jnp.where(pos == p, rolled, out)
    return out


def _mixer_init(relb_ref, bucket_ref, kbd, vbd, kprev_t, vprev, st_scr, mb_scr):
    kbd[...] = jnp.zeros_like(kbd)
    vbd[...] = jnp.zeros_like(vbd)
    for slot in range(vbd.shape[0]):
        for g in range(ATTN_KV_HEADS):
            for half in range(2):
                vbd[slot, g, 2 * half * BLK:(2 * half + 2) * BLK,
                    KV_WIDTH + half * HEAD_DIM:KV_WIDTH + (half + 1) * HEAD_DIM] = (
                        jnp.ones((2 * BLK, HEAD_DIM), BF16))
    kprev_t[...] = jnp.zeros_like(kprev_t)
    vprev[...] = jnp.zeros_like(vprev)
    st_scr[...] = jnp.zeros_like(st_scr)
    bk = bucket_ref[0]
    acc = [jnp.zeros(bk.shape, F32) for _ in range(ATTN_HEADS)]
    for b in range(N_BUCKETS):
        hit = bk == b
        for h in range(ATTN_HEADS):
            acc[h] = jnp.where(hit, relb_ref[h, b], acc[h])
    for tb in range(2):
        masked = bucket_ref[tb] < 0
        for h in range(ATTN_HEADS):
            mb_scr[tb, h // 2, :, (h % 2) * BLK:(h % 2 + 1) * BLK] = jnp.where(masked, NEG, acc[h])


def _mixer_attention(pb_ref, kt_ref, r0, table, sink_ref, omix_ref, kbd, vbd, k_prev_t, v_prev,
                     mb_scr):
    rows = slice(r0, r0 + BLK)
    lo1 = lax.broadcasted_iota(jnp.int32, (BLK, LANES), 1) < HEAD_DIM
    k_t = kt_ref[:, r0:r0 + BLK]
    for g in range(ATTN_KV_HEADS):
        hd = slice(g * HEAD_DIM, (g + 1) * HEAD_DIM)
        kbd[g, 0:HEAD_DIM, 0:BLK] = k_prev_t[hd]
        kbd[g, 0:HEAD_DIM, BLK:2 * BLK] = k_t[hd]
        kbd[g, HEAD_DIM:2 * HEAD_DIM, 2 * BLK:3 * BLK] = k_prev_t[hd]
        kbd[g, HEAD_DIM:2 * HEAD_DIM, 3 * BLK:4 * BLK] = k_t[hd]
    lo_bf = jnp.where(lo1, 1.0, 0.0).astype(BF16)
    hi_bf = jnp.where(lo1, 0.0, 1.0).astype(BF16)
    lo_row = lax.broadcasted_iota(jnp.int32, (1, LANES), 1) < HEAD_DIM
    own = (lax.broadcasted_iota(jnp.int32, (BLK, BLK), 1)
           <= lax.broadcasted_iota(jnp.int32, (BLK, BLK), 0))
    own_bf = jnp.where(own, 1.0, 0.0).astype(BF16)
    prev_bf = jnp.where(own, 0.0, 1.0).astype(BF16)
    v_cur = pb_ref[rows,H_VA:H_VS]
    v_swap = pb_ref[rows,H_VS:H_VG]
    v_parts = (v_cur * lo_bf, v_swap * hi_bf, v_swap * lo_bf, v_cur * hi_bf)
    for n, part in enumerate(v_parts):
        g, half = divmod(n, 2)
        vbd[g, 2 * half * BLK:(2 * half + 1) * BLK, 0:KV_WIDTH] = v_prev[n]
        vbd[g, (2 * half + 1) * BLK:(2 * half + 2) * BLK, 0:KV_WIDTH] = part
    chunks_per_kv = ATTN_HEADS // ATTN_KV_HEADS // 2
    for g in range(ATTN_KV_HEADS):
        c_first = g * chunks_per_kv
        qs = jnp.concatenate(
            [pb_ref[rows,H_QA + (c_first + c) * LANES:H_QA + (c_first + c + 1) * LANES]
             for c in range(chunks_per_kv)], axis=0)
        s = jnp.dot(qs, kbd[g], preferred_element_type=F32)
        prob_rows, sink_rows = [], []
        for c in range(chunks_per_kv):
            probs, maxes, sks = [], [], []
            for e in range(2):
                s_prev = s[c * BLK:(c + 1) * BLK, 2 * e * BLK:(2 * e + 1) * BLK]
                s_own = s[c * BLK:(c + 1) * BLK, (2 * e + 1) * BLK:(2 * e + 2) * BLK]
                se = (jnp.where(own, s_own, s_prev)
                      + mb_scr[table, c_first + c, :, e * BLK:(e + 1) * BLK])
                sk = sink_ref[2 * (c_first + c) + e]
                m = jnp.maximum(jnp.max(se, axis=-1, keepdims=True), sk)
                pe = jnp.exp(se - m).astype(BF16)
                probs += [pe * prev_bf, pe * own_bf]
                maxes.append(m)
                sks.append(sk)
            prob_rows.append(jnp.concatenate(probs, axis=1))
            sink_rows.append(jnp.exp(jnp.where(lo_row, sks[0], sks[1])
                                     - jnp.where(lo1, maxes[0], maxes[1])))
        o_den = jnp.dot(jnp.concatenate(prob_rows, axis=0), vbd[g], preferred_element_type=F32)
        o = o_den[:, :KV_WIDTH] / (o_den[:, KV_WIDTH:] + jnp.concatenate(sink_rows, axis=0))
        for c in range(chunks_per_kv):
            omix_ref[rows,(c_first + c) * LANES:(c_first + c + 1) * LANES] = (
                o[c * BLK:(c + 1) * BLK].astype(omix_ref.dtype))
    return k_t, v_parts


def _mixer_gla(state, p_ref, pb_ref, r0, lev_ref, gn_ref, omix_ref):
    new_state = []
    pairs = range(len(state))
    rows = slice(r0, r0 + BLK)
    lo1 = lax.broadcasted_iota(jnp.int32, (BLK, LANES), 1) < HEAD_DIM
    lo_bf = jnp.where(lo1, 1.0, 0.0).astype(BF16)
    hi_bf = jnp.where(lo1, 0.0, 1.0).astype(BF16)
    row = lax.broadcasted_iota(jnp.int32, (BLK, LANES), 0)
    zero_blk = jnp.zeros((BLK, LANES), BF16)
    for c in pairs:
        c0 = c * LANES
        q_at = lambda a, z: p_ref[r0 + a:r0 + z, F_QG + c0:F_QG + c0 + LANES]
        k_at = lambda a, z: p_ref[r0 + a:r0 + z, F_KG + c0:F_KG + c0 + LANES]
        b_at = lambda a, z: p_ref[r0 + a:r0 + z, F_LA + c0:F_LA + c0 + LANES]

        def pair_scores(qtb, ktb_lo, ktb_hi):
            return _nt_dot(qtb, jnp.concatenate([ktb_lo, ktb_hi], axis=0))

        kb = k_at(0, BLK).astype(BF16)
        s0 = pair_scores(q_at(0, BLK).astype(BF16), kb * lo_bf, kb * hi_bf)
        tile = lambda a, n: a[n * SUBLANES:(n + 1) * SUBLANES]
        n_tiles = BLK // SUBLANES
        lev_t = [lev_ref[n * SUBLANES:(n + 1) * SUBLANES, :] for n in range(n_tiles)]
        sc = [[jnp.where(lev_t[n] == 0, tile(s0[:, e * BLK:(e + 1) * BLK], n), 0.0)
               for n in range(n_tiles)] for e in range(2)]
        for level in range(1, N_LEVELS + 1):
            m = 1 << (level - 1)
            if m >= SUBLANES:
                qs, klos, khis, dest = [], [], [], []
                zeros = jnp.zeros((m, LANES), BF16)
                lane_m = lax.broadcasted_iota(jnp.int32, (m, LANES), 1) < HEAD_DIM
                lo_m = jnp.where(lane_m, 1.0, 0.0).astype(BF16)
                hi_m = jnp.where(lane_m, 0.0, 1.0).astype(BF16)
                for g in range(BLK // (2 * m)):
                    lo_a, up_a, up_z = g * 2 * m, g * 2 * m + m, (g + 1) * 2 * m
                    rb = jnp.broadcast_to(b_at(up_a - 1, up_a), (m, LANES))
                    qs.append((q_at(up_a, up_z) * jnp.exp(b_at(up_a, up_z) - rb)).astype(BF16))
                    kp = (k_at(lo_a, up_a) * jnp.exp(rb - b_at(lo_a, up_a))).astype(BF16)
                    klos += [kp * lo_m, zeros]
                    khis += [kp * hi_m, zeros]
                    dest += list(range(up_a // SUBLANES, up_z // SUBLANES))
                sl = pair_scores(jnp.concatenate(qs, axis=0), jnp.concatenate(klos, axis=0),
                                 jnp.concatenate(khis, axis=0))
            else:
                dest = list(range(n_tiles))
                bc = b_at(0, BLK)
                decay = jnp.exp(-jnp.abs(bc - _boundary_rows(p_ref, r0, F_LA + c0, bc, row, level)))
                upper = ((row >> (level - 1)) & 1) == 1
                kl = (k_at(0, BLK) * decay).astype(BF16)
                sl = pair_scores(
                    (q_at(0, BLK) * decay).astype(BF16) * jnp.where(upper, 1.0, 0.0).astype(BF16),
                    kl * jnp.where(upper | ~lo1, 0.0, 1.0).astype(BF16),
                    kl * jnp.where(upper | lo1, 0.0, 1.0).astype(BF16))
            for src, n in enumerate(dest):
                for e in range(2):
                    sc[e][n] = jnp.where(lev_t[n] == level, tile(sl[:, e * BLK:(e + 1) * BLK], src),
                                         sc[e][n])
        sc = jnp.concatenate([jnp.concatenate(sc[e], axis=0) for e in range(2)], axis=1)
        qc, kc, bc = q_at(0, BLK), k_at(0, BLK), b_at(0, BLK)

        b_last = bc[BLK - 1:BLK, :]
        v0 = pb_ref[rows,H_VG + 2 * c0:H_VG + 2 * c0 + LANES]
        v1 = pb_ref[rows,H_VG + 2 * c0 + LANES:H_VG + 2 * c0 + 2 * LANES]
        v_bd = jnp.concatenate([jnp.concatenate([v0, zero_blk], axis=1),
                                jnp.concatenate([zero_blk, v1], axis=1)], axis=0)
        st_c = state[c]
        stb = st_c.astype(BF16)
        st_rhs = jnp.concatenate([stb * lo_bf, stb * hi_bf], axis=0)
        o = (jnp.dot(sc.astype(BF16), v_bd, preferred_element_type=F32)
             + _nt_dot((qc * jnp.exp(bc)).astype(BF16), st_rhs))
        kd = (kc * jnp.exp(b_last - bc)).astype(BF16)
        upd = _tn_dot(jnp.concatenate([v0, v1], axis=1), kd)
        new_state.append(st_c * jnp.exp(b_last) + jnp.where(lo1, upd[:BLK], upd[BLK:]))
        for e in range(2):
            h = 2 * c + e
            oh = o[:, e * LANES:(e + 1) * LANES]
            og = oh * lax.rsqrt(jnp.mean(oh * oh, axis=-1, keepdims=True) + EPS) * gn_ref[...]
            rg = p_ref[rows,F_RG + h * GLA_DV:F_RG + (h + 1) * GLA_DV]
            gated = og * (rg * _sigmoid(rg))
            omix_ref[rows,ATTN_WIDTH + h * GLA_DV:ATTN_WIDTH + (h + 1) * GLA_DV] = (
                gated.astype(omix_ref.dtype))
    return new_state


def _prompt_mixer_kernel(relb_ref, sink_ref, p_ref, pb_ref, kt_ref, bucket_ref, lev_ref, gn_ref,
                         omix_ref, st_ref, kbd, vbd, kprev_t, vprev, st_scr, mb_scr):
    i = pl.program_id(0)

    @pl.when(i == 0)
    def _init():
        _mixer_init(relb_ref, bucket_ref, kbd, vbd, kprev_t, vprev, st_scr, mb_scr)

    k_prev_t = kprev_t[...]
    v_prev = [vprev[n] for n in range(2 * ATTN_KV_HEADS)]
    state = [st_scr[:, c * LANES:(c + 1) * LANES] for c in range(GLA_HEADS // 2)]
    n_blocks = p_ref.shape[0] // BLK
    for jb in range(n_blocks):
        table = jnp.where(i == 0, 1, 0) if jb == 0 else 0
        k_prev_t, v_prev = _mixer_attention(pb_ref, kt_ref, jb * BLK, table, sink_ref, omix_ref,
                                            kbd.at[jb], vbd.at[jb], k_prev_t, v_prev, mb_scr)
        state = _mixer_gla(state, p_ref, pb_ref, jb * BLK, lev_ref, gn_ref, omix_ref)
    kprev_t[...] = k_prev_t
    for n, part in enumerate(v_prev):
        vprev[n] = part
    for c, st_c in enumerate(state):
        st_scr[:, c * LANES:(c + 1) * LANES] = st_c

    @pl.when(i == pl.num_programs(0) - 1)
    def _final_state():
        for c, st_c in enumerate(state):
            st_ref[c * LANES:(c + 1) * LANES, :] = st_c.T


def _prompt_mixer(p, pb, kt, relb, sinks, gn):
    t = p.shape[0]
    smem = pl.BlockSpec(memory_space=pltpu.SMEM)
    return pl.pallas_call(
        _prompt_mixer_kernel,
        grid=(t // MIX_ROWS,),
        in_specs=[
            smem, smem,
            pl.BlockSpec((MIX_ROWS, F_WIDTH), lambda i: (i, 0)),
            pl.BlockSpec((MIX_ROWS, H_WIDTH), lambda i: (i, 0)),
            pl.BlockSpec((KV_WIDTH, MIX_ROWS), lambda i: (0, i)),
            _const_spec((2, BLK, BLK)),
            _const_spec((BLK, BLK)),
            _const_spec((1, GLA_DV)),
        ],
        out_specs=[
            pl.BlockSpec((MIX_ROWS, MIX_WIDTH), lambda i: (i, 0)),
            pl.BlockSpec((GLA_QK_WIDTH, GLA_DV), lambda i: (0, 0)),
        ],
        out_shape=[
            jax.ShapeDtypeStruct((t, MIX_WIDTH), BF16),
            jax.ShapeDtypeStruct((GLA_QK_WIDTH, GLA_DV), F32),
        ],
        scratch_shapes=[
            pltpu.VMEM((MIX_ROWS // BLK, ATTN_KV_HEADS, 2 * HEAD_DIM, 4 * BLK), BF16),
            pltpu.VMEM((MIX_ROWS // BLK, ATTN_KV_HEADS, 4 * BLK, 2 * KV_WIDTH), BF16),
            pltpu.VMEM((KV_WIDTH, BLK), BF16),
            pltpu.VMEM((2 * ATTN_KV_HEADS, BLK, KV_WIDTH), BF16),
            pltpu.VMEM((GLA_DV, GLA_QK_WIDTH), F32),
            pltpu.VMEM((2, ATTN_HEADS // 2, BLK, 2 * BLK), F32),
        ],
        compiler_params=pltpu.CompilerParams(
            dimension_semantics=("arbitrary",), vmem_limit_bytes=VMEM_LIMIT),
        name="prompt_mixer",
    )(relb, sinks, p, pb, kt, jnp.asarray(_BUCKET_PROMPT), jnp.asarray(_LEV), gn)


def _sample_mixer_kernel(ps_ref, ph_ref, kvfull_ref, pfull_ref, ck_ref, cv_ref, st_ref, relbt_ref,
                         sink_ref, bucket_ref,
                         gn_ref, omix_ref, kwin_ref, vwin_ref, stout_ref, lat_scr, kqt_scr, kvt_scr,
                         bias_scr, s_scr, o_scr, og_scr):
    i = pl.program_id(0)
    nb = pfull_ref.shape[0]

    @pl.when(i == 0)
    def _init():
        lat_scr[...] = _split3_rows(pfull_ref[:, F_LA:F_WIDTH].T)
        kqt_scr[0:GLA_QK_WIDTH] = pfull_ref[:, F_KG:F_RG].T.astype(BF16)
        kqt_scr[GLA_QK_WIDTH:2 * GLA_QK_WIDTH] = pfull_ref[:, F_QG:F_KG].T.astype(BF16)
        kvt = kvfull_ref[...].T
        for s in range(nb // SAMPLE_BLK):
            shift = nb - (s + 1) * SAMPLE_BLK
            kvt_scr[s] = pltpu.roll(kvt, shift, 1) if shift else kvt
        bk = jnp.broadcast_to(bucket_ref[...], (ATTN_HEADS, WINDOW))
        acc = jnp.zeros((ATTN_HEADS, WINDOW), F32)
        for b in range(N_BUCKETS):
            acc = jnp.where(bk == b, relbt_ref[:, b:b + 1], acc)
        bias_scr[...] = acc

    lo = lax.broadcasted_iota(jnp.int32, (1, LANES), 1) < HEAD_DIM
    sub = lax.broadcasted_iota(jnp.int32, (ATTN_HEADS, LANES), 0)
    newest = lax.broadcasted_iota(jnp.int32, (KV_WIDTH, WINDOW), 1) == WINDOW - 1
    heads_per_kv = ATTN_HEADS // ATTN_KV_HEADS

    n_of_col = i * SAMPLE_BLK + lax.broadcasted_iota(jnp.int32, (nb, SAMPLE_BLK * LANES), 1) // LANES
    pick = jnp.where(lax.broadcasted_iota(jnp.int32, (nb, SAMPLE_BLK * LANES), 0) == n_of_col,
                     1.0, 0.0).astype(BF16)
    la_b = (jnp.dot(lat_scr[0:GLA_QK_WIDTH], pick, preferred_element_type=F32)
            + jnp.dot(lat_scr[GLA_QK_WIDTH:2 * GLA_QK_WIDTH], pick, preferred_element_type=F32)
            + jnp.dot(lat_scr[2 * GLA_QK_WIDTH:3 * GLA_QK_WIDTH], pick, preferred_element_type=F32))
    kq_b = jnp.dot(kqt_scr[...], pick, preferred_element_type=F32)

    for j in range(SAMPLE_BLK):
        kv_new = kvt_scr[i]
        if j < SAMPLE_BLK - 1:
            kv_new = pltpu.roll(kv_new, SAMPLE_BLK - 1 - j, 1)
        kwin_ref[j] = jnp.where(newest, kv_new[0:KV_WIDTH], pltpu.roll(ck_ref[j], WINDOW - 1, 1))
        vwin_ref[j] = jnp.where(newest, kv_new[KV_WIDTH:2 * KV_WIDTH],
                                pltpu.roll(cv_ref[j], WINDOW - 1, 1))

    for j in range(SAMPLE_BLK):
        qexp = jnp.zeros((ATTN_HEADS, LANES), F32)
        for c in range(ATTN_HEADS // 2):
            chunk = ph_ref[j:j + 1, H_QA + c * LANES:H_QA + (c + 1) * LANES]
            swapped = pltpu.roll(chunk, HEAD_DIM, 1)
            if (2 * c) // heads_per_kv == 0:
                rows = (jnp.where(lo, chunk, 0.0), jnp.where(lo, swapped, 0.0))
            else:
                rows = (jnp.where(lo, 0.0, swapped), jnp.where(lo, 0.0, chunk))
            for e in range(2):
                qexp = jnp.where(sub == 2 * c + e, rows[e], qexp)
        s_scr[j * ATTN_HEADS:(j + 1) * ATTN_HEADS] = jnp.dot(
            qexp.astype(BF16), kwin_ref[j].astype(BF16), preferred_element_type=F32)

    tile = lambda x: jnp.concatenate([x] * SAMPLE_BLK, axis=0)
    head_of_row = lax.broadcasted_iota(jnp.int32, (ATTN_HEADS, 1), 0)
    sink8 = jnp.zeros((ATTN_HEADS, 1), F32)
    for h in range(ATTN_HEADS):
        sink8 = jnp.where(head_of_row == h, sink_ref[h], sink8)
    sink = tile(sink8)
    s = s_scr[...] + tile(bias_scr[...])
    m = jnp.maximum(jnp.max(s, axis=-1, keepdims=True), sink)
    pe = jnp.exp(s - m)
    inv_den = 1.0 / (jnp.sum(pe, axis=-1, keepdims=True) + jnp.exp(sink - m))
    peb = pe.astype(BF16)
    for j in range(SAMPLE_BLK):
        o_scr[j * ATTN_HEADS:(j + 1) * ATTN_HEADS] = _nt_dot(
            peb[j * ATTN_HEADS:(j + 1) * ATTN_HEADS], vwin_ref[j].astype(BF16))
    o_all = o_scr[...] * inv_den
    o_swap = pltpu.roll(o_all, HEAD_DIM, 1)
    for j in range(SAMPLE_BLK):
        r = j * ATTN_HEADS
        for c in range(ATTN_HEADS // 2):
            if (2 * c) // heads_per_kv == 0:
                piece = jnp.where(lo, o_all[r + 2 * c:r + 2 * c + 1, :], o_swap[r + 2 * c + 1:r + 2 * c + 2, :])
            else:
                piece = jnp.where(lo, o_swap[r + 2 * c:r + 2 * c + 1, :], o_all[r + 2 * c + 1:r + 2 * c + 2, :])
            omix_ref[j:j + 1, c * LANES:(c + 1) * LANES] = piece

    for j in range(SAMPLE_BLK):
        cols = slice(j * LANES, (j + 1) * LANES)
        for h in range(GLA_HEADS):
            rs = slice(h * GLA_DK, (h + 1) * GLA_DK)
            qs = slice(GLA_QK_WIDTH + h * GLA_DK, GLA_QK_WIDTH + (h + 1) * GLA_DK)
            v_row = ph_ref[j:j + 1, H_VG + h * GLA_DV:H_VG + (h + 1) * GLA_DV]
            s_new = jnp.exp(la_b[rs, cols]) * st_ref[j, h] + kq_b[rs, cols] * v_row
            stout_ref[j, h] = s_new
            og_scr[j:j + 1, h * GLA_DV:(h + 1) * GLA_DV] = jnp.sum(
                kq_b[qs, cols] * s_new, axis=0, keepdims=True)
    for h in range(GLA_HEADS):
        hs = slice(h * GLA_DV, (h + 1) * GLA_DV)
        og = og_scr[:, hs]
        og = og * lax.rsqrt(jnp.mean(og * og, axis=-1, keepdims=True) + EPS) * gn_ref[...]
        rg = ps_ref[:, F_RG + h * GLA_DV:F_RG + (h + 1) * GLA_DV]
        omix_ref[:, ATTN_WIDTH + h * GLA_DV:ATTN_WIDTH + (h + 1) * GLA_DV] = og * (rg * _sigmoid(rg))


def _sample_mixer(ps, ph, kv, cache_k, cache_v, state, relbt, sinks, gn):
    nb = ps.shape[0]
    assert nb == LANES
    blk3 = lambda i: (i, 0, 0)
    blk4 = lambda i: (i, 0, 0, 0)
    return pl.pallas_call(
        _sample_mixer_kernel,
        grid=(nb // SAMPLE_BLK,),
        in_specs=[
            pl.BlockSpec((SAMPLE_BLK, F_WIDTH), lambda i: (i, 0)),
            pl.BlockSpec((SAMPLE_BLK, H_WIDTH), lambda i: (i, 0)),
            _const_spec((nb, 2 * KV_WIDTH)),
            _const_spec((nb, F_WIDTH)),
            pl.BlockSpec((SAMPLE_BLK, KV_WIDTH, WINDOW), blk3),
            pl.BlockSpec((SAMPLE_BLK, KV_WIDTH, WINDOW), blk3),
            pl.BlockSpec((SAMPLE_BLK, GLA_HEADS, GLA_DK, GLA_DV), blk4),
            _const_spec((ATTN_HEADS, N_BUCKETS)),
            pl.BlockSpec(memory_space=pltpu.SMEM),
            _const_spec((1, WINDOW)),
            _const_spec((1, GLA_DV)),
        ],
        out_specs=[
            pl.BlockSpec((SAMPLE_BLK, MIX_WIDTH), lambda i: (i, 0)),
            pl.BlockSpec((SAMPLE_BLK, KV_WIDTH, WINDOW), blk3),
            pl.BlockSpec((SAMPLE_BLK, KV_WIDTH, WINDOW), blk3),
            pl.BlockSpec((SAMPLE_BLK, GLA_HEADS, GLA_DK, GLA_DV), blk4),
        ],
        out_shape=[
            jax.ShapeDtypeStruct((nb, MIX_WIDTH), F32),
            jax.ShapeDtypeStruct((nb, KV_WIDTH, WINDOW), F32),
            jax.ShapeDtypeStruct((nb, KV_WIDTH, WINDOW), F32),
            jax.ShapeDtypeStruct((nb, GLA_HEADS, GLA_DK, GLA_DV), F32),
        ],
        scratch_shapes=[
            pltpu.VMEM((3 * GLA_QK_WIDTH, nb), BF16),
            pltpu.VMEM((2 * GLA_QK_WIDTH, nb), BF16),
            pltpu.VMEM((nb // SAMPLE_BLK, 2 * KV_WIDTH, nb), F32),
            pltpu.VMEM((ATTN_HEADS, WINDOW), F32),
            pltpu.VMEM((SAMPLE_BLK * ATTN_HEADS, WINDOW), F32),
            pltpu.VMEM((SAMPLE_BLK * ATTN_HEADS, KV_WIDTH), F32),
            pltpu.VMEM((SAMPLE_BLK, GLA_WIDTH), F32),
        ],
        compiler_params=pltpu.CompilerParams(
            dimension_semantics=("arbitrary",), vmem_limit_bytes=VMEM_LIMIT),
        name="sample_mixer",
    )(ps, ph, kv, ps, cache_k, cache_v, state, relbt, sinks, jnp.asarray(_BUCKET_SAMPLE), gn)


def _finish_kernel(x_ref, mix_ref, xs_ref, mixs_ref, wo_ref, g_ref, wg_ref, wu_ref, wd_ref,
                   y_ref, ys_ref, *, ff_chunks):
    weights = (wo_ref, g_ref, wg_ref, wu_ref, wd_ref)
    _finish_rows(x_ref, mix_ref, *weights, y_ref, ff_chunks=ff_chunks)

    @pl.when(pl.program_id(0) == pl.num_programs(0) - 1)
    def _samples():
        _finish_rows(xs_ref, mixs_ref, *weights, ys_ref, ff_chunks=ff_chunks)


def _finish_rows(x_ref, mix_ref, wo_ref, g_ref, wg_ref, wu_ref, wd_ref, y_ref, *, ff_chunks):
    h = x_ref[...] + jnp.dot(mix_ref[...].astype(BF16), wo_ref[...], preferred_element_type=F32)
    r = lax.rsqrt(jnp.mean(h * h, axis=-1, keepdims=True) + EPS)
    z = (h * g_ref[...]).astype(BF16)
    n_tiles = wd_ref.shape[0] // MXU_TILE
    acc = h
    for c in range(ff_chunks):
        c0 = ((c * n_tiles) // ff_chunks) * MXU_TILE
        c1 = (((c + 1) * n_tiles) // ff_chunks) * MXU_TILE
        gate = jnp.dot(z, wg_ref[:, c0:c1], preferred_element_type=F32) * r
        up = jnp.dot(z, wu_ref[:, c0:c1], preferred_element_type=F32) * r
        act = ((gate * _sigmoid(gate)) * up).astype(BF16)
        acc = acc + jnp.dot(act, wd_ref[c0:c1, :], preferred_element_type=F32)
    y_ref[...] = acc


def _finish(x, mix, xs, mixs, wo, g_ffn, wg, wu, wd, rows):
    t = x.shape[0]
    ns = xs.shape[0]
    d_ff = wd.shape[0]
    assert d_ff % MXU_TILE == 0
    return pl.pallas_call(
        functools.partial(_finish_kernel, ff_chunks=FF_CHUNKS),
        grid=(t // rows,),
        in_specs=[
            pl.BlockSpec((rows, D_MODEL), lambda i: (i, 0)),
            pl.BlockSpec((rows, MIX_WIDTH), lambda i: (i, 0)),
            _sample_rows_spec(ns, pipeline_mode=pl.Buffered(1)),
            _const_spec((ns, MIX_WIDTH)),
            _const_spec((MIX_WIDTH, D_MODEL)),
            _const_spec((1, D_MODEL)),
            _const_spec((D_MODEL, d_ff)),
            _const_spec((D_MODEL, d_ff)),
            _const_spec((d_ff, D_MODEL)),
        ],
        out_specs=[pl.BlockSpec((rows, D_MODEL), lambda i: (i, 0)),
                   _sample_rows_spec(ns)],
        out_shape=[jax.ShapeDtypeStruct((t, D_MODEL), F32),
                   jax.ShapeDtypeStruct((ns, 1, D_MODEL), F32)],
        compiler_params=pltpu.CompilerParams(
            dimension_semantics=("arbitrary",), vmem_limit_bytes=VMEM_LIMIT),
        name="finish",
    )(x, mix, xs, mixs, wo, g_ffn, wg, wu, wd)


PROMPT_ROWS = 1024
PROJ_ROWS = 1024
PREP_COLS = 512
MIX_ROWS = 1024


def kernel(x_prompt, x_sample, cache_k, cache_v, state_gla, attn_norm_g, w_in, q_norm_g, k_norm_g,
           attn_sinks, rel_bias, w_gla_gate2, b_gla_gate, gla_norm_g, w_o, ffn_norm_g, w_gate, w_up,
           w_down):
    depth = w_in.shape[0]
    batch, seq, _ = x_prompt.shape
    dec_batch, dec_seq, _ = x_sample.shape
    wb = cache_k.shape[2]
    assert batch == 1 and dec_seq == 1 and wb == WINDOW
    assert seq % PROMPT_ROWS == 0 and seq % PROJ_ROWS == 0
    assert dec_batch % SAMPLE_BLK == 0 and dec_batch % LANES == 0
    assert rel_bias.shape == (N_BUCKETS, ATTN_HEADS)

    xp = x_prompt.reshape(seq, D_MODEL)
    xs = x_sample
    relb_t = rel_bias.T
    outs = ([], [], [], [], [], [])
    for l in range(depth):
        w_in_p, w2p = _weight_prep(w_in[l].T, w_gla_gate2[l], cols=PREP_COLS)
        proj_w = (attn_norm_g[l][None, :], w_in_p,
                  q_norm_g[l][None, :], k_norm_g[l][None, :], w2p, b_gla_gate[l][None, :])
        gn = gla_norm_g[l][None, :]

        pp, pb, kt_p, kv_win, ps, ph, kv_s, wo_b, wg_b, wu_b, wd_b = _project(
            xp, xs, *proj_w, rows=PROJ_ROWS, to_cast=(w_o[l], w_gate[l], w_up[l], w_down[l]))
        fin_w = (wo_b, ffn_norm_g[l][None, :], wg_b, wu_b, wd_b)
        mix_p, st_p = _prompt_mixer(pp, pb, kt_p, relb_t, attn_sinks[l], gn)
        to_t = lambda c: jnp.transpose(c, (0, 2, 3, 1)).reshape(dec_batch, KV_WIDTH, wb)
        from_t = lambda c: jnp.transpose(c.reshape(dec_batch, ATTN_KV_HEADS, HEAD_DIM, wb), (0, 3, 1, 2))
        mix_s, kwin_t, vwin_t, st_s = _sample_mixer(
            ps, ph, kv_s, to_t(cache_k[l]), to_t(cache_v[l]),
            state_gla[l].astype(F32), relb_t, attn_sinks[l], gn)
        xp_in = xp
        xp, xs = _finish(xp_in, mix_p, xs, mix_s, *fin_w, rows=PROMPT_ROWS)
        outs[0].append(kv_win[:, :KV_WIDTH].reshape(batch, wb, ATTN_KV_HEADS, HEAD_DIM))
        outs[1].append(kv_win[:, KV_WIDTH:].reshape(batch, wb, ATTN_KV_HEADS, HEAD_DIM))
        outs[2].append(st_p.reshape(batch, GLA_HEADS, GLA_DK, GLA_DV).astype(state_gla.dtype))
        outs[3].append(from_t(kwin_t))
        outs[4].append(from_t(vwin_t))
        outs[5].append(st_s.astype(state_gla.dtype))

    y_prompt = xp.reshape(batch, seq, D_MODEL)
    y_sample = xs
    return (y_prompt, y_sample) + tuple(jnp.stack(o) for o in outs)
```
